```python
import math
import jax
import jax.numpy as jnp
from jax import lax
import numpy as np

D_MODEL = 1024
BATCH = 16
SEQ = 2048
DEPTH = 2

GRID_W = 64
CTX_LEN = 256
N_EVEN = (DEPTH + 1) // 2
N_ODD = DEPTH // 2

A_WIDTH = D_MODEL // 2
DIFF_HEADS = 4
DIFF_DH = 64
DIFF_VDIM = 2 * DIFF_DH
DIFF_QK = DIFF_HEADS * 2 * DIFF_DH
B_WIDTH = DIFF_HEADS * DIFF_VDIM
EVEN_IN = 3 * A_WIDTH + 2 * DIFF_QK + B_WIDTH
EVEN_MIX = A_WIDTH + B_WIDTH

C_WIDTH = D_MODEL // 2
C_GROUPS = 4
C_GROUP_DIM = C_WIDTH // C_GROUPS
NA_HEADS = 8
NA_DH = 64
D_WIDTH = NA_HEADS * NA_DH
ODD_IN = C_WIDTH + 3 * D_WIDTH
ODD_MIX = C_WIDTH + D_WIDTH
NA_KR_MAX = 8
NA_KC = 16

D_FF = 2816
N_EXPERTS = 8
TOP_K = 2
D_FF_EXPERT = 3584

ROPE_THETA = 10000.0
LN_EPS = 1e-5
RMS_EPS = 1e-5
Q_BLOCK = 128
NEG_INF = -1e30
DEEPNORM_ALPHA = (2 * DEPTH) ** 0.25
DEEPNORM_BETA = (8 * DEPTH) ** -0.25

kernel_name = 'hybrid_diffusion_conv_diffattn_fnet_natten_moe'


def layer_norm(x, g, b):
    xf = x.astype(jnp.float32)
    mu = jnp.mean(xf, axis=-1, keepdims=True)
    var = jnp.mean(jnp.square(xf - mu), axis=-1, keepdims=True)
    return ((xf - mu) * lax.rsqrt(var + LN_EPS) * g + b).astype(x.dtype)


def adaln(cond, w_mod, b_mod):
    m = (jax.nn.silu(cond) @ w_mod + b_mod)[..., None, :]
    return jnp.split(m, 6, axis=-1)


def modulate(x, shift, scale):
    return x * (1 + scale) + shift


def short_conv3(u, w):
    up = jnp.pad(u, ((0, 0), (1, 1), (0, 0)))
    return up[:, :-2] * w[0] + up[:, 1:-1] * w[1] + up[:, 2:] * w[2]


def short_conv_mixer(pa, w_conv):
    bg, cg, val = jnp.split(pa, 3, axis=-1)
    return bg * short_conv3(cg * val, w_conv)


def axial_rope(n_tok, head_dim):
    t = jnp.arange(n_tok, dtype=jnp.int32)
    row = (t // GRID_W).astype(jnp.float32)
    col = (t % GRID_W).astype(jnp.float32)
    n_freq = head_dim // 4
    inv_freq = ROPE_THETA ** (-jnp.arange(n_freq, dtype=jnp.float32) / n_freq)
    ang = jnp.concatenate([row[:, None] * inv_freq, col[:, None] * inv_freq], axis=-1)
    return jnp.cos(ang), jnp.sin(ang)


def apply_rope(x, cos, sin):
    shape = (1, cos.shape[0]) + (1,) * (x.ndim - 3) + (cos.shape[1],)
    c = cos.reshape(shape)
    s = sin.reshape(shape)
    x1, x2 = jnp.split(x.astype(jnp.float32), 2, axis=-1)
    return jnp.concatenate([x1 * c - x2 * s, x1 * s + x2 * c], axis=-1).astype(x.dtype)


def diff_attend(q, k, v, lam):
    s = jnp.einsum('bqhcd,bkhcd->bhcqk', q, k, preferred_element_type=jnp.float32) * (DIFF_DH ** -0.5)
    p = jax.nn.softmax(s, axis=-1)
    a = p[:, :, 0] - lam * p[:, :, 1]
    return jnp.einsum('bhqk,bkhe->bqhe', a.astype(v.dtype), v, preferred_element_type=jnp.float32)


def diff_head_norm(o, g, lam_init, dtype):
    ms = jnp.mean(jnp.square(o), axis=-1, keepdims=True)
    return (o * lax.rsqrt(ms + RMS_EPS) * g * (1.0 - lam_init)).astype(dtype)


def diff_attention_latent(q, k_all, v_all, lam):
    b, l, h, _, dh = q.shape
    nb = l // Q_BLOCK
    qb = q.reshape(b, nb, Q_BLOCK, h, 2, dh).swapaxes(0, 1)
    ob = lax.map(lambda qi: diff_attend(qi, k_all, v_all, lam), qb)
    return ob.swapaxes(0, 1).reshape(b, l, h, DIFF_VDIM)


def even_mixer(h_lat, h_ctx, w_in, w_conv, lam, sub_g, lam_init, w_o, need_ctx):
    b, l, _ = h_lat.shape
    lc = h_ctx.shape[1]
    dt = h_lat.dtype
    a_end = 3 * A_WIDTH
    if need_ctx:
        pa_ctx, q_ctx, kv_ctx = jnp.split(h_ctx @ w_in, [a_end, a_end + DIFF_QK], axis=-1)
    else:
        kv_ctx = h_ctx @ w_in[:, a_end + DIFF_QK:]
    k_ctx, v_ctx = jnp.split(kv_ctx, [DIFF_QK], axis=-1)
    k_ctx = k_ctx.reshape(b, lc, DIFF_HEADS, 2, DIFF_DH)
    v_ctx = v_ctx.reshape(b, lc, DIFF_HEADS, DIFF_VDIM)
    pa, q, k, v = jnp.split(h_lat @ w_in, [a_end, a_end + DIFF_QK, a_end + 2 * DIFF_QK], axis=-1)
    cos, sin = axial_rope(l, DIFF_DH)
    q = apply_rope(q.reshape(b, l, DIFF_HEADS, 2, DIFF_DH), cos, sin)
    k = apply_rope(k.reshape(b, l, DIFF_HEADS, 2, DIFF_DH), cos, sin)
    v = v.reshape(b, l, DIFF_HEADS, DIFF_VDIM)
    k_all = jnp.concatenate([k_ctx, k], axis=1)
    v_all = jnp.concatenate([v_ctx, v], axis=1)
    o = diff_attention_latent(q, k_all, v_all, lam)
    o = diff_head_norm(o, sub_g, lam_init, dt).reshape(b, l, B_WIDTH)
    y_lat = jnp.concatenate([short_conv_mixer(pa, w_conv), o], axis=-1) @ w_o
    y_ctx = None
    if need_ctx:
        o_c = diff_attend(q_ctx.reshape(b, lc, DIFF_HEADS, 2, DIFF_DH), k_ctx, v_ctx, lam)
        o_c = diff_head_norm(o_c, sub_g, lam_init, dt).reshape(b, lc, B_WIDTH)
        y_ctx = jnp.concatenate([short_conv_mixer(pa_ctx, w_conv), o_c], axis=-1) @ w_o
    return y_lat, y_ctx


def fourier_mix(u):
    b, l, _ = u.shape
    g = u.astype(jnp.float32).reshape(b, l, C_GROUPS, C_GROUP_DIM)
    mu = jnp.mean(g, axis=-1, keepdims=True)
    var = jnp.mean(jnp.square(g - mu), axis=-1, keepdims=True)
    g = (g - mu) * lax.rsqrt(var + LN_EPS)
    f = jnp.real(jnp.fft.fft2(g, axes=(1, 3), norm='ortho'))
    return f.reshape(b, l, C_WIDTH).astype(u.dtype)


def neighbourhood_attention(q, k, v, k_ctx, v_ctx, rpb):
    b, l, _ = q.shape
    rows = l // GRID_W
    kr = min(NA_KR_MAX, rows)
    h, dh = NA_HEADS, NA_DH
    scale = dh ** -0.5
    qg = q.reshape(b, rows, GRID_W, h, dh)
    kg = k.reshape(b, rows, GRID_W, h, dh)
    vg = v.reshape(b, rows, GRID_W, h, dh)
    cols = jnp.arange(GRID_W)
    col_start = jnp.clip(cols - NA_KC // 2, 0, GRID_W - NA_KC)
    col_valid = (cols[None, :] >= col_start[:, None]) & (cols[None, :] < col_start[:, None] + NA_KC)
    dc_idx = jnp.clip(cols[None, :] - cols[:, None] + NA_KC - 1, 0, 2 * NA_KC - 2)
    rpb_c = rpb[:, :, dc_idx].astype(jnp.float32)

    def row_fn(r):
        rs = jnp.clip(r - kr // 2, 0, rows - kr)
        q_r = lax.dynamic_index_in_dim(qg, r, axis=1, keepdims=False)
        k_r = lax.dynamic_slice_in_dim(kg, rs, kr, axis=1)
        v_r = lax.dynamic_slice_in_dim(vg, rs, kr, axis=1)
        dr_idx = rs + jnp.arange(kr) - r + NA_KR_MAX - 1
        bias = jnp.take(rpb_c, dr_idx, axis=1).transpose(0, 2, 1, 3)
        s_loc = jnp.einsum('bqhd,bjkhd->bhqjk', q_r, k_r, preferred_element_type=jnp.float32) * scale
        s_loc = jnp.where(col_valid[:, None, :], s_loc + bias[None], NEG_INF)
        s_ctx = jnp.einsum('bqhd,bkhd->bhqk', q_r, k_ctx, preferred_element_type=jnp.float32) * scale
        s = jnp.concatenate([s_loc.reshape(b, h, GRID_W, kr * GRID_W), s_ctx], axis=-1)
        p = jax.nn.softmax(s, axis=-1).astype(v.dtype)
        p_loc = p[..., :kr * GRID_W].reshape(b, h, GRID_W, kr, GRID_W)
        p_ctx = p[..., kr * GRID_W:]
        return (jnp.einsum('bhqjk,bjkhd->bqhd', p_loc, v_r, preferred_element_type=jnp.float32)
                + jnp.einsum('bhqk,bkhd->bqhd', p_ctx, v_ctx, preferred_element_type=jnp.float32))

    o = lax.map(row_fn, jnp.arange(rows))
    return o.transpose(1, 0, 2, 3, 4).reshape(b, l, h * dh).astype(q.dtype)


def odd_mixer(h_lat, h_ctx, w_in, rpb, w_o, need_ctx):
    b, l, _ = h_lat.shape
    lc = h_ctx.shape[1]
    if need_ctx:
        pc_ctx, q_ctx, kv_ctx = jnp.split(h_ctx @ w_in, [C_WIDTH, C_WIDTH + D_WIDTH], axis=-1)
    else:
        kv_ctx = h_ctx @ w_in[:, C_WIDTH + D_WIDTH:]
    k_ctx, v_ctx = jnp.split(kv_ctx, [D_WIDTH], axis=-1)
    k_ctx = k_ctx.reshape(b, lc, NA_HEADS, NA_DH)
    v_ctx = v_ctx.reshape(b, lc, NA_HEADS, NA_DH)
    pc, q, k, v = jnp.split(h_lat @ w_in, [C_WIDTH, C_WIDTH + D_WIDTH, C_WIDTH + 2 * D_WIDTH], axis=-1)
    o = neighbourhood_attention(q, k, v, k_ctx, v_ctx, rpb)
    y_lat = jnp.concatenate([fourier_mix(pc), o], axis=-1) @ w_o
    y_ctx = None
    if need_ctx:
        qc = q_ctx.reshape(b, lc, NA_HEADS, NA_DH)
        s = jnp.einsum('bqhd,bkhd->bhqk', qc, k_ctx, preferred_element_type=jnp.float32) * (NA_DH ** -0.5)
        p = jax.nn.softmax(s, axis=-1).astype(v_ctx.dtype)
        o_c = jnp.einsum('bhqk,bkhd->bqhd', p, v_ctx).reshape(b, lc, D_WIDTH)
        y_ctx = jnp.concatenate([fourier_mix(pc_ctx), o_c], axis=-1) @ w_o
    return y_lat, y_ctx


def swiglu(h, w_gate, w_up, w_down):
    return (jax.nn.silu(h @ w_gate) * (h @ w_up)) @ w_down


def moe_swiglu(h, w_router, w_gate, w_up, w_down):
    logits = jnp.einsum('bld,de->ble', h, w_router, preferred_element_type=jnp.float32)
    top_v, top_i = lax.top_k(logits, TOP_K)
    top_p = jax.nn.softmax(top_v, axis=-1)
    gates = jnp.sum(jax.nn.one_hot(top_i, N_EXPERTS, dtype=jnp.float32) * top_p[..., None], axis=-2).astype(h.dtype)
    out = jnp.zeros_like(h)
    for e in range(N_EXPERTS):
        out = out + gates[..., e:e + 1] * swiglu(h, w_gate[e], w_up[e], w_down[e])
    return out


def setup_inputs(seed: int = 0) -> dict:
    key = jax.random.key(seed)
    ks = jax.random.split(key, 32)
    d = D_MODEL
    beta = DEEPNORM_BETA

    def nrm(k, shape, scale):
        return jax.random.normal(k, shape, jnp.float32) * scale

    return {
        'x': nrm(ks[0], (BATCH, SEQ, d), 1.0),
        'c': nrm(ks[1], (BATCH, d), 1.0),
        'ctx': nrm(ks[2], (BATCH, CTX_LEN, d), 1.0),
        'c_ctx': nrm(ks[3], (d,), 1.0),
        'w_mod': nrm(ks[4], (DEPTH, d, 6 * d), d ** -0.5),
        'b_mod': nrm(ks[5], (DEPTH, 6 * d), 0.02),
        'ln_g': 1.0 + nrm(ks[6], (DEPTH, 2, d), 0.02),
        'ln_b': nrm(ks[7], (DEPTH, 2, d), 0.02),
        'e_w_in': nrm(ks[8], (N_EVEN, d, EVEN_IN), d ** -0.5),
        'e_conv': nrm(ks[9], (N_EVEN, 3, A_WIDTH), 3 ** -0.5),
        'e_lam_q1': nrm(ks[10], (N_EVEN, DIFF_DH), 0.1),
        'e_lam_k1': nrm(ks[11], (N_EVEN, DIFF_DH), 0.1),
        'e_lam_q2': nrm(ks[12], (N_EVEN, DIFF_DH), 0.1),
        'e_lam_k2': nrm(ks[13], (N_EVEN, DIFF_DH), 0.1),
        'e_subln_g': 1.0 + nrm(ks[14], (N_EVEN, DIFF_VDIM), 0.02),
        'e_w_o': nrm(ks[15], (N_EVEN, EVEN_MIX, d), beta * EVEN_MIX ** -0.5),
        'e_ffn_gate': nrm(ks[16], (N_EVEN, d, D_FF), d ** -0.5),
        'e_ffn_up': nrm(ks[17], (N_EVEN, d, D_FF), d ** -0.5),
        'e_ffn_down': nrm(ks[18], (N_EVEN, D_FF, d), beta * D_FF ** -0.5),
        'o_w_in': nrm(ks[19], (N_ODD, d, ODD_IN), d ** -0.5),
        'o_rpb': nrm(ks[20], (N_ODD, NA_HEADS, 2 * NA_KR_MAX - 1, 2 * NA_KC - 1), 0.1),
        'o_w_o': nrm(ks[21], (N_ODD, ODD_MIX, d), beta * ODD_MIX ** -0.5),
        'o_router': nrm(ks[22], (N_ODD, d, N_EXPERTS), d ** -0.5),
        'o_exp_gate': nrm(ks[23], (N_ODD, N_EXPERTS, d, D_FF_EXPERT), d ** -0.5),
        'o_exp_up': nrm(ks[24], (N_ODD, N_EXPERTS, d, D_FF_EXPERT), d ** -0.5),
        'o_exp_down': nrm(ks[25], (N_ODD, N_EXPERTS, D_FF_EXPERT, d), beta * D_FF_EXPERT ** -0.5),
    }


def reference(x, c, ctx, c_ctx, w_mod, b_mod, ln_g, ln_b,
              e_w_in, e_conv, e_lam_q1, e_lam_k1, e_lam_q2, e_lam_k2, e_subln_g, e_w_o,
              e_ffn_gate, e_ffn_up, e_ffn_down,
              o_w_in, o_rpb, o_w_o, o_router, o_exp_gate, o_exp_up, o_exp_down):
    alpha = DEEPNORM_ALPHA
    x_lat, x_ctx = x, ctx
    for i in range(DEPTH):
        j = i // 2
        even = i % 2 == 0
        need_ctx = i < DEPTH - 1
        sh1, sc1, g1, sh2, sc2, g2 = adaln(c, w_mod[i], b_mod[i])
        csh1, csc1, cg1, csh2, csc2, cg2 = adaln(c_ctx, w_mod[i], b_mod[i])

        h_lat = modulate(x_lat, sh1, sc1)
        h_ctx = modulate(x_ctx, csh1, csc1)
        if even:
            lam_init = 0.8 - 0.6 * math.exp(-0.3 * i)
            f32 = jnp.float32
            lam = (jnp.exp(jnp.sum(e_lam_q1[j].astype(f32) * e_lam_k1[j].astype(f32)))
                   - jnp.exp(jnp.sum(e_lam_q2[j].astype(f32) * e_lam_k2[j].astype(f32))) + lam_init)
            y_lat, y_ctx = even_mixer(h_lat, h_ctx, e_w_in[j], e_conv[j], lam, e_subln_g[j], lam_init,
                                      e_w_o[j], need_ctx)
        else:
            y_lat, y_ctx = odd_mixer(h_lat, h_ctx, o_w_in[j], o_rpb[j], o_w_o[j], need_ctx)
        x_lat = layer_norm(alpha * x_lat + g1 * y_lat, ln_g[i, 0], ln_b[i, 0])

        h_lat = modulate(x_lat, sh2, sc2)
        if even:
            f_lat = swiglu(h_lat, e_ffn_gate[j], e_ffn_up[j], e_ffn_down[j])
        else:
            f_lat = moe_swiglu(h_lat, o_router[j], o_exp_gate[j], o_exp_up[j], o_exp_down[j])
        x_lat = layer_norm(alpha * x_lat + g2 * f_lat, ln_g[i, 1], ln_b[i, 1])

        if need_ctx:
            x_ctx = layer_norm(alpha * x_ctx + cg1 * y_ctx, ln_g[i, 0], ln_b[i, 0])
            h_ctx = modulate(x_ctx, csh2, csc2)
            if even:
                f_ctx = swiglu(h_ctx, e_ffn_gate[j], e_ffn_up[j], e_ffn_down[j])
            else:
                f_ctx = moe_swiglu(h_ctx, o_router[j], o_exp_gate[j], o_exp_up[j], o_exp_down[j])
            x_ctx = layer_norm(alpha * x_ctx + cg2 * f_ctx, ln_g[i, 1], ln_b[i, 1])
    return x_lat
```

```python
import functools
import math

import numpy as np
import jax
import jax.numpy as jnp
from jax import lax
from jax.experimental import pallas as pl
from jax.experimental.pallas import tpu as pltpu

F32 = jnp.float32
BF16 = jnp.bfloat16

D_MODEL = 1024
GRID_W = 64
DEPTH = 2

A_WIDTH = 512
DIFF_HEADS = 4
DIFF_DH = 64
DIFF_VDIM = 128
DIFF_QK = 512
B_WIDTH = 512

C_WIDTH = 512
C_GROUPS = 4
C_GROUP_DIM = 128
NA_HEADS = 8
NA_DH = 64
D_WIDTH = 512
NA_KR = 8
NA_KC = 16

N_EXPERTS = 8

ROPE_THETA = 10000.0
LN_EPS = 1e-5
RMS_EPS = 1e-5
NEG_INF = -1e30
DEEPNORM_ALPHA = (2 * DEPTH) ** 0.25
LOG2E = 1.4426950408889634

LANES = 128
MOD_ROWS = 32
NT_DIMS = (((1,), (1,)), ((), ()))


def _cparams(sem, vmem_mb=48):
    return pltpu.CompilerParams(dimension_semantics=sem, vmem_limit_bytes=vmem_mb * 1024 * 1024)


def _layer_norm(r, g, b):
    mu = jnp.mean(r, axis=-1, keepdims=True)
    d = r - mu
    var = jnp.mean(d * d, axis=-1, keepdims=True)
    return d * lax.rsqrt(var + LN_EPS) * g + b


def _adaln_body(c_ref, w_ref, b_ref, o_ref):
    cnd = c_ref[...]
    s = (cnd * jax.nn.sigmoid(cnd)).astype(BF16)
    o_ref[...] = jnp.dot(s, w_ref[...].astype(BF16), preferred_element_type=F32) + b_ref[...]


def _adaln(cond, w_mod, b_mod):
    depth, d, n = w_mod.shape
    tn = 1536
    return pl.pallas_call(
        _adaln_body,
        grid=(depth, n // tn),
        in_specs=[
            pl.BlockSpec((MOD_ROWS, d), lambda l, j: (0, 0)),
            pl.BlockSpec((None, d, tn), lambda l, j: (l, 0, j)),
            pl.BlockSpec((None, 1, tn), lambda l, j: (l, 0, j)),
        ],
        out_specs=pl.BlockSpec((None, MOD_ROWS, tn), lambda l, j: (l, 0, j)),
        out_shape=jax.ShapeDtypeStruct((depth, MOD_ROWS, n), F32),
        compiler_params=_cparams(("parallel", "parallel")),
        name="adaln",
    )(cond, w_mod, b_mod.reshape(depth, 1, n))


def _proj_body(*refs, chunks, with_vt, with_rope, q_scale):
    x_ref, sh_ref, sc_ref, w_ref = refs[:4]
    i = 4
    if with_vt:
        wvt_ref = refs[i]
        i += 1
    if with_rope:
        cos_ref, sin_ref = refs[i], refs[i + 1]
        i += 2
    y_ref = refs[i]
    vt_ref = refs[i + 1] if with_vt else None

    h = (x_ref[...] * (1.0 + sc_ref[...]) + sh_ref[...]).astype(BF16)
    tm = h.shape[0]
    if with_rope:
        cos = cos_ref[...]
        sin = sin_ref[...]
        lane = lax.broadcasted_iota(jnp.int32, (tm, LANES), 1)
        low_half = (lane % 64) < 32
    for (c0, width, kind) in chunks:
        acc = jnp.dot(h, w_ref[:, c0:c0 + width], preferred_element_type=F32)
        if kind in ("rope", "rope_scale"):
            for j in range(width // LANES):
                a = acc[:, LANES * j:LANES * (j + 1)]
                rot = jnp.where(low_half, pltpu.roll(a, 96, 1), pltpu.roll(a, 32, 1))
                r = a * cos + rot * sin
                if kind == "rope_scale":
                    r = r * q_scale
                y_ref[:, c0 + LANES * j:c0 + LANES * (j + 1)] = r.astype(BF16)
        elif kind == "scale":
            y_ref[:, c0:c0 + width] = (acc * q_scale).astype(BF16)
        else:
            y_ref[:, c0:c0 + width] = acc.astype(BF16)
    if with_vt:
        vt = lax.dot_general(wvt_ref[...], h, NT_DIMS, preferred_element_type=F32)
        vt_ref[...] = vt.astype(BF16)


def _proj(x, mod, shift_col, w, chunks, *, tm, wvt=None, rope=None, q_scale=1.0, name="proj"):
    b, l, d = x.shape
    n = w.shape[1]
    with_vt = wvt is not None
    with_rope = rope is not None
    in_specs = [
        pl.BlockSpec((None, tm, d), lambda bi, i: (bi, i, 0)),
        pl.BlockSpec((None, 1, d), lambda bi, i: (bi, 0, shift_col)),
        pl.BlockSpec((None, 1, d), lambda bi, i: (bi, 0, shift_col + 1)),
        pl.BlockSpec((d, n), lambda bi, i: (0, 0)),
    ]
    args = [x, mod, mod, w]
    if with_vt:
        nv = wvt.shape[0]
        in_specs.append(pl.BlockSpec((nv, d), lambda bi, i: (0, 0)))
        args.append(wvt)
    if with_rope:
        in_specs += [pl.BlockSpec((tm, LANES), lambda bi, i: (i, 0))] * 2
        args += [rope[0], rope[1]]
    out_specs = [pl.BlockSpec((None, tm, n), lambda bi, i: (bi, i, 0))]
    out_shape = [jax.ShapeDtypeStruct((b, l, n), BF16)]
    if with_vt:
        out_specs.append(pl.BlockSpec((None, nv, tm), lambda bi, i: (bi, 0, i)))
        out_shape.append(jax.ShapeDtypeStruct((b, nv, l), BF16))
    res = pl.pallas_call(
        functools.partial(_proj_body, chunks=tuple(chunks), with_vt=with_vt, with_rope=with_rope,
                          q_scale=q_scale),
        grid=(b, l // tm),
        in_specs=in_specs,
        out_specs=out_specs,
        out_shape=out_shape,
        compiler_params=_cparams(("parallel", "parallel")),
        name=name,
    )(*args)
    return res if with_vt else res[0]


def _diffattn_body(*refs, n_seg, lam_init):
    q_ref, lam_ref, g_ref = refs[:3]
    k_refs = [refs[3 + 2 * s] for s in range(n_seg)]
    vt_refs = [refs[4 + 2 * s] for s in range(n_seg)]
    o_ref = refs[3 + 2 * n_seg]

    q = q_ref[...]
    lane = lax.broadcasted_iota(jnp.int32, q.shape, 1)
    zero = jnp.zeros_like(q)
    q1 = jnp.where(lane < DIFF_DH, q, zero)
    q2 = jnp.where(lane >= DIFF_DH, q, zero)

    def softmax_parts(qm):
        s = [lax.dot_general(k_ref[...], qm, NT_DIMS, preferred_element_type=F32) for k_ref in k_refs]
        m = functools.reduce(jnp.maximum, [jnp.max(x, axis=0, keepdims=True) for x in s])
        e = [jnp.exp2(x - m) for x in s]
        tot = functools.reduce(jnp.add, [jnp.sum(x, axis=0, keepdims=True) for x in e])
        return e, tot

    e1, l1 = softmax_parts(q1)
    e2, l2 = softmax_parts(q2)

    lp = lam_ref[...]
    lam = (jnp.exp(jnp.sum(lp[0:1] * lp[1:2], axis=1, keepdims=True))
           - jnp.exp(jnp.sum(lp[2:3] * lp[3:4], axis=1, keepdims=True)) + lam_init)
    r1 = 1.0 / l1
    r2 = lam / l2
    o_t = None
    for s in range(n_seg):
        a = (e1[s] * r1 - e2[s] * r2).astype(BF16)
        t = jnp.dot(vt_refs[s][...], a, preferred_element_type=F32)
        o_t = t if o_t is None else o_t + t
    o = o_t.T
    ms = jnp.mean(o * o, axis=-1, keepdims=True)
    o_ref[...] = (o * lax.rsqrt(ms + RMS_EPS) * g_ref[...] * (1.0 - lam_init)).astype(BF16)


def _diff_attention(yq, segs, lam_pack, sub_g, lam_init, *, tq, q_col0, k_col0, name):
    b, lq, _ = yq.shape
    qb0 = q_col0 // LANES
    kb0 = k_col0 // LANES
    in_specs = [
        pl.BlockSpec((None, tq, LANES), lambda bi, h, i: (bi, i, qb0 + h)),
        pl.BlockSpec((8, LANES), lambda bi, h, i: (0, 0)),
        pl.BlockSpec((1, LANES), lambda bi, h, i: (0, 0)),
    ]
    args = [yq, lam_pack, sub_g]
    for (yk, vt) in segs:
        lk = yk.shape[1]
        in_specs.append(pl.BlockSpec((None, lk, LANES), lambda bi, h, i: (bi, 0, kb0 + h)))
        in_specs.append(pl.BlockSpec((None, DIFF_VDIM, lk), lambda bi, h, i: (bi, h, 0)))
        args += [yk, vt]
    return pl.pallas_call(
        functools.partial(_diffattn_body, n_seg=len(segs), lam_init=lam_init),
        grid=(b, DIFF_HEADS, lq // tq),
        in_specs=in_specs,
        out_specs=pl.BlockSpec((None, tq, LANES), lambda bi, h, i: (bi, i, h)),
        out_shape=jax.ShapeDtypeStruct((b, lq, B_WIDTH), BF16),
        compiler_params=_cparams(("parallel", "parallel", "arbitrary")),
        name=name,
    )(*args)


def _conv_body(bg_ref, cg_ref, val_ref, w_ref, o_ref):
    l = o_ref.shape[0]
    row = lax.broadcasted_iota(jnp.int32, (l, LANES), 0)
    for j in range(A_WIDTH // LANES):
        sl = slice(LANES * j, LANES * (j + 1))
        u = cg_ref[:, sl].astype(F32) * val_ref[:, sl].astype(F32)
        u_prev = jnp.where(row == 0, 0.0, pltpu.roll(u, 1, 0))
        u_next = jnp.where(row == l - 1, 0.0, pltpu.roll(u, l - 1, 0))
        w = w_ref[:, sl]
        y = bg_ref[:, sl].astype(F32) * (u_prev * w[0:1] + u * w[1:2] + u_next * w[2:3])
        o_ref[:, sl] = y.astype(BF16)


def _conv_mixer(y, w_conv, name):
    b, l, _ = y.shape
    return pl.pallas_call(
        _conv_body,
        grid=(b,),
        in_specs=[
            pl.BlockSpec((None, l, A_WIDTH), lambda bi: (bi, 0, 0)),
            pl.BlockSpec((None, l, A_WIDTH), lambda bi: (bi, 0, 1)),
            pl.BlockSpec((None, l, A_WIDTH), lambda bi: (bi, 0, 2)),
            pl.BlockSpec((3, A_WIDTH), lambda bi: (0, 0)),
        ],
        out_specs=pl.BlockSpec((None, l, A_WIDTH), lambda bi: (bi, 0, 0)),
        out_shape=jax.ShapeDtypeStruct((b, l, A_WIDTH), BF16),
        compiler_params=_cparams(("parallel",)),
        name=name,
    )(y, y, y, w_conv)


def _outproj_body(a_ref, b_ref, w_ref, x_ref, gate_ref, lng_ref, lnb_ref, o_ref):
    half = a_ref.shape[1]
    y = (jnp.dot(a_ref[...], w_ref[0:half, :], preferred_element_type=F32)
         + jnp.dot(b_ref[...], w_ref[half:2 * half, :], preferred_element_type=F32))
    r = DEEPNORM_ALPHA * x_ref[...] + gate_ref[...] * y
    o_ref[...] = _layer_norm(r, lng_ref[...], lnb_ref[...])


def _outproj_ln(a, a_colblk, bsrc, b_colblk, w_o, x, mod, gate_col, ln_g, ln_b, *, tm, name):
    b, l, d = x.shape
    half = w_o.shape[0] // 2
    return pl.pallas_call(
        _outproj_body,
        grid=(b, l // tm),
        in_specs=[
            pl.BlockSpec((None, tm, half), lambda bi, i: (bi, i, a_colblk)),
            pl.BlockSpec((None, tm, half), lambda bi, i: (bi, i, b_colblk)),
            pl.BlockSpec((2 * half, d), lambda bi, i: (0, 0)),
            pl.BlockSpec((None, tm, d), lambda bi, i: (bi, i, 0)),
            pl.BlockSpec((None, 1, d), lambda bi, i: (bi, 0, gate_col)),
            pl.BlockSpec((1, d), lambda bi, i: (0, 0)),
            pl.BlockSpec((1, d), lambda bi, i: (0, 0)),
        ],
        out_specs=pl.BlockSpec((None, tm, d), lambda bi, i: (bi, i, 0)),
        out_shape=jax.ShapeDtypeStruct((b, l, d), F32),
        compiler_params=_cparams(("parallel", "parallel")),
        name=name,
    )(a, bsrc, w_o, x, mod, ln_g, ln_b)


def _ffn_body(*refs, use_gates, n_e, n_f):
    if use_gates:
        (x_ref, sh_ref, sc_ref, gate_ref, rg_ref, wg_ref, wu_ref, wd_ref, lng_ref, lnb_ref,
         o_ref, h_scr, acc_scr) = refs
    else:
        (x_ref, sh_ref, sc_ref, gate_ref, wg_ref, wu_ref, wd_ref, lng_ref, lnb_ref,
         o_ref, h_scr, acc_scr) = refs
        rg_ref = None
    e = pl.program_id(2)
    f = pl.program_id(3)

    @pl.when((e == 0) & (f == 0))
    def _():
        h_scr[...] = (x_ref[...] * (1.0 + sc_ref[...]) + sh_ref[...]).astype(BF16)
        acc_scr[...] = jnp.zeros_like(acc_scr)

    h = h_scr[...]
    g = jnp.dot(h, wg_ref[...], preferred_element_type=F32)
    u = jnp.dot(h, wu_ref[...], preferred_element_type=F32)
    a = g * jax.nn.sigmoid(g) * u
    if use_gates:
        rg = rg_ref[...]
        lane = lax.broadcasted_iota(jnp.int32, rg.shape, 1)
        a = a * jnp.sum(jnp.where(lane == e, rg, 0.0), axis=-1, keepdims=True)
    acc_scr[...] += jnp.dot(a.astype(BF16), wd_ref[...], preferred_element_type=F32)

    @pl.when((e == n_e - 1) & (f == n_f - 1))
    def _():
        r = DEEPNORM_ALPHA * x_ref[...] + gate_ref[...] * acc_scr[...]
        o_ref[...] = _layer_norm(r, lng_ref[...], lnb_ref[...])


def _ffn_ln(x, mod, shift_col, wg, wu, wd, ln_g, ln_b, *, tm, tf, route_gates=None, name="ffn"):
    b, l, d = x.shape
    n_e, _, ff = wg.shape
    n_f = ff // tf
    use_gates = route_gates is not None
    in_specs = [
        pl.BlockSpec((None, tm, d), lambda bi, i, e, f: (bi, i, 0)),
        pl.BlockSpec((None, 1, d), lambda bi, i, e, f: (bi, 0, shift_col)),
        pl.BlockSpec((None, 1, d), lambda bi, i, e, f: (bi, 0, shift_col + 1)),
        pl.BlockSpec((None, 1, d), lambda bi, i, e, f: (bi, 0, shift_col + 2)),
    ]
    args = [x, mod, mod, mod]
    if use_gates:
        in_specs.append(pl.BlockSpec((None, tm, LANES), lambda bi, i, e, f: (bi, i, 0)))
        args.append(route_gates)
    in_specs += [
        pl.BlockSpec((None, d, tf), lambda bi, i, e, f: (e, 0, f)),
        pl.BlockSpec((None, d, tf), lambda bi, i, e, f: (e, 0, f)),
        pl.BlockSpec((None, tf, d), lambda bi, i, e, f: (e, f, 0)),
        pl.BlockSpec((1, d), lambda bi, i, e, f: (0, 0)),
        pl.BlockSpec((1, d), lambda bi, i, e, f: (0, 0)),
    ]
    args += [wg, wu, wd, ln_g, ln_b]
    return pl.pallas_call(
        functools.partial(_ffn_body, use_gates=use_gates, n_e=n_e, n_f=n_f),
        grid=(b, l // tm, n_e, n_f),
        in_specs=in_specs,
        out_specs=pl.BlockSpec((None, tm, d), lambda bi, i, e, f: (bi, i, 0)),
        out_shape=jax.ShapeDtypeStruct((b, l, d), F32),
        scratch_shapes=[pltpu.VMEM((tm, d), BF16), pltpu.VMEM((tm, d), F32)],
        compiler_params=_cparams(("parallel", "parallel", "arbitrary", "arbitrary")),
        name=name,
    )(*args)


def _router_body(x_ref, sh_ref, sc_ref, wr_ref, o_ref):
    h = (x_ref[...] * (1.0 + sc_ref[...]) + sh_ref[...]).astype(BF16)
    logits = jnp.dot(h, wr_ref[...], preferred_element_type=F32)
    lane = lax.broadcasted_iota(jnp.int32, logits.shape, 1).astype(F32)
    l1 = jnp.where(lane < N_EXPERTS, logits, -jnp.inf)
    v1 = jnp.max(l1, axis=-1, keepdims=True)
    i1 = jnp.min(jnp.where(l1 == v1, lane, float(LANES)), axis=-1, keepdims=True)
    l2 = jnp.where(lane == i1, -jnp.inf, l1)
    v2 = jnp.max(l2, axis=-1, keepdims=True)
    i2 = jnp.min(jnp.where(l2 == v2, lane, float(LANES)), axis=-1, keepdims=True)
    t = jnp.exp(v2 - v1)
    p1 = 1.0 / (1.0 + t)
    p2 = t / (1.0 + t)
    o_ref[...] = jnp.where(lane == i1, p1, jnp.where(lane == i2, p2, 0.0))


def _router(x, mod, shift_col, w_router_pad, *, tm):
    b, l, d = x.shape
    return pl.pallas_call(
        _router_body,
        grid=(b, l // tm),
        in_specs=[
            pl.BlockSpec((None, tm, d), lambda bi, i: (bi, i, 0)),
            pl.BlockSpec((None, 1, d), lambda bi, i: (bi, 0, shift_col)),
            pl.BlockSpec((None, 1, d), lambda bi, i: (bi, 0, shift_col + 1)),
            pl.BlockSpec((d, LANES), lambda bi, i: (0, 0)),
        ],
        out_specs=pl.BlockSpec((None, tm, LANES), lambda bi, i: (bi, i, 0)),
        out_shape=jax.ShapeDtypeStruct((b, l, LANES), F32),
        compiler_params=_cparams(("parallel", "parallel")),
        name="router",
    )(x, mod, mod, w_router_pad)


def _fourier_body(pc_ref, dl_ref, dc_ref, o_ref, t_scr, *, out_scale, row_chunk):
    l = pc_ref.shape[0]
    for g in range(C_GROUPS):
        sl = slice(C_GROUP_DIM * g, C_GROUP_DIM * (g + 1))
        xg = pc_ref[:, sl].astype(F32)
        mu = jnp.mean(xg, axis=-1, keepdims=True)
        dlt = xg - mu
        var = jnp.mean(dlt * dlt, axis=-1, keepdims=True)
        gn = (dlt * lax.rsqrt(var + LN_EPS)).astype(BF16)
        t = jnp.dot(gn, dc_ref[...], preferred_element_type=F32)
        t_scr[0:l, sl] = t[:, 0:C_GROUP_DIM].astype(BF16)
        t_scr[l:2 * l, sl] = t[:, C_GROUP_DIM:2 * C_GROUP_DIM].astype(BF16)
    for r0 in range(0, l, row_chunk):
        acc = jnp.dot(dl_ref[r0:r0 + row_chunk, :], t_scr[...], preferred_element_type=F32)
        o_ref[r0:r0 + row_chunk, :] = (acc * out_scale).astype(BF16)


def _dft_matrices(l, c):
    j = np.arange(l, dtype=np.int64)
    ang_l = (2.0 * np.pi / l) * ((j[:, None] * j[None, :]) % l)
    dl = np.concatenate([np.cos(ang_l), -np.sin(ang_l)], axis=1)
    m = np.arange(c, dtype=np.int64)
    ang_c = (2.0 * np.pi / c) * ((m[:, None] * m[None, :]) % c)
    dc = np.concatenate([np.cos(ang_c), np.sin(ang_c)], axis=1)
    return dl.astype(np.float32), dc.astype(np.float32)


def _fourier_mixer(y, name="fourier"):
    b, l, _ = y.shape
    dl_np, dc_np = _dft_matrices(l, C_GROUP_DIM)
    dl = jnp.asarray(dl_np, dtype=BF16)
    dc = jnp.asarray(dc_np, dtype=BF16)
    out_scale = 1.0 / math.sqrt(l * C_GROUP_DIM)
    return pl.pallas_call(
        functools.partial(_fourier_body, out_scale=out_scale, row_chunk=min(l, 512)),
        grid=(b,),
        in_specs=[
            pl.BlockSpec((None, l, C_WIDTH), lambda bi: (bi, 0, 0)),
            pl.BlockSpec((l, 2 * l), lambda bi: (0, 0), pipeline_mode=pl.Buffered(1)),
            pl.BlockSpec((C_GROUP_DIM, 2 * C_GROUP_DIM), lambda bi: (0, 0)),
        ],
        out_specs=pl.BlockSpec((None, l, C_WIDTH), lambda bi: (bi, 0, 0)),
        out_shape=jax.ShapeDtypeStruct((b, l, C_WIDTH), BF16),
        scratch_shapes=[pltpu.VMEM((2 * l, C_WIDTH), BF16)],
        compiler_params=_cparams(("parallel",)),
        name=name,
    )(y, dl, dc)


NA_HG = 4
NA_GW = NA_HG * NA_DH
NA_WIN = NA_KR * GRID_W


def _natten_body(q_ref, k_ref, v_ref, kc_ref, vc_ref, bias_ref, o_ref, *, rows_per_step, n_rows):
    rb = pl.program_id(2)
    hq = NA_HG * GRID_W
    rid = lax.broadcasted_iota(jnp.int32, (hq, NA_GW), 0)
    cid = lax.broadcasted_iota(jnp.int32, (hq, NA_GW), 1)
    diag = (rid // GRID_W) == (cid // NA_DH)
    cid_o = lax.broadcasted_iota(jnp.int32, (GRID_W, NA_GW), 1)
    kc = kc_ref[...]
    vc = vc_ref[...]
    for j in range(rows_per_step):
        r = rb * rows_per_step + j
        rs = jnp.clip(r - NA_KR // 2, 0, n_rows - NA_KR)
        start = pl.multiple_of(rs * GRID_W, GRID_W)
        tid = jnp.minimum(r, NA_KR // 2) + jnp.maximum(r - (n_rows - NA_KR // 2), 0)
        q_r = q_ref[GRID_W * j:GRID_W * (j + 1), :]
        q4 = jnp.concatenate([q_r] * NA_HG, axis=0)
        qbd = jnp.where(diag, q4, jnp.zeros_like(q4))
        kw = k_ref[pl.ds(start, NA_WIN), :]
        vw = v_ref[pl.ds(start, NA_WIN), :]
        s_loc = lax.dot_general(qbd, kw, NT_DIMS, preferred_element_type=F32) + bias_ref[tid]
        s_ctx = lax.dot_general(qbd, kc, NT_DIMS, preferred_element_type=F32)
        m = jnp.maximum(jnp.max(s_loc, axis=-1, keepdims=True), jnp.max(s_ctx, axis=-1, keepdims=True))
        e_loc = jnp.exp2(s_loc - m)
        e_ctx = jnp.exp2(s_ctx - m)
        tot = jnp.sum(e_loc, axis=-1, keepdims=True) + jnp.sum(e_ctx, axis=-1, keepdims=True)
        o = (jnp.dot(e_loc.astype(BF16), vw, preferred_element_type=F32)
             + jnp.dot(e_ctx.astype(BF16), vc, preferred_element_type=F32))
        o = o * (1.0 / tot)
        out = jnp.zeros((GRID_W, NA_GW), F32)
        for hh in range(NA_HG):
            out = out + jnp.where((cid_o // NA_DH) == hh, o[GRID_W * hh:GRID_W * (hh + 1), :], 0.0)
        o_ref[GRID_W * j:GRID_W * (j + 1), :] = out.astype(BF16)


def _na_bias_table(rpb, n_rows):
    h = rpb.shape[0]
    cols = jnp.arange(GRID_W)
    col_start = jnp.clip(cols - NA_KC // 2, 0, GRID_W - NA_KC)
    col_valid = (cols[None, :] >= col_start[:, None]) & (cols[None, :] < col_start[:, None] + NA_KC)
    dc_idx = jnp.clip(cols[None, :] - cols[:, None] + NA_KC - 1, 0, 2 * NA_KC - 2)
    rpb_c = rpb[:, :, dc_idx].astype(F32)
    half = NA_KR // 2
    rep_rows = list(range(half)) + [half] + list(range(n_rows - half + 1, n_rows))
    tabs = []
    for r in rep_rows:
        rs = min(max(r - half, 0), n_rows - NA_KR)
        dr_idx = rs + np.arange(NA_KR) - r + NA_KR - 1
        bias = rpb_c[:, dr_idx].transpose(0, 2, 1, 3)
        bias = jnp.where(col_valid[None, :, None, :], bias * LOG2E, NEG_INF)
        tabs.append(bias.reshape(h, GRID_W, NA_WIN))
    tab = jnp.stack(tabs, axis=0)
    return tab.reshape(len(rep_rows), h // NA_HG, NA_HG * GRID_W, NA_WIN)


def _natten(y, q_col0, k_col0, v_col0, y_ctx, kc_col0, vc_col0, bias_tab, *, rows_per_step, name="natten"):
    b, l, _ = y.shape
    lc = y_ctx.shape[1]
    n_rows = l // GRID_W
    n_tab = bias_tab.shape[0]
    n_grp = NA_HEADS // NA_HG
    tq = rows_per_step * GRID_W
    qb, kb, vb = q_col0 // NA_GW, k_col0 // NA_GW, v_col0 // NA_GW
    kcb, vcb = kc_col0 // NA_GW, vc_col0 // NA_GW
    return pl.pallas_call(
        functools.partial(_natten_body, rows_per_step=rows_per_step, n_rows=n_rows),
        grid=(b, n_grp, n_rows // rows_per_step),
        in_specs=[
            pl.BlockSpec((None, tq, NA_GW), lambda bi, g, i: (bi, i, qb + g)),
            pl.BlockSpec((None, l, NA_GW), lambda bi, g, i: (bi, 0, kb + g)),
            pl.BlockSpec((None, l, NA_GW), lambda bi, g, i: (bi, 0, vb + g)),
            pl.BlockSpec((None, lc, NA_GW), lambda bi, g, i: (bi, 0, kcb + g)),
            pl.BlockSpec((None, lc, NA_GW), lambda bi, g, i: (bi, 0, vcb + g)),
            pl.BlockSpec((n_tab, None, NA_HG * GRID_W, NA_WIN), lambda bi, g, i: (0, g, 0, 0)),
        ],
        out_specs=pl.BlockSpec((None, tq, NA_GW), lambda bi, g, i: (bi, i, g)),
        out_shape=jax.ShapeDtypeStruct((b, l, D_WIDTH), BF16),
        compiler_params=_cparams(("parallel", "parallel", "arbitrary")),
        name=name,
    )(y, y, y, y_ctx, y_ctx, bias_tab)


def _rope_tables(l):
    t = jnp.arange(l, dtype=jnp.int32)
    row = (t // GRID_W).astype(F32)
    col = (t % GRID_W).astype(F32)
    n_freq = DIFF_DH // 4
    inv_freq = ROPE_THETA ** (-jnp.arange(n_freq, dtype=F32) / n_freq)
    ang = jnp.concatenate([row[:, None] * inv_freq, col[:, None] * inv_freq], axis=-1)
    c, s = jnp.cos(ang), jnp.sin(ang)
    cos = jnp.tile(jnp.concatenate([c, c], axis=-1), (1, 2))
    sin = jnp.tile(jnp.concatenate([-s, s], axis=-1), (1, 2))
    return cos, sin


def kernel(x, c, ctx, c_ctx, w_mod, b_mod, ln_g, ln_b, e_w_in, e_conv, e_lam_q1, e_lam_k1, e_lam_q2, e_lam_k2, e_subln_g, e_w_o, e_ffn_gate, e_ffn_up, e_ffn_down, o_w_in, o_rpb, o_w_o, o_router, o_exp_gate, o_exp_up, o_exp_down):
    b, l, d = x.shape
    lc = ctx.shape[1]
    assert d == D_MODEL and l % 512 == 0 and lc % 256 == 0 and b + 1 <= MOD_ROWS

    cond = jnp.concatenate([c, c_ctx[None, :], jnp.zeros((MOD_ROWS - b - 1, d), F32)], axis=0)
    mods = _adaln(cond, w_mod, b_mod)

    def layer_mods(i):
        lat = mods[i, :b][:, None, :]
        cx = jnp.broadcast_to(mods[i, b][None, None, :], (b, 1, 6 * d))
        return lat, cx

    q_scale_diff = DIFF_DH ** -0.5 * LOG2E
    q_scale_na = NA_DH ** -0.5 * LOG2E

    mod_lat, mod_ctx = layer_mods(0)
    lam_init = 0.8 - 0.6 * math.exp(-0.3 * 0)
    a_end = 3 * A_WIDTH
    w_in = e_w_in[0]
    w_main = w_in[:, :a_end + 2 * DIFF_QK].astype(BF16)
    w_vt = w_in[:, a_end + 2 * DIFF_QK:].T.astype(BF16)
    rope = _rope_tables(l)
    pa_chunks = [(0, 512, "plain"), (512, 512, "plain"), (1024, 512, "plain")]
    y_lat, vt_lat = _proj(x, mod_lat, 0, w_main,
                          pa_chunks + [(a_end, 512, "rope_scale"), (a_end + 512, 512, "rope")],
                          tm=512, wvt=w_vt, rope=rope, q_scale=q_scale_diff, name="even_inproj_lat")
    y_ctx, vt_ctx = _proj(ctx, mod_ctx, 0, w_main,
                          pa_chunks + [(a_end, 512, "scale"), (a_end + 512, 512, "plain")],
                          tm=lc, wvt=w_vt, q_scale=q_scale_diff, name="even_inproj_ctx")

    lam_pack = jnp.zeros((8, LANES), F32)
    lam_pack = lam_pack.at[0, :DIFF_DH].set(e_lam_q1[0]).at[1, :DIFF_DH].set(e_lam_k1[0])
    lam_pack = lam_pack.at[2, :DIFF_DH].set(e_lam_q2[0]).at[3, :DIFF_DH].set(e_lam_k2[0])
    sub_g = e_subln_g[0].reshape(1, DIFF_VDIM)
    o_lat = _diff_attention(y_lat, [(y_ctx, vt_ctx), (y_lat, vt_lat)], lam_pack, sub_g, lam_init,
                            tq=256, q_col0=a_end, k_col0=a_end + DIFF_QK, name="diffattn_lat")
    o_ctx = _diff_attention(y_ctx, [(y_ctx, vt_ctx)], lam_pack, sub_g, lam_init,
                            tq=lc, q_col0=a_end, k_col0=a_end + DIFF_QK, name="diffattn_ctx")
    cv_lat = _conv_mixer(y_lat, e_conv[0], "conv_lat")
    cv_ctx = _conv_mixer(y_ctx, e_conv[0], "conv_ctx")

    w_o = e_w_o[0].astype(BF16)
    lng0, lnb0 = ln_g[0, 0][None, :], ln_b[0, 0][None, :]
    lng1, lnb1 = ln_g[0, 1][None, :], ln_b[0, 1][None, :]
    x_lat = _outproj_ln(cv_lat, 0, o_lat, 0, w_o, x, mod_lat, 2, lng0, lnb0, tm=512, name="even_outproj_lat")
    x_ctx = _outproj_ln(cv_ctx, 0, o_ctx, 0, w_o, ctx, mod_ctx, 2, lng0, lnb0, tm=lc, name="even_outproj_ctx")

    wg = e_ffn_gate[0].astype(BF16)[None]
    wu = e_ffn_up[0].astype(BF16)[None]
    wd = e_ffn_down[0].astype(BF16)[None]
    x_lat = _ffn_ln(x_lat, mod_lat, 3, wg, wu, wd, lng1, lnb1, tm=512, tf=1408, name="ffn_lat")
    x_ctx = _ffn_ln(x_ctx, mod_ctx, 3, wg, wu, wd, lng1, lnb1, tm=lc, tf=1408, name="ffn_ctx")

    mod_lat, mod_ctx = layer_mods(1)
    w_in = o_w_in[0].astype(BF16)
    y_lat = _proj(x_lat, mod_lat, 0, w_in,
                  [(0, 512, "plain"), (512, 512, "scale"), (1024, 512, "plain"), (1536, 512, "plain")],
                  tm=512, q_scale=q_scale_na, name="odd_inproj_lat")
    y_ctx = _proj(x_ctx, mod_ctx, 0, w_in[:, C_WIDTH + D_WIDTH:],
                  [(0, 512, "plain"), (512, 512, "plain")], tm=lc, name="odd_inproj_ctx")
    f_lat = _fourier_mixer(y_lat)
    bias_tab = _na_bias_table(o_rpb[0], l // GRID_W)
    n_lat = _natten(y_lat, C_WIDTH, C_WIDTH + D_WIDTH, C_WIDTH + 2 * D_WIDTH, y_ctx, 0, D_WIDTH,
                    bias_tab, rows_per_step=4)

    w_o = o_w_o[0].astype(BF16)
    lng0, lnb0 = ln_g[1, 0][None, :], ln_b[1, 0][None, :]
    lng1, lnb1 = ln_g[1, 1][None, :], ln_b[1, 1][None, :]
    x_lat = _outproj_ln(f_lat, 0, n_lat, 0, w_o, x_lat, mod_lat, 2, lng0, lnb0, tm=512, name="odd_outproj_lat")

    w_r = jnp.zeros((d, LANES), BF16).at[:, :N_EXPERTS].set(o_router[0].astype(BF16))
    route_gates = _router(x_lat, mod_lat, 3, w_r, tm=512)
    x_lat = _ffn_ln(x_lat, mod_lat, 3, o_exp_gate[0].astype(BF16), o_exp_up[0].astype(BF16),
                    o_exp_down[0].astype(BF16), lng1, lnb1, tm=1024, tf=512,
                    route_gates=route_gates, name="moe")
    return x_lat
```

```python
import functools
import math

import numpy as np
import jax
import jax.numpy as jnp
from jax import lax
from jax.experimental import pallas as pl
from jax.experimental.pallas import tpu as pltpu

F32 = jnp.float32
BF16 = jnp.bfloat16

D_MODEL = 1024
GRID_W = 64
DEPTH = 2

A_WIDTH = 512
DIFF_HEADS = 4
DIFF_DH = 64
DIFF_VDIM = 128
DIFF_QK = 512
B_WIDTH = 512

C_WIDTH = 512
C_GROUPS = 4
C_GROUP_DIM = 128
NA_HEADS = 8
NA_DH = 64
D_WIDTH = 512
NA_KR = 8
NA_KC = 16

N_EXPERTS = 8

ROPE_THETA = 10000.0
LN_EPS = 1e-5
RMS_EPS = 1e-5
NEG_INF = -1e30
DEEPNORM_ALPHA = (2 * DEPTH) ** 0.25
LOG2E = 1.4426950408889634

LANES = 128
MOD_ROWS = 32
NT_DIMS = (((1,), (1,)), ((), ()))


def _cparams(sem, vmem_mb=48):
    return pltpu.CompilerParams(dimension_semantics=sem, vmem_limit_bytes=vmem_mb * 1024 * 1024)


def _layer_norm(r, g, b):
    mu = jnp.mean(r, axis=-1, keepdims=True)
    d = r - mu
    var = jnp.mean(d * d, axis=-1, keepdims=True)
    return d * lax.rsqrt(var + LN_EPS) * g + b


def _adaln_body(c_ref, w_ref, b_ref, o_ref):
    cnd = c_ref[...]
    s = (cnd * jax.nn.sigmoid(cnd)).astype(BF16)
    o_ref[...] = jnp.dot(s, w_ref[...].astype(BF16), preferred_element_type=F32) + b_ref[...]


def _adaln(cond, w_mod, b_mod):
    depth, d, n = w_mod.shape
    tn = 1536
    return pl.pallas_call(
        _adaln_body,
        grid=(depth, n // tn),
        in_specs=[
            pl.BlockSpec((MOD_ROWS, d), lambda l, j: (0, 0)),
            pl.BlockSpec((None, d, tn), lambda l, j: (l, 0, j)),
            pl.BlockSpec((None, 1, tn), lambda l, j: (l, 0, j)),
        ],
        out_specs=pl.BlockSpec((None, MOD_ROWS, tn), lambda l, j: (l, 0, j)),
        out_shape=jax.ShapeDtypeStruct((depth, MOD_ROWS, n), F32),
        compiler_params=_cparams(("parallel", "parallel")),
        name="adaln",
    )(cond, w_mod, b_mod.reshape(depth, 1, n))


def _proj_body(*refs, chunks, with_vt, with_rope, q_scale):
    x_ref, sh_ref, sc_ref, w_ref = refs[:4]
    i = 4
    if with_vt:
        wvt_ref = refs[i]
        i += 1
    if with_rope:
        cos_ref, sin_ref = refs[i], refs[i + 1]
        i += 2
    y_ref = refs[i]
    vt_ref = refs[i + 1] if with_vt else None

    h = (x_ref[...] * (1.0 + sc_ref[...]) + sh_ref[...]).astype(BF16)
    tm = h.shape[0]
    if with_rope:
        cos = cos_ref[...]
        sin = sin_ref[...]
        lane = lax.broadcasted_iota(jnp.int32, (tm, LANES), 1)
        low_half = (lane % 64) < 32
    for (c0, width, kind) in chunks:
        acc = jnp.dot(h, w_ref[:, c0:c0 + width], preferred_element_type=F32)
        if kind in ("rope", "rope_scale"):
            for j in range(width // LANES):
                a = acc[:, LANES * j:LANES * (j + 1)]
                rot = jnp.where(low_half, pltpu.roll(a, 96, 1), pltpu.roll(a, 32, 1))
                r = a * cos + rot * sin
                if kind == "rope_scale":
                    r = r * q_scale
                y_ref[:, c0 + LANES * j:c0 + LANES * (j + 1)] = r.astype(BF16)
        elif kind == "scale":
            y_ref[:, c0:c0 + width] = (acc * q_scale).astype(BF16)
        else:
            y_ref[:, c0:c0 + width] = acc.astype(BF16)
    if with_vt:
        vt = lax.dot_general(wvt_ref[...], h, NT_DIMS, preferred_element_type=F32)
        vt_ref[...] = vt.astype(BF16)


def _proj(x, mod, shift_col, w, chunks, *, tm, wvt=None, rope=None, q_scale=1.0, name="proj"):
    b, l, d = x.shape
    n = w.shape[1]
    with_vt = wvt is not None
    with_rope = rope is not None
    in_specs = [
        pl.BlockSpec((None, tm, d), lambda bi, i: (bi, i, 0)),
        pl.BlockSpec((None, 1, d), lambda bi, i: (bi, 0, shift_col)),
        pl.BlockSpec((None, 1, d), lambda bi, i: (bi, 0, shift_col + 1)),
        pl.BlockSpec((d, n), lambda bi, i: (0, 0)),
    ]
    args = [x, mod, mod, w]
    if with_vt:
        nv = wvt.shape[0]
        in_specs.append(pl.BlockSpec((nv, d), lambda bi, i: (0, 0)))
        args.append(wvt)
    if with_rope:
        in_specs += [pl.BlockSpec((tm, LANES), lambda bi, i: (i, 0))] * 2
        args += [rope[0], rope[1]]
    out_specs = [pl.BlockSpec((None, tm, n), lambda bi, i: (bi, i, 0))]
    out_shape = [jax.ShapeDtypeStruct((b, l, n), BF16)]
    if with_vt:
        out_specs.append(pl.BlockSpec((None, nv, tm), lambda bi, i: (bi, 0, i)))
        out_shape.append(jax.ShapeDtypeStruct((b, nv, l), BF16))
    res = pl.pallas_call(
        functools.partial(_proj_body, chunks=tuple(chunks), with_vt=with_vt, with_rope=with_rope,
                          q_scale=q_scale),
        grid=(b, l // tm),
        in_specs=in_specs,
        out_specs=out_specs,
        out_shape=out_shape,
        compiler_params=_cparams(("parallel", "parallel")),
        name=name,
    )(*args)
    return res if with_vt else res[0]


def _diffattn_body(*refs, n_seg, lam_init):
    q_ref, lam_ref, g_ref = refs[:3]
    k_refs = [refs[3 + 2 * s] for s in range(n_seg)]
    vt_refs = [refs[4 + 2 * s] for s in range(n_seg)]
    o_ref = refs[3 + 2 * n_seg]

    q = q_ref[...]
    lane = lax.broadcasted_iota(jnp.int32, q.shape, 1)
    zero = jnp.zeros_like(q)
    q1 = jnp.where(lane < DIFF_DH, q, zero)
    q2 = jnp.where(lane >= DIFF_DH, q, zero)

    def softmax_parts(qm):
        s = [lax.dot_general(k_ref[...], qm, NT_DIMS, preferred_element_type=F32) for k_ref in k_refs]
        m = functools.reduce(jnp.maximum, [jnp.max(x, axis=0, keepdims=True) for x in s])
        e = [jnp.exp2(x - m) for x in s]
        tot = functools.reduce(jnp.add, [jnp.sum(x, axis=0, keepdims=True) for x in e])
        return e, tot

    e1, l1 = softmax_parts(q1)
    e2, l2 = softmax_parts(q2)

    lp = lam_ref[...]
    lam = (jnp.exp(jnp.sum(lp[0:1] * lp[1:2], axis=1, keepdims=True))
           - jnp.exp(jnp.sum(lp[2:3] * lp[3:4], axis=1, keepdims=True)) + lam_init)
    r1 = 1.0 / l1
    r2 = lam / l2
    o_t = None
    for s in range(n_seg):
        a = (e1[s] * r1 - e2[s] * r2).astype(BF16)
        t = jnp.dot(vt_refs[s][...], a, preferred_element_type=F32)
        o_t = t if o_t is None else o_t + t
    o = o_t.T
    ms = jnp.mean(o * o, axis=-1, keepdims=True)
    o_ref[...] = (o * lax.rsqrt(ms + RMS_EPS) * g_ref[...] * (1.0 - lam_init)).astype(BF16)


def _diff_attention(yq, segs, lam_pack, sub_g, lam_init, *, tq, q_col0, k_col0, name):
    b, lq, _ = yq.shape
    qb0 = q_col0 // LANES
    kb0 = k_col0 // LANES
    in_specs = [
        pl.BlockSpec((None, tq, LANES), lambda bi, h, i: (bi, i, qb0 + h)),
        pl.BlockSpec((8, LANES), lambda bi, h, i: (0, 0)),
        pl.BlockSpec((1, LANES), lambda bi, h, i: (0, 0)),
    ]
    args = [yq, lam_pack, sub_g]
    for (yk, vt) in segs:
        lk = yk.shape[1]
        in_specs.append(pl.BlockSpec((None, lk, LANES), lambda bi, h, i: (bi, 0, kb0 + h)))
        in_specs.append(pl.BlockSpec((None, DIFF_VDIM, lk), lambda bi, h, i: (bi, h, 0)))
        args += [yk, vt]
    return pl.pallas_call(
        functools.partial(_diffattn_body, n_seg=len(segs), lam_init=lam_init),
        grid=(b, DIFF_HEADS, lq // tq),
        in_specs=in_specs,
        out_specs=pl.BlockSpec((None, tq, LANES), lambda bi, h, i: (bi, i, h)),
        out_shape=jax.ShapeDtypeStruct((b, lq, B_WIDTH), BF16),
        compiler_params=_cparams(("parallel", "parallel", "arbitrary")),
        name=name,
    )(*args)


def _conv_body(bg_ref, cg_ref, val_ref, w_ref, o_ref):
    l = o_ref.shape[0]
    row = lax.broadcasted_iota(jnp.int32, (l, LANES), 0)
    for j in range(A_WIDTH // LANES):
        sl = slice(LANES * j, LANES * (j + 1))
        u = cg_ref[:, sl].astype(F32) * val_ref[:, sl].astype(F32)
        u_prev = jnp.where(row == 0, 0.0, pltpu.roll(u, 1, 0))
        u_next = jnp.where(row == l - 1, 0.0, pltpu.roll(u, l - 1, 0))
        w = w_ref[:, sl]
        y = bg_ref[:, sl].astype(F32) * (u_prev * w[0:1] + u * w[1:2] + u_next * w[2:3])
        o_ref[:, sl] = y.astype(BF16)


def _conv_mixer(y, w_conv, name):
    b, l, _ = y.shape
    return pl.pallas_call(
        _conv_body,
        grid=(b,),
        in_specs=[
            pl.BlockSpec((None, l, A_WIDTH), lambda bi: (bi, 0, 0)),
            pl.BlockSpec((None, l, A_WIDTH), lambda bi: (bi, 0, 1)),
            pl.BlockSpec((None, l, A_WIDTH), lambda bi: (bi, 0, 2)),
            pl.BlockSpec((3, A_WIDTH), lambda bi: (0, 0)),
        ],
        out_specs=pl.BlockSpec((None, l, A_WIDTH), lambda bi: (bi, 0, 0)),
        out_shape=jax.ShapeDtypeStruct((b, l, A_WIDTH), BF16),
        compiler_params=_cparams(("parallel",)),
        name=name,
    )(y, y, y, w_conv)


def _outproj_body(a_ref, b_ref, w_ref, x_ref, gate_ref, lng_ref, lnb_ref, o_ref):
    half = a_ref.shape[1]
    y = (jnp.dot(a_ref[...], w_ref[0:half, :], preferred_element_type=F32)
         + jnp.dot(b_ref[...], w_ref[half:2 * half, :], preferred_element_type=F32))
    r = DEEPNORM_ALPHA * x_ref[...] + gate_ref[...] * y
    o_ref[...] = _layer_norm(r, lng_ref[...], lnb_ref[...])


def _outproj_ln(a, a_colblk, bsrc, b_colblk, w_o, x, mod, gate_col, ln_g, ln_b, *, tm, name):
    b, l, d = x.shape
    half = w_o.shape[0] // 2
    return pl.pallas_call(
        _outproj_body,
        grid=(b, l // tm),
        in_specs=[
            pl.BlockSpec((None, tm, half), lambda bi, i: (bi, i, a_colblk)),
            pl.BlockSpec((None, tm, half), lambda bi, i: (bi, i, b_colblk)),
            pl.BlockSpec((2 * half, d), lambda bi, i: (0, 0)),
            pl.BlockSpec((None, tm, d), lambda bi, i: (bi, i, 0)),
            pl.BlockSpec((None, 1, d), lambda bi, i: (bi, 0, gate_col)),
            pl.BlockSpec((1, d), lambda bi, i: (0, 0)),
            pl.BlockSpec((1, d), lambda bi, i: (0, 0)),
        ],
        out_specs=pl.BlockSpec((None, tm, d), lambda bi, i: (bi, i, 0)),
        out_shape=jax.ShapeDtypeStruct((b, l, d), F32),
        compiler_params=_cparams(("parallel", "parallel")),
        name=name,
    )(a, bsrc, w_o, x, mod, ln_g, ln_b)


def _ffn_body(x_ref, sh_ref, sc_ref, gate_ref, wg_ref, wu_ref, wd_ref, lng_ref, lnb_ref,
              o_ref, h_scr, acc_scr, *, n_f):
    f = pl.program_id(2)

    @pl.when(f == 0)
    def _():
        h_scr[...] = (x_ref[...] * (1.0 + sc_ref[...]) + sh_ref[...]).astype(BF16)
        acc_scr[...] = jnp.zeros_like(acc_scr)

    h = h_scr[...]
    g = jnp.dot(h, wg_ref[...], preferred_element_type=F32)
    u = jnp.dot(h, wu_ref[...], preferred_element_type=F32)
    a = (g * jax.nn.sigmoid(g) * u).astype(BF16)
    acc_scr[...] += jnp.dot(a, wd_ref[...], preferred_element_type=F32)

    @pl.when(f == n_f - 1)
    def _():
        r = DEEPNORM_ALPHA * x_ref[...] + gate_ref[...] * acc_scr[...]
        o_ref[...] = _layer_norm(r, lng_ref[...], lnb_ref[...])


def _ffn_ln(x, mod, shift_col, wg, wu, wd, ln_g, ln_b, *, tm, tf, name="ffn"):
    b, l, d = x.shape
    ff = wg.shape[1]
    n_f = ff // tf
    return pl.pallas_call(
        functools.partial(_ffn_body, n_f=n_f),
        grid=(b, l // tm, n_f),
        in_specs=[
            pl.BlockSpec((None, tm, d), lambda bi, i, f: (bi, i, 0)),
            pl.BlockSpec((None, 1, d), lambda bi, i, f: (bi, 0, shift_col)),
            pl.BlockSpec((None, 1, d), lambda bi, i, f: (bi, 0, shift_col + 1)),
            pl.BlockSpec((None, 1, d), lambda bi, i, f: (bi, 0, shift_col + 2)),
            pl.BlockSpec((d, tf), lambda bi, i, f: (0, f)),
            pl.BlockSpec((d, tf), lambda bi, i, f: (0, f)),
            pl.BlockSpec((tf, d), lambda bi, i, f: (f, 0)),
            pl.BlockSpec((1, d), lambda bi, i, f: (0, 0)),
            pl.BlockSpec((1, d), lambda bi, i, f: (0, 0)),
        ],
        out_specs=pl.BlockSpec((None, tm, d), lambda bi, i, f: (bi, i, 0)),
        out_shape=jax.ShapeDtypeStruct((b, l, d), F32),
        scratch_shapes=[pltpu.VMEM((tm, d), BF16), pltpu.VMEM((tm, d), F32)],
        compiler_params=_cparams(("parallel", "parallel", "arbitrary")),
        name=name,
    )(x, mod, mod, mod, wg, wu, wd, ln_g, ln_b)


MOE_TM = 1024
ROUTE_TM = 512
ZERO_ROWS = 256
META_I1, META_I2, META_R1, META_R2, META_P1, META_P2 = range(6)


def _route_body(x_ref, sh_ref, sc_ref, wr_ref, meta_ref, cnt_ref, carry_scr):
    @pl.when(pl.program_id(0) == 0)
    def _():
        carry_scr[...] = jnp.zeros_like(carry_scr)

    h = (x_ref[...] * (1.0 + sc_ref[...]) + sh_ref[...]).astype(BF16)
    logits = jnp.dot(h, wr_ref[...], preferred_element_type=F32)
    tm = logits.shape[0]
    lane = lax.broadcasted_iota(jnp.int32, logits.shape, 1).astype(F32)
    l1 = jnp.where(lane < N_EXPERTS, logits, -jnp.inf)
    v1 = jnp.max(l1, axis=-1, keepdims=True)
    i1 = jnp.min(jnp.where(l1 == v1, lane, float(LANES)), axis=-1, keepdims=True)
    l2 = jnp.where(lane == i1, -jnp.inf, l1)
    v2 = jnp.max(l2, axis=-1, keepdims=True)
    i2 = jnp.min(jnp.where(l2 == v2, lane, float(LANES)), axis=-1, keepdims=True)
    t = jnp.exp(v2 - v1)
    p1 = 1.0 / (1.0 + t)
    p2 = t / (1.0 + t)

    member = jnp.where(lane == i1, 1.0, jnp.where(lane == i2, 1.0, 0.0))
    rr = lax.broadcasted_iota(jnp.int32, (tm, tm), 0)
    cc = lax.broadcasted_iota(jnp.int32, (tm, tm), 1)
    earlier = jnp.where(cc < rr, 1.0, 0.0).astype(BF16)
    base = carry_scr[0:1, :]
    rank = jnp.dot(earlier, member.astype(BF16), preferred_element_type=F32) + base
    total = base + jnp.sum(member, axis=0, keepdims=True)
    carry_scr[0:1, :] = total
    r1 = jnp.sum(jnp.where(lane == i1, rank, 0.0), axis=-1, keepdims=True)
    r2 = jnp.sum(jnp.where(lane == i2, rank, 0.0), axis=-1, keepdims=True)

    meta = jnp.zeros_like(logits)
    for k, val in ((META_I1, i1), (META_I2, i2), (META_R1, r1), (META_R2, r2), (META_P1, p1), (META_P2, p2)):
        meta = jnp.where(lane == float(k), val, meta)
    meta_ref[...] = meta
    cnt_ref[...] = jnp.broadcast_to(total, cnt_ref.shape)


def _route(x2, mod, shift_col, w_router_pad, *, seq_len):
    m, d = x2.shape
    tm = ROUTE_TM
    return pl.pallas_call(
        _route_body,
        grid=(m // tm,),
        in_specs=[
            pl.BlockSpec((tm, d), lambda i: (i, 0)),
            pl.BlockSpec((None, 1, d), lambda i: ((i * tm) // seq_len, 0, shift_col)),
            pl.BlockSpec((None, 1, d), lambda i: ((i * tm) // seq_len, 0, shift_col + 1)),
            pl.BlockSpec((d, LANES), lambda i: (0, 0)),
        ],
        out_specs=[pl.BlockSpec((tm, LANES), lambda i: (i, 0)),
                   pl.BlockSpec((8, LANES), lambda i: (0, 0))],
        out_shape=[jax.ShapeDtypeStruct((m, LANES), F32), jax.ShapeDtypeStruct((8, LANES), F32)],
        scratch_shapes=[pltpu.VMEM((8, LANES), F32)],
        compiler_params=_cparams(("arbitrary",)),
        name="moe_route",
    )(x2, mod, mod, w_router_pad)


def _row_copy(src, src_row, dst, dst_row, sem):
    return pltpu.make_async_copy(src.at[pl.ds(src_row, 1), :], dst.at[pl.ds(dst_row, 1), :], sem)


def _dispatch_body(pad_ref, pos_hbm, x_ref, sh_ref, sc_ref, xs_hbm, h_scr, z_scr, pos_smem, pos_sem, row_sem):
    step = pl.program_id(0)
    tm = x_ref.shape[0]
    pos_cp = pltpu.make_async_copy(pos_hbm.at[step], pos_smem, pos_sem)
    pos_cp.start()
    h_scr[...] = x_ref[...] * (1.0 + sc_ref[...]) + sh_ref[...]
    pos_cp.wait()

    def issue(r, carry):
        _row_copy(h_scr, r, xs_hbm, pos_smem[2 * r], row_sem).start()
        _row_copy(h_scr, r, xs_hbm, pos_smem[2 * r + 1], row_sem).start()
        return carry

    def drain(r, carry):
        _row_copy(h_scr, 0, xs_hbm, 0, row_sem).wait()
        _row_copy(h_scr, 0, xs_hbm, 0, row_sem).wait()
        return carry

    lax.fori_loop(0, tm, issue, 0)
    lax.fori_loop(0, tm, drain, 0)

    @pl.when(step == pl.num_programs(0) - 1)
    def _():
        z_scr[...] = jnp.zeros_like(z_scr)
        for e in range(N_EXPERTS):
            start = pad_ref[e]
            count = pad_ref[N_EXPERTS + e]

            def fill(k, carry, start=start):
                _row_copy(z_scr, 0, xs_hbm, start + k, row_sem).start()
                return carry

            def fill_wait(k, carry):
                _row_copy(z_scr, 0, xs_hbm, 0, row_sem).wait()
                return carry

            lax.fori_loop(0, count, fill, 0)
            lax.fori_loop(0, count, fill_wait, 0)

        zrows = z_scr.shape[0]
        used_rows = pad_ref[2 * N_EXPERTS]
        n_chunks = (xs_hbm.shape[0] - used_rows) // zrows

        def chunk_copy(k):
            row0 = pl.multiple_of(used_rows + k * zrows, zrows)
            return pltpu.make_async_copy(z_scr, xs_hbm.at[pl.ds(row0, zrows), :], row_sem)

        def fill_chunk(k, carry):
            chunk_copy(k).start()
            return carry

        def fill_chunk_wait(k, carry):
            chunk_copy(k).wait()
            return carry

        lax.fori_loop(0, n_chunks, fill_chunk, 0)
        lax.fori_loop(0, n_chunks, fill_chunk_wait, 0)


def _dispatch(x2, mod, shift_col, pos, pad_info, n_rows, *, seq_len):
    m, d = x2.shape
    tm = ROUTE_TM
    return pl.pallas_call(
        _dispatch_body,
        grid_spec=pltpu.PrefetchScalarGridSpec(
            num_scalar_prefetch=1,
            grid=(m // tm,),
            in_specs=[
                pl.BlockSpec(memory_space=pl.ANY),
                pl.BlockSpec((tm, d), lambda i, pad: (i, 0)),
                pl.BlockSpec((None, 1, d), lambda i, pad: ((i * tm) // seq_len, 0, shift_col)),
                pl.BlockSpec((None, 1, d), lambda i, pad: ((i * tm) // seq_len, 0, shift_col + 1)),
            ],
            out_specs=pl.BlockSpec(memory_space=pl.ANY),
            scratch_shapes=[
                pltpu.VMEM((tm, d), F32),
                pltpu.VMEM((ZERO_ROWS, d), F32),
                pltpu.SMEM((2 * tm,), jnp.int32),
                pltpu.SemaphoreType.DMA,
                pltpu.SemaphoreType.DMA,
            ],
        ),
        out_shape=jax.ShapeDtypeStruct((n_rows, d), F32),
        compiler_params=_cparams(("arbitrary",)),
        name="moe_dispatch",
    )(pad_info, pos, x2, mod, mod)


def _experts_body(te_ref, tbi_ref, tbo_ref, tv_ref, xs_ref, wg_ref, wu_ref, wd_ref, ys_ref, h_scr, acc_scr, *, n_f):
    t = pl.program_id(0)
    f = pl.program_id(1)

    @pl.when(f == 0)
    def _():
        h_scr[...] = xs_ref[...].astype(BF16)
        acc_scr[...] = jnp.zeros_like(acc_scr)

    @pl.when(tv_ref[t] > 0)
    def _():
        h = h_scr[...]
        g = jnp.dot(h, wg_ref[...], preferred_element_type=F32)
        u = jnp.dot(h, wu_ref[...], preferred_element_type=F32)
        a = (g * jax.nn.sigmoid(g) * u).astype(BF16)
        acc_scr[...] += jnp.dot(a, wd_ref[...], preferred_element_type=F32)

    @pl.when(f == n_f - 1)
    def _():
        ys_ref[...] = acc_scr[...]


def _experts(xs, tile_expert, tile_in, tile_out, tile_valid, wg, wu, wd, *, tf):
    n_rows, d = xs.shape
    tm = MOE_TM
    n_tiles = tile_expert.shape[0]
    ff = wg.shape[2]
    n_f = ff // tf
    return pl.pallas_call(
        functools.partial(_experts_body, n_f=n_f),
        grid_spec=pltpu.PrefetchScalarGridSpec(
            num_scalar_prefetch=4,
            grid=(n_tiles, n_f),
            in_specs=[
                pl.BlockSpec((tm, d), lambda t, f, te, tbi, tbo, tv: (tbi[t], 0)),
                pl.BlockSpec((None, d, tf), lambda t, f, te, tbi, tbo, tv: (te[t], 0, f * tv[t])),
                pl.BlockSpec((None, d, tf), lambda t, f, te, tbi, tbo, tv: (te[t], 0, f * tv[t])),
                pl.BlockSpec((None, tf, d), lambda t, f, te, tbi, tbo, tv: (te[t], f * tv[t], 0)),
            ],
            out_specs=pl.BlockSpec((tm, d), lambda t, f, te, tbi, tbo, tv: (tbo[t], 0)),
            scratch_shapes=[pltpu.VMEM((tm, d), BF16), pltpu.VMEM((tm, d), F32)],
        ),
        out_shape=jax.ShapeDtypeStruct((n_rows, d), F32),
        compiler_params=_cparams(("arbitrary", "arbitrary")),
        name="moe_experts",
    )(tile_expert, tile_in, tile_out, tile_valid, xs, wg, wu, wd)


def _combine_body(pos_hbm, ys_hbm, meta_ref, x_ref, gate_ref, lng_ref, lnb_ref, o_ref,
                  y1_scr, y2_scr, pos_smem, pos_sem, row_sem):
    step = pl.program_id(0)
    tm = x_ref.shape[0]
    pos_cp = pltpu.make_async_copy(pos_hbm.at[step], pos_smem, pos_sem)
    pos_cp.start()
    pos_cp.wait()

    def issue(r, carry):
        _row_copy(ys_hbm, pos_smem[2 * r], y1_scr, r, row_sem).start()
        _row_copy(ys_hbm, pos_smem[2 * r + 1], y2_scr, r, row_sem).start()
        return carry

    def drain(r, carry):
        _row_copy(ys_hbm, 0, y1_scr, 0, row_sem).wait()
        _row_copy(ys_hbm, 0, y2_scr, 0, row_sem).wait()
        return carry

    lax.fori_loop(0, tm, issue, 0)
    lax.fori_loop(0, tm, drain, 0)

    meta = meta_ref[...]
    p1 = meta[:, META_P1:META_P1 + 1]
    p2 = meta[:, META_P2:META_P2 + 1]
    mix = p1 * y1_scr[...] + p2 * y2_scr[...]
    r = DEEPNORM_ALPHA * x_ref[...] + gate_ref[...] * mix
    o_ref[...] = _layer_norm(r, lng_ref[...], lnb_ref[...])


def _combine_ln(ys, pos, meta, x2, mod, gate_col, ln_g, ln_b, *, seq_len):
    m, d = x2.shape
    tm = ROUTE_TM
    return pl.pallas_call(
        _combine_body,
        grid=(m // tm,),
        in_specs=[
            pl.BlockSpec(memory_space=pl.ANY),
            pl.BlockSpec(memory_space=pl.ANY),
            pl.BlockSpec((tm, LANES), lambda i: (i, 0)),
            pl.BlockSpec((tm, d), lambda i: (i, 0)),
            pl.BlockSpec((None, 1, d), lambda i: ((i * tm) // seq_len, 0, gate_col)),
            pl.BlockSpec((1, d), lambda i: (0, 0)),
            pl.BlockSpec((1, d), lambda i: (0, 0)),
        ],
        out_specs=pl.BlockSpec((tm, d), lambda i: (i, 0)),
        out_shape=jax.ShapeDtypeStruct((m, d), F32),
        scratch_shapes=[
            pltpu.VMEM((tm, d), F32),
            pltpu.VMEM((tm, d), F32),
            pltpu.SMEM((2 * tm,), jnp.int32),
            pltpu.SemaphoreType.DMA,
            pltpu.SemaphoreType.DMA,
        ],
        compiler_params=_cparams(("arbitrary",)),
        name="moe_combine",
    )(pos, ys, meta, x2, mod, ln_g, ln_b)


def _moe_ln(x, mod, w_router, wg, wu, wd, ln_g, ln_b, *, tf):
    b, l, d = x.shape
    m = b * l
    x2 = x.reshape(m, d)
    w_r = jnp.zeros((d, LANES), BF16).at[:, :N_EXPERTS].set(w_router.astype(BF16))
    meta, cnt = _route(x2, mod, 3, w_r, seq_len=l)

    counts = cnt[0, :N_EXPERTS].astype(jnp.int32)
    n_tile_e = (counts + MOE_TM - 1) // MOE_TM
    tile_end = jnp.cumsum(n_tile_e)
    offs = (tile_end - n_tile_e) * MOE_TM
    idx = meta[:, META_I1:META_I2 + 1].astype(jnp.int32)
    rank = meta[:, META_R1:META_R2 + 1].astype(jnp.int32)
    pos = (offs[idx] + rank).reshape(m // ROUTE_TM, 2 * ROUTE_TM)
    n_tiles = (2 * m) // MOE_TM + N_EXPERTS
    tid = jnp.arange(n_tiles, dtype=jnp.int32)
    tile_valid = (tid < tile_end[-1]).astype(jnp.int32)
    tile_expert = jnp.minimum(jnp.sum((tid[:, None] >= tile_end[None, :]).astype(jnp.int32), axis=1),
                              N_EXPERTS - 1).astype(jnp.int32)
    tile_in = jnp.where(tile_valid > 0, tid, 0).astype(jnp.int32)
    tile_out = tid
    pad_info = jnp.concatenate([offs + counts, n_tile_e * MOE_TM - counts,
                                tile_end[-1:] * MOE_TM]).astype(jnp.int32)
    n_rows = n_tiles * MOE_TM

    xs = _dispatch(x2, mod, 3, pos, pad_info, n_rows, seq_len=l)
    ys = _experts(xs, tile_expert, tile_in, tile_out, tile_valid, wg, wu, wd, tf=tf)
    out = _combine_ln(ys, pos, meta, x2, mod, 5, ln_g, ln_b, seq_len=l)
    return out.reshape(b, l, d)


def _fourier_body(pc_ref, dl_ref, dc_ref, o_ref, t_scr, *, out_scale, row_chunk):
    l = pc_ref.shape[0]
    for g in range(C_GROUPS):
        sl = slice(C_GROUP_DIM * g, C_GROUP_DIM * (g + 1))
        xg = pc_ref[:, sl].astype(F32)
        mu = jnp.mean(xg, axis=-1, keepdims=True)
        dlt = xg - mu
        var = jnp.mean(dlt * dlt, axis=-1, keepdims=True)
        gn = (dlt * lax.rsqrt(var + LN_EPS)).astype(BF16)
        t = jnp.dot(gn, dc_ref[...], preferred_element_type=F32)
        t_scr[0:l, sl] = t[:, 0:C_GROUP_DIM].astype(BF16)
        t_scr[l:2 * l, sl] = t[:, C_GROUP_DIM:2 * C_GROUP_DIM].astype(BF16)
    for r0 in range(0, l, row_chunk):
        acc = jnp.dot(dl_ref[r0:r0 + row_chunk, :], t_scr[...], preferred_element_type=F32)
        o_ref[r0:r0 + row_chunk, :] = (acc * out_scale).astype(BF16)


def _dft_matrices(l, c):
    j = np.arange(l, dtype=np.int64)
    ang_l = (2.0 * np.pi / l) * ((j[:, None] * j[None, :]) % l)
    dl = np.concatenate([np.cos(ang_l), -np.sin(ang_l)], axis=1)
    m = np.arange(c, dtype=np.int64)
    ang_c = (2.0 * np.pi / c) * ((m[:, None] * m[None, :]) % c)
    dc = np.concatenate([np.cos(ang_c), np.sin(ang_c)], axis=1)
    return dl.astype(np.float32), dc.astype(np.float32)


def _fourier_mixer(y, name="fourier"):
    b, l, _ = y.shape
    dl_np, dc_np = _dft_matrices(l, C_GROUP_DIM)
    dl = jnp.asarray(dl_np, dtype=BF16)
    dc = jnp.asarray(dc_np, dtype=BF16)
    out_scale = 1.0 / math.sqrt(l * C_GROUP_DIM)
    return pl.pallas_call(
        functools.partial(_fourier_body, out_scale=out_scale, row_chunk=min(l, 512)),
        grid=(b,),
        in_specs=[
            pl.BlockSpec((None, l, C_WIDTH), lambda bi: (bi, 0, 0)),
            pl.BlockSpec((l, 2 * l), lambda bi: (0, 0), pipeline_mode=pl.Buffered(1)),
            pl.BlockSpec((C_GROUP_DIM, 2 * C_GROUP_DIM), lambda bi: (0, 0)),
        ],
        out_specs=pl.BlockSpec((None, l, C_WIDTH), lambda bi: (bi, 0, 0)),
        out_shape=jax.ShapeDtypeStruct((b, l, C_WIDTH), BF16),
        scratch_shapes=[pltpu.VMEM((2 * l, C_WIDTH), BF16)],
        compiler_params=_cparams(("parallel",)),
        name=name,
    )(y, dl, dc)


NA_HG = 4
NA_GW = NA_HG * NA_DH
NA_WIN = NA_KR * GRID_W


def _natten_body(q_ref, k_ref, v_ref, kc_ref, vc_ref, bias_ref, o_ref, *, rows_per_step, n_rows):
    rb = pl.program_id(2)
    hq = NA_HG * GRID_W
    rid = lax.broadcasted_iota(jnp.int32, (hq, NA_GW), 0)
    cid = lax.broadcasted_iota(jnp.int32, (hq, NA_GW), 1)
    diag = (rid // GRID_W) == (cid // NA_DH)
    cid_o = lax.broadcasted_iota(jnp.int32, (GRID_W, NA_GW), 1)
    kc = kc_ref[...]
    vc = vc_ref[...]
    for j in range(rows_per_step):
        r = rb * rows_per_step + j
        rs = jnp.clip(r - NA_KR // 2, 0, n_rows - NA_KR)
        start = pl.multiple_of(rs * GRID_W, GRID_W)
        tid = jnp.minimum(r, NA_KR // 2) + jnp.maximum(r - (n_rows - NA_KR // 2), 0)
        q_r = q_ref[GRID_W * j:GRID_W * (j + 1), :]
        q4 = jnp.concatenate([q_r] * NA_HG, axis=0)
        qbd = jnp.where(diag, q4, jnp.zeros_like(q4))
        kw = k_ref[pl.ds(start, NA_WIN), :]
        vw = v_ref[pl.ds(start, NA_WIN), :]
        s_loc = lax.dot_general(qbd, kw, NT_DIMS, preferred_element_type=F32) + bias_ref[tid]
        s_ctx = lax.dot_general(qbd, kc, NT_DIMS, preferred_element_type=F32)
        m = jnp.maximum(jnp.max(s_loc, axis=-1, keepdims=True), jnp.max(s_ctx, axis=-1, keepdims=True))
        e_loc = jnp.exp2(s_loc - m)
        e_ctx = jnp.exp2(s_ctx - m)
        tot = jnp.sum(e_loc, axis=-1, keepdims=True) + jnp.sum(e_ctx, axis=-1, keepdims=True)
        o = (jnp.dot(e_loc.astype(BF16), vw, preferred_element_type=F32)
             + jnp.dot(e_ctx.astype(BF16), vc, preferred_element_type=F32))
        o = o * (1.0 / tot)
        out = jnp.zeros((GRID_W, NA_GW), F32)
        for hh in range(NA_HG):
            out = out + jnp.where((cid_o // NA_DH) == hh, o[GRID_W * hh:GRID_W * (hh + 1), :], 0.0)
        o_ref[GRID_W * j:GRID_W * (j + 1), :] = out.astype(BF16)


def _na_bias_table(rpb, n_rows):
    h = rpb.shape[0]
    cols = jnp.arange(GRID_W)
    col_start = jnp.clip(cols - NA_KC // 2, 0, GRID_W - NA_KC)
    col_valid = (cols[None, :] >= col_start[:, None]) & (cols[None, :] < col_start[:, None] + NA_KC)
    dc_idx = jnp.clip(cols[None, :] - cols[:, None] + NA_KC - 1, 0, 2 * NA_KC - 2)
    rpb_c = rpb[:, :, dc_idx].astype(F32)
    half = NA_KR // 2
    rep_rows = list(range(half)) + [half] + list(range(n_rows - half + 1, n_rows))
    tabs = []
    for r in rep_rows:
        rs = min(max(r - half, 0), n_rows - NA_KR)
        dr_idx = rs + np.arange(NA_KR) - r + NA_KR - 1
        bias = rpb_c[:, dr_idx].transpose(0, 2, 1, 3)
        bias = jnp.where(col_valid[None, :, None, :], bias * LOG2E, NEG_INF)
        tabs.append(bias.reshape(h, GRID_W, NA_WIN))
    tab = jnp.stack(tabs, axis=0)
    return tab.reshape(len(rep_rows), h // NA_HG, NA_HG * GRID_W, NA_WIN)


def _natten(y, q_col0, k_col0, v_col0, y_ctx, kc_col0, vc_col0, bias_tab, *, rows_per_step, name="natten"):
    b, l, _ = y.shape
    lc = y_ctx.shape[1]
    n_rows = l // GRID_W
    n_tab = bias_tab.shape[0]
    n_grp = NA_HEADS // NA_HG
    tq = rows_per_step * GRID_W
    qb, kb, vb = q_col0 // NA_GW, k_col0 // NA_GW, v_col0 // NA_GW
    kcb, vcb = kc_col0 // NA_GW, vc_col0 // NA_GW
    return pl.pallas_call(
        functools.partial(_natten_body, rows_per_step=rows_per_step, n_rows=n_rows),
        grid=(b, n_grp, n_rows // rows_per_step),
        in_specs=[
            pl.BlockSpec((None, tq, NA_GW), lambda bi, g, i: (bi, i, qb + g)),
            pl.BlockSpec((None, l, NA_GW), lambda bi, g, i: (bi, 0, kb + g)),
            pl.BlockSpec((None, l, NA_GW), lambda bi, g, i: (bi, 0, vb + g)),
            pl.BlockSpec((None, lc, NA_GW), lambda bi, g, i: (bi, 0, kcb + g)),
            pl.BlockSpec((None, lc, NA_GW), lambda bi, g, i: (bi, 0, vcb + g)),
            pl.BlockSpec((n_tab, None, NA_HG * GRID_W, NA_WIN), lambda bi, g, i: (0, g, 0, 0)),
        ],
        out_specs=pl.BlockSpec((None, tq, NA_GW), lambda bi, g, i: (bi, i, g)),
        out_shape=jax.ShapeDtypeStruct((b, l, D_WIDTH), BF16),
        compiler_params=_cparams(("parallel", "parallel", "arbitrary")),
        name=name,
    )(y, y, y, y_ctx, y_ctx, bias_tab)


def _rope_tables(l):
    t = jnp.arange(l, dtype=jnp.int32)
    row = (t // GRID_W).astype(F32)
    col = (t % GRID_W).astype(F32)
    n_freq = DIFF_DH // 4
    inv_freq = ROPE_THETA ** (-jnp.arange(n_freq, dtype=F32) / n_freq)
    ang = jnp.concatenate([row[:, None] * inv_freq, col[:, None] * inv_freq], axis=-1)
    c, s = jnp.cos(ang), jnp.sin(ang)
    cos = jnp.tile(jnp.concatenate([c, c], axis=-1), (1, 2))
    sin = jnp.tile(jnp.concatenate([-s, s], axis=-1), (1, 2))
    return cos, sin


def kernel(x, c, ctx, c_ctx, w_mod, b_mod, ln_g, ln_b, e_w_in, e_conv, e_lam_q1, e_lam_k1, e_lam_q2, e_lam_k2, e_subln_g, e_w_o, e_ffn_gate, e_ffn_up, e_ffn_down, o_w_in, o_rpb, o_w_o, o_router, o_exp_gate, o_exp_up, o_exp_down):
    b, l, d = x.shape
    lc = ctx.shape[1]
    assert d == D_MODEL and l % 512 == 0 and lc % 256 == 0 and b + 1 <= MOD_ROWS

    cond = jnp.concatenate([c, c_ctx[None, :], jnp.zeros((MOD_ROWS - b - 1, d), F32)], axis=0)
    mods = _adaln(cond, w_mod, b_mod)

    def layer_mods(i):
        lat = mods[i, :b][:, None, :]
        cx = jnp.broadcast_to(mods[i, b][None, None, :], (b, 1, 6 * d))
        return lat, cx

    q_scale_diff = DIFF_DH ** -0.5 * LOG2E
    q_scale_na = NA_DH ** -0.5 * LOG2E

    mod_lat, mod_ctx = layer_mods(0)
    lam_init = 0.8 - 0.6 * math.exp(-0.3 * 0)
    a_end = 3 * A_WIDTH
    w_in = e_w_in[0]
    w_main = w_in[:, :a_end + 2 * DIFF_QK].astype(BF16)
    w_vt = w_in[:, a_end + 2 * DIFF_QK:].T.astype(BF16)
    rope = _rope_tables(l)
    pa_chunks = [(0, 512, "plain"), (512, 512, "plain"), (1024, 512, "plain")]
    y_lat, vt_lat = _proj(x, mod_lat, 0, w_main,
                          pa_chunks + [(a_end, 512, "rope_scale"), (a_end + 512, 512, "rope")],
                          tm=512, wvt=w_vt, rope=rope, q_scale=q_scale_diff, name="even_inproj_lat")
    y_ctx, vt_ctx = _proj(ctx, mod_ctx, 0, w_main,
                          pa_chunks + [(a_end, 512, "scale"), (a_end + 512, 512, "plain")],
                          tm=lc, wvt=w_vt, q_scale=q_scale_diff, name="even_inproj_ctx")

    lam_pack = jnp.zeros((8, LANES), F32)
    lam_pack = lam_pack.at[0, :DIFF_DH].set(e_lam_q1[0]).at[1, :DIFF_DH].set(e_lam_k1[0])
    lam_pack = lam_pack.at[2, :DIFF_DH].set(e_lam_q2[0]).at[3, :DIFF_DH].set(e_lam_k2[0])
    sub_g = e_subln_g[0].reshape(1, DIFF_VDIM)
    o_lat = _diff_attention(y_lat, [(y_ctx, vt_ctx), (y_lat, vt_lat)], lam_pack, sub_g, lam_init,
                            tq=256, q_col0=a_end, k_col0=a_end + DIFF_QK, name="diffattn_lat")
    o_ctx = _diff_attention(y_ctx, [(y_ctx, vt_ctx)], lam_pack, sub_g, lam_init,
                            tq=lc, q_col0=a_end, k_col0=a_end + DIFF_QK, name="diffattn_ctx")
    cv_lat = _conv_mixer(y_lat, e_conv[0], "conv_lat")
    cv_ctx = _conv_mixer(y_ctx, e_conv[0], "conv_ctx")

    w_o = e_w_o[0].astype(BF16)
    lng0, lnb0 = ln_g[0, 0][None, :], ln_b[0, 0][None, :]
    lng1, lnb1 = ln_g[0, 1][None, :], ln_b[0, 1][None, :]
    x_lat = _outproj_ln(cv_lat, 0, o_lat, 0, w_o, x, mod_lat, 2, lng0, lnb0, tm=512, name="even_outproj_lat")
    x_ctx = _outproj_ln(cv_ctx, 0, o_ctx, 0, w_o, ctx, mod_ctx, 2, lng0, lnb0, tm=lc, name="even_outproj_ctx")

    wg = e_ffn_gate[0].astype(BF16)
    wu = e_ffn_up[0].astype(BF16)
    wd = e_ffn_down[0].astype(BF16)
    x_lat = _ffn_ln(x_lat, mod_lat, 3, wg, wu, wd, lng1, lnb1, tm=512, tf=1408, name="ffn_lat")
    x_ctx = _ffn_ln(x_ctx, mod_ctx, 3, wg, wu, wd, lng1, lnb1, tm=lc, tf=1408, name="ffn_ctx")

    mod_lat, mod_ctx = layer_mods(1)
    w_in = o_w_in[0].astype(BF16)
    y_lat = _proj(x_lat, mod_lat, 0, w_in,
                  [(0, 512, "plain"), (512, 512, "scale"), (1024, 512, "plain"), (1536, 512, "plain")],
                  tm=512, q_scale=q_scale_na, name="odd_inproj_lat")
    y_ctx = _proj(x_ctx, mod_ctx, 0, w_in[:, C_WIDTH + D_WIDTH:],
                  [(0, 512, "plain"), (512, 512, "plain")], tm=lc, name="odd_inproj_ctx")
    f_lat = _fourier_mixer(y_lat)
    bias_tab = _na_bias_table(o_rpb[0], l // GRID_W)
    n_lat = _natten(y_lat, C_WIDTH, C_WIDTH + D_WIDTH, C_WIDTH + 2 * D_WIDTH, y_ctx, 0, D_WIDTH,
                    bias_tab, rows_per_step=4)

    w_o = o_w_o[0].astype(BF16)
    lng0, lnb0 = ln_g[1, 0][None, :], ln_b[1, 0][None, :]
    lng1, lnb1 = ln_g[1, 1][None, :], ln_b[1, 1][None, :]
    x_lat = _outproj_ln(f_lat, 0, n_lat, 0, w_o, x_lat, mod_lat, 2, lng0, lnb0, tm=512, name="odd_outproj_lat")

    return _moe_ln(x_lat, mod_lat, o_router[0], o_exp_gate[0].astype(BF16), o_exp_up[0].astype(BF16),
                   o_exp_down[0].astype(BF16), lng1, lnb1, tf=512)
```

```python
import functools
import math

import numpy as np
import jax
import jax.numpy as jnp
from jax import lax
from jax.experimental import pallas as pl
from jax.experimental.pallas import tpu as pltpu

F32 = jnp.float32
BF16 = jnp.bfloat16

D_MODEL = 1024
GRID_W = 64
DEPTH = 2

A_WIDTH = 512
DIFF_HEADS = 4
DIFF_DH = 64
DIFF_VDIM = 128
DIFF_QK = 512
B_WIDTH = 512

C_WIDTH = 512
C_GROUPS = 4
C_GROUP_DIM = 128
NA_HEADS = 8
NA_DH = 64
D_WIDTH = 512
NA_KR = 8
NA_KC = 16

N_EXPERTS = 8

ROPE_THETA = 10000.0
LN_EPS = 1e-5
RMS_EPS = 1e-5
NEG_INF = -1e30
DEEPNORM_ALPHA = (2 * DEPTH) ** 0.25
LOG2E = 1.4426950408889634

LANES = 128
SUBLANES = 8
MOD_ROWS = 32
NT_DIMS = (((1,), (1,)), ((), ()))


def _cparams(sem, vmem_mb=48):
    return pltpu.CompilerParams(dimension_semantics=sem, vmem_limit_bytes=vmem_mb * 1024 * 1024)


def _layer_norm(r, g, b):
    mu = jnp.mean(r, axis=-1, keepdims=True)
    d = r - mu
    var = jnp.mean(d * d, axis=-1, keepdims=True)
    return d * lax.rsqrt(var + LN_EPS) * g + b


def _adaln_body(c_ref, w_ref, b_ref, o_ref):
    cnd = c_ref[...]
    s = (cnd * jax.nn.sigmoid(cnd)).astype(BF16)
    o_ref[...] = jnp.dot(s, w_ref[...].astype(BF16), preferred_element_type=F32) + b_ref[...]


def _adaln(cond, w_mod, b_mod):
    depth, d, n = w_mod.shape
    tn = 1536
    return pl.pallas_call(
        _adaln_body,
        grid=(depth, n // tn),
        in_specs=[
            pl.BlockSpec((MOD_ROWS, d), lambda l, j: (0, 0)),
            pl.BlockSpec((None, d, tn), lambda l, j: (l, 0, j)),
            pl.BlockSpec((None, 1, tn), lambda l, j: (l, 0, j)),
        ],
        out_specs=pl.BlockSpec((None, MOD_ROWS, tn), lambda l, j: (l, 0, j)),
        out_shape=jax.ShapeDtypeStruct((depth, MOD_ROWS, n), F32),
        compiler_params=_cparams(("parallel", "parallel")),
        name="adaln",
    )(cond, w_mod, b_mod.reshape(depth, 1, n))


def _proj_body(*refs, chunks, with_vt, with_rope, q_scale):
    x_ref, sh_ref, sc_ref, w_ref = refs[:4]
    i = 4
    if with_vt:
        wvt_ref = refs[i]
        i += 1
    if with_rope:
        cos_ref, sin_ref = refs[i], refs[i + 1]
        i += 2
    y_ref = refs[i]
    vt_ref = refs[i + 1] if with_vt else None

    h = (x_ref[...] * (1.0 + sc_ref[...]) + sh_ref[...]).astype(BF16)
    tm = h.shape[0]
    if with_rope:
        cos = cos_ref[...]
        sin = sin_ref[...]
        lane = lax.broadcasted_iota(jnp.int32, (tm, LANES), 1)
        low_half = (lane % 64) < 32
    for (c0, width, kind) in chunks:
        acc = jnp.dot(h, w_ref[:, c0:c0 + width], preferred_element_type=F32)
        if kind in ("rope", "rope_scale"):
            for j in range(width // LANES):
                a = acc[:, LANES * j:LANES * (j + 1)]
                rot = jnp.where(low_half, pltpu.roll(a, 96, 1), pltpu.roll(a, 32, 1))
                r = a * cos + rot * sin
                if kind == "rope_scale":
                    r = r * q_scale
                y_ref[:, c0 + LANES * j:c0 + LANES * (j + 1)] = r.astype(BF16)
        elif kind == "scale":
            y_ref[:, c0:c0 + width] = (acc * q_scale).astype(BF16)
        else:
            y_ref[:, c0:c0 + width] = acc.astype(BF16)
    if with_vt:
        vt = lax.dot_general(wvt_ref[...], h, NT_DIMS, preferred_element_type=F32)
        vt_ref[...] = vt.astype(BF16)


def _proj(x, mod, shift_col, w, chunks, *, tm, wvt=None, rope=None, q_scale=1.0, name="proj"):
    b, l, d = x.shape
    n = w.shape[1]
    with_vt = wvt is not None
    with_rope = rope is not None
    in_specs = [
        pl.BlockSpec((None, tm, d), lambda bi, i: (bi, i, 0)),
        pl.BlockSpec((None, 1, d), lambda bi, i: (bi, 0, shift_col)),
        pl.BlockSpec((None, 1, d), lambda bi, i: (bi, 0, shift_col + 1)),
        pl.BlockSpec((d, n), lambda bi, i: (0, 0)),
    ]
    args = [x, mod, mod, w]
    if with_vt:
        nv = wvt.shape[0]
        in_specs.append(pl.BlockSpec((nv, d), lambda bi, i: (0, 0)))
        args.append(wvt)
    if with_rope:
        in_specs += [pl.BlockSpec((tm, LANES), lambda bi, i: (i, 0))] * 2
        args += [rope[0], rope[1]]
    out_specs = [pl.BlockSpec((None, tm, n), lambda bi, i: (bi, i, 0))]
    out_shape = [jax.ShapeDtypeStruct((b, l, n), BF16)]
    if with_vt:
        out_specs.append(pl.BlockSpec((None, nv, tm), lambda bi, i: (bi, 0, i)))
        out_shape.append(jax.ShapeDtypeStruct((b, nv, l), BF16))
    res = pl.pallas_call(
        functools.partial(_proj_body, chunks=tuple(chunks), with_vt=with_vt, with_rope=with_rope,
                          q_scale=q_scale),
        grid=(b, l // tm),
        in_specs=in_specs,
        out_specs=out_specs,
        out_shape=out_shape,
        compiler_params=_cparams(("parallel", "parallel")),
        name=name,
    )(*args)
    return res if with_vt else res[0]


def _diffattn_body(*refs, n_seg, lam_init):
    q_ref, lam_ref, g_ref = refs[:3]
    k_refs = [refs[3 + 2 * s] for s in range(n_seg)]
    vt_refs = [refs[4 + 2 * s] for s in range(n_seg)]
    o_ref = refs[3 + 2 * n_seg]

    q = q_ref[...]
    lane = lax.broadcasted_iota(jnp.int32, q.shape, 1)
    zero = jnp.zeros_like(q)
    q1 = jnp.where(lane < DIFF_DH, q, zero)
    q2 = jnp.where(lane >= DIFF_DH, q, zero)

    def softmax_parts(qm):
        s = [lax.dot_general(k_ref[...], qm, NT_DIMS, preferred_element_type=F32) for k_ref in k_refs]
        m = functools.reduce(jnp.maximum, [jnp.max(x, axis=0, keepdims=True) for x in s])
        e = [jnp.exp2(x - m) for x in s]
        tot = functools.reduce(jnp.add, [jnp.sum(x, axis=0, keepdims=True) for x in e])
        return e, tot

    e1, l1 = softmax_parts(q1)
    e2, l2 = softmax_parts(q2)

    lp = lam_ref[...]
    lam = (jnp.exp(jnp.sum(lp[0:1] * lp[1:2], axis=1, keepdims=True))
           - jnp.exp(jnp.sum(lp[2:3] * lp[3:4], axis=1, keepdims=True)) + lam_init)
    r1 = 1.0 / l1
    r2 = lam / l2
    o_t = None
    for s in range(n_seg):
        a = (e1[s] * r1 - e2[s] * r2).astype(BF16)
        t = jnp.dot(vt_refs[s][...], a, preferred_element_type=F32)
        o_t = t if o_t is None else o_t + t
    o = o_t.T
    ms = jnp.mean(o * o, axis=-1, keepdims=True)
    o_ref[...] = (o * lax.rsqrt(ms + RMS_EPS) * g_ref[...] * (1.0 - lam_init)).astype(BF16)


def _diff_attention(yq, segs, lam_pack, sub_g, lam_init, *, tq, q_col0, k_col0, name):
    b, lq, _ = yq.shape
    qb0 = q_col0 // LANES
    kb0 = k_col0 // LANES
    in_specs = [
        pl.BlockSpec((None, tq, LANES), lambda bi, h, i: (bi, i, qb0 + h)),
        pl.BlockSpec((8, LANES), lambda bi, h, i: (0, 0)),
        pl.BlockSpec((1, LANES), lambda bi, h, i: (0, 0)),
    ]
    args = [yq, lam_pack, sub_g]
    for (yk, vt) in segs:
        lk = yk.shape[1]
        in_specs.append(pl.BlockSpec((None, lk, LANES), lambda bi, h, i: (bi, 0, kb0 + h)))
        in_specs.append(pl.BlockSpec((None, DIFF_VDIM, lk), lambda bi, h, i: (bi, h, 0)))
        args += [yk, vt]
    return pl.pallas_call(
        functools.partial(_diffattn_body, n_seg=len(segs), lam_init=lam_init),
        grid=(b, DIFF_HEADS, lq // tq),
        in_specs=in_specs,
        out_specs=pl.BlockSpec((None, tq, LANES), lambda bi, h, i: (bi, i, h)),
        out_shape=jax.ShapeDtypeStruct((b, lq, B_WIDTH), BF16),
        compiler_params=_cparams(("parallel", "parallel", "arbitrary")),
        name=name,
    )(*args)


def _conv_body(bg_ref, cg_ref, val_ref, w_ref, o_ref):
    l = o_ref.shape[0]
    row = lax.broadcasted_iota(jnp.int32, (l, LANES), 0)
    for j in range(A_WIDTH // LANES):
        sl = slice(LANES * j, LANES * (j + 1))
        u = cg_ref[:, sl].astype(F32) * val_ref[:, sl].astype(F32)
        u_prev = jnp.where(row == 0, 0.0, pltpu.roll(u, 1, 0))
        u_next = jnp.where(row == l - 1, 0.0, pltpu.roll(u, l - 1, 0))
        w = w_ref[:, sl]
        y = bg_ref[:, sl].astype(F32) * (u_prev * w[0:1] + u * w[1:2] + u_next * w[2:3])
        o_ref[:, sl] = y.astype(BF16)


def _conv_mixer(y, w_conv, name):
    b, l, _ = y.shape
    return pl.pallas_call(
        _conv_body,
        grid=(b,),
        in_specs=[
            pl.BlockSpec((None, l, A_WIDTH), lambda bi: (bi, 0, 0)),
            pl.BlockSpec((None, l, A_WIDTH), lambda bi: (bi, 0, 1)),
            pl.BlockSpec((None, l, A_WIDTH), lambda bi: (bi, 0, 2)),
            pl.BlockSpec((3, A_WIDTH), lambda bi: (0, 0)),
        ],
        out_specs=pl.BlockSpec((None, l, A_WIDTH), lambda bi: (bi, 0, 0)),
        out_shape=jax.ShapeDtypeStruct((b, l, A_WIDTH), BF16),
        compiler_params=_cparams(("parallel",)),
        name=name,
    )(y, y, y, w_conv)


def _outproj_body(a_ref, b_ref, w_ref, x_ref, gate_ref, lng_ref, lnb_ref, o_ref):
    half = a_ref.shape[1]
    y = (jnp.dot(a_ref[...], w_ref[0:half, :], preferred_element_type=F32)
         + jnp.dot(b_ref[...], w_ref[half:2 * half, :], preferred_element_type=F32))
    r = DEEPNORM_ALPHA * x_ref[...] + gate_ref[...] * y
    o_ref[...] = _layer_norm(r, lng_ref[...], lnb_ref[...])


def _outproj_ln(a, a_colblk, bsrc, b_colblk, w_o, x, mod, gate_col, ln_g, ln_b, *, tm, name):
    b, l, d = x.shape
    half = w_o.shape[0] // 2
    return pl.pallas_call(
        _outproj_body,
        grid=(b, l // tm),
        in_specs=[
            pl.BlockSpec((None, tm, half), lambda bi, i: (bi, i, a_colblk)),
            pl.BlockSpec((None, tm, half), lambda bi, i: (bi, i, b_colblk)),
            pl.BlockSpec((2 * half, d), lambda bi, i: (0, 0)),
            pl.BlockSpec((None, tm, d), lambda bi, i: (bi, i, 0)),
            pl.BlockSpec((None, 1, d), lambda bi, i: (bi, 0, gate_col)),
            pl.BlockSpec((1, d), lambda bi, i: (0, 0)),
            pl.BlockSpec((1, d), lambda bi, i: (0, 0)),
        ],
        out_specs=pl.BlockSpec((None, tm, d), lambda bi, i: (bi, i, 0)),
        out_shape=jax.ShapeDtypeStruct((b, l, d), F32),
        compiler_params=_cparams(("parallel", "parallel")),
        name=name,
    )(a, bsrc, w_o, x, mod, ln_g, ln_b)


def _ffn_body(x_ref, sh_ref, sc_ref, gate_ref, wg_ref, wu_ref, wd_ref, lng_ref, lnb_ref,
              o_ref, h_scr, acc_scr, *, n_f):
    f = pl.program_id(2)

    @pl.when(f == 0)
    def _():
        h_scr[...] = (x_ref[...] * (1.0 + sc_ref[...]) + sh_ref[...]).astype(BF16)
        acc_scr[...] = jnp.zeros_like(acc_scr)

    h = h_scr[...]
    g = jnp.dot(h, wg_ref[...], preferred_element_type=F32)
    u = jnp.dot(h, wu_ref[...], preferred_element_type=F32)
    a = (g * jax.nn.sigmoid(g) * u).astype(BF16)
    acc_scr[...] += jnp.dot(a, wd_ref[...], preferred_element_type=F32)

    @pl.when(f == n_f - 1)
    def _():
        r = DEEPNORM_ALPHA * x_ref[...] + gate_ref[...] * acc_scr[...]
        o_ref[...] = _layer_norm(r, lng_ref[...], lnb_ref[...])


def _ffn_ln(x, mod, shift_col, wg, wu, wd, ln_g, ln_b, *, tm, tf, name="ffn"):
    b, l, d = x.shape
    ff = wg.shape[1]
    n_f = ff // tf
    return pl.pallas_call(
        functools.partial(_ffn_body, n_f=n_f),
        grid=(b, l // tm, n_f),
        in_specs=[
            pl.BlockSpec((None, tm, d), lambda bi, i, f: (bi, i, 0)),
            pl.BlockSpec((None, 1, d), lambda bi, i, f: (bi, 0, shift_col)),
            pl.BlockSpec((None, 1, d), lambda bi, i, f: (bi, 0, shift_col + 1)),
            pl.BlockSpec((None, 1, d), lambda bi, i, f: (bi, 0, shift_col + 2)),
            pl.BlockSpec((d, tf), lambda bi, i, f: (0, f)),
            pl.BlockSpec((d, tf), lambda bi, i, f: (0, f)),
            pl.BlockSpec((tf, d), lambda bi, i, f: (f, 0)),
            pl.BlockSpec((1, d), lambda bi, i, f: (0, 0)),
            pl.BlockSpec((1, d), lambda bi, i, f: (0, 0)),
        ],
        out_specs=pl.BlockSpec((None, tm, d), lambda bi, i, f: (bi, i, 0)),
        out_shape=jax.ShapeDtypeStruct((b, l, d), F32),
        scratch_shapes=[pltpu.VMEM((tm, d), BF16), pltpu.VMEM((tm, d), F32)],
        compiler_params=_cparams(("parallel", "parallel", "arbitrary")),
        name=name,
    )(x, mod, mod, mod, wg, wu, wd, ln_g, ln_b)


MOE_TM = 1024
ROUTE_TM = 512
ZERO_ROWS = 256
META_I1, META_I2, META_R1, META_R2, META_P1, META_P2 = range(6)


def _route_body(x_ref, sh_ref, sc_ref, wr_ref, meta_ref, cnt_ref, carry_scr):
    @pl.when(pl.program_id(0) == 0)
    def _():
        carry_scr[...] = jnp.zeros_like(carry_scr)

    h = (x_ref[...] * (1.0 + sc_ref[...]) + sh_ref[...]).astype(BF16)
    logits = jnp.dot(h, wr_ref[...], preferred_element_type=F32)
    tm = logits.shape[0]
    lane = lax.broadcasted_iota(jnp.int32, logits.shape, 1).astype(F32)
    l1 = jnp.where(lane < N_EXPERTS, logits, -jnp.inf)
    v1 = jnp.max(l1, axis=-1, keepdims=True)
    i1 = jnp.min(jnp.where(l1 == v1, lane, float(LANES)), axis=-1, keepdims=True)
    l2 = jnp.where(lane == i1, -jnp.inf, l1)
    v2 = jnp.max(l2, axis=-1, keepdims=True)
    i2 = jnp.min(jnp.where(l2 == v2, lane, float(LANES)), axis=-1, keepdims=True)
    t = jnp.exp(v2 - v1)
    p1 = 1.0 / (1.0 + t)
    p2 = t / (1.0 + t)

    member = jnp.where(lane == i1, 1.0, jnp.where(lane == i2, 1.0, 0.0))
    rr = lax.broadcasted_iota(jnp.int32, (tm, tm), 0)
    cc = lax.broadcasted_iota(jnp.int32, (tm, tm), 1)
    earlier = jnp.where(cc < rr, 1.0, 0.0).astype(BF16)
    base = carry_scr[0:1, :]
    rank = jnp.dot(earlier, member.astype(BF16), preferred_element_type=F32) + base
    total = base + jnp.sum(member, axis=0, keepdims=True)
    carry_scr[0:1, :] = total
    r1 = jnp.sum(jnp.where(lane == i1, rank, 0.0), axis=-1, keepdims=True)
    r2 = jnp.sum(jnp.where(lane == i2, rank, 0.0), axis=-1, keepdims=True)

    meta = jnp.zeros_like(logits)
    for k, val in ((META_I1, i1), (META_I2, i2), (META_R1, r1), (META_R2, r2), (META_P1, p1), (META_P2, p2)):
        meta = jnp.where(lane == float(k), val, meta)
    meta_ref[...] = meta
    cnt_ref[...] = jnp.broadcast_to(total, cnt_ref.shape)


def _route(x2, mod, shift_col, w_router_pad, *, seq_len):
    m, d = x2.shape
    tm = ROUTE_TM
    return pl.pallas_call(
        _route_body,
        grid=(m // tm,),
        in_specs=[
            pl.BlockSpec((tm, d), lambda i: (i, 0)),
            pl.BlockSpec((None, 1, d), lambda i: ((i * tm) // seq_len, 0, shift_col)),
            pl.BlockSpec((None, 1, d), lambda i: ((i * tm) // seq_len, 0, shift_col + 1)),
            pl.BlockSpec((d, LANES), lambda i: (0, 0)),
        ],
        out_specs=[pl.BlockSpec((tm, LANES), lambda i: (i, 0)),
                   pl.BlockSpec((8, LANES), lambda i: (0, 0))],
        out_shape=[jax.ShapeDtypeStruct((m, LANES), F32), jax.ShapeDtypeStruct((8, LANES), F32)],
        scratch_shapes=[pltpu.VMEM((8, LANES), F32)],
        compiler_params=_cparams(("arbitrary",)),
        name="moe_route",
    )(x2, mod, mod, w_router_pad)


def _to_tiles(tile_ref, value):
    n = value.shape[0]
    for j in range(SUBLANES):
        tile_ref[pl.ds(j, n, stride=SUBLANES), :] = value[:, LANES * j:LANES * (j + 1)]


def _from_tiles(tile_ref):
    n = tile_ref.shape[0] // SUBLANES
    return jnp.concatenate([tile_ref[pl.ds(j, n, stride=SUBLANES), :] for j in range(SUBLANES)], axis=-1)


def _tile_rows(ref, row, n=1):
    return ref.at[pl.ds(pl.multiple_of(row * SUBLANES, SUBLANES), n * SUBLANES), :]


def _row_copy(src, src_row, dst, dst_row, sem):
    return pltpu.make_async_copy(_tile_rows(src, src_row), _tile_rows(dst, dst_row), sem)


def _rows_wait(src, dst, n, sem):
    pltpu.make_async_copy(_tile_rows(src, 0, n), _tile_rows(dst, 0, n), sem).wait()


def _dispatch_body(pad_ref, pos_hbm, x_ref, sh_ref, sc_ref, xs_hbm, h_scr, z_scr, pos_smem, pos_sem, row_sem):
    step = pl.program_id(0)
    tm = x_ref.shape[0]
    pos_cp = pltpu.make_async_copy(pos_hbm.at[step], pos_smem, pos_sem)
    pos_cp.start()
    _to_tiles(h_scr, x_ref[...] * (1.0 + sc_ref[...]) + sh_ref[...])
    pos_cp.wait()

    def issue(r, carry):
        _row_copy(h_scr, r, xs_hbm, pos_smem[2 * r], row_sem).start()
        _row_copy(h_scr, r, xs_hbm, pos_smem[2 * r + 1], row_sem).start()
        return carry

    lax.fori_loop(0, tm, issue, 0, unroll=8)
    _rows_wait(h_scr, xs_hbm, tm, row_sem)
    _rows_wait(h_scr, xs_hbm, tm, row_sem)

    @pl.when(step == pl.num_programs(0) - 1)
    def _():
        z_scr[...] = jnp.zeros_like(z_scr)
        for e in range(N_EXPERTS):
            start = pad_ref[e]
            count = pad_ref[N_EXPERTS + e]

            def fill(k, carry, start=start):
                _row_copy(z_scr, 0, xs_hbm, start + k, row_sem).start()
                return carry

            def fill_wait(k, carry):
                _row_copy(z_scr, 0, xs_hbm, 0, row_sem).wait()
                return carry

            lax.fori_loop(0, count, fill, 0)
            lax.fori_loop(0, count, fill_wait, 0)

        zrows = z_scr.shape[0] // SUBLANES
        used_rows = pad_ref[2 * N_EXPERTS]
        n_chunks = (xs_hbm.shape[0] // SUBLANES - used_rows) // zrows

        def chunk_copy(k):
            return pltpu.make_async_copy(z_scr, _tile_rows(xs_hbm, used_rows + k * zrows, zrows), row_sem)

        def fill_chunk(k, carry):
            chunk_copy(k).start()
            return carry

        def fill_chunk_wait(k, carry):
            chunk_copy(k).wait()
            return carry

        lax.fori_loop(0, n_chunks, fill_chunk, 0)
        lax.fori_loop(0, n_chunks, fill_chunk_wait, 0)


def _dispatch(x2, mod, shift_col, pos, pad_info, n_rows, *, seq_len):
    m, d = x2.shape
    tm = ROUTE_TM
    return pl.pallas_call(
        _dispatch_body,
        grid_spec=pltpu.PrefetchScalarGridSpec(
            num_scalar_prefetch=1,
            grid=(m // tm,),
            in_specs=[
                pl.BlockSpec(memory_space=pl.ANY),
                pl.BlockSpec((tm, d), lambda i, pad: (i, 0)),
                pl.BlockSpec((None, 1, d), lambda i, pad: ((i * tm) // seq_len, 0, shift_col)),
                pl.BlockSpec((None, 1, d), lambda i, pad: ((i * tm) // seq_len, 0, shift_col + 1)),
            ],
            out_specs=pl.BlockSpec(memory_space=pl.ANY),
            scratch_shapes=[
                pltpu.VMEM((tm * SUBLANES, LANES), F32),
                pltpu.VMEM((ZERO_ROWS * SUBLANES, LANES), F32),
                pltpu.SMEM((2 * tm,), jnp.int32),
                pltpu.SemaphoreType.DMA,
                pltpu.SemaphoreType.DMA,
            ],
        ),
        out_shape=jax.ShapeDtypeStruct((n_rows * SUBLANES, LANES), F32),
        compiler_params=_cparams(("arbitrary",)),
        name="moe_dispatch",
    )(pad_info, pos, x2, mod, mod)


def _experts_body(te_ref, tbi_ref, tbo_ref, tv_ref, xs_ref, wg_ref, wu_ref, wd_ref, ys_ref, h_scr, acc_scr, *, n_f):
    t = pl.program_id(0)
    f = pl.program_id(1)

    @pl.when(f == 0)
    def _():
        h_scr[...] = _from_tiles(xs_ref).astype(BF16)
        acc_scr[...] = jnp.zeros_like(acc_scr)

    @pl.when(tv_ref[t] > 0)
    def _():
        h = h_scr[...]
        g = jnp.dot(h, wg_ref[...].astype(BF16), preferred_element_type=F32)
        u = jnp.dot(h, wu_ref[...].astype(BF16), preferred_element_type=F32)
        a = (g * jax.nn.sigmoid(g) * u).astype(BF16)
        acc_scr[...] += jnp.dot(a, wd_ref[...].astype(BF16), preferred_element_type=F32)

    @pl.when(f == n_f - 1)
    def _():
        _to_tiles(ys_ref, acc_scr[...])


def _experts(xs, tile_expert, tile_in, tile_out, tile_valid, wg, wu, wd, *, tf):
    n_rows = xs.shape[0] // SUBLANES
    d = wg.shape[1]
    tm = MOE_TM
    n_tiles = tile_expert.shape[0]
    ff = wg.shape[2]
    n_f = ff // tf
    return pl.pallas_call(
        functools.partial(_experts_body, n_f=n_f),
        grid_spec=pltpu.PrefetchScalarGridSpec(
            num_scalar_prefetch=4,
            grid=(n_tiles, n_f),
            in_specs=[
                pl.BlockSpec((tm * SUBLANES, LANES), lambda t, f, te, tbi, tbo, tv: (tbi[t], 0)),
                pl.BlockSpec((None, d, tf), lambda t, f, te, tbi, tbo, tv: (te[t], 0, f * tv[t])),
                pl.BlockSpec((None, d, tf), lambda t, f, te, tbi, tbo, tv: (te[t], 0, f * tv[t])),
                pl.BlockSpec((None, tf, d), lambda t, f, te, tbi, tbo, tv: (te[t], f * tv[t], 0)),
            ],
            out_specs=pl.BlockSpec((tm * SUBLANES, LANES), lambda t, f, te, tbi, tbo, tv: (tbo[t], 0)),
            scratch_shapes=[pltpu.VMEM((tm, d), BF16), pltpu.VMEM((tm, d), F32)],
        ),
        out_shape=jax.ShapeDtypeStruct((n_rows * SUBLANES, LANES), F32),
        compiler_params=_cparams(("arbitrary", "arbitrary")),
        name="moe_experts",
    )(tile_expert, tile_in, tile_out, tile_valid, xs, wg, wu, wd)


def _combine_body(pos_hbm, ys_hbm, meta_ref, x_ref, gate_ref, lng_ref, lnb_ref, o_ref,
                  y1_scr, y2_scr, pos_smem, pos_sem, row_sem):
    step = pl.program_id(0)
    tm = x_ref.shape[0]
    pos_cp = pltpu.make_async_copy(pos_hbm.at[step], pos_smem, pos_sem)
    pos_cp.start()
    pos_cp.wait()

    def issue(r, carry):
        _row_copy(ys_hbm, pos_smem[2 * r], y1_scr, r, row_sem).start()
        _row_copy(ys_hbm, pos_smem[2 * r + 1], y2_scr, r, row_sem).start()
        return carry

    lax.fori_loop(0, tm, issue, 0, unroll=8)
    _rows_wait(ys_hbm, y1_scr, tm, row_sem)
    _rows_wait(ys_hbm, y2_scr, tm, row_sem)

    meta = meta_ref[...]
    p1 = meta[:, META_P1:META_P1 + 1]
    p2 = meta[:, META_P2:META_P2 + 1]
    mix = p1 * _from_tiles(y1_scr) + p2 * _from_tiles(y2_scr)
    r = DEEPNORM_ALPHA * x_ref[...] + gate_ref[...] * mix
    o_ref[...] = _layer_norm(r, lng_ref[...], lnb_ref[...])


def _combine_ln(ys, pos, meta, x2, mod, gate_col, ln_g, ln_b, *, seq_len):
    m, d = x2.shape
    tm = ROUTE_TM
    return pl.pallas_call(
        _combine_body,
        grid=(m // tm,),
        in_specs=[
            pl.BlockSpec(memory_space=pl.ANY),
            pl.BlockSpec(memory_space=pl.ANY),
            pl.BlockSpec((tm, LANES), lambda i: (i, 0)),
            pl.BlockSpec((tm, d), lambda i: (i, 0)),
            pl.BlockSpec((None, 1, d), lambda i: ((i * tm) // seq_len, 0, gate_col)),
            pl.BlockSpec((1, d), lambda i: (0, 0)),
            pl.BlockSpec((1, d), lambda i: (0, 0)),
        ],
        out_specs=pl.BlockSpec((tm, d), lambda i: (i, 0)),
        out_shape=jax.ShapeDtypeStruct((m, d), F32),
        scratch_shapes=[
            pltpu.VMEM((tm * SUBLANES, LANES), F32),
            pltpu.VMEM((tm * SUBLANES, LANES), F32),
            pltpu.SMEM((2 * tm,), jnp.int32),
            pltpu.SemaphoreType.DMA,
            pltpu.SemaphoreType.DMA,
        ],
        compiler_params=_cparams(("arbitrary",)),
        name="moe_combine",
    )(pos, ys, meta, x2, mod, ln_g, ln_b)


def _moe_ln(x, mod, w_router, wg, wu, wd, ln_g, ln_b, *, tf):
    b, l, d = x.shape
    assert d == SUBLANES * LANES
    m = b * l
    x2 = x.reshape(m, d)
    w_r = jnp.zeros((d, LANES), BF16).at[:, :N_EXPERTS].set(w_router.astype(BF16))
    meta, cnt = _route(x2, mod, 3, w_r, seq_len=l)

    counts = cnt[0, :N_EXPERTS].astype(jnp.int32)
    n_tile_e = (counts + MOE_TM - 1) // MOE_TM
    tile_end = jnp.cumsum(n_tile_e)
    offs = (tile_end - n_tile_e) * MOE_TM
    idx = meta[:, META_I1:META_I2 + 1].astype(jnp.int32)
    rank = meta[:, META_R1:META_R2 + 1].astype(jnp.int32)
    base = jnp.sum(jnp.where(idx[..., None] == jnp.arange(N_EXPERTS, dtype=jnp.int32), offs, 0), axis=-1)
    pos = (base + rank).reshape(m // ROUTE_TM, 2 * ROUTE_TM)
    n_tiles = (2 * m) // MOE_TM + N_EXPERTS
    tid = jnp.arange(n_tiles, dtype=jnp.int32)
    tile_valid = (tid < tile_end[-1]).astype(jnp.int32)
    tile_expert = jnp.minimum(jnp.sum((tid[:, None] >= tile_end[None, :]).astype(jnp.int32), axis=1),
                              N_EXPERTS - 1).astype(jnp.int32)
    tile_in = jnp.where(tile_valid > 0, tid, 0).astype(jnp.int32)
    tile_out = tid
    pad_info = jnp.concatenate([offs + counts, n_tile_e * MOE_TM - counts,
                                tile_end[-1:] * MOE_TM]).astype(jnp.int32)
    n_rows = n_tiles * MOE_TM

    xs = _dispatch(x2, mod, 3, pos, pad_info, n_rows, seq_len=l)
    ys = _experts(xs, tile_expert, tile_in, tile_out, tile_valid, wg, wu, wd, tf=tf)
    out = _combine_ln(ys, pos, meta, x2, mod, 5, ln_g, ln_b, seq_len=l)
    return out.reshape(b, l, d)


def _fourier_body(pc_ref, dl_ref, dc_ref, o_ref, t_scr, *, out_scale, row_chunk):
    l = pc_ref.shape[0]
    for g in range(C_GROUPS):
        sl = slice(C_GROUP_DIM * g, C_GROUP_DIM * (g + 1))
        xg = pc_ref[:, sl].astype(F32)
        mu = jnp.mean(xg, axis=-1, keepdims=True)
        dlt = xg - mu
        var = jnp.mean(dlt * dlt, axis=-1, keepdims=True)
        gn = (dlt * lax.rsqrt(var + LN_EPS)).astype(BF16)
        t = jnp.dot(gn, dc_ref[...], preferred_element_type=F32)
        t_scr[0:l, sl] = t[:, 0:C_GROUP_DIM].astype(BF16)
        t_scr[l:2 * l, sl] = t[:, C_GROUP_DIM:2 * C_GROUP_DIM].astype(BF16)
    for r0 in range(0, l, row_chunk):
        acc = jnp.dot(dl_ref[r0:r0 + row_chunk, :], t_scr[...], preferred_element_type=F32)
        o_ref[r0:r0 + row_chunk, :] = (acc * out_scale).astype(BF16)


def _dft_matrices(l, c):
    j = np.arange(l, dtype=np.int64)
    ang_l = (2.0 * np.pi / l) * ((j[:, None] * j[None, :]) % l)
    dl = np.concatenate([np.cos(ang_l), -np.sin(ang_l)], axis=1)
    m = np.arange(c, dtype=np.int64)
    ang_c = (2.0 * np.pi / c) * ((m[:, None] * m[None, :]) % c)
    dc = np.concatenate([np.cos(ang_c), np.sin(ang_c)], axis=1)
    return dl.astype(np.float32), dc.astype(np.float32)


def _fourier_mixer(y, name="fourier"):
    b, l, _ = y.shape
    dl_np, dc_np = _dft_matrices(l, C_GROUP_DIM)
    dl = jnp.asarray(dl_np, dtype=F32).astype(BF16)
    dc = jnp.asarray(dc_np, dtype=F32).astype(BF16)
    out_scale = 1.0 / math.sqrt(l * C_GROUP_DIM)
    return pl.pallas_call(
        functools.partial(_fourier_body, out_scale=out_scale, row_chunk=min(l, 512)),
        grid=(b,),
        in_specs=[
            pl.BlockSpec((None, l, C_WIDTH), lambda bi: (bi, 0, 0)),
            pl.BlockSpec((l, 2 * l), lambda bi: (0, 0), pipeline_mode=pl.Buffered(1)),
            pl.BlockSpec((C_GROUP_DIM, 2 * C_GROUP_DIM), lambda bi: (0, 0)),
        ],
        out_specs=pl.BlockSpec((None, l, C_WIDTH), lambda bi: (bi, 0, 0)),
        out_shape=jax.ShapeDtypeStruct((b, l, C_WIDTH), BF16),
        scratch_shapes=[pltpu.VMEM((2 * l, C_WIDTH), BF16)],
        compiler_params=_cparams(("parallel",)),
        name=name,
    )(y, dl, dc)


NA_HG = 4
NA_GW = NA_HG * NA_DH
NA_WIN = NA_KR * GRID_W


def _natten_body(q_ref, k_ref, v_ref, kc_ref, vc_ref, bias_ref, o_ref, *, rows_per_step, n_rows):
    rb = pl.program_id(2)
    hq = NA_HG * GRID_W
    rid = lax.broadcasted_iota(jnp.int32, (hq, NA_GW), 0)
    cid = lax.broadcasted_iota(jnp.int32, (hq, NA_GW), 1)
    diag = (rid // GRID_W) == (cid // NA_DH)
    cid_o = lax.broadcasted_iota(jnp.int32, (GRID_W, NA_GW), 1)
    kc = kc_ref[...]
    vc = vc_ref[...]
    for j in range(rows_per_step):
        r = rb * rows_per_step + j
        rs = jnp.clip(r - NA_KR // 2, 0, n_rows - NA_KR)
        start = pl.multiple_of(rs * GRID_W, GRID_W)
        tid = jnp.minimum(r, NA_KR // 2) + jnp.maximum(r - (n_rows - NA_KR // 2), 0)
        q_r = q_ref[GRID_W * j:GRID_W * (j + 1), :]
        q4 = jnp.concatenate([q_r] * NA_HG, axis=0)
        qbd = jnp.where(diag, q4, jnp.zeros_like(q4))
        kw = k_ref[pl.ds(start, NA_WIN), :]
        vw = v_ref[pl.ds(start, NA_WIN), :]
        s_loc = lax.dot_general(qbd, kw, NT_DIMS, preferred_element_type=F32) + bias_ref[tid]
        s_ctx = lax.dot_general(qbd, kc, NT_DIMS, preferred_element_type=F32)
        m = jnp.maximum(jnp.max(s_loc, axis=-1, keepdims=True), jnp.max(s_ctx, axis=-1, keepdims=True))
        e_loc = jnp.exp2(s_loc - m)
        e_ctx = jnp.exp2(s_ctx - m)
        tot = jnp.sum(e_loc, axis=-1, keepdims=True) + jnp.sum(e_ctx, axis=-1, keepdims=True)
        o = (jnp.dot(e_loc.astype(BF16), vw, preferred_element_type=F32)
             + jnp.dot(e_ctx.astype(BF16), vc, preferred_element_type=F32))
        o = o * (1.0 / tot)
        out = jnp.zeros((GRID_W, NA_GW), F32)
        for hh in range(NA_HG):
            out = out + jnp.where((cid_o // NA_DH) == hh, o[GRID_W * hh:GRID_W * (hh + 1), :], 0.0)
        o_ref[GRID_W * j:GRID_W * (j + 1), :] = out.astype(BF16)


def _na_bias_table(rpb, n_rows):
    h = rpb.shape[0]
    cols = jnp.arange(GRID_W)
    col_start = jnp.clip(cols - NA_KC // 2, 0, GRID_W - NA_KC)
    col_valid = (cols[None, :] >= col_start[:, None]) & (cols[None, :] < col_start[:, None] + NA_KC)
    dc_idx = jnp.clip(cols[None, :] - cols[:, None] + NA_KC - 1, 0, 2 * NA_KC - 2)
    rpb_c = rpb[:, :, dc_idx].astype(F32)
    half = NA_KR // 2
    rep_rows = list(range(half)) + [half] + list(range(n_rows - half + 1, n_rows))
    tabs = []
    for r in rep_rows:
        rs = min(max(r - half, 0), n_rows - NA_KR)
        dr_idx = rs + np.arange(NA_KR) - r + NA_KR - 1
        bias = rpb_c[:, dr_idx].transpose(0, 2, 1, 3)
        bias = jnp.where(col_valid[None, :, None, :], bias * LOG2E, NEG_INF)
        tabs.append(bias.reshape(h, GRID_W, NA_WIN))
    tab = jnp.stack(tabs, axis=0)
    return tab.reshape(len(rep_rows), h // NA_HG, NA_HG * GRID_W, NA_WIN)


def _natten(y, q_col0, k_col0, v_col0, y_ctx, kc_col0, vc_col0, bias_tab, *, rows_per_step, name="natten"):
    b, l, _ = y.shape
    lc = y_ctx.shape[1]
    n_rows = l // GRID_W
    n_tab = bias_tab.shape[0]
    n_grp = NA_HEADS // NA_HG
    tq = rows_per_step * GRID_W
    qb, kb, vb = q_col0 // NA_GW, k_col0 // NA_GW, v_col0 // NA_GW
    kcb, vcb = kc_col0 // NA_GW, vc_col0 // NA_GW
    return pl.pallas_call(
        functools.partial(_natten_body, rows_per_step=rows_per_step, n_rows=n_rows),
        grid=(b, n_grp, n_rows // rows_per_step),
        in_specs=[
            pl.BlockSpec((None, tq, NA_GW), lambda bi, g, i: (bi, i, qb + g)),
            pl.BlockSpec((None, l, NA_GW), lambda bi, g, i: (bi, 0, kb + g)),
            pl.BlockSpec((None, l, NA_GW), lambda bi, g, i: (bi, 0, vb + g)),
            pl.BlockSpec((None, lc, NA_GW), lambda bi, g, i: (bi, 0, kcb + g)),
            pl.BlockSpec((None, lc, NA_GW), lambda bi, g, i: (bi, 0, vcb + g)),
            pl.BlockSpec((n_tab, None, NA_HG * GRID_W, NA_WIN), lambda bi, g, i: (0, g, 0, 0)),
        ],
        out_specs=pl.BlockSpec((None, tq, NA_GW), lambda bi, g, i: (bi, i, g)),
        out_shape=jax.ShapeDtypeStruct((b, l, D_WIDTH), BF16),
        compiler_params=_cparams(("parallel", "parallel", "arbitrary")),
        name=name,
    )(y, y, y, y_ctx, y_ctx, bias_tab)


def _rope_tables(l):
    t = jnp.arange(l, dtype=jnp.int32)
    row = (t // GRID_W).astype(F32)
    col = (t % GRID_W).astype(F32)
    n_freq = DIFF_DH // 4
    inv_freq = ROPE_THETA ** (-jnp.arange(n_freq, dtype=F32) / n_freq)
    ang = jnp.concatenate([row[:, None] * inv_freq, col[:, None] * inv_freq], axis=-1)
    c, s = jnp.cos(ang), jnp.sin(ang)
    cos = jnp.tile(jnp.concatenate([c, c], axis=-1), (1, 2))
    sin = jnp.tile(jnp.concatenate([-s, s], axis=-1), (1, 2))
    return cos, sin


def kernel(x, c, ctx, c_ctx, w_mod, b_mod, ln_g, ln_b, e_w_in, e_conv, e_lam_q1, e_lam_k1, e_lam_q2, e_lam_k2, e_subln_g, e_w_o, e_ffn_gate, e_ffn_up, e_ffn_down, o_w_in, o_rpb, o_w_o, o_router, o_exp_gate, o_exp_up, o_exp_down):
    b, l, d = x.shape
    lc = ctx.shape[1]
    assert d == D_MODEL and l % 512 == 0 and lc % 256 == 0 and b + 1 <= MOD_ROWS

    cond = jnp.concatenate([c, c_ctx[None, :], jnp.zeros((MOD_ROWS - b - 1, d), F32)], axis=0)
    mods = _adaln(cond, w_mod, b_mod)

    def layer_mods(i):
        lat = mods[i, :b][:, None, :]
        cx = jnp.broadcast_to(mods[i, b][None, None, :], (b, 1, 6 * d))
        return lat, cx

    q_scale_diff = DIFF_DH ** -0.5 * LOG2E
    q_scale_na = NA_DH ** -0.5 * LOG2E

    mod_lat, mod_ctx = layer_mods(0)
    lam_init = 0.8 - 0.6 * math.exp(-0.3 * 0)
    a_end = 3 * A_WIDTH
    w_in = e_w_in[0]
    w_main = w_in[:, :a_end + 2 * DIFF_QK].astype(BF16)
    w_vt = w_in[:, a_end + 2 * DIFF_QK:].T.astype(BF16)
    rope = _rope_tables(l)
    pa_chunks = [(0, 512, "plain"), (512, 512, "plain"), (1024, 512, "plain")]
    y_lat, vt_lat = _proj(x, mod_lat, 0, w_main,
                          pa_chunks + [(a_end, 512, "rope_scale"), (a_end + 512, 512, "rope")],
                          tm=512, wvt=w_vt, rope=rope, q_scale=q_scale_diff, name="even_inproj_lat")
    y_ctx, vt_ctx = _proj(ctx, mod_ctx, 0, w_main,
                          pa_chunks + [(a_end, 512, "scale"), (a_end + 512, 512, "plain")],
                          tm=lc, wvt=w_vt, q_scale=q_scale_diff, name="even_inproj_ctx")

    lam_pack = jnp.zeros((8, LANES), F32)
    lam_pack = lam_pack.at[0, :DIFF_DH].set(e_lam_q1[0]).at[1, :DIFF_DH].set(e_lam_k1[0])
    lam_pack = lam_pack.at[2, :DIFF_DH].set(e_lam_q2[0]).at[3, :DIFF_DH].set(e_lam_k2[0])
    sub_g = e_subln_g[0].reshape(1, DIFF_VDIM)
    o_lat = _diff_attention(y_lat, [(y_ctx, vt_ctx), (y_lat, vt_lat)], lam_pack, sub_g, lam_init,
                            tq=256, q_col0=a_end, k_col0=a_end + DIFF_QK, name="diffattn_lat")
    o_ctx = _diff_attention(y_ctx, [(y_ctx, vt_ctx)], lam_pack, sub_g, lam_init,
                            tq=lc, q_col0=a_end, k_col0=a_end + DIFF_QK, name="diffattn_ctx")
    cv_lat = _conv_mixer(y_lat, e_conv[0], "conv_lat")
    cv_ctx = _conv_mixer(y_ctx, e_conv[0], "conv_ctx")

    w_o = e_w_o[0].astype(BF16)
    lng0, lnb0 = ln_g[0, 0][None, :], ln_b[0, 0][None, :]
    lng1, lnb1 = ln_g[0, 1][None, :], ln_b[0, 1][None, :]
    x_lat = _outproj_ln(cv_lat, 0, o_lat, 0, w_o, x, mod_lat, 2, lng0, lnb0, tm=512, name="even_outproj_lat")
    x_ctx = _outproj_ln(cv_ctx, 0, o_ctx, 0, w_o, ctx, mod_ctx, 2, lng0, lnb0, tm=lc, name="even_outproj_ctx")

    wg = e_ffn_gate[0].astype(BF16)
    wu = e_ffn_up[0].astype(BF16)
    wd = e_ffn_down[0].astype(BF16)
    x_lat = _ffn_ln(x_lat, mod_lat, 3, wg, wu, wd, lng1, lnb1, tm=512, tf=1408, name="ffn_lat")
    x_ctx = _ffn_ln(x_ctx, mod_ctx, 3, wg, wu, wd, lng1, lnb1, tm=lc, tf=1408, name="ffn_ctx")

    mod_lat, mod_ctx = layer_mods(1)
    w_in = o_w_in[0].astype(BF16)
    y_lat = _proj(x_lat, mod_lat, 0, w_in,
                  [(0, 512, "plain"), (512, 512, "scale"), (1024, 512, "plain"), (1536, 512, "plain")],
                  tm=512, q_scale=q_scale_na, name="odd_inproj_lat")
    y_ctx = _proj(x_ctx, mod_ctx, 0, w_in[:, C_WIDTH + D_WIDTH:],
                  [(0, 512, "plain"), (512, 512, "plain")], tm=lc, name="odd_inproj_ctx")
    f_lat = _fourier_mixer(y_lat)
    bias_tab = _na_bias_table(o_rpb[0], l // GRID_W)
    n_lat = _natten(y_lat, C_WIDTH, C_WIDTH + D_WIDTH, C_WIDTH + 2 * D_WIDTH, y_ctx, 0, D_WIDTH,
                    bias_tab, rows_per_step=4)

    w_o = o_w_o[0].astype(BF16)
    lng0, lnb0 = ln_g[1, 0][None, :], ln_b[1, 0][None, :]
    lng1, lnb1 = ln_g[1, 1][None, :], ln_b[1, 1][None, :]
    x_lat = _outproj_ln(f_lat, 0, n_lat, 0, w_o, x_lat, mod_lat, 2, lng0, lnb0, tm=512, name="odd_outproj_lat")

    return _moe_ln(x_lat, mod_lat, o_router[0], o_exp_gate[0], o_exp_up[0], o_exp_down[0], lng1, lnb1, tf=512)
```

```python
import functools
import math

import numpy as np
import jax
import jax.numpy as jnp
from jax import lax
from jax.experimental import pallas as pl
from jax.experimental.pallas import tpu as pltpu

F32 = jnp.float32
BF16 = jnp.bfloat16

D_MODEL = 1024
GRID_W = 64
DEPTH = 2

A_WIDTH = 512
DIFF_HEADS = 4
DIFF_DH = 64
DIFF_VDIM = 128
DIFF_QK = 512
B_WIDTH = 512

C_WIDTH = 512
C_GROUPS = 4
C_GROUP_DIM = 128
NA_HEADS = 8
NA_DH = 64
D_WIDTH = 512
NA_KR = 8
NA_KC = 16

N_EXPERTS = 8

ROPE_THETA = 10000.0
LN_EPS = 1e-5
RMS_EPS = 1e-5
NEG_INF = -1e30
DEEPNORM_ALPHA = (2 * DEPTH) ** 0.25
LOG2E = 1.4426950408889634

LANES = 128
SUBLANES = 8
MOD_ROWS = 32
NT_DIMS = (((1,), (1,)), ((), ()))


def _cparams(sem, vmem_mb=48):
    return pltpu.CompilerParams(dimension_semantics=sem, vmem_limit_bytes=vmem_mb * 1024 * 1024)


def _layer_norm(r, g, b):
    mu = jnp.mean(r, axis=-1, keepdims=True)
    d = r - mu
    var = jnp.mean(d * d, axis=-1, keepdims=True)
    return d * lax.rsqrt(var + LN_EPS) * g + b


def _adaln_body(c_ref, w_ref, b_ref, o_ref):
    cnd = c_ref[...]
    s = (cnd * jax.nn.sigmoid(cnd)).astype(BF16)
    o_ref[...] = jnp.dot(s, w_ref[...].astype(BF16), preferred_element_type=F32) + b_ref[...]


def _adaln(cond, w_mod, b_mod):
    depth, d, n = w_mod.shape
    tn = 1536
    return pl.pallas_call(
        _adaln_body,
        grid=(depth, n // tn),
        in_specs=[
            pl.BlockSpec((MOD_ROWS, d), lambda l, j: (0, 0)),
            pl.BlockSpec((None, d, tn), lambda l, j: (l, 0, j)),
            pl.BlockSpec((None, 1, tn), lambda l, j: (l, 0, j)),
        ],
        out_specs=pl.BlockSpec((None, MOD_ROWS, tn), lambda l, j: (l, 0, j)),
        out_shape=jax.ShapeDtypeStruct((depth, MOD_ROWS, n), F32),
        compiler_params=_cparams(("parallel", "parallel")),
        name="adaln",
    )(cond, w_mod, b_mod.reshape(depth, 1, n))


def _proj_body(*refs, chunks, with_vt, with_rope, q_scale):
    x_ref, sh_ref, sc_ref, w_ref = refs[:4]
    i = 4
    if with_vt:
        wvt_ref = refs[i]
        i += 1
    if with_rope:
        cos_ref, sin_ref = refs[i], refs[i + 1]
        i += 2
    y_ref = refs[i]
    vt_ref = refs[i + 1] if with_vt else None

    h = (x_ref[...] * (1.0 + sc_ref[...]) + sh_ref[...]).astype(BF16)
    tm = h.shape[0]
    if with_rope:
        cos = cos_ref[...]
        sin = sin_ref[...]
        lane = lax.broadcasted_iota(jnp.int32, (tm, LANES), 1)
        low_half = (lane % 64) < 32
    for (c0, width, kind) in chunks:
        acc = jnp.dot(h, w_ref[:, c0:c0 + width], preferred_element_type=F32)
        if kind in ("rope", "rope_scale"):
            for j in range(width // LANES):
                a = acc[:, LANES * j:LANES * (j + 1)]
                rot = jnp.where(low_half, pltpu.roll(a, 96, 1), pltpu.roll(a, 32, 1))
                r = a * cos + rot * sin
                if kind == "rope_scale":
                    r = r * q_scale
                y_ref[:, c0 + LANES * j:c0 + LANES * (j + 1)] = r.astype(BF16)
        elif kind == "scale":
            y_ref[:, c0:c0 + width] = (acc * q_scale).astype(BF16)
        else:
            y_ref[:, c0:c0 + width] = acc.astype(BF16)
    if with_vt:
        vt = lax.dot_general(wvt_ref[...], h, NT_DIMS, preferred_element_type=F32)
        vt_ref[...] = vt.astype(BF16)


def _proj(x, mod, shift_col, w, chunks, *, tm, wvt=None, rope=None, q_scale=1.0, name="proj"):
    b, l, d = x.shape
    n = w.shape[1]
    with_vt = wvt is not None
    with_rope = rope is not None
    in_specs = [
        pl.BlockSpec((None, tm, d), lambda bi, i: (bi, i, 0)),
        pl.BlockSpec((None, 1, d), lambda bi, i: (bi, 0, shift_col)),
        pl.BlockSpec((None, 1, d), lambda bi, i: (bi, 0, shift_col + 1)),
        pl.BlockSpec((d, n), lambda bi, i: (0, 0)),
    ]
    args = [x, mod, mod, w]
    if with_vt:
        nv = wvt.shape[0]
        in_specs.append(pl.BlockSpec((nv, d), lambda bi, i: (0, 0)))
        args.append(wvt)
    if with_rope:
        in_specs += [pl.BlockSpec((tm, LANES), lambda bi, i: (i, 0))] * 2
        args += [rope[0], rope[1]]
    out_specs = [pl.BlockSpec((None, tm, n), lambda bi, i: (bi, i, 0))]
    out_shape = [jax.ShapeDtypeStruct((b, l, n), BF16)]
    if with_vt:
        out_specs.append(pl.BlockSpec((None, nv, tm), lambda bi, i: (bi, 0, i)))
        out_shape.append(jax.ShapeDtypeStruct((b, nv, l), BF16))
    res = pl.pallas_call(
        functools.partial(_proj_body, chunks=tuple(chunks), with_vt=with_vt, with_rope=with_rope,
                          q_scale=q_scale),
        grid=(b, l // tm),
        in_specs=in_specs,
        out_specs=out_specs,
        out_shape=out_shape,
        compiler_params=_cparams(("parallel", "parallel")),
        name=name,
    )(*args)
    return res if with_vt else res[0]


DIFF_HP = 4


def _diffattn_body(*refs, n_seg, lam_init):
    q_ref, lam_ref, g_ref = refs[:3]
    k_refs = [refs[3 + 2 * s] for s in range(n_seg)]
    vt_refs = [refs[4 + 2 * s] for s in range(n_seg)]
    o_ref = refs[3 + 2 * n_seg]

    tq = q_ref.shape[0]
    lane = lax.broadcasted_iota(jnp.int32, (tq, LANES), 1)
    lp = lam_ref[...]
    lam = (jnp.exp(jnp.sum(lp[0:1] * lp[1:2], axis=1, keepdims=True))
           - jnp.exp(jnp.sum(lp[2:3] * lp[3:4], axis=1, keepdims=True)) + lam_init)

    scores = []
    for h in range(DIFF_HP):
        cols = slice(LANES * h, LANES * (h + 1))
        q = q_ref[:, cols]
        zero = jnp.zeros_like(q)
        per_comp = []
        for qm in (jnp.where(lane < DIFF_DH, q, zero), jnp.where(lane >= DIFF_DH, q, zero)):
            per_comp.append([lax.dot_general(k_ref[:, cols], qm, NT_DIMS, preferred_element_type=F32)
                             for k_ref in k_refs])
        scores.append(per_comp)

    def unnormalised(s, rows):
        m = functools.reduce(jnp.maximum, [jnp.max(x, axis=0, keepdims=True) for x in s])
        tot = None
        acc = None
        for x, vt_ref in zip(s, vt_refs):
            e = jnp.exp2(x - m)
            t = jnp.sum(e, axis=0, keepdims=True)
            pv = jnp.dot(vt_ref[rows, :], e.astype(BF16), preferred_element_type=F32)
            tot = t if tot is None else tot + t
            acc = pv if acc is None else acc + pv
        return acc, tot

    for h in range(DIFF_HP):
        cols = slice(LANES * h, LANES * (h + 1))
        rows = slice(DIFF_VDIM * h, DIFF_VDIM * (h + 1))
        acc1, l1 = unnormalised(scores[h][0], rows)
        acc2, l2 = unnormalised(scores[h][1], rows)
        o_t = acc1 * (1.0 / l1) - acc2 * (lam / l2)
        o = o_t.T
        ms = jnp.mean(o * o, axis=-1, keepdims=True)
        o_ref[:, cols] = (o * lax.rsqrt(ms + RMS_EPS) * g_ref[...] * (1.0 - lam_init)).astype(BF16)


def _diff_attention(yq, segs, lam_pack, sub_g, lam_init, *, tq, q_col0, k_col0, name):
    b, lq, _ = yq.shape
    width = DIFF_HP * LANES
    qb0 = q_col0 // width
    kb0 = k_col0 // width
    in_specs = [
        pl.BlockSpec((None, tq, width), lambda bi, h, i: (bi, i, qb0 + h)),
        pl.BlockSpec((8, LANES), lambda bi, h, i: (0, 0)),
        pl.BlockSpec((1, LANES), lambda bi, h, i: (0, 0)),
    ]
    args = [yq, lam_pack, sub_g]
    for (yk, vt) in segs:
        lk = yk.shape[1]
        in_specs.append(pl.BlockSpec((None, lk, width), lambda bi, h, i: (bi, 0, kb0 + h)))
        in_specs.append(pl.BlockSpec((None, DIFF_HP * DIFF_VDIM, lk), lambda bi, h, i: (bi, h, 0)))
        args += [yk, vt]
    return pl.pallas_call(
        functools.partial(_diffattn_body, n_seg=len(segs), lam_init=lam_init),
        grid=(b, DIFF_HEADS // DIFF_HP, lq // tq),
        in_specs=in_specs,
        out_specs=pl.BlockSpec((None, tq, width), lambda bi, h, i: (bi, i, h)),
        out_shape=jax.ShapeDtypeStruct((b, lq, B_WIDTH), BF16),
        compiler_params=_cparams(("parallel", "parallel", "arbitrary")),
        name=name,
    )(*args)


def _conv_body(bg_ref, cg_ref, val_ref, w_ref, o_ref):
    l = o_ref.shape[0]
    row = lax.broadcasted_iota(jnp.int32, (l, LANES), 0)
    for j in range(A_WIDTH // LANES):
        sl = slice(LANES * j, LANES * (j + 1))
        u = cg_ref[:, sl].astype(F32) * val_ref[:, sl].astype(F32)
        u_prev = jnp.where(row == 0, 0.0, pltpu.roll(u, 1, 0))
        u_next = jnp.where(row == l - 1, 0.0, pltpu.roll(u, l - 1, 0))
        w = w_ref[:, sl]
        y = bg_ref[:, sl].astype(F32) * (u_prev * w[0:1] + u * w[1:2] + u_next * w[2:3])
        o_ref[:, sl] = y.astype(BF16)


def _conv_mixer(y, w_conv, name):
    b, l, _ = y.shape
    return pl.pallas_call(
        _conv_body,
        grid=(b,),
        in_specs=[
            pl.BlockSpec((None, l, A_WIDTH), lambda bi: (bi, 0, 0)),
            pl.BlockSpec((None, l, A_WIDTH), lambda bi: (bi, 0, 1)),
            pl.BlockSpec((None, l, A_WIDTH), lambda bi: (bi, 0, 2)),
            pl.BlockSpec((3, A_WIDTH), lambda bi: (0, 0)),
        ],
        out_specs=pl.BlockSpec((None, l, A_WIDTH), lambda bi: (bi, 0, 0)),
        out_shape=jax.ShapeDtypeStruct((b, l, A_WIDTH), BF16),
        compiler_params=_cparams(("parallel",)),
        name=name,
    )(y, y, y, w_conv)


def _outproj_body(a_ref, b_ref, w_ref, x_ref, gate_ref, lng_ref, lnb_ref, o_ref):
    half = a_ref.shape[1]
    y = (jnp.dot(a_ref[...], w_ref[0:half, :], preferred_element_type=F32)
         + jnp.dot(b_ref[...], w_ref[half:2 * half, :], preferred_element_type=F32))
    r = DEEPNORM_ALPHA * x_ref[...] + gate_ref[...] * y
    o_ref[...] = _layer_norm(r, lng_ref[...], lnb_ref[...])


def _outproj_ln(a, a_colblk, bsrc, b_colblk, w_o, x, mod, gate_col, ln_g, ln_b, *, tm, name):
    b, l, d = x.shape
    half = w_o.shape[0] // 2
    return pl.pallas_call(
        _outproj_body,
        grid=(b, l // tm),
        in_specs=[
            pl.BlockSpec((None, tm, half), lambda bi, i: (bi, i, a_colblk)),
            pl.BlockSpec((None, tm, half), lambda bi, i: (bi, i, b_colblk)),
            pl.BlockSpec((2 * half, d), lambda bi, i: (0, 0)),
            pl.BlockSpec((None, tm, d), lambda bi, i: (bi, i, 0)),
            pl.BlockSpec((None, 1, d), lambda bi, i: (bi, 0, gate_col)),
            pl.BlockSpec((1, d), lambda bi, i: (0, 0)),
            pl.BlockSpec((1, d), lambda bi, i: (0, 0)),
        ],
        out_specs=pl.BlockSpec((None, tm, d), lambda bi, i: (bi, i, 0)),
        out_shape=jax.ShapeDtypeStruct((b, l, d), F32),
        compiler_params=_cparams(("parallel", "parallel")),
        name=name,
    )(a, bsrc, w_o, x, mod, ln_g, ln_b)


def _ffn_body(x_ref, sh_ref, sc_ref, gate_ref, wg_ref, wu_ref, wd_ref, lng_ref, lnb_ref,
              o_ref, h_scr, acc_scr, *, n_f):
    f = pl.program_id(2)

    @pl.when(f == 0)
    def _():
        h_scr[...] = (x_ref[...] * (1.0 + sc_ref[...]) + sh_ref[...]).astype(BF16)
        acc_scr[...] = jnp.zeros_like(acc_scr)

    h = h_scr[...]
    g = jnp.dot(h, wg_ref[...], preferred_element_type=F32)
    u = jnp.dot(h, wu_ref[...], preferred_element_type=F32)
    a = (g * jax.nn.sigmoid(g) * u).astype(BF16)
    acc_scr[...] += jnp.dot(a, wd_ref[...], preferred_element_type=F32)

    @pl.when(f == n_f - 1)
    def _():
        r = DEEPNORM_ALPHA * x_ref[...] + gate_ref[...] * acc_scr[...]
        o_ref[...] = _layer_norm(r, lng_ref[...], lnb_ref[...])


def _ffn_ln(x, mod, shift_col, wg, wu, wd, ln_g, ln_b, *, tm, tf, name="ffn"):
    b, l, d = x.shape
    ff = wg.shape[1]
    n_f = ff // tf
    return pl.pallas_call(
        functools.partial(_ffn_body, n_f=n_f),
        grid=(b, l // tm, n_f),
        in_specs=[
            pl.BlockSpec((None, tm, d), lambda bi, i, f: (bi, i, 0)),
            pl.BlockSpec((None, 1, d), lambda bi, i, f: (bi, 0, shift_col)),
            pl.BlockSpec((None, 1, d), lambda bi, i, f: (bi, 0, shift_col + 1)),
            pl.BlockSpec((None, 1, d), lambda bi, i, f: (bi, 0, shift_col + 2)),
            pl.BlockSpec((d, tf), lambda bi, i, f: (0, f)),
            pl.BlockSpec((d, tf), lambda bi, i, f: (0, f)),
            pl.BlockSpec((tf, d), lambda bi, i, f: (f, 0)),
            pl.BlockSpec((1, d), lambda bi, i, f: (0, 0)),
            pl.BlockSpec((1, d), lambda bi, i, f: (0, 0)),
        ],
        out_specs=pl.BlockSpec((None, tm, d), lambda bi, i, f: (bi, i, 0)),
        out_shape=jax.ShapeDtypeStruct((b, l, d), F32),
        scratch_shapes=[pltpu.VMEM((tm, d), BF16), pltpu.VMEM((tm, d), F32)],
        compiler_params=_cparams(("parallel", "parallel", "arbitrary")),
        name=name,
    )(x, mod, mod, mod, wg, wu, wd, ln_g, ln_b)


MOE_TM = 1024
ROUTE_TM = 512
ZERO_ROWS = 256
META_I1, META_I2, META_R1, META_R2, META_P1, META_P2 = range(6)


def _route_body(x_ref, sh_ref, sc_ref, wr_ref, meta_ref, cnt_ref, carry_scr):
    @pl.when(pl.program_id(0) == 0)
    def _():
        carry_scr[...] = jnp.zeros_like(carry_scr)

    h = (x_ref[...] * (1.0 + sc_ref[...]) + sh_ref[...]).astype(BF16)
    logits = jnp.dot(h, wr_ref[...], preferred_element_type=F32)
    tm = logits.shape[0]
    lane = lax.broadcasted_iota(jnp.int32, logits.shape, 1).astype(F32)
    l1 = jnp.where(lane < N_EXPERTS, logits, -jnp.inf)
    v1 = jnp.max(l1, axis=-1, keepdims=True)
    i1 = jnp.min(jnp.where(l1 == v1, lane, float(LANES)), axis=-1, keepdims=True)
    l2 = jnp.where(lane == i1, -jnp.inf, l1)
    v2 = jnp.max(l2, axis=-1, keepdims=True)
    i2 = jnp.min(jnp.where(l2 == v2, lane, float(LANES)), axis=-1, keepdims=True)
    t = jnp.exp(v2 - v1)
    p1 = 1.0 / (1.0 + t)
    p2 = t / (1.0 + t)

    member = jnp.where(lane == i1, 1.0, jnp.where(lane == i2, 1.0, 0.0))
    rr = lax.broadcasted_iota(jnp.int32, (tm, tm), 0)
    cc = lax.broadcasted_iota(jnp.int32, (tm, tm), 1)
    earlier = jnp.where(cc < rr, 1.0, 0.0).astype(BF16)
    base = carry_scr[0:1, :]
    rank = jnp.dot(earlier, member.astype(BF16), preferred_element_type=F32) + base
    total = base + jnp.sum(member, axis=0, keepdims=True)
    carry_scr[0:1, :] = total
    r1 = jnp.sum(jnp.where(lane == i1, rank, 0.0), axis=-1, keepdims=True)
    r2 = jnp.sum(jnp.where(lane == i2, rank, 0.0), axis=-1, keepdims=True)

    meta = jnp.zeros_like(logits)
    for k, val in ((META_I1, i1), (META_I2, i2), (META_R1, r1), (META_R2, r2), (META_P1, p1), (META_P2, p2)):
        meta = jnp.where(lane == float(k), val, meta)
    meta_ref[...] = meta
    cnt_ref[...] = jnp.broadcast_to(total, cnt_ref.shape)


def _route(x2, mod, shift_col, w_router_pad, *, seq_len):
    m, d = x2.shape
    tm = ROUTE_TM
    return pl.pallas_call(
        _route_body,
        grid=(m // tm,),
        in_specs=[
            pl.BlockSpec((tm, d), lambda i: (i, 0)),
            pl.BlockSpec((None, 1, d), lambda i: ((i * tm) // seq_len, 0, shift_col)),
            pl.BlockSpec((None, 1, d), lambda i: ((i * tm) // seq_len, 0, shift_col + 1)),
            pl.BlockSpec((d, LANES), lambda i: (0, 0)),
        ],
        out_specs=[pl.BlockSpec((tm, LANES), lambda i: (i, 0)),
                   pl.BlockSpec((8, LANES), lambda i: (0, 0))],
        out_shape=[jax.ShapeDtypeStruct((m, LANES), F32), jax.ShapeDtypeStruct((8, LANES), F32)],
        scratch_shapes=[pltpu.VMEM((8, LANES), F32)],
        compiler_params=_cparams(("arbitrary",)),
        name="moe_route",
    )(x2, mod, mod, w_router_pad)


def _to_tiles(tile_ref, value):
    n = value.shape[0]
    for j in range(SUBLANES):
        tile_ref[pl.ds(j, n, stride=SUBLANES), :] = value[:, LANES * j:LANES * (j + 1)]


def _from_tiles(tile_ref):
    n = tile_ref.shape[0] // SUBLANES
    return jnp.concatenate([tile_ref[pl.ds(j, n, stride=SUBLANES), :] for j in range(SUBLANES)], axis=-1)


def _tile_rows(ref, row, n=1):
    return ref.at[pl.ds(pl.multiple_of(row * SUBLANES, SUBLANES), n * SUBLANES), :]


def _row_copy(src, src_row, dst, dst_row, sem):
    return pltpu.make_async_copy(_tile_rows(src, src_row), _tile_rows(dst, dst_row), sem)


def _rows_wait(src, dst, n, sem):
    pltpu.make_async_copy(_tile_rows(src, 0, n), _tile_rows(dst, 0, n), sem).wait()


def _dispatch_body(pad_ref, pos_hbm, x_ref, sh_ref, sc_ref, xs_hbm, h_scr, z_scr, pos_smem, pos_sem, row_sem):
    step = pl.program_id(0)
    tm = x_ref.shape[0]
    pos_cp = pltpu.make_async_copy(pos_hbm.at[step], pos_smem, pos_sem)
    pos_cp.start()
    _to_tiles(h_scr, x_ref[...] * (1.0 + sc_ref[...]) + sh_ref[...])
    pos_cp.wait()

    def issue(r, carry):
        _row_copy(h_scr, r, xs_hbm, pos_smem[2 * r], row_sem).start(priority=0)
        _row_copy(h_scr, r, xs_hbm, pos_smem[2 * r + 1], row_sem).start(priority=1)
        return carry

    lax.fori_loop(0, tm, issue, 0, unroll=8)
    _rows_wait(h_scr, xs_hbm, tm, row_sem)
    _rows_wait(h_scr, xs_hbm, tm, row_sem)

    @pl.when(step == pl.num_programs(0) - 1)
    def _():
        z_scr[...] = jnp.zeros_like(z_scr)
        for e in range(N_EXPERTS):
            start = pad_ref[e]
            count = pad_ref[N_EXPERTS + e]

            def fill(k, carry, start=start):
                _row_copy(z_scr, 0, xs_hbm, start + k, row_sem).start()
                return carry

            def fill_wait(k, carry):
                _row_copy(z_scr, 0, xs_hbm, 0, row_sem).wait()
                return carry

            lax.fori_loop(0, count, fill, 0)
            lax.fori_loop(0, count, fill_wait, 0)

        zrows = z_scr.shape[0] // SUBLANES
        used_rows = pad_ref[2 * N_EXPERTS]
        n_chunks = (xs_hbm.shape[0] // SUBLANES - used_rows) // zrows

        def chunk_copy(k):
            return pltpu.make_async_copy(z_scr, _tile_rows(xs_hbm, used_rows + k * zrows, zrows), row_sem)

        def fill_chunk(k, carry):
            chunk_copy(k).start()
            return carry

        def fill_chunk_wait(k, carry):
            chunk_copy(k).wait()
            return carry

        lax.fori_loop(0, n_chunks, fill_chunk, 0)
        lax.fori_loop(0, n_chunks, fill_chunk_wait, 0)


def _dispatch(x2, mod, shift_col, pos, pad_info, n_rows, *, seq_len):
    m, d = x2.shape
    tm = ROUTE_TM
    return pl.pallas_call(
        _dispatch_body,
        grid_spec=pltpu.PrefetchScalarGridSpec(
            num_scalar_prefetch=1,
            grid=(m // tm,),
            in_specs=[
                pl.BlockSpec(memory_space=pl.ANY),
                pl.BlockSpec((tm, d), lambda i, pad: (i, 0)),
                pl.BlockSpec((None, 1, d), lambda i, pad: ((i * tm) // seq_len, 0, shift_col)),
                pl.BlockSpec((None, 1, d), lambda i, pad: ((i * tm) // seq_len, 0, shift_col + 1)),
            ],
            out_specs=pl.BlockSpec(memory_space=pl.ANY),
            scratch_shapes=[
                pltpu.VMEM((tm * SUBLANES, LANES), F32),
                pltpu.VMEM((ZERO_ROWS * SUBLANES, LANES), F32),
                pltpu.SMEM((2 * tm,), jnp.int32),
                pltpu.SemaphoreType.DMA,
                pltpu.SemaphoreType.DMA,
            ],
        ),
        out_shape=jax.ShapeDtypeStruct((n_rows * SUBLANES, LANES), F32),
        compiler_params=_cparams(("arbitrary",)),
        name="moe_dispatch",
    )(pad_info, pos, x2, mod, mod)


def _experts_body(te_ref, tbi_ref, tbo_ref, tv_ref, xs_ref, wg_ref, wu_ref, wd_ref, ys_ref, h_scr, acc_scr, *, n_f):
    t = pl.program_id(0)
    f = pl.program_id(1)

    @pl.when(f == 0)
    def _():
        h_scr[...] = _from_tiles(xs_ref).astype(BF16)
        acc_scr[...] = jnp.zeros_like(acc_scr)

    @pl.when(tv_ref[t] > 0)
    def _():
        h = h_scr[...]
        g = jnp.dot(h, wg_ref[...].astype(BF16), preferred_element_type=F32)
        u = jnp.dot(h, wu_ref[...].astype(BF16), preferred_element_type=F32)
        a = (g * jax.nn.sigmoid(g) * u).astype(BF16)
        acc_scr[...] += jnp.dot(a, wd_ref[...].astype(BF16), preferred_element_type=F32)

    @pl.when(f == n_f - 1)
    def _():
        _to_tiles(ys_ref, acc_scr[...])


def _experts(xs, tile_expert, tile_in, tile_out, tile_valid, wg, wu, wd, *, tf):
    n_rows = xs.shape[0] // SUBLANES
    d = wg.shape[1]
    tm = MOE_TM
    n_tiles = tile_expert.shape[0]
    ff = wg.shape[2]
    n_f = ff // tf
    return pl.pallas_call(
        functools.partial(_experts_body, n_f=n_f),
        grid_spec=pltpu.PrefetchScalarGridSpec(
            num_scalar_prefetch=4,
            grid=(n_tiles, n_f),
            in_specs=[
                pl.BlockSpec((tm * SUBLANES, LANES), lambda t, f, te, tbi, tbo, tv: (tbi[t], 0)),
                pl.BlockSpec((None, d, tf), lambda t, f, te, tbi, tbo, tv: (te[t], 0, f * tv[t])),
                pl.BlockSpec((None, d, tf), lambda t, f, te, tbi, tbo, tv: (te[t], 0, f * tv[t])),
                pl.BlockSpec((None, tf, d), lambda t, f, te, tbi, tbo, tv: (te[t], f * tv[t], 0)),
            ],
            out_specs=pl.BlockSpec((tm * SUBLANES, LANES), lambda t, f, te, tbi, tbo, tv: (tbo[t], 0)),
            scratch_shapes=[pltpu.VMEM((tm, d), BF16), pltpu.VMEM((tm, d), F32)],
        ),
        out_shape=jax.ShapeDtypeStruct((n_rows * SUBLANES, LANES), F32),
        compiler_params=_cparams(("arbitrary", "arbitrary")),
        name="moe_experts",
    )(tile_expert, tile_in, tile_out, tile_valid, xs, wg, wu, wd)


def _combine_body(pos_hbm, ys_hbm, meta_ref, x_ref, gate_ref, lng_ref, lnb_ref, o_ref,
                  y1_scr, y2_scr, pos_smem, pos_sem, row_sem):
    step = pl.program_id(0)
    tm = x_ref.shape[0]
    pos_cp = pltpu.make_async_copy(pos_hbm.at[step], pos_smem, pos_sem)
    pos_cp.start()
    pos_cp.wait()

    def issue(r, carry):
        _row_copy(ys_hbm, pos_smem[2 * r], y1_scr, r, row_sem).start(priority=0)
        _row_copy(ys_hbm, pos_smem[2 * r + 1], y2_scr, r, row_sem).start(priority=1)
        return carry

    lax.fori_loop(0, tm, issue, 0, unroll=8)
    _rows_wait(ys_hbm, y1_scr, tm, row_sem)
    _rows_wait(ys_hbm, y2_scr, tm, row_sem)

    meta = meta_ref[...]
    p1 = meta[:, META_P1:META_P1 + 1]
    p2 = meta[:, META_P2:META_P2 + 1]
    mix = p1 * _from_tiles(y1_scr) + p2 * _from_tiles(y2_scr)
    r = DEEPNORM_ALPHA * x_ref[...] + gate_ref[...] * mix
    o_ref[...] = _layer_norm(r, lng_ref[...], lnb_ref[...])


def _combine_ln(ys, pos, meta, x2, mod, gate_col, ln_g, ln_b, *, seq_len):
    m, d = x2.shape
    tm = ROUTE_TM
    return pl.pallas_call(
        _combine_body,
        grid=(m // tm,),
        in_specs=[
            pl.BlockSpec(memory_space=pl.ANY),
            pl.BlockSpec(memory_space=pl.ANY),
            pl.BlockSpec((tm, LANES), lambda i: (i, 0)),
            pl.BlockSpec((tm, d), lambda i: (i, 0)),
            pl.BlockSpec((None, 1, d), lambda i: ((i * tm) // seq_len, 0, gate_col)),
            pl.BlockSpec((1, d), lambda i: (0, 0)),
            pl.BlockSpec((1, d), lambda i: (0, 0)),
        ],
        out_specs=pl.BlockSpec((tm, d), lambda i: (i, 0)),
        out_shape=jax.ShapeDtypeStruct((m, d), F32),
        scratch_shapes=[
            pltpu.VMEM((tm * SUBLANES, LANES), F32),
            pltpu.VMEM((tm * SUBLANES, LANES), F32),
            pltpu.SMEM((2 * tm,), jnp.int32),
            pltpu.SemaphoreType.DMA,
            pltpu.SemaphoreType.DMA,
        ],
        compiler_params=_cparams(("arbitrary",)),
        name="moe_combine",
    )(pos, ys, meta, x2, mod, ln_g, ln_b)


def _moe_ln(x, mod, w_router, wg, wu, wd, ln_g, ln_b, *, tf):
    b, l, d = x.shape
    assert d == SUBLANES * LANES
    m = b * l
    x2 = x.reshape(m, d)
    w_r = jnp.zeros((d, LANES), BF16).at[:, :N_EXPERTS].set(w_router.astype(BF16))
    meta, cnt = _route(x2, mod, 3, w_r, seq_len=l)

    counts = cnt[0, :N_EXPERTS].astype(jnp.int32)
    n_tile_e = (counts + MOE_TM - 1) // MOE_TM
    tile_end = jnp.cumsum(n_tile_e)
    offs = (tile_end - n_tile_e) * MOE_TM
    idx = meta[:, META_I1:META_I2 + 1].astype(jnp.int32)
    rank = meta[:, META_R1:META_R2 + 1].astype(jnp.int32)
    base = jnp.sum(jnp.where(idx[..., None] == jnp.arange(N_EXPERTS, dtype=jnp.int32), offs, 0), axis=-1)
    pos = (base + rank).reshape(m // ROUTE_TM, 2 * ROUTE_TM)
    n_tiles = (2 * m) // MOE_TM + N_EXPERTS
    tid = jnp.arange(n_tiles, dtype=jnp.int32)
    tile_valid = (tid < tile_end[-1]).astype(jnp.int32)
    tile_expert = jnp.minimum(jnp.sum((tid[:, None] >= tile_end[None, :]).astype(jnp.int32), axis=1),
                              N_EXPERTS - 1).astype(jnp.int32)
    tile_in = jnp.where(tile_valid > 0, tid, 0).astype(jnp.int32)
    tile_out = tid
    pad_info = jnp.concatenate([offs + counts, n_tile_e * MOE_TM - counts,
                                tile_end[-1:] * MOE_TM]).astype(jnp.int32)
    n_rows = n_tiles * MOE_TM

    xs = _dispatch(x2, mod, 3, pos, pad_info, n_rows, seq_len=l)
    ys = _experts(xs, tile_expert, tile_in, tile_out, tile_valid, wg, wu, wd, tf=tf)
    out = _combine_ln(ys, pos, meta, x2, mod, 5, ln_g, ln_b, seq_len=l)
    return out.reshape(b, l, d)


def _fourier_body(pc_ref, dl_ref, dc_ref, o_ref, t_scr, *, out_scale, row_chunk):
    l = pc_ref.shape[0]
    for g in range(C_GROUPS):
        sl = slice(C_GROUP_DIM * g, C_GROUP_DIM * (g + 1))
        xg = pc_ref[:, sl].astype(F32)
        mu = jnp.mean(xg, axis=-1, keepdims=True)
        dlt = xg - mu
        var = jnp.mean(dlt * dlt, axis=-1, keepdims=True)
        gn = (dlt * lax.rsqrt(var + LN_EPS)).astype(BF16)
        t = jnp.dot(gn, dc_ref[...], preferred_element_type=F32)
        t_scr[0:l, sl] = t[:, 0:C_GROUP_DIM].astype(BF16)
        t_scr[l:2 * l, sl] = t[:, C_GROUP_DIM:2 * C_GROUP_DIM].astype(BF16)
    for r0 in range(0, l, row_chunk):
        acc = jnp.dot(dl_ref[r0:r0 + row_chunk, :], t_scr[...], preferred_element_type=F32)
        o_ref[r0:r0 + row_chunk, :] = (acc * out_scale).astype(BF16)


def _dft_matrices(l, c):
    j = np.arange(l, dtype=np.int64)
    ang_l = (2.0 * np.pi / l) * ((j[:, None] * j[None, :]) % l)
    dl = np.concatenate([np.cos(ang_l), -np.sin(ang_l)], axis=1)
    m = np.arange(c, dtype=np.int64)
    ang_c = (2.0 * np.pi / c) * ((m[:, None] * m[None, :]) % c)
    dc = np.concatenate([np.cos(ang_c), np.sin(ang_c)], axis=1)
    return dl.astype(np.float32), dc.astype(np.float32)


def _fourier_mixer(y, name="fourier"):
    b, l, _ = y.shape
    dl_np, dc_np = _dft_matrices(l, C_GROUP_DIM)
    dl = jnp.asarray(dl_np, dtype=F32).astype(BF16)
    dc = jnp.asarray(dc_np, dtype=F32).astype(BF16)
    out_scale = 1.0 / math.sqrt(l * C_GROUP_DIM)
    return pl.pallas_call(
        functools.partial(_fourier_body, out_scale=out_scale, row_chunk=min(l, 512)),
        grid=(b,),
        in_specs=[
            pl.BlockSpec((None, l, C_WIDTH), lambda bi: (bi, 0, 0)),
            pl.BlockSpec((l, 2 * l), lambda bi: (0, 0), pipeline_mode=pl.Buffered(1)),
            pl.BlockSpec((C_GROUP_DIM, 2 * C_GROUP_DIM), lambda bi: (0, 0)),
        ],
        out_specs=pl.BlockSpec((None, l, C_WIDTH), lambda bi: (bi, 0, 0)),
        out_shape=jax.ShapeDtypeStruct((b, l, C_WIDTH), BF16),
        scratch_shapes=[pltpu.VMEM((2 * l, C_WIDTH), BF16)],
        compiler_params=_cparams(("parallel",)),
        name=name,
    )(y, dl, dc)


NA_HG = 4
NA_GW = NA_HG * NA_DH
NA_WIN = NA_KR * GRID_W


def _natten_body(q_ref, k_ref, v_ref, kc_ref, vc_ref, bias_ref, o_ref, *, rows_per_step, n_rows):
    rb = pl.program_id(2)
    hq = NA_HG * GRID_W
    rid = lax.broadcasted_iota(jnp.int32, (hq, NA_GW), 0)
    cid = lax.broadcasted_iota(jnp.int32, (hq, NA_GW), 1)
    diag = (rid // GRID_W) == (cid // NA_DH)
    cid_o = lax.broadcasted_iota(jnp.int32, (GRID_W, NA_GW), 1)
    kc = kc_ref[...]
    vc = vc_ref[...]
    staged = []
    for j in range(rows_per_step):
        r = rb * rows_per_step + j
        rs = jnp.clip(r - NA_KR // 2, 0, n_rows - NA_KR)
        start = pl.multiple_of(rs * GRID_W, GRID_W)
        tid = jnp.minimum(r, NA_KR // 2) + jnp.maximum(r - (n_rows - NA_KR // 2), 0)
        q_r = q_ref[GRID_W * j:GRID_W * (j + 1), :]
        q4 = jnp.concatenate([q_r] * NA_HG, axis=0)
        qbd = jnp.where(diag, q4, jnp.zeros_like(q4))
        kw = k_ref[pl.ds(start, NA_WIN), :]
        s_loc = lax.dot_general(qbd, kw, NT_DIMS, preferred_element_type=F32) + bias_ref[tid]
        s_ctx = lax.dot_general(qbd, kc, NT_DIMS, preferred_element_type=F32)
        staged.append((start, s_loc, s_ctx))
    for j in range(rows_per_step):
        start, s_loc, s_ctx = staged[j]
        vw = v_ref[pl.ds(start, NA_WIN), :]
        m = jnp.maximum(jnp.max(s_loc, axis=-1, keepdims=True), jnp.max(s_ctx, axis=-1, keepdims=True))
        e_loc = jnp.exp2(s_loc - m)
        e_ctx = jnp.exp2(s_ctx - m)
        tot = jnp.sum(e_loc, axis=-1, keepdims=True) + jnp.sum(e_ctx, axis=-1, keepdims=True)
        o = (jnp.dot(e_loc.astype(BF16), vw, preferred_element_type=F32)
             + jnp.dot(e_ctx.astype(BF16), vc, preferred_element_type=F32))
        o = o * (1.0 / tot)
        out = jnp.zeros((GRID_W, NA_GW), F32)
        for hh in range(NA_HG):
            out = out + jnp.where((cid_o // NA_DH) == hh, o[GRID_W * hh:GRID_W * (hh + 1), :], 0.0)
        o_ref[GRID_W * j:GRID_W * (j + 1), :] = out.astype(BF16)


def _na_bias_table(rpb, n_rows):
    h = rpb.shape[0]
    cols = jnp.arange(GRID_W)
    col_start = jnp.clip(cols - NA_KC // 2, 0, GRID_W - NA_KC)
    col_valid = (cols[None, :] >= col_start[:, None]) & (cols[None, :] < col_start[:, None] + NA_KC)
    dc_idx = jnp.clip(cols[None, :] - cols[:, None] + NA_KC - 1, 0, 2 * NA_KC - 2)
    rpb_c = rpb[:, :, dc_idx].astype(F32)
    half = NA_KR // 2
    rep_rows = list(range(half)) + [half] + list(range(n_rows - half + 1, n_rows))
    tabs = []
    for r in rep_rows:
        rs = min(max(r - half, 0), n_rows - NA_KR)
        dr_idx = rs + np.arange(NA_KR) - r + NA_KR - 1
        bias = rpb_c[:, dr_idx].transpose(0, 2, 1, 3)
        bias = jnp.where(col_valid[None, :, None, :], bias * LOG2E, NEG_INF)
        tabs.append(bias.reshape(h, GRID_W, NA_WIN))
    tab = jnp.stack(tabs, axis=0)
    return tab.reshape(len(rep_rows), h // NA_HG, NA_HG * GRID_W, NA_WIN)


def _natten(y, q_col0, k_col0, v_col0, y_ctx, kc_col0, vc_col0, bias_tab, *, rows_per_step, name="natten"):
    b, l, _ = y.shape
    lc = y_ctx.shape[1]
    n_rows = l // GRID_W
    n_tab = bias_tab.shape[0]
    n_grp = NA_HEADS // NA_HG
    tq = rows_per_step * GRID_W
    qb, kb, vb = q_col0 // NA_GW, k_col0 // NA_GW, v_col0 // NA_GW
    kcb, vcb = kc_col0 // NA_GW, vc_col0 // NA_GW
    return pl.pallas_call(
        functools.partial(_natten_body, rows_per_step=rows_per_step, n_rows=n_rows),
        grid=(b, n_grp, n_rows // rows_per_step),
        in_specs=[
            pl.BlockSpec((None, tq, NA_GW), lambda bi, g, i: (bi, i, qb + g)),
            pl.BlockSpec((None, l, NA_GW), lambda bi, g, i: (bi, 0, kb + g)),
            pl.BlockSpec((None, l, NA_GW), lambda bi, g, i: (bi, 0, vb + g)),
            pl.BlockSpec((None, lc, NA_GW), lambda bi, g, i: (bi, 0, kcb + g)),
            pl.BlockSpec((None, lc, NA_GW), lambda bi, g, i: (bi, 0, vcb + g)),
            pl.BlockSpec((n_tab, None, NA_HG * GRID_W, NA_WIN), lambda bi, g, i: (0, g, 0, 0)),
        ],
        out_specs=pl.BlockSpec((None, tq, NA_GW), lambda bi, g, i: (bi, i, g)),
        out_shape=jax.ShapeDtypeStruct((b, l, D_WIDTH), BF16),
        compiler_params=_cparams(("parallel", "parallel", "arbitrary")),
        name=name,
    )(y, y, y, y_ctx, y_ctx, bias_tab)


def _rope_tables(l):
    t = jnp.arange(l, dtype=jnp.int32)
    row = (t // GRID_W).astype(F32)
    col = (t % GRID_W).astype(F32)
    n_freq = DIFF_DH // 4
    inv_freq = ROPE_THETA ** (-jnp.arange(n_freq, dtype=F32) / n_freq)
    ang = jnp.concatenate([row[:, None] * inv_freq, col[:, None] * inv_freq], axis=-1)
    c, s = jnp.cos(ang), jnp.sin(ang)
    cos = jnp.tile(jnp.concatenate([c, c], axis=-1), (1, 2))
    sin = jnp.tile(jnp.concatenate([-s, s], axis=-1), (1, 2))
    return cos, sin


def kernel(x, c, ctx, c_ctx, w_mod, b_mod, ln_g, ln_b, e_w_in, e_conv, e_lam_q1, e_lam_k1, e_lam_q2, e_lam_k2, e_subln_g, e_w_o, e_ffn_gate, e_ffn_up, e_ffn_down, o_w_in, o_rpb, o_w_o, o_router, o_exp_gate, o_exp_up, o_exp_down):
    b, l, d = x.shape
    lc = ctx.shape[1]
    assert d == D_MODEL and l % 512 == 0 and lc % 256 == 0 and b + 1 <= MOD_ROWS

    cond = jnp.concatenate([c, c_ctx[None, :], jnp.zeros((MOD_ROWS - b - 1, d), F32)], axis=0)
    mods = _adaln(cond, w_mod, b_mod)

    def layer_mods(i):
        lat = mods[i, :b][:, None, :]
        cx = jnp.broadcast_to(mods[i, b][None, None, :], (b, 1, 6 * d))
        return lat, cx

    q_scale_diff = DIFF_DH ** -0.5 * LOG2E
    q_scale_na = NA_DH ** -0.5 * LOG2E

    mod_lat, mod_ctx = layer_mods(0)
    lam_init = 0.8 - 0.6 * math.exp(-0.3 * 0)
    a_end = 3 * A_WIDTH
    w_in = e_w_in[0]
    w_main = w_in[:, :a_end + 2 * DIFF_QK].astype(BF16)
    w_vt = w_in[:, a_end + 2 * DIFF_QK:].T.astype(BF16)
    rope = _rope_tables(l)
    pa_chunks = [(0, 512, "plain"), (512, 512, "plain"), (1024, 512, "plain")]
    y_lat, vt_lat = _proj(x, mod_lat, 0, w_main,
                          pa_chunks + [(a_end, 512, "rope_scale"), (a_end + 512, 512, "rope")],
                          tm=512, wvt=w_vt, rope=rope, q_scale=q_scale_diff, name="even_inproj_lat")
    y_ctx, vt_ctx = _proj(ctx, mod_ctx, 0, w_main,
                          pa_chunks + [(a_end, 512, "scale"), (a_end + 512, 512, "plain")],
                          tm=lc, wvt=w_vt, q_scale=q_scale_diff, name="even_inproj_ctx")

    lam_pack = jnp.zeros((8, LANES), F32)
    lam_pack = lam_pack.at[0, :DIFF_DH].set(e_lam_q1[0]).at[1, :DIFF_DH].set(e_lam_k1[0])
    lam_pack = lam_pack.at[2, :DIFF_DH].set(e_lam_q2[0]).at[3, :DIFF_DH].set(e_lam_k2[0])
    sub_g = e_subln_g[0].reshape(1, DIFF_VDIM)
    o_lat = _diff_attention(y_lat, [(y_ctx, vt_ctx), (y_lat, vt_lat)], lam_pack, sub_g, lam_init,
                            tq=256, q_col0=a_end, k_col0=a_end + DIFF_QK, name="diffattn_lat")
    o_ctx = _diff_attention(y_ctx, [(y_ctx, vt_ctx)], lam_pack, sub_g, lam_init,
                            tq=lc, q_col0=a_end, k_col0=a_end + DIFF_QK, name="diffattn_ctx")
    cv_lat = _conv_mixer(y_lat, e_conv[0], "conv_lat")
    cv_ctx = _conv_mixer(y_ctx, e_conv[0], "conv_ctx")

    w_o = e_w_o[0].astype(BF16)
    lng0, lnb0 = ln_g[0, 0][None, :], ln_b[0, 0][None, :]
    lng1, lnb1 = ln_g[0, 1][None, :], ln_b[0, 1][None, :]
    x_lat = _outproj_ln(cv_lat, 0, o_lat, 0, w_o, x, mod_lat, 2, lng0, lnb0, tm=512, name="even_outproj_lat")
    x_ctx = _outproj_ln(cv_ctx, 0, o_ctx, 0, w_o, ctx, mod_ctx, 2, lng0, lnb0, tm=lc, name="even_outproj_ctx")

    wg = e_ffn_gate[0].astype(BF16)
    wu = e_ffn_up[0].astype(BF16)
    wd = e_ffn_down[0].astype(BF16)
    x_lat = _ffn_ln(x_lat, mod_lat, 3, wg, wu, wd, lng1, lnb1, tm=512, tf=1408, name="ffn_lat")
    x_ctx = _ffn_ln(x_ctx, mod_ctx, 3, wg, wu, wd, lng1, lnb1, tm=lc, tf=1408, name="ffn_ctx")

    mod_lat, mod_ctx = layer_mods(1)
    w_in = o_w_in[0].astype(BF16)
    y_lat = _proj(x_lat, mod_lat, 0, w_in,
                  [(0, 512, "plain"), (512, 512, "scale"), (1024, 512, "plain"), (1536, 512, "plain")],
                  tm=512, q_scale=q_scale_na, name="odd_inproj_lat")
    y_ctx = _proj(x_ctx, mod_ctx, 0, w_in[:, C_WIDTH + D_WIDTH:],
                  [(0, 512, "plain"), (512, 512, "plain")], tm=lc, name="odd_inproj_ctx")
    f_lat = _fourier_mixer(y_lat)
    bias_tab = _na_bias_table(o_rpb[0], l // GRID_W)
    n_lat = _natten(y_lat, C_WIDTH, C_WIDTH + D_WIDTH, C_WIDTH + 2 * D_WIDTH, y_ctx, 0, D_WIDTH,
                    bias_tab, rows_per_step=4)

    w_o = o_w_o[0].astype(BF16)
    lng0, lnb0 = ln_g[1, 0][None, :], ln_b[1, 0][None, :]
    lng1, lnb1 = ln_g[1, 1][None, :], ln_b[1, 1][None, :]
    x_lat = _outproj_ln(f_lat, 0, n_lat, 0, w_o, x_lat, mod_lat, 2, lng0, lnb0, tm=512, name="odd_outproj_lat")

    return _moe_ln(x_lat, mod_lat, o_router[0], o_exp_gate[0], o_exp_up[0], o_exp_down[0], lng1, lnb1, tf=512)
```

```python
import functools
import math

import numpy as np
import jax
import jax.numpy as jnp
from jax import lax
from jax.experimental import pallas as pl
from jax.experimental.pallas import tpu as pltpu

F32 = jnp.float32
BF16 = jnp.bfloat16

D_MODEL = 1024
GRID_W = 64
DEPTH = 2

A_WIDTH = 512
DIFF_HEADS = 4
DIFF_DH = 64
DIFF_VDIM = 128
DIFF_QK = 512
B_WIDTH = 512

C_WIDTH = 512
C_GROUPS = 4
C_GROUP_DIM = 128
NA_HEADS = 8
NA_DH = 64
D_WIDTH = 512
NA_KR = 8
NA_KC = 16

N_EXPERTS = 8

ROPE_THETA = 10000.0
LN_EPS = 1e-5
RMS_EPS = 1e-5
NEG_INF = -1e30
DEEPNORM_ALPHA = (2 * DEPTH) ** 0.25
LOG2E = 1.4426950408889634

LANES = 128
SUBLANES = 8
MOD_ROWS = 32
NT_DIMS = (((1,), (1,)), ((), ()))


def _cparams(sem, vmem_mb=48):
    return pltpu.CompilerParams(dimension_semantics=sem, vmem_limit_bytes=vmem_mb * 1024 * 1024)


def _layer_norm(r, g, b):
    mu = jnp.mean(r, axis=-1, keepdims=True)
    d = r - mu
    var = jnp.mean(d * d, axis=-1, keepdims=True)
    return d * lax.rsqrt(var + LN_EPS) * g + b


def _adaln_body(c_ref, w_ref, b_ref, o_ref):
    cnd = c_ref[...]
    s = (cnd * jax.nn.sigmoid(cnd)).astype(BF16)
    o_ref[...] = jnp.dot(s, w_ref[...].astype(BF16), preferred_element_type=F32) + b_ref[...]


def _adaln(cond, w_mod, b_mod):
    depth, d, n = w_mod.shape
    tn = 1536
    return pl.pallas_call(
        _adaln_body,
        grid=(depth, n // tn),
        in_specs=[
            pl.BlockSpec((MOD_ROWS, d), lambda l, j: (0, 0)),
            pl.BlockSpec((None, d, tn), lambda l, j: (l, 0, j)),
            pl.BlockSpec((None, 1, tn), lambda l, j: (l, 0, j)),
        ],
        out_specs=pl.BlockSpec((None, MOD_ROWS, tn), lambda l, j: (l, 0, j)),
        out_shape=jax.ShapeDtypeStruct((depth, MOD_ROWS, n), F32),
        compiler_params=_cparams(("parallel", "parallel")),
        name="adaln",
    )(cond, w_mod, b_mod.reshape(depth, 1, n))


def _proj_body(*refs, chunks, with_vt, with_rope, q_scale):
    x_ref, sh_ref, sc_ref, w_ref = refs[:4]
    i = 4
    if with_vt:
        wvt_ref = refs[i]
        i += 1
    if with_rope:
        cos_ref, sin_ref = refs[i], refs[i + 1]
        i += 2
    y_ref = refs[i]
    vt_ref = refs[i + 1] if with_vt else None

    h = (x_ref[...] * (1.0 + sc_ref[...]) + sh_ref[...]).astype(BF16)
    tm = h.shape[0]
    if with_rope:
        cos = cos_ref[...]
        sin = sin_ref[...]
        lane = lax.broadcasted_iota(jnp.int32, (tm, LANES), 1)
        low_half = (lane % 64) < 32
    for (c0, width, kind) in chunks:
        acc = jnp.dot(h, w_ref[:, c0:c0 + width], preferred_element_type=F32)
        if kind in ("rope", "rope_scale"):
            for j in range(width // LANES):
                a = acc[:, LANES * j:LANES * (j + 1)]
                rot = jnp.where(low_half, pltpu.roll(a, 96, 1), pltpu.roll(a, 32, 1))
                r = a * cos + rot * sin
                if kind == "rope_scale":
                    r = r * q_scale
                y_ref[:, c0 + LANES * j:c0 + LANES * (j + 1)] = r.astype(BF16)
        elif kind == "scale":
            y_ref[:, c0:c0 + width] = (acc * q_scale).astype(BF16)
        else:
            y_ref[:, c0:c0 + width] = acc.astype(BF16)
    if with_vt:
        vt = lax.dot_general(wvt_ref[...], h, NT_DIMS, preferred_element_type=F32)
        vt_ref[...] = vt.astype(BF16)


def _proj(x, mod, shift_col, w, chunks, *, tm, wvt=None, rope=None, q_scale=1.0, name="proj"):
    b, l, d = x.shape
    n = w.shape[1]
    with_vt = wvt is not None
    with_rope = rope is not None
    in_specs = [
        pl.BlockSpec((None, tm, d), lambda bi, i: (bi, i, 0)),
        pl.BlockSpec((None, 1, d), lambda bi, i: (bi, 0, shift_col)),
        pl.BlockSpec((None, 1, d), lambda bi, i: (bi, 0, shift_col + 1)),
        pl.BlockSpec((d, n), lambda bi, i: (0, 0)),
    ]
    args = [x, mod, mod, w]
    if with_vt:
        nv = wvt.shape[0]
        in_specs.append(pl.BlockSpec((nv, d), lambda bi, i: (0, 0)))
        args.append(wvt)
    if with_rope:
        in_specs += [pl.BlockSpec((tm, LANES), lambda bi, i: (i, 0))] * 2
        args += [rope[0], rope[1]]
    out_specs = [pl.BlockSpec((None, tm, n), lambda bi, i: (bi, i, 0))]
    out_shape = [jax.ShapeDtypeStruct((b, l, n), BF16)]
    if with_vt:
        out_specs.append(pl.BlockSpec((None, nv, tm), lambda bi, i: (bi, 0, i)))
        out_shape.append(jax.ShapeDtypeStruct((b, nv, l), BF16))
    res = pl.pallas_call(
        functools.partial(_proj_body, chunks=tuple(chunks), with_vt=with_vt, with_rope=with_rope,
                          q_scale=q_scale),
        grid=(b, l // tm),
        in_specs=in_specs,
        out_specs=out_specs,
        out_shape=out_shape,
        compiler_params=_cparams(("parallel", "parallel")),
        name=name,
    )(*args)
    return res if with_vt else res[0]


DIFF_HP = 4


def _diffattn_body(*refs, n_seg, lam_init):
    q_ref, lam_ref, g_ref = refs[:3]
    k_refs = [refs[3 + 2 * s] for s in range(n_seg)]
    vt_refs = [refs[4 + 2 * s] for s in range(n_seg)]
    o_ref = refs[3 + 2 * n_seg]

    tq = q_ref.shape[0]
    lane = lax.broadcasted_iota(jnp.int32, (tq, LANES), 1)
    lp = lam_ref[...]
    lam = (jnp.exp(jnp.sum(lp[0:1] * lp[1:2], axis=1, keepdims=True))
           - jnp.exp(jnp.sum(lp[2:3] * lp[3:4], axis=1, keepdims=True)) + lam_init)

    scores = []
    for h in range(DIFF_HP):
        cols = slice(LANES * h, LANES * (h + 1))
        q = q_ref[:, cols]
        zero = jnp.zeros_like(q)
        per_comp = []
        for qm in (jnp.where(lane < DIFF_DH, q, zero), jnp.where(lane >= DIFF_DH, q, zero)):
            per_comp.append([lax.dot_general(k_ref[:, cols], qm, NT_DIMS, preferred_element_type=F32)
                             for k_ref in k_refs])
        scores.append(per_comp)

    def unnormalised(s, rows):
        m = functools.reduce(jnp.maximum, [jnp.max(x, axis=0, keepdims=True) for x in s])
        tot = None
        acc = None
        for x, vt_ref in zip(s, vt_refs):
            e = jnp.exp2(x - m)
            t = jnp.sum(e, axis=0, keepdims=True)
            pv = jnp.dot(vt_ref[rows, :], e.astype(BF16), preferred_element_type=F32)
            tot = t if tot is None else tot + t
            acc = pv if acc is None else acc + pv
        return acc, tot

    for h in range(DIFF_HP):
        cols = slice(LANES * h, LANES * (h + 1))
        rows = slice(DIFF_VDIM * h, DIFF_VDIM * (h + 1))
        acc1, l1 = unnormalised(scores[h][0], rows)
        acc2, l2 = unnormalised(scores[h][1], rows)
        o_t = acc1 * (1.0 / l1) - acc2 * (lam / l2)
        o = o_t.T
        ms = jnp.mean(o * o, axis=-1, keepdims=True)
        o_ref[:, cols] = (o * lax.rsqrt(ms + RMS_EPS) * g_ref[...] * (1.0 - lam_init)).astype(BF16)


def _diff_attention(yq, segs, lam_pack, sub_g, lam_init, *, tq, q_col0, k_col0, name):
    b, lq, _ = yq.shape
    width = DIFF_HP * LANES
    qb0 = q_col0 // width
    kb0 = k_col0 // width
    in_specs = [
        pl.BlockSpec((None, tq, width), lambda bi, h, i: (bi, i, qb0 + h)),
        pl.BlockSpec((8, LANES), lambda bi, h, i: (0, 0)),
        pl.BlockSpec((1, LANES), lambda bi, h, i: (0, 0)),
    ]
    args = [yq, lam_pack, sub_g]
    for (yk, vt) in segs:
        lk = yk.shape[1]
        in_specs.append(pl.BlockSpec((None, lk, width), lambda bi, h, i: (bi, 0, kb0 + h)))
        in_specs.append(pl.BlockSpec((None, DIFF_HP * DIFF_VDIM, lk), lambda bi, h, i: (bi, h, 0)))
        args += [yk, vt]
    return pl.pallas_call(
        functools.partial(_diffattn_body, n_seg=len(segs), lam_init=lam_init),
        grid=(b, DIFF_HEADS // DIFF_HP, lq // tq),
        in_specs=in_specs,
        out_specs=pl.BlockSpec((None, tq, width), lambda bi, h, i: (bi, i, h)),
        out_shape=jax.ShapeDtypeStruct((b, lq, B_WIDTH), BF16),
        compiler_params=_cparams(("parallel", "parallel", "arbitrary")),
        name=name,
    )(*args)


HALO_ROWS = 16


def _conv_outproj_body(bg_ref, cg_ref, val_ref, cgp_ref, valp_ref, cgn_ref, valn_ref, wc_ref,
                       o_att_ref, w_ref, x_ref, gate_ref, lng_ref, lnb_ref, o_ref):
    i = pl.program_id(1)
    tm = x_ref.shape[0]
    half = o_att_ref.shape[1]
    u = cg_ref[...].astype(F32) * val_ref[...].astype(F32)
    u_before = cgp_ref[...].astype(F32) * valp_ref[...].astype(F32)
    u_after = cgn_ref[...].astype(F32) * valn_ref[...].astype(F32)
    first = jnp.where(i == 0, 0.0, u_before[HALO_ROWS - 1:HALO_ROWS, :])
    last = jnp.where(i == pl.num_programs(1) - 1, 0.0, u_after[0:1, :])
    row = lax.broadcasted_iota(jnp.int32, (tm, half), 0)
    u_prev = jnp.where(row == 0, first, pltpu.roll(u, 1, 0))
    u_next = jnp.where(row == tm - 1, last, pltpu.roll(u, tm - 1, 0))
    wc = wc_ref[...]
    conv = bg_ref[...].astype(F32) * (u_prev * wc[0:1] + u * wc[1:2] + u_next * wc[2:3])
    y = (jnp.dot(conv.astype(BF16), w_ref[0:half, :], preferred_element_type=F32)
         + jnp.dot(o_att_ref[...], w_ref[half:2 * half, :], preferred_element_type=F32))
    r = DEEPNORM_ALPHA * x_ref[...] + gate_ref[...] * y
    o_ref[...] = _layer_norm(r, lng_ref[...], lnb_ref[...])


def _conv_outproj_ln(y, w_conv, o_att, w_o, x, mod, gate_col, ln_g, ln_b, *, tm, name):
    b, l, d = x.shape
    half = w_o.shape[0] // 2
    per = tm // HALO_ROWS
    n_halo = l // HALO_ROWS
    main = lambda col: pl.BlockSpec((None, tm, half), lambda bi, i: (bi, i, col))
    before = lambda col: pl.BlockSpec((None, HALO_ROWS, half),
                                      lambda bi, i: (bi, jnp.maximum(i * per - 1, 0), col))
    after = lambda col: pl.BlockSpec((None, HALO_ROWS, half),
                                     lambda bi, i: (bi, jnp.minimum((i + 1) * per, n_halo - 1), col))
    return pl.pallas_call(
        _conv_outproj_body,
        grid=(b, l // tm),
        in_specs=[
            main(0), main(1), main(2), before(1), before(2), after(1), after(2),
            pl.BlockSpec((3, half), lambda bi, i: (0, 0)),
            pl.BlockSpec((None, tm, half), lambda bi, i: (bi, i, 0)),
            pl.BlockSpec((2 * half, d), lambda bi, i: (0, 0)),
            pl.BlockSpec((None, tm, d), lambda bi, i: (bi, i, 0)),
            pl.BlockSpec((None, 1, d), lambda bi, i: (bi, 0, gate_col)),
            pl.BlockSpec((1, d), lambda bi, i: (0, 0)),
            pl.BlockSpec((1, d), lambda bi, i: (0, 0)),
        ],
        out_specs=pl.BlockSpec((None, tm, d), lambda bi, i: (bi, i, 0)),
        out_shape=jax.ShapeDtypeStruct((b, l, d), F32),
        compiler_params=_cparams(("parallel", "parallel")),
        name=name,
    )(y, y, y, y, y, y, y, w_conv, o_att, w_o, x, mod, ln_g, ln_b)


def _outproj_body(a_ref, b_ref, w_ref, x_ref, gate_ref, lng_ref, lnb_ref, o_ref):
    half = a_ref.shape[1]
    y = (jnp.dot(a_ref[...], w_ref[0:half, :], preferred_element_type=F32)
         + jnp.dot(b_ref[...], w_ref[half:2 * half, :], preferred_element_type=F32))
    r = DEEPNORM_ALPHA * x_ref[...] + gate_ref[...] * y
    o_ref[...] = _layer_norm(r, lng_ref[...], lnb_ref[...])


def _outproj_ln(a, a_colblk, bsrc, b_colblk, w_o, x, mod, gate_col, ln_g, ln_b, *, tm, name):
    b, l, d = x.shape
    half = w_o.shape[0] // 2
    return pl.pallas_call(
        _outproj_body,
        grid=(b, l // tm),
        in_specs=[
            pl.BlockSpec((None, tm, half), lambda bi, i: (bi, i, a_colblk)),
            pl.BlockSpec((None, tm, half), lambda bi, i: (bi, i, b_colblk)),
            pl.BlockSpec((2 * half, d), lambda bi, i: (0, 0)),
            pl.BlockSpec((None, tm, d), lambda bi, i: (bi, i, 0)),
            pl.BlockSpec((None, 1, d), lambda bi, i: (bi, 0, gate_col)),
            pl.BlockSpec((1, d), lambda bi, i: (0, 0)),
            pl.BlockSpec((1, d), lambda bi, i: (0, 0)),
        ],
        out_specs=pl.BlockSpec((None, tm, d), lambda bi, i: (bi, i, 0)),
        out_shape=jax.ShapeDtypeStruct((b, l, d), F32),
        compiler_params=_cparams(("parallel", "parallel")),
        name=name,
    )(a, bsrc, w_o, x, mod, ln_g, ln_b)


MXU_WIDTH = 256


def _ffn_body(x_ref, sh_ref, sc_ref, gate_ref, wg_ref, wu_ref, wd_ref, lng_ref, lnb_ref, o_ref, *, chunks):
    x = x_ref[...]
    h = (x * (1.0 + sc_ref[...]) + sh_ref[...]).astype(BF16)
    acc = None
    for (c0, c1) in chunks:
        g = jnp.dot(h, wg_ref[:, c0:c1], preferred_element_type=F32)
        u = jnp.dot(h, wu_ref[:, c0:c1], preferred_element_type=F32)
        a = (g * jax.nn.sigmoid(g) * u).astype(BF16)
        part = jnp.dot(a, wd_ref[c0:c1, :], preferred_element_type=F32)
        acc = part if acc is None else acc + part
    r = DEEPNORM_ALPHA * x + gate_ref[...] * acc
    o_ref[...] = _layer_norm(r, lng_ref[...], lnb_ref[...])


def _ffn_ln(x, mod, shift_col, wg, wu, wd, ln_g, ln_b, *, tm, n_chunks, name="ffn"):
    b, l, d = x.shape
    ff = wg.shape[1]
    n_mxu = ff // MXU_WIDTH
    assert ff == n_mxu * MXU_WIDTH
    bounds = [MXU_WIDTH * ((n_mxu * k + n_chunks - 1) // n_chunks) for k in range(n_chunks + 1)]
    chunks = tuple((bounds[k], bounds[k + 1]) for k in range(n_chunks))
    resident = pl.Buffered(1)
    return pl.pallas_call(
        functools.partial(_ffn_body, chunks=chunks),
        grid=(b, l // tm),
        in_specs=[
            pl.BlockSpec((None, tm, d), lambda bi, i: (bi, i, 0)),
            pl.BlockSpec((None, 1, d), lambda bi, i: (bi, 0, shift_col)),
            pl.BlockSpec((None, 1, d), lambda bi, i: (bi, 0, shift_col + 1)),
            pl.BlockSpec((None, 1, d), lambda bi, i: (bi, 0, shift_col + 2)),
            pl.BlockSpec((d, ff), lambda bi, i: (0, 0), pipeline_mode=resident),
            pl.BlockSpec((d, ff), lambda bi, i: (0, 0), pipeline_mode=resident),
            pl.BlockSpec((ff, d), lambda bi, i: (0, 0), pipeline_mode=resident),
            pl.BlockSpec((1, d), lambda bi, i: (0, 0)),
            pl.BlockSpec((1, d), lambda bi, i: (0, 0)),
        ],
        out_specs=pl.BlockSpec((None, tm, d), lambda bi, i: (bi, i, 0)),
        out_shape=jax.ShapeDtypeStruct((b, l, d), F32),
        compiler_params=_cparams(("parallel", "parallel")),
        name=name,
    )(x, mod, mod, mod, wg, wu, wd, ln_g, ln_b)


MOE_TM = 1024
ROUTE_TM = 512
ZERO_ROWS = 256
META_I1, META_I2, META_R1, META_R2, META_P1, META_P2 = range(6)


def _route_body(x_ref, sh_ref, sc_ref, wr_ref, meta_ref, cnt_ref, carry_scr):
    @pl.when(pl.program_id(0) == 0)
    def _():
        carry_scr[...] = jnp.zeros_like(carry_scr)

    h = (x_ref[...] * (1.0 + sc_ref[...]) + sh_ref[...]).astype(BF16)
    logits = jnp.dot(h, wr_ref[...], preferred_element_type=F32)
    tm = logits.shape[0]
    lane = lax.broadcasted_iota(jnp.int32, logits.shape, 1).astype(F32)
    l1 = jnp.where(lane < N_EXPERTS, logits, -jnp.inf)
    v1 = jnp.max(l1, axis=-1, keepdims=True)
    i1 = jnp.min(jnp.where(l1 == v1, lane, float(LANES)), axis=-1, keepdims=True)
    l2 = jnp.where(lane == i1, -jnp.inf, l1)
    v2 = jnp.max(l2, axis=-1, keepdims=True)
    i2 = jnp.min(jnp.where(l2 == v2, lane, float(LANES)), axis=-1, keepdims=True)
    t = jnp.exp(v2 - v1)
    p1 = 1.0 / (1.0 + t)
    p2 = t / (1.0 + t)

    member = jnp.where(lane == i1, 1.0, jnp.where(lane == i2, 1.0, 0.0))
    rr = lax.broadcasted_iota(jnp.int32, (tm, tm), 0)
    cc = lax.broadcasted_iota(jnp.int32, (tm, tm), 1)
    earlier = jnp.where(cc < rr, 1.0, 0.0).astype(BF16)
    base = carry_scr[0:1, :]
    rank = jnp.dot(earlier, member.astype(BF16), preferred_element_type=F32) + base
    total = base + jnp.sum(member, axis=0, keepdims=True)
    carry_scr[0:1, :] = total
    r1 = jnp.sum(jnp.where(lane == i1, rank, 0.0), axis=-1, keepdims=True)
    r2 = jnp.sum(jnp.where(lane == i2, rank, 0.0), axis=-1, keepdims=True)

    meta = jnp.zeros_like(logits)
    for k, val in ((META_I1, i1), (META_I2, i2), (META_R1, r1), (META_R2, r2), (META_P1, p1), (META_P2, p2)):
        meta = jnp.where(lane == float(k), val, meta)
    meta_ref[...] = meta
    cnt_ref[...] = jnp.broadcast_to(total, cnt_ref.shape)


def _route(x2, mod, shift_col, w_router_pad, *, seq_len):
    m, d = x2.shape
    tm = ROUTE_TM
    return pl.pallas_call(
        _route_body,
        grid=(m // tm,),
        in_specs=[
            pl.BlockSpec((tm, d), lambda i: (i, 0)),
            pl.BlockSpec((None, 1, d), lambda i: ((i * tm) // seq_len, 0, shift_col)),
            pl.BlockSpec((None, 1, d), lambda i: ((i * tm) // seq_len, 0, shift_col + 1)),
            pl.BlockSpec((d, LANES), lambda i: (0, 0)),
        ],
        out_specs=[pl.BlockSpec((tm, LANES), lambda i: (i, 0)),
                   pl.BlockSpec((8, LANES), lambda i: (0, 0))],
        out_shape=[jax.ShapeDtypeStruct((m, LANES), F32), jax.ShapeDtypeStruct((8, LANES), F32)],
        scratch_shapes=[pltpu.VMEM((8, LANES), F32)],
        compiler_params=_cparams(("arbitrary",)),
        name="moe_route",
    )(x2, mod, mod, w_router_pad)


def _to_tiles(tile_ref, value):
    n = value.shape[0]
    for j in range(SUBLANES):
        tile_ref[pl.ds(j, n, stride=SUBLANES), :] = value[:, LANES * j:LANES * (j + 1)]


def _from_tiles(tile_ref):
    n = tile_ref.shape[0] // SUBLANES
    return jnp.concatenate([tile_ref[pl.ds(j, n, stride=SUBLANES), :] for j in range(SUBLANES)], axis=-1)


def _tile_rows(ref, row, n=1):
    return ref.at[pl.ds(pl.multiple_of(row * SUBLANES, SUBLANES), n * SUBLANES), :]


def _row_copy(src, src_row, dst, dst_row, sem):
    return pltpu.make_async_copy(_tile_rows(src, src_row), _tile_rows(dst, dst_row), sem)


def _rows_wait(src, dst, n, sem):
    pltpu.make_async_copy(_tile_rows(src, 0, n), _tile_rows(dst, 0, n), sem).wait()


def _dispatch_body(pad_ref, pos_hbm, x_ref, sh_ref, sc_ref, xs_hbm, h_scr, z_scr, pos_smem, pos_sem, row_sem):
    step = pl.program_id(0)
    tm = x_ref.shape[0]
    pos_cp = pltpu.make_async_copy(pos_hbm.at[step], pos_smem, pos_sem)
    pos_cp.start()
    _to_tiles(h_scr, x_ref[...] * (1.0 + sc_ref[...]) + sh_ref[...])
    pos_cp.wait()

    def issue(r, carry):
        _row_copy(h_scr, r, xs_hbm, pos_smem[2 * r], row_sem).start(priority=0)
        _row_copy(h_scr, r, xs_hbm, pos_smem[2 * r + 1], row_sem).start(priority=1)
        return carry

    lax.fori_loop(0, tm, issue, 0, unroll=8)
    _rows_wait(h_scr, xs_hbm, tm, row_sem)
    _rows_wait(h_scr, xs_hbm, tm, row_sem)

    @pl.when(step == pl.num_programs(0) - 1)
    def _():
        z_scr[...] = jnp.zeros_like(z_scr)
        for e in range(N_EXPERTS):
            start = pad_ref[e]
            count = pad_ref[N_EXPERTS + e]

            def fill(k, carry, start=start):
                _row_copy(z_scr, 0, xs_hbm, start + k, row_sem).start()
                return carry

            def fill_wait(k, carry):
                _row_copy(z_scr, 0, xs_hbm, 0, row_sem).wait()
                return carry

            lax.fori_loop(0, count, fill, 0)
            lax.fori_loop(0, count, fill_wait, 0)

        zrows = z_scr.shape[0] // SUBLANES
        used_rows = pad_ref[2 * N_EXPERTS]
        n_chunks = (xs_hbm.shape[0] // SUBLANES - used_rows) // zrows

        def chunk_copy(k):
            return pltpu.make_async_copy(z_scr, _tile_rows(xs_hbm, used_rows + k * zrows, zrows), row_sem)

        def fill_chunk(k, carry):
            chunk_copy(k).start()
            return carry

        def fill_chunk_wait(k, carry):
            chunk_copy(k).wait()
            return carry

        lax.fori_loop(0, n_chunks, fill_chunk, 0)
        lax.fori_loop(0, n_chunks, fill_chunk_wait, 0)


def _dispatch(x2, mod, shift_col, pos, pad_info, n_rows, *, seq_len):
    m, d = x2.shape
    tm = ROUTE_TM
    return pl.pallas_call(
        _dispatch_body,
        grid_spec=pltpu.PrefetchScalarGridSpec(
            num_scalar_prefetch=1,
            grid=(m // tm,),
            in_specs=[
                pl.BlockSpec(memory_space=pl.ANY),
                pl.BlockSpec((tm, d), lambda i, pad: (i, 0)),
                pl.BlockSpec((None, 1, d), lambda i, pad: ((i * tm) // seq_len, 0, shift_col)),
                pl.BlockSpec((None, 1, d), lambda i, pad: ((i * tm) // seq_len, 0, shift_col + 1)),
            ],
            out_specs=pl.BlockSpec(memory_space=pl.ANY),
            scratch_shapes=[
                pltpu.VMEM((tm * SUBLANES, LANES), F32),
                pltpu.VMEM((ZERO_ROWS * SUBLANES, LANES), F32),
                pltpu.SMEM((2 * tm,), jnp.int32),
                pltpu.SemaphoreType.DMA,
                pltpu.SemaphoreType.DMA,
            ],
        ),
        out_shape=jax.ShapeDtypeStruct((n_rows * SUBLANES, LANES), F32),
        compiler_params=_cparams(("arbitrary",)),
        name="moe_dispatch",
    )(pad_info, pos, x2, mod, mod)


def _experts_body(te_ref, tbi_ref, tbo_ref, tv_ref, xs_ref, wg_ref, wu_ref, wd_ref, ys_ref, h_scr, acc_scr, *, n_f):
    t = pl.program_id(0)
    f = pl.program_id(1)

    @pl.when(f == 0)
    def _():
        h_scr[...] = _from_tiles(xs_ref).astype(BF16)
        acc_scr[...] = jnp.zeros_like(acc_scr)

    @pl.when(tv_ref[t] > 0)
    def _():
        h = h_scr[...]
        g = jnp.dot(h, wg_ref[...].astype(BF16), preferred_element_type=F32)
        u = jnp.dot(h, wu_ref[...].astype(BF16), preferred_element_type=F32)
        a = (g * jax.nn.sigmoid(g) * u).astype(BF16)
        acc_scr[...] += jnp.dot(a, wd_ref[...].astype(BF16), preferred_element_type=F32)

    @pl.when(f == n_f - 1)
    def _():
        _to_tiles(ys_ref, acc_scr[...])


def _experts(xs, tile_expert, tile_in, tile_out, tile_valid, wg, wu, wd, *, tf):
    n_rows = xs.shape[0] // SUBLANES
    d = wg.shape[1]
    tm = MOE_TM
    n_tiles = tile_expert.shape[0]
    ff = wg.shape[2]
    n_f = ff // tf
    return pl.pallas_call(
        functools.partial(_experts_body, n_f=n_f),
        grid_spec=pltpu.PrefetchScalarGridSpec(
            num_scalar_prefetch=4,
            grid=(n_tiles, n_f),
            in_specs=[
                pl.BlockSpec((tm * SUBLANES, LANES), lambda t, f, te, tbi, tbo, tv: (tbi[t], 0)),
                pl.BlockSpec((None, d, tf), lambda t, f, te, tbi, tbo, tv: (te[t], 0, f * tv[t])),
                pl.BlockSpec((None, d, tf), lambda t, f, te, tbi, tbo, tv: (te[t], 0, f * tv[t])),
                pl.BlockSpec((None, tf, d), lambda t, f, te, tbi, tbo, tv: (te[t], f * tv[t], 0)),
            ],
            out_specs=pl.BlockSpec((tm * SUBLANES, LANES), lambda t, f, te, tbi, tbo, tv: (tbo[t], 0)),
            scratch_shapes=[pltpu.VMEM((tm, d), BF16), pltpu.VMEM((tm, d), F32)],
        ),
        out_shape=jax.ShapeDtypeStruct((n_rows * SUBLANES, LANES), F32),
        compiler_params=_cparams(("arbitrary", "arbitrary")),
        name="moe_experts",
    )(tile_expert, tile_in, tile_out, tile_valid, xs, wg, wu, wd)


def _combine_body(pos_hbm, ys_hbm, meta_ref, x_ref, gate_ref, lng_ref, lnb_ref, o_ref,
                  y1_scr, y2_scr, pos_smem, pos_sem, row_sem):
    step = pl.program_id(0)
    tm = x_ref.shape[0]
    pos_cp = pltpu.make_async_copy(pos_hbm.at[step], pos_smem, pos_sem)
    pos_cp.start()
    pos_cp.wait()

    def issue(r, carry):
        _row_copy(ys_hbm, pos_smem[2 * r], y1_scr, r, row_sem).start(priority=0)
        _row_copy(ys_hbm, pos_smem[2 * r + 1], y2_scr, r, row_sem).start(priority=1)
        return carry

    lax.fori_loop(0, tm, issue, 0, unroll=8)
    _rows_wait(ys_hbm, y1_scr, tm, row_sem)
    _rows_wait(ys_hbm, y2_scr, tm, row_sem)

    meta = meta_ref[...]
    p1 = meta[:, META_P1:META_P1 + 1]
    p2 = meta[:, META_P2:META_P2 + 1]
    mix = p1 * _from_tiles(y1_scr) + p2 * _from_tiles(y2_scr)
    r = DEEPNORM_ALPHA * x_ref[...] + gate_ref[...] * mix
    o_ref[...] = _layer_norm(r, lng_ref[...], lnb_ref[...])


def _combine_ln(ys, pos, meta, x2, mod, gate_col, ln_g, ln_b, *, seq_len):
    m, d = x2.shape
    tm = ROUTE_TM
    return pl.pallas_call(
        _combine_body,
        grid=(m // tm,),
        in_specs=[
            pl.BlockSpec(memory_space=pl.ANY),
            pl.BlockSpec(memory_space=pl.ANY),
            pl.BlockSpec((tm, LANES), lambda i: (i, 0)),
            pl.BlockSpec((tm, d), lambda i: (i, 0)),
            pl.BlockSpec((None, 1, d), lambda i: ((i * tm) // seq_len, 0, gate_col)),
            pl.BlockSpec((1, d), lambda i: (0, 0)),
            pl.BlockSpec((1, d), lambda i: (0, 0)),
        ],
        out_specs=pl.BlockSpec((tm, d), lambda i: (i, 0)),
        out_shape=jax.ShapeDtypeStruct((m, d), F32),
        scratch_shapes=[
            pltpu.VMEM((tm * SUBLANES, LANES), F32),
            pltpu.VMEM((tm * SUBLANES, LANES), F32),
            pltpu.SMEM((2 * tm,), jnp.int32),
            pltpu.SemaphoreType.DMA,
            pltpu.SemaphoreType.DMA,
        ],
        compiler_params=_cparams(("arbitrary",)),
        name="moe_combine",
    )(pos, ys, meta, x2, mod, ln_g, ln_b)


def _moe_ln(x, mod, w_router, wg, wu, wd, ln_g, ln_b, *, tf):
    b, l, d = x.shape
    assert d == SUBLANES * LANES
    m = b * l
    x2 = x.reshape(m, d)
    w_r = jnp.zeros((d, LANES), BF16).at[:, :N_EXPERTS].set(w_router.astype(BF16))
    meta, cnt = _route(x2, mod, 3, w_r, seq_len=l)

    counts = cnt[0, :N_EXPERTS].astype(jnp.int32)
    n_tile_e = (counts + MOE_TM - 1) // MOE_TM
    tile_end = jnp.cumsum(n_tile_e)
    offs = (tile_end - n_tile_e) * MOE_TM
    idx = meta[:, META_I1:META_I2 + 1].astype(jnp.int32)
    rank = meta[:, META_R1:META_R2 + 1].astype(jnp.int32)
    base = jnp.sum(jnp.where(idx[..., None] == jnp.arange(N_EXPERTS, dtype=jnp.int32), offs, 0), axis=-1)
    pos = (base + rank).reshape(m // ROUTE_TM, 2 * ROUTE_TM)
    n_tiles = (2 * m) // MOE_TM + N_EXPERTS
    tid = jnp.arange(n_tiles, dtype=jnp.int32)
    tile_valid = (tid < tile_end[-1]).astype(jnp.int32)
    tile_expert = jnp.minimum(jnp.sum((tid[:, None] >= tile_end[None, :]).astype(jnp.int32), axis=1),
                              N_EXPERTS - 1).astype(jnp.int32)
    tile_in = jnp.where(tile_valid > 0, tid, 0).astype(jnp.int32)
    tile_out = tid
    pad_info = jnp.concatenate([offs + counts, n_tile_e * MOE_TM - counts,
                                tile_end[-1:] * MOE_TM]).astype(jnp.int32)
    n_rows = n_tiles * MOE_TM

    xs = _dispatch(x2, mod, 3, pos, pad_info, n_rows, seq_len=l)
    ys = _experts(xs, tile_expert, tile_in, tile_out, tile_valid, wg, wu, wd, tf=tf)
    out = _combine_ln(ys, pos, meta, x2, mod, 5, ln_g, ln_b, seq_len=l)
    return out.reshape(b, l, d)


def _fourier_body(pc_ref, dl_ref, dc_ref, o_ref, t_scr, *, out_scale, row_chunk):
    l = pc_ref.shape[0]
    for g in range(C_GROUPS):
        sl = slice(C_GROUP_DIM * g, C_GROUP_DIM * (g + 1))
        xg = pc_ref[:, sl].astype(F32)
        mu = jnp.mean(xg, axis=-1, keepdims=True)
        dlt = xg - mu
        var = jnp.mean(dlt * dlt, axis=-1, keepdims=True)
        gn = (dlt * lax.rsqrt(var + LN_EPS)).astype(BF16)
        t = jnp.dot(gn, dc_ref[...], preferred_element_type=F32)
        t_scr[0:l, sl] = t[:, 0:C_GROUP_DIM].astype(BF16)
        t_scr[l:2 * l, sl] = t[:, C_GROUP_DIM:2 * C_GROUP_DIM].astype(BF16)
    for r0 in range(0, l, row_chunk):
        acc = jnp.dot(dl_ref[r0:r0 + row_chunk, :], t_scr[...], preferred_element_type=F32)
        o_ref[r0:r0 + row_chunk, :] = (acc * out_scale).astype(BF16)


def _dft_matrices(l, c):
    j = np.arange(l, dtype=np.int64)
    ang_l = (2.0 * np.pi / l) * ((j[:, None] * j[None, :]) % l)
    dl = np.concatenate([np.cos(ang_l), -np.sin(ang_l)], axis=1)
    m = np.arange(c, dtype=np.int64)
    ang_c = (2.0 * np.pi / c) * ((m[:, None] * m[None, :]) % c)
    dc = np.concatenate([np.cos(ang_c), np.sin(ang_c)], axis=1)
    return dl.astype(np.float32), dc.astype(np.float32)


def _fourier_mixer(y, name="fourier"):
    b, l, _ = y.shape
    dl_np, dc_np = _dft_matrices(l, C_GROUP_DIM)
    dl = jnp.asarray(dl_np, dtype=F32).astype(BF16)
    dc = jnp.asarray(dc_np, dtype=F32).astype(BF16)
    out_scale = 1.0 / math.sqrt(l * C_GROUP_DIM)
    return pl.pallas_call(
        functools.partial(_fourier_body, out_scale=out_scale, row_chunk=min(l, 512)),
        grid=(b,),
        in_specs=[
            pl.BlockSpec((None, l, C_WIDTH), lambda bi: (bi, 0, 0)),
            pl.BlockSpec((l, 2 * l), lambda bi: (0, 0), pipeline_mode=pl.Buffered(1)),
            pl.BlockSpec((C_GROUP_DIM, 2 * C_GROUP_DIM), lambda bi: (0, 0)),
        ],
        out_specs=pl.BlockSpec((None, l, C_WIDTH), lambda bi: (bi, 0, 0)),
        out_shape=jax.ShapeDtypeStruct((b, l, C_WIDTH), BF16),
        scratch_shapes=[pltpu.VMEM((2 * l, C_WIDTH), BF16)],
        compiler_params=_cparams(("parallel",)),
        name=name,
    )(y, dl, dc)


NA_HG = 4
NA_GW = NA_HG * NA_DH
NA_WIN = NA_KR * GRID_W


def _natten_body(q_ref, k_ref, v_ref, kc_ref, vc_ref, bias_ref, o_ref, *, rows_per_step, n_rows):
    rb = pl.program_id(2)
    hq = NA_HG * GRID_W
    rid = lax.broadcasted_iota(jnp.int32, (hq, NA_GW), 0)
    cid = lax.broadcasted_iota(jnp.int32, (hq, NA_GW), 1)
    diag = (rid // GRID_W) == (cid // NA_DH)
    cid_o = lax.broadcasted_iota(jnp.int32, (GRID_W, NA_GW), 1)
    kc = kc_ref[...]
    vc = vc_ref[...]
    staged = []
    for j in range(rows_per_step):
        r = rb * rows_per_step + j
        rs = jnp.clip(r - NA_KR // 2, 0, n_rows - NA_KR)
        start = pl.multiple_of(rs * GRID_W, GRID_W)
        tid = jnp.minimum(r, NA_KR // 2) + jnp.maximum(r - (n_rows - NA_KR // 2), 0)
        q_r = q_ref[GRID_W * j:GRID_W * (j + 1), :]
        q4 = jnp.concatenate([q_r] * NA_HG, axis=0)
        qbd = jnp.where(diag, q4, jnp.zeros_like(q4))
        kw = k_ref[pl.ds(start, NA_WIN), :]
        s_loc = lax.dot_general(qbd, kw, NT_DIMS, preferred_element_type=F32) + bias_ref[tid]
        s_ctx = lax.dot_general(qbd, kc, NT_DIMS, preferred_element_type=F32)
        staged.append((start, s_loc, s_ctx))
    for j in range(rows_per_step):
        start, s_loc, s_ctx = staged[j]
        vw = v_ref[pl.ds(start, NA_WIN), :]
        m = jnp.maximum(jnp.max(s_loc, axis=-1, keepdims=True), jnp.max(s_ctx, axis=-1, keepdims=True))
        e_loc = jnp.exp2(s_loc - m)
        e_ctx = jnp.exp2(s_ctx - m)
        tot = jnp.sum(e_loc, axis=-1, keepdims=True) + jnp.sum(e_ctx, axis=-1, keepdims=True)
        o = (jnp.dot(e_loc.astype(BF16), vw, preferred_element_type=F32)
             + jnp.dot(e_ctx.astype(BF16), vc, preferred_element_type=F32))
        o = o * (1.0 / tot)
        out = jnp.zeros((GRID_W, NA_GW), F32)
        for hh in range(NA_HG):
            out = out + jnp.where((cid_o // NA_DH) == hh, o[GRID_W * hh:GRID_W * (hh + 1), :], 0.0)
        o_ref[GRID_W * j:GRID_W * (j + 1), :] = out.astype(BF16)


def _na_bias_table(rpb, n_rows):
    h = rpb.shape[0]
    cols = jnp.arange(GRID_W)
    col_start = jnp.clip(cols - NA_KC // 2, 0, GRID_W - NA_KC)
    col_valid = (cols[None, :] >= col_start[:, None]) & (cols[None, :] < col_start[:, None] + NA_KC)
    dc_idx = jnp.clip(cols[None, :] - cols[:, None] + NA_KC - 1, 0, 2 * NA_KC - 2)
    onehot = (dc_idx[:, :, None] == jnp.arange(2 * NA_KC - 1)).astype(F32)
    rpb_c = jnp.einsum("qkc,hdc->hdqk", onehot, rpb.astype(F32), precision=lax.Precision.HIGHEST)
    half = NA_KR // 2
    rep_rows = list(range(half)) + [half] + list(range(n_rows - half + 1, n_rows))
    tabs = []
    for r in rep_rows:
        rs = min(max(r - half, 0), n_rows - NA_KR)
        dr_idx = rs + np.arange(NA_KR) - r + NA_KR - 1
        bias = rpb_c[:, dr_idx].transpose(0, 2, 1, 3)
        bias = jnp.where(col_valid[None, :, None, :], bias * LOG2E, NEG_INF)
        tabs.append(bias.reshape(h, GRID_W, NA_WIN))
    tab = jnp.stack(tabs, axis=0)
    return tab.reshape(len(rep_rows), h // NA_HG, NA_HG * GRID_W, NA_WIN)


def _natten(y, q_col0, k_col0, v_col0, y_ctx, kc_col0, vc_col0, bias_tab, *, rows_per_step, name="natten"):
    b, l, _ = y.shape
    lc = y_ctx.shape[1]
    n_rows = l // GRID_W
    n_tab = bias_tab.shape[0]
    n_grp = NA_HEADS // NA_HG
    tq = rows_per_step * GRID_W
    qb, kb, vb = q_col0 // NA_GW, k_col0 // NA_GW, v_col0 // NA_GW
    kcb, vcb = kc_col0 // NA_GW, vc_col0 // NA_GW
    return pl.pallas_call(
        functools.partial(_natten_body, rows_per_step=rows_per_step, n_rows=n_rows),
        grid=(b, n_grp, n_rows // rows_per_step),
        in_specs=[
            pl.BlockSpec((None, tq, NA_GW), lambda bi, g, i: (bi, i, qb + g)),
            pl.BlockSpec((None, l, NA_GW), lambda bi, g, i: (bi, 0, kb + g)),
            pl.BlockSpec((None, l, NA_GW), lambda bi, g, i: (bi, 0, vb + g)),
            pl.BlockSpec((None, lc, NA_GW), lambda bi, g, i: (bi, 0, kcb + g)),
            pl.BlockSpec((None, lc, NA_GW), lambda bi, g, i: (bi, 0, vcb + g)),
            pl.BlockSpec((n_tab, None, NA_HG * GRID_W, NA_WIN), lambda bi, g, i: (0, g, 0, 0)),
        ],
        out_specs=pl.BlockSpec((None, tq, NA_GW), lambda bi, g, i: (bi, i, g)),
        out_shape=jax.ShapeDtypeStruct((b, l, D_WIDTH), BF16),
        compiler_params=_cparams(("parallel", "parallel", "arbitrary")),
        name=name,
    )(y, y, y, y_ctx, y_ctx, bias_tab)


def _rope_tables(l):
    t = jnp.arange(l, dtype=jnp.int32)
    row = (t // GRID_W).astype(F32)
    col = (t % GRID_W).astype(F32)
    n_freq = DIFF_DH // 4
    inv_freq = ROPE_THETA ** (-jnp.arange(n_freq, dtype=F32) / n_freq)
    ang = jnp.concatenate([row[:, None] * inv_freq, col[:, None] * inv_freq], axis=-1)
    c, s = jnp.cos(ang), jnp.sin(ang)
    cos = jnp.tile(jnp.concatenate([c, c], axis=-1), (1, 2))
    sin = jnp.tile(jnp.concatenate([-s, s], axis=-1), (1, 2))
    return cos, sin


def kernel(x, c, ctx, c_ctx, w_mod, b_mod, ln_g, ln_b, e_w_in, e_conv, e_lam_q1, e_lam_k1, e_lam_q2, e_lam_k2, e_subln_g, e_w_o, e_ffn_gate, e_ffn_up, e_ffn_down, o_w_in, o_rpb, o_w_o, o_router, o_exp_gate, o_exp_up, o_exp_down):
    b, l, d = x.shape
    lc = ctx.shape[1]
    assert d == D_MODEL and l % 512 == 0 and lc % 256 == 0 and b + 1 <= MOD_ROWS

    cond = jnp.concatenate([c, c_ctx[None, :], jnp.zeros((MOD_ROWS - b - 1, d), F32)], axis=0)
    mods = _adaln(cond, w_mod, b_mod)

    def layer_mods(i):
        lat = mods[i, :b][:, None, :]
        cx = jnp.broadcast_to(mods[i, b][None, None, :], (b, 1, 6 * d))
        return lat, cx

    q_scale_diff = DIFF_DH ** -0.5 * LOG2E
    q_scale_na = NA_DH ** -0.5 * LOG2E

    mod_lat, mod_ctx = layer_mods(0)
    lam_init = 0.8 - 0.6 * math.exp(-0.3 * 0)
    a_end = 3 * A_WIDTH
    w_in = e_w_in[0]
    w_main = w_in[:, :a_end + 2 * DIFF_QK].astype(BF16)
    w_vt = w_in[:, a_end + 2 * DIFF_QK:].T.astype(BF16)
    rope = _rope_tables(l)
    pa_chunks = [(0, 512, "plain"), (512, 512, "plain"), (1024, 512, "plain")]
    y_lat, vt_lat = _proj(x, mod_lat, 0, w_main,
                          pa_chunks + [(a_end, 512, "rope_scale"), (a_end + 512, 512, "rope")],
                          tm=512, wvt=w_vt, rope=rope, q_scale=q_scale_diff, name="even_inproj_lat")
    y_ctx, vt_ctx = _proj(ctx, mod_ctx, 0, w_main,
                          pa_chunks + [(a_end, 512, "scale"), (a_end + 512, 512, "plain")],
                          tm=lc, wvt=w_vt, q_scale=q_scale_diff, name="even_inproj_ctx")

    lam_pack = jnp.zeros((8, LANES), F32)
    lam_pack = lam_pack.at[0, :DIFF_DH].set(e_lam_q1[0]).at[1, :DIFF_DH].set(e_lam_k1[0])
    lam_pack = lam_pack.at[2, :DIFF_DH].set(e_lam_q2[0]).at[3, :DIFF_DH].set(e_lam_k2[0])
    sub_g = e_subln_g[0].reshape(1, DIFF_VDIM)
    o_lat = _diff_attention(y_lat, [(y_ctx, vt_ctx), (y_lat, vt_lat)], lam_pack, sub_g, lam_init,
                            tq=256, q_col0=a_end, k_col0=a_end + DIFF_QK, name="diffattn_lat")
    o_ctx = _diff_attention(y_ctx, [(y_ctx, vt_ctx)], lam_pack, sub_g, lam_init,
                            tq=lc, q_col0=a_end, k_col0=a_end + DIFF_QK, name="diffattn_ctx")

    w_o = e_w_o[0].astype(BF16)
    lng0, lnb0 = ln_g[0, 0][None, :], ln_b[0, 0][None, :]
    lng1, lnb1 = ln_g[0, 1][None, :], ln_b[0, 1][None, :]
    x_lat = _conv_outproj_ln(y_lat, e_conv[0], o_lat, w_o, x, mod_lat, 2, lng0, lnb0, tm=512,
                             name="even_outproj_lat")
    x_ctx = _conv_outproj_ln(y_ctx, e_conv[0], o_ctx, w_o, ctx, mod_ctx, 2, lng0, lnb0, tm=lc,
                             name="even_outproj_ctx")

    wg = e_ffn_gate[0].astype(BF16)
    wu = e_ffn_up[0].astype(BF16)
    wd = e_ffn_down[0].astype(BF16)
    x_lat = _ffn_ln(x_lat, mod_lat, 3, wg, wu, wd, lng1, lnb1, tm=512, n_chunks=2, name="ffn_lat")
    x_ctx = _ffn_ln(x_ctx, mod_ctx, 3, wg, wu, wd, lng1, lnb1, tm=lc, n_chunks=2, name="ffn_ctx")

    mod_lat, mod_ctx = layer_mods(1)
    w_in = o_w_in[0].astype(BF16)
    y_lat = _proj(x_lat, mod_lat, 0, w_in,
                  [(0, 512, "plain"), (512, 512, "scale"), (1024, 512, "plain"), (1536, 512, "plain")],
                  tm=512, q_scale=q_scale_na, name="odd_inproj_lat")
    y_ctx = _proj(x_ctx, mod_ctx, 0, w_in[:, C_WIDTH + D_WIDTH:],
                  [(0, 512, "plain"), (512, 512, "plain")], tm=lc, name="odd_inproj_ctx")
    f_lat = _fourier_mixer(y_lat)
    bias_tab = _na_bias_table(o_rpb[0], l // GRID_W)
    n_lat = _natten(y_lat, C_WIDTH, C_WIDTH + D_WIDTH, C_WIDTH + 2 * D_WIDTH, y_ctx, 0, D_WIDTH,
                    bias_tab, rows_per_step=4)

    w_o = o_w_o[0].astype(BF16)
    lng0, lnb0 = ln_g[1, 0][None, :], ln_b[1, 0][None, :]
    lng1, lnb1 = ln_g[1, 1][None, :], ln_b[1, 1][None, :]
    x_lat = _outproj_ln(f_lat, 0, n_lat, 0, w_o, x_lat, mod_lat, 2, lng0, lnb0, tm=512, name="odd_outproj_lat")

    return _moe_ln(x_lat, mod_lat, o_router[0], o_exp_gate[0], o_exp_up[0], o_exp_down[0], lng1, lnb1, tf=512)
```

```python
import functools
import math

import numpy as np
import jax
import jax.numpy as jnp
from jax import lax
from jax.experimental import pallas as pl
from jax.experimental.pallas import tpu as pltpu

F32 = jnp.float32
BF16 = jnp.bfloat16

D_MODEL = 1024
GRID_W = 64
DEPTH = 2

A_WIDTH = 512
DIFF_HEADS = 4
DIFF_DH = 64
DIFF_VDIM = 128
DIFF_QK = 512
B_WIDTH = 512

C_WIDTH = 512
C_GROUPS = 4
C_GROUP_DIM = 128
NA_HEADS = 8
NA_DH = 64
D_WIDTH = 512
NA_KR = 8
NA_KC = 16

N_EXPERTS = 8

ROPE_THETA = 10000.0
LN_EPS = 1e-5
RMS_EPS = 1e-5
NEG_INF = -1e30
DEEPNORM_ALPHA = (2 * DEPTH) ** 0.25
LOG2E = 1.4426950408889634

LANES = 128
SUBLANES = 8
MOD_ROWS = 32
NT_DIMS = (((1,), (1,)), ((), ()))


def _cparams(sem, vmem_mb=48):
    return pltpu.CompilerParams(dimension_semantics=sem, vmem_limit_bytes=vmem_mb * 1024 * 1024)


def _layer_norm(r, g, b):
    mu = jnp.mean(r, axis=-1, keepdims=True)
    d = r - mu
    var = jnp.mean(d * d, axis=-1, keepdims=True)
    return d * lax.rsqrt(var + LN_EPS) * g + b


def _adaln_body(c_ref, w_ref, b_ref, o_ref):
    cnd = c_ref[...]
    s = (cnd * jax.nn.sigmoid(cnd)).astype(BF16)
    o_ref[...] = jnp.dot(s, w_ref[...].astype(BF16), preferred_element_type=F32) + b_ref[...]


def _adaln(cond, w_mod, b_mod):
    depth, d, n = w_mod.shape
    tn = 1536
    return pl.pallas_call(
        _adaln_body,
        grid=(depth, n // tn),
        in_specs=[
            pl.BlockSpec((MOD_ROWS, d), lambda l, j: (0, 0)),
            pl.BlockSpec((None, d, tn), lambda l, j: (l, 0, j)),
            pl.BlockSpec((None, 1, tn), lambda l, j: (l, 0, j)),
        ],
        out_specs=pl.BlockSpec((None, MOD_ROWS, tn), lambda l, j: (l, 0, j)),
        out_shape=jax.ShapeDtypeStruct((depth, MOD_ROWS, n), F32),
        compiler_params=_cparams(("parallel", "parallel")),
        name="adaln",
    )(cond, w_mod, b_mod.reshape(depth, 1, n))


def _proj_body(*refs, chunks, with_vt, with_rope, q_scale):
    x_ref, sh_ref, sc_ref, w_ref = refs[:4]
    i = 4
    if with_vt:
        wvt_ref = refs[i]
        i += 1
    if with_rope:
        cos_ref, sin_ref = refs[i], refs[i + 1]
        i += 2
    y_ref = refs[i]
    vt_ref = refs[i + 1] if with_vt else None

    h = (x_ref[...] * (1.0 + sc_ref[...]) + sh_ref[...]).astype(BF16)
    tm = h.shape[0]
    if with_rope:
        cos = cos_ref[...]
        sin = sin_ref[...]
        lane = lax.broadcasted_iota(jnp.int32, (tm, LANES), 1)
        low_half = (lane % 64) < 32
    for (c0, width, kind) in chunks:
        acc = jnp.dot(h, w_ref[:, c0:c0 + width], preferred_element_type=F32)
        if kind in ("rope", "rope_scale"):
            for j in range(width // LANES):
                a = acc[:, LANES * j:LANES * (j + 1)]
                rot = jnp.where(low_half, pltpu.roll(a, 96, 1), pltpu.roll(a, 32, 1))
                r = a * cos + rot * sin
                if kind == "rope_scale":
                    r = r * q_scale
                y_ref[:, c0 + LANES * j:c0 + LANES * (j + 1)] = r.astype(BF16)
        elif kind == "scale":
            y_ref[:, c0:c0 + width] = (acc * q_scale).astype(BF16)
        else:
            y_ref[:, c0:c0 + width] = acc.astype(BF16)
    if with_vt:
        vt = lax.dot_general(wvt_ref[...], h, NT_DIMS, preferred_element_type=F32)
        vt_ref[...] = vt.astype(BF16)


def _proj(x, mod, shift_col, w, chunks, *, tm, wvt=None, rope=None, q_scale=1.0, name="proj"):
    b, l, d = x.shape
    n = w.shape[1]
    with_vt = wvt is not None
    with_rope = rope is not None
    in_specs = [
        pl.BlockSpec((None, tm, d), lambda bi, i: (bi, i, 0)),
        pl.BlockSpec((None, 1, d), lambda bi, i: (bi, 0, shift_col)),
        pl.BlockSpec((None, 1, d), lambda bi, i: (bi, 0, shift_col + 1)),
        pl.BlockSpec((d, n), lambda bi, i: (0, 0)),
    ]
    args = [x, mod, mod, w]
    if with_vt:
        nv = wvt.shape[0]
        in_specs.append(pl.BlockSpec((nv, d), lambda bi, i: (0, 0)))
        args.append(wvt)
    if with_rope:
        in_specs += [pl.BlockSpec((tm, LANES), lambda bi, i: (i, 0))] * 2
        args += [rope[0], rope[1]]
    out_specs = [pl.BlockSpec((None, tm, n), lambda bi, i: (bi, i, 0))]
    out_shape = [jax.ShapeDtypeStruct((b, l, n), BF16)]
    if with_vt:
        out_specs.append(pl.BlockSpec((None, nv, tm), lambda bi, i: (bi, 0, i)))
        out_shape.append(jax.ShapeDtypeStruct((b, nv, l), BF16))
    res = pl.pallas_call(
        functools.partial(_proj_body, chunks=tuple(chunks), with_vt=with_vt, with_rope=with_rope,
                          q_scale=q_scale),
        grid=(b, l // tm),
        in_specs=in_specs,
        out_specs=out_specs,
        out_shape=out_shape,
        compiler_params=_cparams(("parallel", "parallel")),
        name=name,
    )(*args)
    return res if with_vt else res[0]


DIFF_HP = 4


def _diffattn_body(*refs, n_seg, lam_init):
    q_ref, lam_ref, g_ref = refs[:3]
    k_refs = [refs[3 + 2 * s] for s in range(n_seg)]
    vt_refs = [refs[4 + 2 * s] for s in range(n_seg)]
    o_ref = refs[3 + 2 * n_seg]

    tq = q_ref.shape[0]
    lane = lax.broadcasted_iota(jnp.int32, (tq, LANES), 1)
    lp = lam_ref[...]
    lam = (jnp.exp(jnp.sum(lp[0:1] * lp[1:2], axis=1, keepdims=True))
           - jnp.exp(jnp.sum(lp[2:3] * lp[3:4], axis=1, keepdims=True)) + lam_init)

    scores = []
    for h in range(DIFF_HP):
        cols = slice(LANES * h, LANES * (h + 1))
        q = q_ref[:, cols]
        zero = jnp.zeros_like(q)
        per_comp = []
        for qm in (jnp.where(lane < DIFF_DH, q, zero), jnp.where(lane >= DIFF_DH, q, zero)):
            per_comp.append([lax.dot_general(k_ref[:, cols], qm, NT_DIMS, preferred_element_type=F32)
                             for k_ref in k_refs])
        scores.append(per_comp)

    def unnormalised(s, rows):
        m = functools.reduce(jnp.maximum, [jnp.max(x, axis=0, keepdims=True) for x in s])
        tot = None
        acc = None
        for x, vt_ref in zip(s, vt_refs):
            e = jnp.exp2(x - m)
            t = jnp.sum(e, axis=0, keepdims=True)
            pv = jnp.dot(vt_ref[rows, :], e.astype(BF16), preferred_element_type=F32)
            tot = t if tot is None else tot + t
            acc = pv if acc is None else acc + pv
        return acc, tot

    for h in range(DIFF_HP):
        cols = slice(LANES * h, LANES * (h + 1))
        rows = slice(DIFF_VDIM * h, DIFF_VDIM * (h + 1))
        acc1, l1 = unnormalised(scores[h][0], rows)
        acc2, l2 = unnormalised(scores[h][1], rows)
        o_t = acc1 * (1.0 / l1) - acc2 * (lam / l2)
        o = o_t.T
        ms = jnp.mean(o * o, axis=-1, keepdims=True)
        o_ref[:, cols] = (o * lax.rsqrt(ms + RMS_EPS) * g_ref[...] * (1.0 - lam_init)).astype(BF16)


def _diff_attention(yq, segs, lam_pack, sub_g, lam_init, *, tq, q_col0, k_col0, name):
    b, lq, _ = yq.shape
    width = DIFF_HP * LANES
    qb0 = q_col0 // width
    kb0 = k_col0 // width
    in_specs = [
        pl.BlockSpec((None, tq, width), lambda bi, h, i: (bi, i, qb0 + h)),
        pl.BlockSpec((8, LANES), lambda bi, h, i: (0, 0)),
        pl.BlockSpec((1, LANES), lambda bi, h, i: (0, 0)),
    ]
    args = [yq, lam_pack, sub_g]
    for (yk, vt) in segs:
        lk = yk.shape[1]
        in_specs.append(pl.BlockSpec((None, lk, width), lambda bi, h, i: (bi, 0, kb0 + h)))
        in_specs.append(pl.BlockSpec((None, DIFF_HP * DIFF_VDIM, lk), lambda bi, h, i: (bi, h, 0)))
        args += [yk, vt]
    return pl.pallas_call(
        functools.partial(_diffattn_body, n_seg=len(segs), lam_init=lam_init),
        grid=(b, DIFF_HEADS // DIFF_HP, lq // tq),
        in_specs=in_specs,
        out_specs=pl.BlockSpec((None, tq, width), lambda bi, h, i: (bi, i, h)),
        out_shape=jax.ShapeDtypeStruct((b, lq, B_WIDTH), BF16),
        compiler_params=_cparams(("parallel", "parallel", "arbitrary")),
        name=name,
    )(*args)


HALO_ROWS = 16


def _conv_outproj_body(bg_ref, cg_ref, val_ref, cgp_ref, valp_ref, cgn_ref, valn_ref, wc_ref,
                       o_att_ref, w_ref, x_ref, gate_ref, lng_ref, lnb_ref, o_ref):
    i = pl.program_id(1)
    tm = x_ref.shape[0]
    half = o_att_ref.shape[1]
    u = cg_ref[...].astype(F32) * val_ref[...].astype(F32)
    u_before = cgp_ref[...].astype(F32) * valp_ref[...].astype(F32)
    u_after = cgn_ref[...].astype(F32) * valn_ref[...].astype(F32)
    first = jnp.where(i == 0, 0.0, u_before[HALO_ROWS - 1:HALO_ROWS, :])
    last = jnp.where(i == pl.num_programs(1) - 1, 0.0, u_after[0:1, :])
    row = lax.broadcasted_iota(jnp.int32, (tm, half), 0)
    u_prev = jnp.where(row == 0, first, pltpu.roll(u, 1, 0))
    u_next = jnp.where(row == tm - 1, last, pltpu.roll(u, tm - 1, 0))
    wc = wc_ref[...]
    conv = bg_ref[...].astype(F32) * (u_prev * wc[0:1] + u * wc[1:2] + u_next * wc[2:3])
    y = (jnp.dot(conv.astype(BF16), w_ref[0:half, :], preferred_element_type=F32)
         + jnp.dot(o_att_ref[...], w_ref[half:2 * half, :], preferred_element_type=F32))
    r = DEEPNORM_ALPHA * x_ref[...] + gate_ref[...] * y
    o_ref[...] = _layer_norm(r, lng_ref[...], lnb_ref[...])


def _conv_outproj_ln(y, w_conv, o_att, w_o, x, mod, gate_col, ln_g, ln_b, *, tm, name):
    b, l, d = x.shape
    half = w_o.shape[0] // 2
    per = tm // HALO_ROWS
    n_halo = l // HALO_ROWS
    main = lambda col: pl.BlockSpec((None, tm, half), lambda bi, i: (bi, i, col))
    before = lambda col: pl.BlockSpec((None, HALO_ROWS, half),
                                      lambda bi, i: (bi, jnp.maximum(i * per - 1, 0), col))
    after = lambda col: pl.BlockSpec((None, HALO_ROWS, half),
                                     lambda bi, i: (bi, jnp.minimum((i + 1) * per, n_halo - 1), col))
    return pl.pallas_call(
        _conv_outproj_body,
        grid=(b, l // tm),
        in_specs=[
            main(0), main(1), main(2), before(1), before(2), after(1), after(2),
            pl.BlockSpec((3, half), lambda bi, i: (0, 0)),
            pl.BlockSpec((None, tm, half), lambda bi, i: (bi, i, 0)),
            pl.BlockSpec((2 * half, d), lambda bi, i: (0, 0)),
            pl.BlockSpec((None, tm, d), lambda bi, i: (bi, i, 0)),
            pl.BlockSpec((None, 1, d), lambda bi, i: (bi, 0, gate_col)),
            pl.BlockSpec((1, d), lambda bi, i: (0, 0)),
            pl.BlockSpec((1, d), lambda bi, i: (0, 0)),
        ],
        out_specs=pl.BlockSpec((None, tm, d), lambda bi, i: (bi, i, 0)),
        out_shape=jax.ShapeDtypeStruct((b, l, d), F32),
        compiler_params=_cparams(("parallel", "parallel")),
        name=name,
    )(y, y, y, y, y, y, y, w_conv, o_att, w_o, x, mod, ln_g, ln_b)


def _outproj_rows(a_ref, b_ref, w_ref, x_ref, gate_ref, lng_ref, lnb_ref):
    half = a_ref.shape[1]
    y = (jnp.dot(a_ref[...], w_ref[0:half, :], preferred_element_type=F32)
         + jnp.dot(b_ref[...], w_ref[half:2 * half, :], preferred_element_type=F32))
    r = DEEPNORM_ALPHA * x_ref[...] + gate_ref[...] * y
    return _layer_norm(r, lng_ref[...], lnb_ref[...])


MXU_WIDTH = 256


def _ffn_body(x_ref, sh_ref, sc_ref, gate_ref, wg_ref, wu_ref, wd_ref, lng_ref, lnb_ref, o_ref, *, chunks):
    x = x_ref[...]
    h = (x * (1.0 + sc_ref[...]) + sh_ref[...]).astype(BF16)
    acc = None
    for (c0, c1) in chunks:
        g = jnp.dot(h, wg_ref[:, c0:c1], preferred_element_type=F32)
        u = jnp.dot(h, wu_ref[:, c0:c1], preferred_element_type=F32)
        a = (g * jax.nn.sigmoid(g) * u).astype(BF16)
        part = jnp.dot(a, wd_ref[c0:c1, :], preferred_element_type=F32)
        acc = part if acc is None else acc + part
    r = DEEPNORM_ALPHA * x + gate_ref[...] * acc
    o_ref[...] = _layer_norm(r, lng_ref[...], lnb_ref[...])


def _ffn_ln(x, mod, shift_col, wg, wu, wd, ln_g, ln_b, *, tm, n_chunks, name="ffn"):
    b, l, d = x.shape
    ff = wg.shape[1]
    n_mxu = ff // MXU_WIDTH
    assert ff == n_mxu * MXU_WIDTH
    bounds = [MXU_WIDTH * ((n_mxu * k + n_chunks - 1) // n_chunks) for k in range(n_chunks + 1)]
    chunks = tuple((bounds[k], bounds[k + 1]) for k in range(n_chunks))
    resident = pl.Buffered(1)
    return pl.pallas_call(
        functools.partial(_ffn_body, chunks=chunks),
        grid=(b, l // tm),
        in_specs=[
            pl.BlockSpec((None, tm, d), lambda bi, i: (bi, i, 0)),
            pl.BlockSpec((None, 1, d), lambda bi, i: (bi, 0, shift_col)),
            pl.BlockSpec((None, 1, d), lambda bi, i: (bi, 0, shift_col + 1)),
            pl.BlockSpec((None, 1, d), lambda bi, i: (bi, 0, shift_col + 2)),
            pl.BlockSpec((d, ff), lambda bi, i: (0, 0), pipeline_mode=resident),
            pl.BlockSpec((d, ff), lambda bi, i: (0, 0), pipeline_mode=resident),
            pl.BlockSpec((ff, d), lambda bi, i: (0, 0), pipeline_mode=resident),
            pl.BlockSpec((1, d), lambda bi, i: (0, 0)),
            pl.BlockSpec((1, d), lambda bi, i: (0, 0)),
        ],
        out_specs=pl.BlockSpec((None, tm, d), lambda bi, i: (bi, i, 0)),
        out_shape=jax.ShapeDtypeStruct((b, l, d), F32),
        compiler_params=_cparams(("parallel", "parallel")),
        name=name,
    )(x, mod, mod, mod, wg, wu, wd, ln_g, ln_b)


MOE_TM = 1024
ROUTE_TM = 512
ZERO_ROWS = 256
META_I1, META_I2, META_R1, META_R2, META_P1, META_P2 = range(6)


def _outproj_route_body(a_ref, b_ref, w_ref, x_ref, gate_ref, lng_ref, lnb_ref, sh_ref, sc_ref, wr_ref,
                        o_ref, meta_ref, meta_t_ref, cnt_ref, carry_scr):
    x_new = _outproj_rows(a_ref, b_ref, w_ref, x_ref, gate_ref, lng_ref, lnb_ref)
    o_ref[...] = x_new
    _route_rows(x_new, sh_ref, sc_ref, wr_ref, meta_ref, meta_t_ref, cnt_ref, carry_scr)


def _route_rows(x, sh_ref, sc_ref, wr_ref, meta_ref, meta_t_ref, cnt_ref, carry_scr):
    @pl.when(pl.program_id(0) == 0)
    def _():
        carry_scr[...] = jnp.zeros_like(carry_scr)

    h = (x * (1.0 + sc_ref[...]) + sh_ref[...]).astype(BF16)
    logits = jnp.dot(h, wr_ref[...], preferred_element_type=F32)
    tm = logits.shape[0]
    lane = lax.broadcasted_iota(jnp.int32, logits.shape, 1).astype(F32)
    l1 = jnp.where(lane < N_EXPERTS, logits, -jnp.inf)
    v1 = jnp.max(l1, axis=-1, keepdims=True)
    i1 = jnp.min(jnp.where(l1 == v1, lane, float(LANES)), axis=-1, keepdims=True)
    l2 = jnp.where(lane == i1, -jnp.inf, l1)
    v2 = jnp.max(l2, axis=-1, keepdims=True)
    i2 = jnp.min(jnp.where(l2 == v2, lane, float(LANES)), axis=-1, keepdims=True)
    t = jnp.exp(v2 - v1)
    p1 = 1.0 / (1.0 + t)
    p2 = t / (1.0 + t)

    member = jnp.where(lane == i1, 1.0, jnp.where(lane == i2, 1.0, 0.0))
    rr = lax.broadcasted_iota(jnp.int32, (tm, tm), 0)
    cc = lax.broadcasted_iota(jnp.int32, (tm, tm), 1)
    earlier = jnp.where(cc < rr, 1.0, 0.0).astype(BF16)
    base = carry_scr[0:1, :]
    rank = jnp.dot(earlier, member.astype(BF16), preferred_element_type=F32) + base
    total = base + jnp.sum(member, axis=0, keepdims=True)
    carry_scr[0:1, :] = total
    r1 = jnp.sum(jnp.where(lane == i1, rank, 0.0), axis=-1, keepdims=True)
    r2 = jnp.sum(jnp.where(lane == i2, rank, 0.0), axis=-1, keepdims=True)

    meta = jnp.zeros_like(logits)
    for k, val in ((META_I1, i1), (META_I2, i2), (META_R1, r1), (META_R2, r2), (META_P1, p1), (META_P2, p2)):
        meta = jnp.where(lane == float(k), val, meta)
    meta_ref[...] = meta
    meta_t_ref[...] = meta.T[0:SUBLANES, :]
    cnt_ref[...] = jnp.broadcast_to(total, cnt_ref.shape)


def _outproj_route(a2, b2, w_o, x2, mod, gate_col, ln_g, ln_b, shift_col, w_router_pad, *, seq_len):
    m, d = x2.shape
    half = w_o.shape[0] // 2
    tm = ROUTE_TM
    batch = lambda col: pl.BlockSpec((None, 1, d), lambda i: ((i * tm) // seq_len, 0, col))
    return pl.pallas_call(
        _outproj_route_body,
        grid=(m // tm,),
        in_specs=[
            pl.BlockSpec((tm, half), lambda i: (i, 0)),
            pl.BlockSpec((tm, half), lambda i: (i, 0)),
            pl.BlockSpec((2 * half, d), lambda i: (0, 0)),
            pl.BlockSpec((tm, d), lambda i: (i, 0)),
            batch(gate_col),
            pl.BlockSpec((1, d), lambda i: (0, 0)),
            pl.BlockSpec((1, d), lambda i: (0, 0)),
            batch(shift_col),
            batch(shift_col + 1),
            pl.BlockSpec((d, LANES), lambda i: (0, 0)),
        ],
        out_specs=[pl.BlockSpec((tm, d), lambda i: (i, 0)),
                   pl.BlockSpec((tm, LANES), lambda i: (i, 0)),
                   pl.BlockSpec((SUBLANES, tm), lambda i: (0, i)),
                   pl.BlockSpec((8, LANES), lambda i: (0, 0))],
        out_shape=[jax.ShapeDtypeStruct((m, d), F32),
                   jax.ShapeDtypeStruct((m, LANES), F32), jax.ShapeDtypeStruct((SUBLANES, m), F32),
                   jax.ShapeDtypeStruct((8, LANES), F32)],
        scratch_shapes=[pltpu.VMEM((8, LANES), F32)],
        compiler_params=_cparams(("arbitrary",)),
        name="odd_outproj_route",
    )(a2, b2, w_o, x2, mod, ln_g, ln_b, mod, mod, w_router_pad)


def _to_tiles(tile_ref, value):
    n = value.shape[0]
    for j in range(SUBLANES):
        tile_ref[pl.ds(j, n, stride=SUBLANES), :] = value[:, LANES * j:LANES * (j + 1)]


def _from_tiles(tile_ref):
    n = tile_ref.shape[0] // SUBLANES
    return jnp.concatenate([tile_ref[pl.ds(j, n, stride=SUBLANES), :] for j in range(SUBLANES)], axis=-1)


def _tile_rows(ref, row, n=1):
    return ref.at[pl.ds(pl.multiple_of(row * SUBLANES, SUBLANES), n * SUBLANES), :]


def _row_copy(src, src_row, dst, dst_row, sem):
    return pltpu.make_async_copy(_tile_rows(src, src_row), _tile_rows(dst, dst_row), sem)


def _rows_wait(src, dst, n, sem):
    pltpu.make_async_copy(_tile_rows(src, 0, n), _tile_rows(dst, 0, n), sem).wait()


def _dispatch_body(pad_ref, pos_hbm, x_ref, sh_ref, sc_ref, xs_hbm, h_scr, z_scr, pos_smem, pos_sem, row_sem):
    step = pl.program_id(0)
    tm = x_ref.shape[0]
    pos_cp = pltpu.make_async_copy(pos_hbm.at[step], pos_smem, pos_sem)
    pos_cp.start()
    _to_tiles(h_scr, x_ref[...] * (1.0 + sc_ref[...]) + sh_ref[...])
    pos_cp.wait()

    def issue(r, carry):
        _row_copy(h_scr, r, xs_hbm, pos_smem[r], row_sem).start(priority=0)
        _row_copy(h_scr, r, xs_hbm, pos_smem[tm + r], row_sem).start(priority=1)
        return carry

    lax.fori_loop(0, tm, issue, 0, unroll=8)
    _rows_wait(h_scr, xs_hbm, tm, row_sem)
    _rows_wait(h_scr, xs_hbm, tm, row_sem)

    @pl.when(step == pl.num_programs(0) - 1)
    def _():
        z_scr[...] = jnp.zeros_like(z_scr)
        for e in range(N_EXPERTS):
            start = pad_ref[e]
            count = pad_ref[N_EXPERTS + e]

            def fill(k, carry, start=start):
                _row_copy(z_scr, 0, xs_hbm, start + k, row_sem).start()
                return carry

            def fill_wait(k, carry):
                _row_copy(z_scr, 0, xs_hbm, 0, row_sem).wait()
                return carry

            lax.fori_loop(0, count, fill, 0)
            lax.fori_loop(0, count, fill_wait, 0)

        zrows = z_scr.shape[0] // SUBLANES
        used_rows = pad_ref[2 * N_EXPERTS]
        n_chunks = (xs_hbm.shape[0] // SUBLANES - used_rows) // zrows

        def chunk_copy(k):
            return pltpu.make_async_copy(z_scr, _tile_rows(xs_hbm, used_rows + k * zrows, zrows), row_sem)

        def fill_chunk(k, carry):
            chunk_copy(k).start()
            return carry

        def fill_chunk_wait(k, carry):
            chunk_copy(k).wait()
            return carry

        lax.fori_loop(0, n_chunks, fill_chunk, 0)
        lax.fori_loop(0, n_chunks, fill_chunk_wait, 0)


def _dispatch(x2, mod, shift_col, pos, pad_info, n_rows, *, seq_len):
    m, d = x2.shape
    tm = ROUTE_TM
    return pl.pallas_call(
        _dispatch_body,
        grid_spec=pltpu.PrefetchScalarGridSpec(
            num_scalar_prefetch=1,
            grid=(m // tm,),
            in_specs=[
                pl.BlockSpec(memory_space=pl.ANY),
                pl.BlockSpec((tm, d), lambda i, pad: (i, 0)),
                pl.BlockSpec((None, 1, d), lambda i, pad: ((i * tm) // seq_len, 0, shift_col)),
                pl.BlockSpec((None, 1, d), lambda i, pad: ((i * tm) // seq_len, 0, shift_col + 1)),
            ],
            out_specs=pl.BlockSpec(memory_space=pl.ANY),
            scratch_shapes=[
                pltpu.VMEM((tm * SUBLANES, LANES), F32),
                pltpu.VMEM((ZERO_ROWS * SUBLANES, LANES), F32),
                pltpu.SMEM((2 * tm,), jnp.int32),
                pltpu.SemaphoreType.DMA,
                pltpu.SemaphoreType.DMA,
            ],
        ),
        out_shape=jax.ShapeDtypeStruct((n_rows * SUBLANES, LANES), F32),
        compiler_params=_cparams(("arbitrary",)),
        name="moe_dispatch",
    )(pad_info, pos, x2, mod, mod)


def _experts_body(te_ref, tbi_ref, tbo_ref, tv_ref, xs_ref, wg_ref, wu_ref, wd_ref, ys_ref, h_scr, acc_scr, *, n_f):
    t = pl.program_id(0)
    f = pl.program_id(1)

    @pl.when(f == 0)
    def _():
        h_scr[...] = _from_tiles(xs_ref).astype(BF16)
        acc_scr[...] = jnp.zeros_like(acc_scr)

    @pl.when(tv_ref[t] > 0)
    def _():
        h = h_scr[...]
        g = jnp.dot(h, wg_ref[...].astype(BF16), preferred_element_type=F32)
        u = jnp.dot(h, wu_ref[...].astype(BF16), preferred_element_type=F32)
        a = (g * jax.nn.sigmoid(g) * u).astype(BF16)
        acc_scr[...] += jnp.dot(a, wd_ref[...].astype(BF16), preferred_element_type=F32)

    @pl.when(f == n_f - 1)
    def _():
        _to_tiles(ys_ref, acc_scr[...])


def _experts(xs, tile_expert, tile_in, tile_out, tile_valid, wg, wu, wd, *, tf):
    n_rows = xs.shape[0] // SUBLANES
    d = wg.shape[1]
    tm = MOE_TM
    n_tiles = tile_expert.shape[0]
    ff = wg.shape[2]
    n_f = ff // tf
    return pl.pallas_call(
        functools.partial(_experts_body, n_f=n_f),
        grid_spec=pltpu.PrefetchScalarGridSpec(
            num_scalar_prefetch=4,
            grid=(n_tiles, n_f),
            in_specs=[
                pl.BlockSpec((tm * SUBLANES, LANES), lambda t, f, te, tbi, tbo, tv: (tbi[t], 0)),
                pl.BlockSpec((None, d, tf), lambda t, f, te, tbi, tbo, tv: (te[t], 0, f * tv[t])),
                pl.BlockSpec((None, d, tf), lambda t, f, te, tbi, tbo, tv: (te[t], 0, f * tv[t])),
                pl.BlockSpec((None, tf, d), lambda t, f, te, tbi, tbo, tv: (te[t], f * tv[t], 0)),
            ],
            out_specs=pl.BlockSpec((tm * SUBLANES, LANES), lambda t, f, te, tbi, tbo, tv: (tbo[t], 0)),
            scratch_shapes=[pltpu.VMEM((tm, d), BF16), pltpu.VMEM((tm, d), F32)],
        ),
        out_shape=jax.ShapeDtypeStruct((n_rows * SUBLANES, LANES), F32),
        compiler_params=_cparams(("arbitrary", "arbitrary")),
        name="moe_experts",
    )(tile_expert, tile_in, tile_out, tile_valid, xs, wg, wu, wd)


def _combine_body(pos_hbm, ys_hbm, meta_ref, x_ref, gate_ref, lng_ref, lnb_ref, o_ref,
                  y1_scr, y2_scr, pos_smem, pos_sem, row_sem):
    step = pl.program_id(0)
    tm = x_ref.shape[0]
    pos_cp = pltpu.make_async_copy(pos_hbm.at[step], pos_smem, pos_sem)
    pos_cp.start()
    pos_cp.wait()

    def issue(r, carry):
        _row_copy(ys_hbm, pos_smem[r], y1_scr, r, row_sem).start(priority=0)
        _row_copy(ys_hbm, pos_smem[tm + r], y2_scr, r, row_sem).start(priority=1)
        return carry

    lax.fori_loop(0, tm, issue, 0, unroll=8)
    _rows_wait(ys_hbm, y1_scr, tm, row_sem)
    _rows_wait(ys_hbm, y2_scr, tm, row_sem)

    meta = meta_ref[...]
    p1 = meta[:, META_P1:META_P1 + 1]
    p2 = meta[:, META_P2:META_P2 + 1]
    mix = p1 * _from_tiles(y1_scr) + p2 * _from_tiles(y2_scr)
    r = DEEPNORM_ALPHA * x_ref[...] + gate_ref[...] * mix
    o_ref[...] = _layer_norm(r, lng_ref[...], lnb_ref[...])


def _combine_ln(ys, pos, meta, x2, mod, gate_col, ln_g, ln_b, *, seq_len):
    m, d = x2.shape
    tm = ROUTE_TM
    return pl.pallas_call(
        _combine_body,
        grid=(m // tm,),
        in_specs=[
            pl.BlockSpec(memory_space=pl.ANY),
            pl.BlockSpec(memory_space=pl.ANY),
            pl.BlockSpec((tm, LANES), lambda i: (i, 0)),
            pl.BlockSpec((tm, d), lambda i: (i, 0)),
            pl.BlockSpec((None, 1, d), lambda i: ((i * tm) // seq_len, 0, gate_col)),
            pl.BlockSpec((1, d), lambda i: (0, 0)),
            pl.BlockSpec((1, d), lambda i: (0, 0)),
        ],
        out_specs=pl.BlockSpec((tm, d), lambda i: (i, 0)),
        out_shape=jax.ShapeDtypeStruct((m, d), F32),
        scratch_shapes=[
            pltpu.VMEM((tm * SUBLANES, LANES), F32),
            pltpu.VMEM((tm * SUBLANES, LANES), F32),
            pltpu.SMEM((2 * tm,), jnp.int32),
            pltpu.SemaphoreType.DMA,
            pltpu.SemaphoreType.DMA,
        ],
        compiler_params=_cparams(("arbitrary",)),
        name="moe_combine",
    )(pos, ys, meta, x2, mod, ln_g, ln_b)


def _outproj_moe_ln(a, bsrc, w_o, x, mod, ln_g0, ln_b0, w_router, wg, wu, wd, ln_g, ln_b, *, tf):
    b, l, d = x.shape
    assert d == SUBLANES * LANES
    m = b * l
    w_r = jnp.zeros((d, LANES), BF16).at[:, :N_EXPERTS].set(w_router.astype(BF16))
    x2, meta, meta_t, cnt = _outproj_route(a.reshape(m, -1), bsrc.reshape(m, -1), w_o, x.reshape(m, d), mod, 2,
                                           ln_g0, ln_b0, 3, w_r, seq_len=l)

    counts = cnt[0, :N_EXPERTS].astype(jnp.int32)
    n_tile_e = (counts + MOE_TM - 1) // MOE_TM
    tile_end = jnp.cumsum(n_tile_e)
    offs = (tile_end - n_tile_e) * MOE_TM
    idx = meta_t[META_I1:META_I2 + 1].astype(jnp.int32)
    rank = meta_t[META_R1:META_R2 + 1].astype(jnp.int32)
    base = jnp.zeros_like(idx)
    for e in range(N_EXPERTS):
        base = jnp.where(idx == e, offs[e], base)
    pos = (base + rank).reshape(2, m // ROUTE_TM, ROUTE_TM).transpose(1, 0, 2).reshape(m // ROUTE_TM, 2 * ROUTE_TM)
    n_tiles = (2 * m) // MOE_TM + N_EXPERTS
    tid = jnp.arange(n_tiles, dtype=jnp.int32)
    tile_valid = (tid < tile_end[-1]).astype(jnp.int32)
    tile_expert = jnp.minimum(jnp.sum((tid[:, None] >= tile_end[None, :]).astype(jnp.int32), axis=1),
                              N_EXPERTS - 1).astype(jnp.int32)
    tile_in = jnp.where(tile_valid > 0, tid, 0).astype(jnp.int32)
    tile_out = tid
    pad_info = jnp.concatenate([offs + counts, n_tile_e * MOE_TM - counts,
                                tile_end[-1:] * MOE_TM]).astype(jnp.int32)
    n_rows = n_tiles * MOE_TM

    xs = _dispatch(x2, mod, 3, pos, pad_info, n_rows, seq_len=l)
    ys = _experts(xs, tile_expert, tile_in, tile_out, tile_valid, wg, wu, wd, tf=tf)
    out = _combine_ln(ys, pos, meta, x2, mod, 5, ln_g, ln_b, seq_len=l)
    return out.reshape(b, l, d)


def _fourier_body(pc_ref, dl_ref, dc_ref, o_ref, t_scr, *, out_scale, row_chunk):
    l = pc_ref.shape[0]
    for g in range(C_GROUPS):
        sl = slice(C_GROUP_DIM * g, C_GROUP_DIM * (g + 1))
        xg = pc_ref[:, sl].astype(F32)
        mu = jnp.mean(xg, axis=-1, keepdims=True)
        dlt = xg - mu
        var = jnp.mean(dlt * dlt, axis=-1, keepdims=True)
        gn = (dlt * lax.rsqrt(var + LN_EPS)).astype(BF16)
        t = jnp.dot(gn, dc_ref[...], preferred_element_type=F32)
        t_scr[0:l, sl] = t[:, 0:C_GROUP_DIM].astype(BF16)
        t_scr[l:2 * l, sl] = t[:, C_GROUP_DIM:2 * C_GROUP_DIM].astype(BF16)
    for r0 in range(0, l, row_chunk):
        acc = jnp.dot(dl_ref[r0:r0 + row_chunk, :], t_scr[...], preferred_element_type=F32)
        o_ref[r0:r0 + row_chunk, :] = (acc * out_scale).astype(BF16)


def _dft_matrices(l, c):
    j = np.arange(l, dtype=np.int64)
    ang_l = (2.0 * np.pi / l) * ((j[:, None] * j[None, :]) % l)
    dl = np.concatenate([np.cos(ang_l), -np.sin(ang_l)], axis=1)
    m = np.arange(c, dtype=np.int64)
    ang_c = (2.0 * np.pi / c) * ((m[:, None] * m[None, :]) % c)
    dc = np.concatenate([np.cos(ang_c), np.sin(ang_c)], axis=1)
    return dl.astype(np.float32), dc.astype(np.float32)


def _fourier_mixer(y, name="fourier"):
    b, l, _ = y.shape
    dl_np, dc_np = _dft_matrices(l, C_GROUP_DIM)
    dl = jnp.asarray(dl_np, dtype=F32).astype(BF16)
    dc = jnp.asarray(dc_np, dtype=F32).astype(BF16)
    out_scale = 1.0 / math.sqrt(l * C_GROUP_DIM)
    return pl.pallas_call(
        functools.partial(_fourier_body, out_scale=out_scale, row_chunk=min(l, 512)),
        grid=(b,),
        in_specs=[
            pl.BlockSpec((None, l, C_WIDTH), lambda bi: (bi, 0, 0)),
            pl.BlockSpec((l, 2 * l), lambda bi: (0, 0), pipeline_mode=pl.Buffered(1)),
            pl.BlockSpec((C_GROUP_DIM, 2 * C_GROUP_DIM), lambda bi: (0, 0)),
        ],
        out_specs=pl.BlockSpec((None, l, C_WIDTH), lambda bi: (bi, 0, 0)),
        out_shape=jax.ShapeDtypeStruct((b, l, C_WIDTH), BF16),
        scratch_shapes=[pltpu.VMEM((2 * l, C_WIDTH), BF16)],
        compiler_params=_cparams(("parallel",)),
        name=name,
    )(y, dl, dc)


NA_HG = 4
NA_GW = NA_HG * NA_DH
NA_WIN = NA_KR * GRID_W


def _natten_body(q_ref, k_ref, v_ref, kc_ref, vc_ref, bias_ref, o_ref, *, rows_per_step, n_rows):
    rb = pl.program_id(2)
    hq = NA_HG * GRID_W
    rid = lax.broadcasted_iota(jnp.int32, (hq, NA_GW), 0)
    cid = lax.broadcasted_iota(jnp.int32, (hq, NA_GW), 1)
    diag = (rid // GRID_W) == (cid // NA_DH)
    cid_o = lax.broadcasted_iota(jnp.int32, (GRID_W, NA_GW), 1)
    kc = kc_ref[...]
    vc = vc_ref[...]
    staged = []
    for j in range(rows_per_step):
        r = rb * rows_per_step + j
        rs = jnp.clip(r - NA_KR // 2, 0, n_rows - NA_KR)
        start = pl.multiple_of(rs * GRID_W, GRID_W)
        tid = jnp.minimum(r, NA_KR // 2) + jnp.maximum(r - (n_rows - NA_KR // 2), 0)
        q_r = q_ref[GRID_W * j:GRID_W * (j + 1), :]
        q4 = jnp.concatenate([q_r] * NA_HG, axis=0)
        qbd = jnp.where(diag, q4, jnp.zeros_like(q4))
        kw = k_ref[pl.ds(start, NA_WIN), :]
        s_loc = lax.dot_general(qbd, kw, NT_DIMS, preferred_element_type=F32) + bias_ref[tid]
        s_ctx = lax.dot_general(qbd, kc, NT_DIMS, preferred_element_type=F32)
        staged.append((start, s_loc, s_ctx))
    for j in range(rows_per_step):
        start, s_loc, s_ctx = staged[j]
        vw = v_ref[pl.ds(start, NA_WIN), :]
        m = jnp.maximum(jnp.max(s_loc, axis=-1, keepdims=True), jnp.max(s_ctx, axis=-1, keepdims=True))
        e_loc = jnp.exp2(s_loc - m)
        e_ctx = jnp.exp2(s_ctx - m)
        tot = jnp.sum(e_loc, axis=-1, keepdims=True) + jnp.sum(e_ctx, axis=-1, keepdims=True)
        o = (jnp.dot(e_loc.astype(BF16), vw, preferred_element_type=F32)
             + jnp.dot(e_ctx.astype(BF16), vc, preferred_element_type=F32))
        o = o * (1.0 / tot)
        out = jnp.zeros((GRID_W, NA_GW), F32)
        for hh in range(NA_HG):
            out = out + jnp.where((cid_o // NA_DH) == hh, o[GRID_W * hh:GRID_W * (hh + 1), :], 0.0)
        o_ref[GRID_W * j:GRID_W * (j + 1), :] = out.astype(BF16)


def _na_bias_table(rpb, n_rows):
    h = rpb.shape[0]
    cols = jnp.arange(GRID_W)
    col_start = jnp.clip(cols - NA_KC // 2, 0, GRID_W - NA_KC)
    col_valid = (cols[None, :] >= col_start[:, None]) & (cols[None, :] < col_start[:, None] + NA_KC)
    dc_idx = jnp.clip(cols[None, :] - cols[:, None] + NA_KC - 1, 0, 2 * NA_KC - 2)
    onehot = (dc_idx[:, :, None] == jnp.arange(2 * NA_KC - 1)).astype(F32)
    rpb_c = jnp.einsum("qkc,hdc->hqdk", onehot, rpb.astype(F32), precision=lax.Precision.HIGHEST)
    full = jnp.where(col_valid[None, :, None, :], rpb_c * LOG2E, NEG_INF)
    full = full.reshape(h // NA_HG, NA_HG * GRID_W, (2 * NA_KR - 1) * GRID_W)
    half = NA_KR // 2
    rep_rows = list(range(half)) + [half] + list(range(n_rows - half + 1, n_rows))
    tabs = []
    for r in rep_rows:
        rs = min(max(r - half, 0), n_rows - NA_KR)
        d0 = rs - r + NA_KR - 1
        tabs.append(full[:, :, GRID_W * d0:GRID_W * d0 + NA_WIN])
    return jnp.stack(tabs, axis=0)


def _natten(y, q_col0, k_col0, v_col0, y_ctx, kc_col0, vc_col0, bias_tab, *, rows_per_step, name="natten"):
    b, l, _ = y.shape
    lc = y_ctx.shape[1]
    n_rows = l // GRID_W
    n_tab = bias_tab.shape[0]
    n_grp = NA_HEADS // NA_HG
    tq = rows_per_step * GRID_W
    qb, kb, vb = q_col0 // NA_GW, k_col0 // NA_GW, v_col0 // NA_GW
    kcb, vcb = kc_col0 // NA_GW, vc_col0 // NA_GW
    return pl.pallas_call(
        functools.partial(_natten_body, rows_per_step=rows_per_step, n_rows=n_rows),
        grid=(b, n_grp, n_rows // rows_per_step),
        in_specs=[
            pl.BlockSpec((None, tq, NA_GW), lambda bi, g, i: (bi, i, qb + g)),
            pl.BlockSpec((None, l, NA_GW), lambda bi, g, i: (bi, 0, kb + g)),
            pl.BlockSpec((None, l, NA_GW), lambda bi, g, i: (bi, 0, vb + g)),
            pl.BlockSpec((None, lc, NA_GW), lambda bi, g, i: (bi, 0, kcb + g)),
            pl.BlockSpec((None, lc, NA_GW), lambda bi, g, i: (bi, 0, vcb + g)),
            pl.BlockSpec((n_tab, None, NA_HG * GRID_W, NA_WIN), lambda bi, g, i: (0, g, 0, 0)),
        ],
        out_specs=pl.BlockSpec((None, tq, NA_GW), lambda bi, g, i: (bi, i, g)),
        out_shape=jax.ShapeDtypeStruct((b, l, D_WIDTH), BF16),
        compiler_params=_cparams(("parallel", "parallel", "arbitrary")),
        name=name,
    )(y, y, y, y_ctx, y_ctx, bias_tab)


def _rope_tables(l):
    t = jnp.arange(l, dtype=jnp.int32)
    row = (t // GRID_W).astype(F32)
    col = (t % GRID_W).astype(F32)
    n_freq = DIFF_DH // 4
    inv_freq = ROPE_THETA ** (-jnp.arange(n_freq, dtype=F32) / n_freq)
    ang = jnp.concatenate([row[:, None] * inv_freq, col[:, None] * inv_freq], axis=-1)
    c, s = jnp.cos(ang), jnp.sin(ang)
    cos = jnp.tile(jnp.concatenate([c, c], axis=-1), (1, 2))
    sin = jnp.tile(jnp.concatenate([-s, s], axis=-1), (1, 2))
    return cos, sin


def kernel(x, c, ctx, c_ctx, w_mod, b_mod, ln_g, ln_b, e_w_in, e_conv, e_lam_q1, e_lam_k1, e_lam_q2, e_lam_k2, e_subln_g, e_w_o, e_ffn_gate, e_ffn_up, e_ffn_down, o_w_in, o_rpb, o_w_o, o_router, o_exp_gate, o_exp_up, o_exp_down):
    b, l, d = x.shape
    lc = ctx.shape[1]
    assert d == D_MODEL and l % 512 == 0 and lc % 256 == 0 and b + 1 <= MOD_ROWS

    cond = jnp.concatenate([c, c_ctx[None, :], jnp.zeros((MOD_ROWS - b - 1, d), F32)], axis=0)
    mods = _adaln(cond, w_mod, b_mod)

    def layer_mods(i):
        lat = mods[i, :b][:, None, :]
        cx = jnp.broadcast_to(mods[i, b][None, None, :], (b, 1, 6 * d))
        return lat, cx

    q_scale_diff = DIFF_DH ** -0.5 * LOG2E
    q_scale_na = NA_DH ** -0.5 * LOG2E

    mod_lat, mod_ctx = layer_mods(0)
    lam_init = 0.8 - 0.6 * math.exp(-0.3 * 0)
    a_end = 3 * A_WIDTH
    w_in = e_w_in[0]
    w_main = w_in[:, :a_end + 2 * DIFF_QK].astype(BF16)
    w_vt = w_in[:, a_end + 2 * DIFF_QK:].T.astype(BF16)
    rope = _rope_tables(l)
    pa_chunks = [(0, 512, "plain"), (512, 512, "plain"), (1024, 512, "plain")]
    y_lat, vt_lat = _proj(x, mod_lat, 0, w_main,
                          pa_chunks + [(a_end, 512, "rope_scale"), (a_end + 512, 512, "rope")],
                          tm=512, wvt=w_vt, rope=rope, q_scale=q_scale_diff, name="even_inproj_lat")
    y_ctx, vt_ctx = _proj(ctx, mod_ctx, 0, w_main,
                          pa_chunks + [(a_end, 512, "scale"), (a_end + 512, 512, "plain")],
                          tm=lc, wvt=w_vt, q_scale=q_scale_diff, name="even_inproj_ctx")

    lam_pack = jnp.zeros((8, LANES), F32)
    lam_pack = lam_pack.at[0, :DIFF_DH].set(e_lam_q1[0]).at[1, :DIFF_DH].set(e_lam_k1[0])
    lam_pack = lam_pack.at[2, :DIFF_DH].set(e_lam_q2[0]).at[3, :DIFF_DH].set(e_lam_k2[0])
    sub_g = e_subln_g[0].reshape(1, DIFF_VDIM)
    o_lat = _diff_attention(y_lat, [(y_ctx, vt_ctx), (y_lat, vt_lat)], lam_pack, sub_g, lam_init,
                            tq=256, q_col0=a_end, k_col0=a_end + DIFF_QK, name="diffattn_lat")
    o_ctx = _diff_attention(y_ctx, [(y_ctx, vt_ctx)], lam_pack, sub_g, lam_init,
                            tq=lc, q_col0=a_end, k_col0=a_end + DIFF_QK, name="diffattn_ctx")

    w_o = e_w_o[0].astype(BF16)
    lng0, lnb0 = ln_g[0, 0][None, :], ln_b[0, 0][None, :]
    lng1, lnb1 = ln_g[0, 1][None, :], ln_b[0, 1][None, :]
    x_lat = _conv_outproj_ln(y_lat, e_conv[0], o_lat, w_o, x, mod_lat, 2, lng0, lnb0, tm=512,
                             name="even_outproj_lat")
    x_ctx = _conv_outproj_ln(y_ctx, e_conv[0], o_ctx, w_o, ctx, mod_ctx, 2, lng0, lnb0, tm=lc,
                             name="even_outproj_ctx")

    wg = e_ffn_gate[0].astype(BF16)
    wu = e_ffn_up[0].astype(BF16)
    wd = e_ffn_down[0].astype(BF16)
    x_lat = _ffn_ln(x_lat, mod_lat, 3, wg, wu, wd, lng1, lnb1, tm=512, n_chunks=2, name="ffn_lat")
    x_ctx = _ffn_ln(x_ctx, mod_ctx, 3, wg, wu, wd, lng1, lnb1, tm=lc, n_chunks=2, name="ffn_ctx")

    mod_lat, mod_ctx = layer_mods(1)
    w_in = o_w_in[0].astype(BF16)
    y_lat = _proj(x_lat, mod_lat, 0, w_in,
                  [(0, 512, "plain"), (512, 512, "scale"), (1024, 512, "plain"), (1536, 512, "plain")],
                  tm=512, q_scale=q_scale_na, name="odd_inproj_lat")
    y_ctx = _proj(x_ctx, mod_ctx, 0, w_in[:, C_WIDTH + D_WIDTH:],
                  [(0, 512, "plain"), (512, 512, "plain")], tm=lc, name="odd_inproj_ctx")
    f_lat = _fourier_mixer(y_lat)
    bias_tab = _na_bias_table(o_rpb[0], l // GRID_W)
    n_lat = _natten(y_lat, C_WIDTH, C_WIDTH + D_WIDTH, C_WIDTH + 2 * D_WIDTH, y_ctx, 0, D_WIDTH,
                    bias_tab, rows_per_step=8)

    w_o = o_w_o[0].astype(BF16)
    lng0, lnb0 = ln_g[1, 0][None, :], ln_b[1, 0][None, :]
    lng1, lnb1 = ln_g[1, 1][None, :], ln_b[1, 1][None, :]
    return _outproj_moe_ln(f_lat, n_lat, w_o, x_lat, mod_lat, lng0, lnb0, o_router[0],
                           o_exp_gate[0], o_exp_up[0], o_exp_down[0], lng1, lnb1, tf=512)
```

```python
import functools
import math

import numpy as np
import jax
import jax.numpy as jnp
from jax import lax
from jax.experimental import pallas as pl
from jax.experimental.pallas import tpu as pltpu

F32 = jnp.float32
BF16 = jnp.bfloat16

D_MODEL = 1024
GRID_W = 64
DEPTH = 2

A_WIDTH = 512
DIFF_HEADS = 4
DIFF_DH = 64
DIFF_VDIM = 128
DIFF_QK = 512
B_WIDTH = 512

C_WIDTH = 512
C_GROUPS = 4
C_GROUP_DIM = 128
NA_HEADS = 8
NA_DH = 64
D_WIDTH = 512
NA_KR = 8
NA_KC = 16

N_EXPERTS = 8

ROPE_THETA = 10000.0
LN_EPS = 1e-5
RMS_EPS = 1e-5
NEG_INF = -1e30
DEEPNORM_ALPHA = (2 * DEPTH) ** 0.25
LOG2E = 1.4426950408889634

LANES = 128
SUBLANES = 8
MOD_ROWS = 32
NT_DIMS = (((1,), (1,)), ((), ()))


def _cparams(sem, vmem_mb=48):
    return pltpu.CompilerParams(dimension_semantics=sem, vmem_limit_bytes=vmem_mb * 1024 * 1024)


def _layer_norm(r, g, b):
    mu = jnp.mean(r, axis=-1, keepdims=True)
    d = r - mu
    var = jnp.mean(d * d, axis=-1, keepdims=True)
    return d * lax.rsqrt(var + LN_EPS) * g + b


def _adaln_body(c_ref, w_ref, b_ref, o_ref):
    cnd = c_ref[...]
    s = (cnd * jax.nn.sigmoid(cnd)).astype(BF16)
    o_ref[...] = jnp.dot(s, w_ref[...].astype(BF16), preferred_element_type=F32) + b_ref[...]


def _adaln(cond, w_mod, b_mod):
    depth, d, n = w_mod.shape
    tn = 1536
    return pl.pallas_call(
        _adaln_body,
        grid=(depth, n // tn),
        in_specs=[
            pl.BlockSpec((MOD_ROWS, d), lambda l, j: (0, 0)),
            pl.BlockSpec((None, d, tn), lambda l, j: (l, 0, j)),
            pl.BlockSpec((None, 1, tn), lambda l, j: (l, 0, j)),
        ],
        out_specs=pl.BlockSpec((None, MOD_ROWS, tn), lambda l, j: (l, 0, j)),
        out_shape=jax.ShapeDtypeStruct((depth, MOD_ROWS, n), F32),
        compiler_params=_cparams(("parallel", "parallel")),
        name="adaln",
    )(cond, w_mod, b_mod.reshape(depth, 1, n))


def _proj_body(*refs, chunks, with_vt, with_rope, q_scale):
    x_ref, sh_ref, sc_ref, w_ref = refs[:4]
    i = 4
    if with_vt:
        wvt_ref = refs[i]
        i += 1
    if with_rope:
        cos_ref, sin_ref = refs[i], refs[i + 1]
        i += 2
    y_ref = refs[i]
    vt_ref = refs[i + 1] if with_vt else None

    h = (x_ref[...] * (1.0 + sc_ref[...]) + sh_ref[...]).astype(BF16)
    tm = h.shape[0]
    if with_rope:
        cos = cos_ref[...]
        sin = sin_ref[...]
        lane = lax.broadcasted_iota(jnp.int32, (tm, LANES), 1)
        low_half = (lane % 64) < 32
    for (c0, width, kind) in chunks:
        acc = jnp.dot(h, w_ref[:, c0:c0 + width], preferred_element_type=F32)
        if kind in ("rope", "rope_scale"):
            for j in range(width // LANES):
                a = acc[:, LANES * j:LANES * (j + 1)]
                rot = jnp.where(low_half, pltpu.roll(a, 96, 1), pltpu.roll(a, 32, 1))
                r = a * cos + rot * sin
                if kind == "rope_scale":
                    r = r * q_scale
                y_ref[:, c0 + LANES * j:c0 + LANES * (j + 1)] = r.astype(BF16)
        elif kind == "scale":
            y_ref[:, c0:c0 + width] = (acc * q_scale).astype(BF16)
        else:
            y_ref[:, c0:c0 + width] = acc.astype(BF16)
    if with_vt:
        vt = lax.dot_general(wvt_ref[...], h, NT_DIMS, preferred_element_type=F32)
        vt_ref[...] = vt.astype(BF16)


def _proj(x, mod, shift_col, w, chunks, *, tm, wvt=None, rope=None, q_scale=1.0, name="proj"):
    b, l, d = x.shape
    n = w.shape[1]
    with_vt = wvt is not None
    with_rope = rope is not None
    in_specs = [
        pl.BlockSpec((None, tm, d), lambda bi, i: (bi, i, 0)),
        pl.BlockSpec((None, 1, d), lambda bi, i: (bi, 0, shift_col)),
        pl.BlockSpec((None, 1, d), lambda bi, i: (bi, 0, shift_col + 1)),
        pl.BlockSpec((d, n), lambda bi, i: (0, 0)),
    ]
    args = [x, mod, mod, w]
    if with_vt:
        nv = wvt.shape[0]
        in_specs.append(pl.BlockSpec((nv, d), lambda bi, i: (0, 0)))
        args.append(wvt)
    if with_rope:
        in_specs += [pl.BlockSpec((tm, LANES), lambda bi, i: (i, 0))] * 2
        args += [rope[0], rope[1]]
    out_specs = [pl.BlockSpec((None, tm, n), lambda bi, i: (bi, i, 0))]
    out_shape = [jax.ShapeDtypeStruct((b, l, n), BF16)]
    if with_vt:
        out_specs.append(pl.BlockSpec((None, nv, tm), lambda bi, i: (bi, 0, i)))
        out_shape.append(jax.ShapeDtypeStruct((b, nv, l), BF16))
    res = pl.pallas_call(
        functools.partial(_proj_body, chunks=tuple(chunks), with_vt=with_vt, with_rope=with_rope,
                          q_scale=q_scale),
        grid=(b, l // tm),
        in_specs=in_specs,
        out_specs=out_specs,
        out_shape=out_shape,
        compiler_params=_cparams(("parallel", "parallel")),
        name=name,
    )(*args)
    return res if with_vt else res[0]


DIFF_HP = 4


def _diffattn_body(*refs, n_seg, lam_init):
    q_ref, lam_ref, g_ref = refs[:3]
    k_refs = [refs[3 + 2 * s] for s in range(n_seg)]
    vt_refs = [refs[4 + 2 * s] for s in range(n_seg)]
    o_ref = refs[3 + 2 * n_seg]

    tq = q_ref.shape[0]
    lane = lax.broadcasted_iota(jnp.int32, (tq, LANES), 1)
    lp = lam_ref[...]
    lam = (jnp.exp(jnp.sum(lp[0:1] * lp[1:2], axis=1, keepdims=True))
           - jnp.exp(jnp.sum(lp[2:3] * lp[3:4], axis=1, keepdims=True)) + lam_init)

    scores = []
    for h in range(DIFF_HP):
        cols = slice(LANES * h, LANES * (h + 1))
        q = q_ref[:, cols]
        zero = jnp.zeros_like(q)
        per_comp = []
        for qm in (jnp.where(lane < DIFF_DH, q, zero), jnp.where(lane >= DIFF_DH, q, zero)):
            per_comp.append([lax.dot_general(k_ref[:, cols], qm, NT_DIMS, preferred_element_type=F32)
                             for k_ref in k_refs])
        scores.append(per_comp)

    def unnormalised(s, rows):
        m = functools.reduce(jnp.maximum, [jnp.max(x, axis=0, keepdims=True) for x in s])
        tot = None
        acc = None
        for x, vt_ref in zip(s, vt_refs):
            e = jnp.exp2(x - m)
            t = jnp.sum(e, axis=0, keepdims=True)
            pv = jnp.dot(vt_ref[rows, :], e.astype(BF16), preferred_element_type=F32)
            tot = t if tot is None else tot + t
            acc = pv if acc is None else acc + pv
        return acc, tot

    for h in range(DIFF_HP):
        cols = slice(LANES * h, LANES * (h + 1))
        rows = slice(DIFF_VDIM * h, DIFF_VDIM * (h + 1))
        acc1, l1 = unnormalised(scores[h][0], rows)
        acc2, l2 = unnormalised(scores[h][1], rows)
        o_t = acc1 * (1.0 / l1) - acc2 * (lam / l2)
        o = o_t.T
        ms = jnp.mean(o * o, axis=-1, keepdims=True)
        o_ref[:, cols] = (o * lax.rsqrt(ms + RMS_EPS) * g_ref[...] * (1.0 - lam_init)).astype(BF16)


def _diff_attention(yq, segs, lam_pack, sub_g, lam_init, *, tq, q_col0, k_col0, name):
    b, lq, _ = yq.shape
    width = DIFF_HP * LANES
    qb0 = q_col0 // width
    kb0 = k_col0 // width
    in_specs = [
        pl.BlockSpec((None, tq, width), lambda bi, h, i: (bi, i, qb0 + h)),
        pl.BlockSpec((8, LANES), lambda bi, h, i: (0, 0)),
        pl.BlockSpec((1, LANES), lambda bi, h, i: (0, 0)),
    ]
    args = [yq, lam_pack, sub_g]
    for (yk, vt) in segs:
        lk = yk.shape[1]
        in_specs.append(pl.BlockSpec((None, lk, width), lambda bi, h, i: (bi, 0, kb0 + h)))
        in_specs.append(pl.BlockSpec((None, DIFF_HP * DIFF_VDIM, lk), lambda bi, h, i: (bi, h, 0)))
        args += [yk, vt]
    return pl.pallas_call(
        functools.partial(_diffattn_body, n_seg=len(segs), lam_init=lam_init),
        grid=(b, DIFF_HEADS // DIFF_HP, lq // tq),
        in_specs=in_specs,
        out_specs=pl.BlockSpec((None, tq, width), lambda bi, h, i: (bi, i, h)),
        out_shape=jax.ShapeDtypeStruct((b, lq, B_WIDTH), BF16),
        compiler_params=_cparams(("parallel", "parallel", "arbitrary")),
        name=name,
    )(*args)


HALO_ROWS = 16


def _conv_outproj_body(bg_ref, cg_ref, val_ref, cgp_ref, valp_ref, cgn_ref, valn_ref, wc_ref,
                       o_att_ref, w_ref, x_ref, gate_ref, lng_ref, lnb_ref, o_ref):
    i = pl.program_id(1)
    tm = x_ref.shape[0]
    half = o_att_ref.shape[1]
    u = cg_ref[...].astype(F32) * val_ref[...].astype(F32)
    u_before = cgp_ref[...].astype(F32) * valp_ref[...].astype(F32)
    u_after = cgn_ref[...].astype(F32) * valn_ref[...].astype(F32)
    first = jnp.where(i == 0, 0.0, u_before[HALO_ROWS - 1:HALO_ROWS, :])
    last = jnp.where(i == pl.num_programs(1) - 1, 0.0, u_after[0:1, :])
    row = lax.broadcasted_iota(jnp.int32, (tm, half), 0)
    u_prev = jnp.where(row == 0, first, pltpu.roll(u, 1, 0))
    u_next = jnp.where(row == tm - 1, last, pltpu.roll(u, tm - 1, 0))
    wc = wc_ref[...]
    conv = bg_ref[...].astype(F32) * (u_prev * wc[0:1] + u * wc[1:2] + u_next * wc[2:3])
    y = (jnp.dot(conv.astype(BF16), w_ref[0:half, :], preferred_element_type=F32)
         + jnp.dot(o_att_ref[...], w_ref[half:2 * half, :], preferred_element_type=F32))
    r = DEEPNORM_ALPHA * x_ref[...] + gate_ref[...] * y
    o_ref[...] = _layer_norm(r, lng_ref[...], lnb_ref[...])


def _conv_outproj_ln(y, w_conv, o_att, w_o, x, mod, gate_col, ln_g, ln_b, *, tm, name):
    b, l, d = x.shape
    half = w_o.shape[0] // 2
    per = tm // HALO_ROWS
    n_halo = l // HALO_ROWS
    main = lambda col: pl.BlockSpec((None, tm, half), lambda bi, i: (bi, i, col))
    before = lambda col: pl.BlockSpec((None, HALO_ROWS, half),
                                      lambda bi, i: (bi, jnp.maximum(i * per - 1, 0), col))
    after = lambda col: pl.BlockSpec((None, HALO_ROWS, half),
                                     lambda bi, i: (bi, jnp.minimum((i + 1) * per, n_halo - 1), col))
    return pl.pallas_call(
        _conv_outproj_body,
        grid=(b, l // tm),
        in_specs=[
            main(0), main(1), main(2), before(1), before(2), after(1), after(2),
            pl.BlockSpec((3, half), lambda bi, i: (0, 0)),
            pl.BlockSpec((None, tm, half), lambda bi, i: (bi, i, 0)),
            pl.BlockSpec((2 * half, d), lambda bi, i: (0, 0)),
            pl.BlockSpec((None, tm, d), lambda bi, i: (bi, i, 0)),
            pl.BlockSpec((None, 1, d), lambda bi, i: (bi, 0, gate_col)),
            pl.BlockSpec((1, d), lambda bi, i: (0, 0)),
            pl.BlockSpec((1, d), lambda bi, i: (0, 0)),
        ],
        out_specs=pl.BlockSpec((None, tm, d), lambda bi, i: (bi, i, 0)),
        out_shape=jax.ShapeDtypeStruct((b, l, d), F32),
        compiler_params=_cparams(("parallel", "parallel")),
        name=name,
    )(y, y, y, y, y, y, y, w_conv, o_att, w_o, x, mod, ln_g, ln_b)


def _outproj_rows(a_ref, b_ref, w_ref, x_ref, gate_ref, lng_ref, lnb_ref):
    half = a_ref.shape[1]
    y = (jnp.dot(a_ref[...], w_ref[0:half, :], preferred_element_type=F32)
         + jnp.dot(b_ref[...], w_ref[half:2 * half, :], preferred_element_type=F32))
    r = DEEPNORM_ALPHA * x_ref[...] + gate_ref[...] * y
    return _layer_norm(r, lng_ref[...], lnb_ref[...])


MXU_WIDTH = 256


def _ffn_body(x_ref, sh_ref, sc_ref, gate_ref, wg_ref, wu_ref, wd_ref, lng_ref, lnb_ref, o_ref, *, chunks):
    x = x_ref[...]
    h = (x * (1.0 + sc_ref[...]) + sh_ref[...]).astype(BF16)
    acc = None
    for (c0, c1) in chunks:
        g = jnp.dot(h, wg_ref[:, c0:c1], preferred_element_type=F32)
        u = jnp.dot(h, wu_ref[:, c0:c1], preferred_element_type=F32)
        a = (g * jax.nn.sigmoid(g) * u).astype(BF16)
        part = jnp.dot(a, wd_ref[c0:c1, :], preferred_element_type=F32)
        acc = part if acc is None else acc + part
    r = DEEPNORM_ALPHA * x + gate_ref[...] * acc
    o_ref[...] = _layer_norm(r, lng_ref[...], lnb_ref[...])


def _ffn_ln(x, mod, shift_col, wg, wu, wd, ln_g, ln_b, *, tm, n_chunks, name="ffn"):
    b, l, d = x.shape
    ff = wg.shape[1]
    n_mxu = ff // MXU_WIDTH
    assert ff == n_mxu * MXU_WIDTH
    bounds = [MXU_WIDTH * ((n_mxu * k + n_chunks - 1) // n_chunks) for k in range(n_chunks + 1)]
    chunks = tuple((bounds[k], bounds[k + 1]) for k in range(n_chunks))
    resident = pl.Buffered(1)
    return pl.pallas_call(
        functools.partial(_ffn_body, chunks=chunks),
        grid=(b, l // tm),
        in_specs=[
            pl.BlockSpec((None, tm, d), lambda bi, i: (bi, i, 0)),
            pl.BlockSpec((None, 1, d), lambda bi, i: (bi, 0, shift_col)),
            pl.BlockSpec((None, 1, d), lambda bi, i: (bi, 0, shift_col + 1)),
            pl.BlockSpec((None, 1, d), lambda bi, i: (bi, 0, shift_col + 2)),
            pl.BlockSpec((d, ff), lambda bi, i: (0, 0), pipeline_mode=resident),
            pl.BlockSpec((d, ff), lambda bi, i: (0, 0), pipeline_mode=resident),
            pl.BlockSpec((ff, d), lambda bi, i: (0, 0), pipeline_mode=resident),
            pl.BlockSpec((1, d), lambda bi, i: (0, 0)),
            pl.BlockSpec((1, d), lambda bi, i: (0, 0)),
        ],
        out_specs=pl.BlockSpec((None, tm, d), lambda bi, i: (bi, i, 0)),
        out_shape=jax.ShapeDtypeStruct((b, l, d), F32),
        compiler_params=_cparams(("parallel", "parallel")),
        name=name,
    )(x, mod, mod, mod, wg, wu, wd, ln_g, ln_b)


MOE_TM = 1024
ROUTE_TM = 512
ZERO_ROWS = 256
META_I1, META_I2, META_R1, META_R2, META_P1, META_P2 = range(6)


def _outproj_route_body(a_ref, b_ref, w_ref, x_ref, gate_ref, lng_ref, lnb_ref, sh_ref, sc_ref, wr_ref,
                        o_ref, meta_ref, meta_t_ref, cnt_ref, carry_scr):
    x_new = _outproj_rows(a_ref, b_ref, w_ref, x_ref, gate_ref, lng_ref, lnb_ref)
    o_ref[...] = x_new
    _route_rows(x_new, sh_ref, sc_ref, wr_ref, meta_ref, meta_t_ref, cnt_ref, carry_scr)


def _route_rows(x, sh_ref, sc_ref, wr_ref, meta_ref, meta_t_ref, cnt_ref, carry_scr):
    @pl.when(pl.program_id(0) == 0)
    def _():
        carry_scr[...] = jnp.zeros_like(carry_scr)

    h = (x * (1.0 + sc_ref[...]) + sh_ref[...]).astype(BF16)
    logits = jnp.dot(h, wr_ref[...], preferred_element_type=F32)
    tm = logits.shape[0]
    lane = lax.broadcasted_iota(jnp.int32, logits.shape, 1).astype(F32)
    l1 = jnp.where(lane < N_EXPERTS, logits, -jnp.inf)
    v1 = jnp.max(l1, axis=-1, keepdims=True)
    i1 = jnp.min(jnp.where(l1 == v1, lane, float(LANES)), axis=-1, keepdims=True)
    l2 = jnp.where(lane == i1, -jnp.inf, l1)
    v2 = jnp.max(l2, axis=-1, keepdims=True)
    i2 = jnp.min(jnp.where(l2 == v2, lane, float(LANES)), axis=-1, keepdims=True)
    t = jnp.exp(v2 - v1)
    p1 = 1.0 / (1.0 + t)
    p2 = t / (1.0 + t)

    member = jnp.where(lane == i1, 1.0, jnp.where(lane == i2, 1.0, 0.0))
    rr = lax.broadcasted_iota(jnp.int32, (tm, tm), 0)
    cc = lax.broadcasted_iota(jnp.int32, (tm, tm), 1)
    earlier = jnp.where(cc < rr, 1.0, 0.0).astype(BF16)
    base = carry_scr[0:1, :]
    rank = jnp.dot(earlier, member.astype(BF16), preferred_element_type=F32) + base
    total = base + jnp.sum(member, axis=0, keepdims=True)
    carry_scr[0:1, :] = total
    r1 = jnp.sum(jnp.where(lane == i1, rank, 0.0), axis=-1, keepdims=True)
    r2 = jnp.sum(jnp.where(lane == i2, rank, 0.0), axis=-1, keepdims=True)

    meta = jnp.zeros_like(logits)
    for k, val in ((META_I1, i1), (META_I2, i2), (META_R1, r1), (META_R2, r2), (META_P1, p1), (META_P2, p2)):
        meta = jnp.where(lane == float(k), val, meta)
    meta_ref[...] = meta
    meta_t_ref[...] = meta.T[0:SUBLANES, :]
    cnt_ref[...] = jnp.broadcast_to(total, cnt_ref.shape)


def _outproj_route(a2, b2, w_o, x2, mod, gate_col, ln_g, ln_b, shift_col, w_router_pad, *, seq_len):
    m, d = x2.shape
    half = w_o.shape[0] // 2
    tm = ROUTE_TM
    batch = lambda col: pl.BlockSpec((None, 1, d), lambda i: ((i * tm) // seq_len, 0, col))
    return pl.pallas_call(
        _outproj_route_body,
        grid=(m // tm,),
        in_specs=[
            pl.BlockSpec((tm, half), lambda i: (i, 0)),
            pl.BlockSpec((tm, half), lambda i: (i, 0)),
            pl.BlockSpec((2 * half, d), lambda i: (0, 0)),
            pl.BlockSpec((tm, d), lambda i: (i, 0)),
            batch(gate_col),
            pl.BlockSpec((1, d), lambda i: (0, 0)),
            pl.BlockSpec((1, d), lambda i: (0, 0)),
            batch(shift_col),
            batch(shift_col + 1),
            pl.BlockSpec((d, LANES), lambda i: (0, 0)),
        ],
        out_specs=[pl.BlockSpec((tm, d), lambda i: (i, 0)),
                   pl.BlockSpec((tm, LANES), lambda i: (i, 0)),
                   pl.BlockSpec((SUBLANES, tm), lambda i: (0, i)),
                   pl.BlockSpec((8, LANES), lambda i: (0, 0))],
        out_shape=[jax.ShapeDtypeStruct((m, d), F32),
                   jax.ShapeDtypeStruct((m, LANES), F32), jax.ShapeDtypeStruct((SUBLANES, m), F32),
                   jax.ShapeDtypeStruct((8, LANES), F32)],
        scratch_shapes=[pltpu.VMEM((8, LANES), F32)],
        compiler_params=_cparams(("arbitrary",)),
        name="odd_outproj_route",
    )(a2, b2, w_o, x2, mod, ln_g, ln_b, mod, mod, w_router_pad)


def _to_tiles(tile_ref, value):
    n = value.shape[0]
    for j in range(SUBLANES):
        tile_ref[pl.ds(j, n, stride=SUBLANES), :] = value[:, LANES * j:LANES * (j + 1)]


def _from_tiles(tile_ref):
    n = tile_ref.shape[0] // SUBLANES
    return jnp.concatenate([tile_ref[pl.ds(j, n, stride=SUBLANES), :] for j in range(SUBLANES)], axis=-1)


def _tile_rows(ref, row, n=1):
    return ref.at[pl.ds(pl.multiple_of(row * SUBLANES, SUBLANES), n * SUBLANES), :]


def _row_copy(src, src_row, dst, dst_row, sem):
    return pltpu.make_async_copy(_tile_rows(src, src_row), _tile_rows(dst, dst_row), sem)


def _rows_wait(src, dst, n, sem):
    pltpu.make_async_copy(_tile_rows(src, 0, n), _tile_rows(dst, 0, n), sem).wait()


def _dispatch_body(pad_ref, pos_hbm, x_ref, sh_ref, sc_ref, xs_hbm, h_scr, z_scr, pos_smem, pos_sem, row_sem):
    step = pl.program_id(0)
    tm = x_ref.shape[0]
    pos_cp = pltpu.make_async_copy(pos_hbm.at[step], pos_smem, pos_sem)
    pos_cp.start()
    _to_tiles(h_scr, x_ref[...] * (1.0 + sc_ref[...]) + sh_ref[...])
    pos_cp.wait()

    def issue(r, carry):
        _row_copy(h_scr, r, xs_hbm, pos_smem[r], row_sem).start(priority=0)
        _row_copy(h_scr, r, xs_hbm, pos_smem[tm + r], row_sem).start(priority=1)
        return carry

    lax.fori_loop(0, tm, issue, 0, unroll=8)
    _rows_wait(h_scr, xs_hbm, tm, row_sem)
    _rows_wait(h_scr, xs_hbm, tm, row_sem)

    @pl.when(step == pl.num_programs(0) - 1)
    def _():
        z_scr[...] = jnp.zeros_like(z_scr)
        for e in range(N_EXPERTS):
            start = pad_ref[e]
            count = pad_ref[N_EXPERTS + e]

            def fill(k, carry, start=start):
                _row_copy(z_scr, 0, xs_hbm, start + k, row_sem).start()
                return carry

            def fill_wait(k, carry):
                _row_copy(z_scr, 0, xs_hbm, 0, row_sem).wait()
                return carry

            lax.fori_loop(0, count, fill, 0)
            lax.fori_loop(0, count, fill_wait, 0)

        zrows = z_scr.shape[0] // SUBLANES
        used_rows = pad_ref[2 * N_EXPERTS]
        n_chunks = (xs_hbm.shape[0] // SUBLANES - used_rows) // zrows

        def chunk_copy(k):
            return pltpu.make_async_copy(z_scr, _tile_rows(xs_hbm, used_rows + k * zrows, zrows), row_sem)

        def fill_chunk(k, carry):
            chunk_copy(k).start()
            return carry

        def fill_chunk_wait(k, carry):
            chunk_copy(k).wait()
            return carry

        lax.fori_loop(0, n_chunks, fill_chunk, 0)
        lax.fori_loop(0, n_chunks, fill_chunk_wait, 0)


def _dispatch(x2, mod, shift_col, pos, pad_info, n_rows, *, seq_len):
    m, d = x2.shape
    tm = ROUTE_TM
    return pl.pallas_call(
        _dispatch_body,
        grid_spec=pltpu.PrefetchScalarGridSpec(
            num_scalar_prefetch=1,
            grid=(m // tm,),
            in_specs=[
                pl.BlockSpec(memory_space=pl.ANY),
                pl.BlockSpec((tm, d), lambda i, pad: (i, 0)),
                pl.BlockSpec((None, 1, d), lambda i, pad: ((i * tm) // seq_len, 0, shift_col)),
                pl.BlockSpec((None, 1, d), lambda i, pad: ((i * tm) // seq_len, 0, shift_col + 1)),
            ],
            out_specs=pl.BlockSpec(memory_space=pl.ANY),
            scratch_shapes=[
                pltpu.VMEM((tm * SUBLANES, LANES), F32),
                pltpu.VMEM((ZERO_ROWS * SUBLANES, LANES), F32),
                pltpu.SMEM((2 * tm,), jnp.int32),
                pltpu.SemaphoreType.DMA,
                pltpu.SemaphoreType.DMA,
            ],
        ),
        out_shape=jax.ShapeDtypeStruct((n_rows * SUBLANES, LANES), F32),
        compiler_params=_cparams(("arbitrary",)),
        name="moe_dispatch",
    )(pad_info, pos, x2, mod, mod)


TAIL_LEVELS = 2


def _experts_body(te_ref, tbi_ref, tbo_ref, tr_ref, xs_ref, wg_ref, wu_ref, wd_ref, ys_ref, h_scr, acc_scr, *, n_f):
    t = pl.program_id(0)
    f = pl.program_id(1)

    @pl.when(f == 0)
    def _():
        h_scr[...] = _from_tiles(xs_ref).astype(BF16)
        acc_scr[...] = jnp.zeros_like(acc_scr)

    def swiglu_rows(n):
        h = h_scr[0:n, :]
        g = jnp.dot(h, wg_ref[...].astype(BF16), preferred_element_type=F32)
        u = jnp.dot(h, wu_ref[...].astype(BF16), preferred_element_type=F32)
        a = (g * jax.nn.sigmoid(g) * u).astype(BF16)
        acc_scr[0:n, :] += jnp.dot(a, wd_ref[...].astype(BF16), preferred_element_type=F32)

    rows = tr_ref[t]
    tm = h_scr.shape[0]
    bounds = (0,) + tuple(tm >> k for k in range(TAIL_LEVELS, -1, -1))
    for lo, hi in zip(bounds[:-1], bounds[1:]):
        pl.when((rows > lo) & (rows <= hi))(functools.partial(swiglu_rows, hi))

    @pl.when(f == n_f - 1)
    def _():
        _to_tiles(ys_ref, acc_scr[...])


def _experts(xs, tile_expert, tile_in, tile_out, tile_rows, wg, wu, wd, *, tf):
    n_rows = xs.shape[0] // SUBLANES
    d = wg.shape[1]
    tm = MOE_TM
    n_tiles = tile_expert.shape[0]
    ff = wg.shape[2]
    n_f = ff // tf
    ff_blk = lambda f, tr, t: f * jnp.minimum(tr[t], 1)
    return pl.pallas_call(
        functools.partial(_experts_body, n_f=n_f),
        grid_spec=pltpu.PrefetchScalarGridSpec(
            num_scalar_prefetch=4,
            grid=(n_tiles, n_f),
            in_specs=[
                pl.BlockSpec((tm * SUBLANES, LANES), lambda t, f, te, tbi, tbo, tr: (tbi[t], 0)),
                pl.BlockSpec((None, d, tf), lambda t, f, te, tbi, tbo, tr: (te[t], 0, ff_blk(f, tr, t))),
                pl.BlockSpec((None, d, tf), lambda t, f, te, tbi, tbo, tr: (te[t], 0, ff_blk(f, tr, t))),
                pl.BlockSpec((None, tf, d), lambda t, f, te, tbi, tbo, tr: (te[t], ff_blk(f, tr, t), 0)),
            ],
            out_specs=pl.BlockSpec((tm * SUBLANES, LANES), lambda t, f, te, tbi, tbo, tr: (tbo[t], 0)),
            scratch_shapes=[pltpu.VMEM((tm, d), BF16), pltpu.VMEM((tm, d), F32)],
        ),
        out_shape=jax.ShapeDtypeStruct((n_rows * SUBLANES, LANES), F32),
        compiler_params=_cparams(("arbitrary", "arbitrary")),
        name="moe_experts",
    )(tile_expert, tile_in, tile_out, tile_rows, xs, wg, wu, wd)


def _combine_body(pos_hbm, ys_hbm, meta_ref, x_ref, gate_ref, lng_ref, lnb_ref, o_ref,
                  y1_scr, y2_scr, pos_smem, pos_sem, row_sems):
    s = pl.program_id(0)
    n_tiles = pl.num_programs(0) - 1
    tm = x_ref.shape[0]

    def gather(buf):
        pos_cp = pltpu.make_async_copy(pos_hbm.at[s], pos_smem, pos_sem)
        pos_cp.start()
        pos_cp.wait()

        def issue(r, carry):
            _row_copy(ys_hbm, pos_smem[r], y1_scr.at[buf], r, row_sems.at[buf]).start(priority=0)
            _row_copy(ys_hbm, pos_smem[tm + r], y2_scr.at[buf], r, row_sems.at[buf]).start(priority=1)
            return carry

        lax.fori_loop(0, tm, issue, 0, unroll=8)

    def finish(buf):
        _rows_wait(ys_hbm, y1_scr.at[buf], tm, row_sems.at[buf])
        _rows_wait(ys_hbm, y2_scr.at[buf], tm, row_sems.at[buf])
        meta = meta_ref[...]
        p1 = meta[:, META_P1:META_P1 + 1]
        p2 = meta[:, META_P2:META_P2 + 1]
        mix = p1 * _from_tiles(y1_scr.at[buf]) + p2 * _from_tiles(y2_scr.at[buf])
        r = DEEPNORM_ALPHA * x_ref[...] + gate_ref[...] * mix
        o_ref[...] = _layer_norm(r, lng_ref[...], lnb_ref[...])

    for parity in (0, 1):
        pl.when((s % 2 == parity) & (s < n_tiles))(functools.partial(gather, parity))
    for parity in (0, 1):
        pl.when((s % 2 == parity) & (s >= 1))(functools.partial(finish, 1 - parity))


def _combine_ln(ys, pos, meta, x2, mod, gate_col, ln_g, ln_b, *, seq_len):
    m, d = x2.shape
    tm = ROUTE_TM
    prev = lambda i: jnp.maximum(i - 1, 0)
    return pl.pallas_call(
        _combine_body,
        grid=(m // tm + 1,),
        in_specs=[
            pl.BlockSpec(memory_space=pl.ANY),
            pl.BlockSpec(memory_space=pl.ANY),
            pl.BlockSpec((tm, LANES), lambda i: (prev(i), 0)),
            pl.BlockSpec((tm, d), lambda i: (prev(i), 0)),
            pl.BlockSpec((None, 1, d), lambda i: ((prev(i) * tm) // seq_len, 0, gate_col)),
            pl.BlockSpec((1, d), lambda i: (0, 0)),
            pl.BlockSpec((1, d), lambda i: (0, 0)),
        ],
        out_specs=pl.BlockSpec((tm, d), lambda i: (prev(i), 0)),
        out_shape=jax.ShapeDtypeStruct((m, d), F32),
        scratch_shapes=[
            pltpu.VMEM((2, tm * SUBLANES, LANES), F32),
            pltpu.VMEM((2, tm * SUBLANES, LANES), F32),
            pltpu.SMEM((2 * tm,), jnp.int32),
            pltpu.SemaphoreType.DMA,
            pltpu.SemaphoreType.DMA((2,)),
        ],
        compiler_params=_cparams(("arbitrary",)),
        name="moe_combine",
    )(pos, ys, meta, x2, mod, ln_g, ln_b)


def _outproj_moe_ln(a, bsrc, w_o, x, mod, ln_g0, ln_b0, w_router, wg, wu, wd, ln_g, ln_b, *, tf):
    b, l, d = x.shape
    assert d == SUBLANES * LANES
    m = b * l
    w_r = jnp.zeros((d, LANES), BF16).at[:, :N_EXPERTS].set(w_router.astype(BF16))
    x2, meta, meta_t, cnt = _outproj_route(a.reshape(m, -1), bsrc.reshape(m, -1), w_o, x.reshape(m, d), mod, 2,
                                           ln_g0, ln_b0, 3, w_r, seq_len=l)

    counts = cnt[0, :N_EXPERTS].astype(jnp.int32)
    n_tile_e = (counts + MOE_TM - 1) // MOE_TM
    tile_end = jnp.cumsum(n_tile_e)
    offs = (tile_end - n_tile_e) * MOE_TM
    idx = meta_t[META_I1:META_I2 + 1].astype(jnp.int32)
    rank = meta_t[META_R1:META_R2 + 1].astype(jnp.int32)
    base = jnp.zeros_like(idx)
    for e in range(N_EXPERTS):
        base = jnp.where(idx == e, offs[e], base)
    pos = (base + rank).reshape(2, m // ROUTE_TM, ROUTE_TM).transpose(1, 0, 2).reshape(m // ROUTE_TM, 2 * ROUTE_TM)
    n_tiles = (2 * m) // MOE_TM + N_EXPERTS
    tid = jnp.arange(n_tiles, dtype=jnp.int32)
    tile_valid = (tid < tile_end[-1]).astype(jnp.int32)
    tile_expert = jnp.minimum(jnp.sum((tid[:, None] >= tile_end[None, :]).astype(jnp.int32), axis=1),
                              N_EXPERTS - 1).astype(jnp.int32)
    tile_in = jnp.where(tile_valid > 0, tid, 0).astype(jnp.int32)
    tile_out = tid
    row0 = tid * MOE_TM - jnp.take(offs, tile_expert)
    tile_rows = (jnp.clip(jnp.take(counts, tile_expert) - row0, 0, MOE_TM) * tile_valid).astype(jnp.int32)
    pad_info = jnp.concatenate([offs + counts, n_tile_e * MOE_TM - counts,
                                tile_end[-1:] * MOE_TM]).astype(jnp.int32)
    n_rows = n_tiles * MOE_TM

    xs = _dispatch(x2, mod, 3, pos, pad_info, n_rows, seq_len=l)
    ys = _experts(xs, tile_expert, tile_in, tile_out, tile_rows, wg, wu, wd, tf=tf)
    out = _combine_ln(ys, pos, meta, x2, mod, 5, ln_g, ln_b, seq_len=l)
    return out.reshape(b, l, d)


def _fourier_body(pc_ref, dl_ref, dc_ref, o_ref, t_scr, *, out_scale, row_chunk):
    l = pc_ref.shape[0]
    for g in range(C_GROUPS):
        sl = slice(C_GROUP_DIM * g, C_GROUP_DIM * (g + 1))
        xg = pc_ref[:, sl].astype(F32)
        mu = jnp.mean(xg, axis=-1, keepdims=True)
        dlt = xg - mu
        var = jnp.mean(dlt * dlt, axis=-1, keepdims=True)
        gn = (dlt * lax.rsqrt(var + LN_EPS)).astype(BF16)
        t = jnp.dot(gn, dc_ref[...], preferred_element_type=F32)
        t_scr[0:l, sl] = t[:, 0:C_GROUP_DIM].astype(BF16)
        t_scr[l:2 * l, sl] = t[:, C_GROUP_DIM:2 * C_GROUP_DIM].astype(BF16)
    for r0 in range(0, l, row_chunk):
        acc = jnp.dot(dl_ref[r0:r0 + row_chunk, :], t_scr[...], preferred_element_type=F32)
        o_ref[r0:r0 + row_chunk, :] = (acc * out_scale).astype(BF16)


def _dft_matrices(l, c):
    j = np.arange(l, dtype=np.int64)
    ang_l = (2.0 * np.pi / l) * ((j[:, None] * j[None, :]) % l)
    dl = np.concatenate([np.cos(ang_l), -np.sin(ang_l)], axis=1)
    m = np.arange(c, dtype=np.int64)
    ang_c = (2.0 * np.pi / c) * ((m[:, None] * m[None, :]) % c)
    dc = np.concatenate([np.cos(ang_c), np.sin(ang_c)], axis=1)
    return dl.astype(np.float32), dc.astype(np.float32)


def _fourier_mixer(y, name="fourier"):
    b, l, _ = y.shape
    dl_np, dc_np = _dft_matrices(l, C_GROUP_DIM)
    dl = jnp.asarray(dl_np, dtype=F32).astype(BF16)
    dc = jnp.asarray(dc_np, dtype=F32).astype(BF16)
    out_scale = 1.0 / math.sqrt(l * C_GROUP_DIM)
    return pl.pallas_call(
        functools.partial(_fourier_body, out_scale=out_scale, row_chunk=min(l, 512)),
        grid=(b,),
        in_specs=[
            pl.BlockSpec((None, l, C_WIDTH), lambda bi: (bi, 0, 0)),
            pl.BlockSpec((l, 2 * l), lambda bi: (0, 0), pipeline_mode=pl.Buffered(1)),
            pl.BlockSpec((C_GROUP_DIM, 2 * C_GROUP_DIM), lambda bi: (0, 0)),
        ],
        out_specs=pl.BlockSpec((None, l, C_WIDTH), lambda bi: (bi, 0, 0)),
        out_shape=jax.ShapeDtypeStruct((b, l, C_WIDTH), BF16),
        scratch_shapes=[pltpu.VMEM((2 * l, C_WIDTH), BF16)],
        compiler_params=_cparams(("parallel",)),
        name=name,
    )(y, dl, dc)


NA_HG = 4
NA_GW = NA_HG * NA_DH
NA_WIN = NA_KR * GRID_W


def _natten_body(q_ref, k_ref, v_ref, kc_ref, vc_ref, bias_ref, o_ref, *, rows_per_step, n_rows):
    rb = pl.program_id(2)
    hq = NA_HG * GRID_W
    rid = lax.broadcasted_iota(jnp.int32, (hq, NA_GW), 0)
    cid = lax.broadcasted_iota(jnp.int32, (hq, NA_GW), 1)
    diag = (rid // GRID_W) == (cid // NA_DH)
    cid_o = lax.broadcasted_iota(jnp.int32, (GRID_W, NA_GW), 1)
    kc = kc_ref[...]
    vc = vc_ref[...]
    staged = []
    for j in range(rows_per_step):
        r = rb * rows_per_step + j
        rs = jnp.clip(r - NA_KR // 2, 0, n_rows - NA_KR)
        start = pl.multiple_of(rs * GRID_W, GRID_W)
        tid = jnp.minimum(r, NA_KR // 2) + jnp.maximum(r - (n_rows - NA_KR // 2), 0)
        q_r = q_ref[GRID_W * j:GRID_W * (j + 1), :]
        q4 = jnp.concatenate([q_r] * NA_HG, axis=0)
        qbd = jnp.where(diag, q4, jnp.zeros_like(q4))
        kw = k_ref[pl.ds(start, NA_WIN), :]
        s_loc = lax.dot_general(qbd, kw, NT_DIMS, preferred_element_type=F32) + bias_ref[tid]
        s_ctx = lax.dot_general(qbd, kc, NT_DIMS, preferred_element_type=F32)
        staged.append((start, s_loc, s_ctx))
    for j in range(rows_per_step):
        start, s_loc, s_ctx = staged[j]
        vw = v_ref[pl.ds(start, NA_WIN), :]
        m = jnp.maximum(jnp.max(s_loc, axis=-1, keepdims=True), jnp.max(s_ctx, axis=-1, keepdims=True))
        e_loc = jnp.exp2(s_loc - m)
        e_ctx = jnp.exp2(s_ctx - m)
        tot = jnp.sum(e_loc, axis=-1, keepdims=True) + jnp.sum(e_ctx, axis=-1, keepdims=True)
        o = (jnp.dot(e_loc.astype(BF16), vw, preferred_element_type=F32)
             + jnp.dot(e_ctx.astype(BF16), vc, preferred_element_type=F32))
        o = o * (1.0 / tot)
        out = jnp.zeros((GRID_W, NA_GW), F32)
        for hh in range(NA_HG):
            out = out + jnp.where((cid_o // NA_DH) == hh, o[GRID_W * hh:GRID_W * (hh + 1), :], 0.0)
        o_ref[GRID_W * j:GRID_W * (j + 1), :] = out.astype(BF16)


def _na_bias_table(rpb, n_rows):
    h = rpb.shape[0]
    cols = jnp.arange(GRID_W)
    col_start = jnp.clip(cols - NA_KC // 2, 0, GRID_W - NA_KC)
    col_valid = (cols[None, :] >= col_start[:, None]) & (cols[None, :] < col_start[:, None] + NA_KC)
    dc_idx = jnp.clip(cols[None, :] - cols[:, None] + NA_KC - 1, 0, 2 * NA_KC - 2)
    onehot = (dc_idx[:, :, None] == jnp.arange(2 * NA_KC - 1)).astype(F32)
    rpb_c = jnp.einsum("qkc,hdc->hqdk", onehot, rpb.astype(F32), precision=lax.Precision.HIGHEST)
    full = jnp.where(col_valid[None, :, None, :], rpb_c * LOG2E, NEG_INF)
    full = full.reshape(h // NA_HG, NA_HG * GRID_W, (2 * NA_KR - 1) * GRID_W)
    half = NA_KR // 2
    rep_rows = list(range(half)) + [half] + list(range(n_rows - half + 1, n_rows))
    tabs = []
    for r in rep_rows:
        rs = min(max(r - half, 0), n_rows - NA_KR)
        d0 = rs - r + NA_KR - 1
        tabs.append(full[:, :, GRID_W * d0:GRID_W * d0 + NA_WIN])
    return jnp.stack(tabs, axis=0)


def _natten(y, q_col0, k_col0, v_col0, y_ctx, kc_col0, vc_col0, bias_tab, *, rows_per_step, name="natten"):
    b, l, _ = y.shape
    lc = y_ctx.shape[1]
    n_rows = l // GRID_W
    n_tab = bias_tab.shape[0]
    n_grp = NA_HEADS // NA_HG
    tq = rows_per_step * GRID_W
    qb, kb, vb = q_col0 // NA_GW, k_col0 // NA_GW, v_col0 // NA_GW
    kcb, vcb = kc_col0 // NA_GW, vc_col0 // NA_GW
    return pl.pallas_call(
        functools.partial(_natten_body, rows_per_step=rows_per_step, n_rows=n_rows),
        grid=(b, n_grp, n_rows // rows_per_step),
        in_specs=[
            pl.BlockSpec((None, tq, NA_GW), lambda bi, g, i: (bi, i, qb + g)),
            pl.BlockSpec((None, l, NA_GW), lambda bi, g, i: (bi, 0, kb + g)),
            pl.BlockSpec((None, l, NA_GW), lambda bi, g, i: (bi, 0, vb + g)),
            pl.BlockSpec((None, lc, NA_GW), lambda bi, g, i: (bi, 0, kcb + g)),
            pl.BlockSpec((None, lc, NA_GW), lambda bi, g, i: (bi, 0, vcb + g)),
            pl.BlockSpec((n_tab, None, NA_HG * GRID_W, NA_WIN), lambda bi, g, i: (0, g, 0, 0)),
        ],
        out_specs=pl.BlockSpec((None, tq, NA_GW), lambda bi, g, i: (bi, i, g)),
        out_shape=jax.ShapeDtypeStruct((b, l, D_WIDTH), BF16),
        compiler_params=_cparams(("parallel", "parallel", "arbitrary")),
        name=name,
    )(y, y, y, y_ctx, y_ctx, bias_tab)


def _rope_tables(l):
    t = jnp.arange(l, dtype=jnp.int32)
    row = (t // GRID_W).astype(F32)
    col = (t % GRID_W).astype(F32)
    n_freq = DIFF_DH // 4
    inv_freq = ROPE_THETA ** (-jnp.arange(n_freq, dtype=F32) / n_freq)
    ang = jnp.concatenate([row[:, None] * inv_freq, col[:, None] * inv_freq], axis=-1)
    c, s = jnp.cos(ang), jnp.sin(ang)
    cos = jnp.tile(jnp.concatenate([c, c], axis=-1), (1, 2))
    sin = jnp.tile(jnp.concatenate([-s, s], axis=-1), (1, 2))
    return cos, sin


def kernel(x, c, ctx, c_ctx, w_mod, b_mod, ln_g, ln_b, e_w_in, e_conv, e_lam_q1, e_lam_k1, e_lam_q2, e_lam_k2, e_subln_g, e_w_o, e_ffn_gate, e_ffn_up, e_ffn_down, o_w_in, o_rpb, o_w_o, o_router, o_exp_gate, o_exp_up, o_exp_down):
    b, l, d = x.shape
    lc = ctx.shape[1]
    assert d == D_MODEL and l % 512 == 0 and lc % 256 == 0 and b + 1 <= MOD_ROWS

    cond = jnp.concatenate([c, c_ctx[None, :], jnp.zeros((MOD_ROWS - b - 1, d), F32)], axis=0)
    mods = _adaln(cond, w_mod, b_mod)

    def layer_mods(i):
        lat = mods[i, :b][:, None, :]
        cx = jnp.broadcast_to(mods[i, b][None, None, :], (b, 1, 6 * d))
        return lat, cx

    q_scale_diff = DIFF_DH ** -0.5 * LOG2E
    q_scale_na = NA_DH ** -0.5 * LOG2E

    mod_lat, mod_ctx = layer_mods(0)
    lam_init = 0.8 - 0.6 * math.exp(-0.3 * 0)
    a_end = 3 * A_WIDTH
    w_in = e_w_in[0]
    w_main = w_in[:, :a_end + 2 * DIFF_QK].astype(BF16)
    w_vt = w_in[:, a_end + 2 * DIFF_QK:].T.astype(BF16)
    rope = _rope_tables(l)
    pa_chunks = [(0, 512, "plain"), (512, 512, "plain"), (1024, 512, "plain")]
    y_lat, vt_lat = _proj(x, mod_lat, 0, w_main,
                          pa_chunks + [(a_end, 512, "rope_scale"), (a_end + 512, 512, "rope")],
                          tm=512, wvt=w_vt, rope=rope, q_scale=q_scale_diff, name="even_inproj_lat")
    y_ctx, vt_ctx = _proj(ctx, mod_ctx, 0, w_main,
                          pa_chunks + [(a_end, 512, "scale"), (a_end + 512, 512, "plain")],
                          tm=lc, wvt=w_vt, q_scale=q_scale_diff, name="even_inproj_ctx")

    lam_pack = jnp.zeros((8, LANES), F32)
    lam_pack = lam_pack.at[0, :DIFF_DH].set(e_lam_q1[0]).at[1, :DIFF_DH].set(e_lam_k1[0])
    lam_pack = lam_pack.at[2, :DIFF_DH].set(e_lam_q2[0]).at[3, :DIFF_DH].set(e_lam_k2[0])
    sub_g = e_subln_g[0].reshape(1, DIFF_VDIM)
    o_lat = _diff_attention(y_lat, [(y_ctx, vt_ctx), (y_lat, vt_lat)], lam_pack, sub_g, lam_init,
                            tq=256, q_col0=a_end, k_col0=a_end + DIFF_QK, name="diffattn_lat")
    o_ctx = _diff_attention(y_ctx, [(y_ctx, vt_ctx)], lam_pack, sub_g, lam_init,
                            tq=lc, q_col0=a_end, k_col0=a_end + DIFF_QK, name="diffattn_ctx")

    w_o = e_w_o[0].astype(BF16)
    lng0, lnb0 = ln_g[0, 0][None, :], ln_b[0, 0][None, :]
    lng1, lnb1 = ln_g[0, 1][None, :], ln_b[0, 1][None, :]
    x_lat = _conv_outproj_ln(y_lat, e_conv[0], o_lat, w_o, x, mod_lat, 2, lng0, lnb0, tm=512,
                             name="even_outproj_lat")
    x_ctx = _conv_outproj_ln(y_ctx, e_conv[0], o_ctx, w_o, ctx, mod_ctx, 2, lng0, lnb0, tm=lc,
                             name="even_outproj_ctx")

    wg = e_ffn_gate[0].astype(BF16)
    wu = e_ffn_up[0].astype(BF16)
    wd = e_ffn_down[0].astype(BF16)
    x_lat = _ffn_ln(x_lat, mod_lat, 3, wg, wu, wd, lng1, lnb1, tm=512, n_chunks=2, name="ffn_lat")
    x_ctx = _ffn_ln(x_ctx, mod_ctx, 3, wg, wu, wd, lng1, lnb1, tm=lc, n_chunks=2, name="ffn_ctx")

    mod_lat, mod_ctx = layer_mods(1)
    w_in = o_w_in[0].astype(BF16)
    y_lat = _proj(x_lat, mod_lat, 0, w_in,
                  [(0, 512, "plain"), (512, 512, "scale"), (1024, 512, "plain"), (1536, 512, "plain")],
                  tm=512, q_scale=q_scale_na, name="odd_inproj_lat")
    y_ctx = _proj(x_ctx, mod_ctx, 0, w_in[:, C_WIDTH + D_WIDTH:],
                  [(0, 512, "plain"), (512, 512, "plain")], tm=lc, name="odd_inproj_ctx")
    f_lat = _fourier_mixer(y_lat)
    bias_tab = _na_bias_table(o_rpb[0], l // GRID_W)
    n_lat = _natten(y_lat, C_WIDTH, C_WIDTH + D_WIDTH, C_WIDTH + 2 * D_WIDTH, y_ctx, 0, D_WIDTH,
                    bias_tab, rows_per_step=16)

    w_o = o_w_o[0].astype(BF16)
    lng0, lnb0 = ln_g[1, 0][None, :], ln_b[1, 0][None, :]
    lng1, lnb1 = ln_g[1, 1][None, :], ln_b[1, 1][None, :]
    return _outproj_moe_ln(f_lat, n_lat, w_o, x_lat, mod_lat, lng0, lnb0, o_router[0],
                           o_exp_gate[0], o_exp_up[0], o_exp_down[0], lng1, lnb1, tf=512)
```

```python
import functools
import math

import numpy as np
import jax
import jax.numpy as jnp
from jax import lax
from jax.experimental import pallas as pl
from jax.experimental.pallas import tpu as pltpu

F32 = jnp.float32
BF16 = jnp.bfloat16

D_MODEL = 1024
GRID_W = 64
DEPTH = 2

A_WIDTH = 512
DIFF_HEADS = 4
DIFF_DH = 64
DIFF_VDIM = 128
DIFF_QK = 512
B_WIDTH = 512

C_WIDTH = 512
C_GROUPS = 4
C_GROUP_DIM = 128
NA_HEADS = 8
NA_DH = 64
D_WIDTH = 512
NA_KR = 8
NA_KC = 16

N_EXPERTS = 8

ROPE_THETA = 10000.0
LN_EPS = 1e-5
RMS_EPS = 1e-5
NEG_INF = -1e30
DEEPNORM_ALPHA = (2 * DEPTH) ** 0.25
LOG2E = 1.4426950408889634

LANES = 128
SUBLANES = 8
MOD_ROWS = 32
NT_DIMS = (((1,), (1,)), ((), ()))


def _cparams(sem, vmem_mb=48):
    return pltpu.CompilerParams(dimension_semantics=sem, vmem_limit_bytes=vmem_mb * 1024 * 1024)


def _layer_norm(r, g, b):
    mu = jnp.mean(r, axis=-1, keepdims=True)
    d = r - mu
    var = jnp.mean(d * d, axis=-1, keepdims=True)
    return d * lax.rsqrt(var + LN_EPS) * g + b


def _adaln_body(c_ref, w_ref, b_ref, o_ref):
    cnd = c_ref[...]
    s = (cnd * jax.nn.sigmoid(cnd)).astype(BF16)
    o_ref[...] = jnp.dot(s, w_ref[...].astype(BF16), preferred_element_type=F32) + b_ref[...]


def _adaln(cond, w_mod, b_mod):
    depth, d, n = w_mod.shape
    tn = 1536
    return pl.pallas_call(
        _adaln_body,
        grid=(depth, n // tn),
        in_specs=[
            pl.BlockSpec((MOD_ROWS, d), lambda l, j: (0, 0)),
            pl.BlockSpec((None, d, tn), lambda l, j: (l, 0, j)),
            pl.BlockSpec((None, 1, tn), lambda l, j: (l, 0, j)),
        ],
        out_specs=pl.BlockSpec((None, MOD_ROWS, tn), lambda l, j: (l, 0, j)),
        out_shape=jax.ShapeDtypeStruct((depth, MOD_ROWS, n), F32),
        compiler_params=_cparams(("parallel", "parallel")),
        name="adaln",
    )(cond, w_mod, b_mod.reshape(depth, 1, n))


def _proj_body(*refs, chunks, with_vt, with_rope, q_scale):
    x_ref, sh_ref, sc_ref, w_ref = refs[:4]
    i = 4
    if with_vt:
        wvt_ref = refs[i]
        i += 1
    if with_rope:
        cos_ref, sin_ref = refs[i], refs[i + 1]
        i += 2
    y_ref = refs[i]
    vt_ref = refs[i + 1] if with_vt else None

    h = (x_ref[...] * (1.0 + sc_ref[...]) + sh_ref[...]).astype(BF16)
    tm = h.shape[0]
    if with_rope:
        cos = cos_ref[...]
        sin = sin_ref[...]
        lane = lax.broadcasted_iota(jnp.int32, (tm, LANES), 1)
        low_half = (lane % 64) < 32
    for (c0, width, kind) in chunks:
        acc = jnp.dot(h, w_ref[:, c0:c0 + width], preferred_element_type=F32)
        if kind in ("rope", "rope_scale"):
            for j in range(width // LANES):
                a = acc[:, LANES * j:LANES * (j + 1)]
                rot = jnp.where(low_half, pltpu.roll(a, 96, 1), pltpu.roll(a, 32, 1))
                r = a * cos + rot * sin
                if kind == "rope_scale":
                    r = r * q_scale
                y_ref[:, c0 + LANES * j:c0 + LANES * (j + 1)] = r.astype(BF16)
        elif kind == "scale":
            y_ref[:, c0:c0 + width] = (acc * q_scale).astype(BF16)
        else:
            y_ref[:, c0:c0 + width] = acc.astype(BF16)
    if with_vt:
        vt = lax.dot_general(wvt_ref[...], h, NT_DIMS, preferred_element_type=F32)
        vt_ref[...] = vt.astype(BF16)


def _proj(x, mod, shift_col, w, chunks, *, tm, wvt=None, rope=None, q_scale=1.0, name="proj"):
    b, l, d = x.shape
    n = w.shape[1]
    with_vt = wvt is not None
    with_rope = rope is not None
    in_specs = [
        pl.BlockSpec((None, tm, d), lambda bi, i: (bi, i, 0)),
        pl.BlockSpec((None, 1, d), lambda bi, i: (bi, 0, shift_col)),
        pl.BlockSpec((None, 1, d), lambda bi, i: (bi, 0, shift_col + 1)),
        pl.BlockSpec((d, n), lambda bi, i: (0, 0)),
    ]
    args = [x, mod, mod, w]
    if with_vt:
        nv = wvt.shape[0]
        in_specs.append(pl.BlockSpec((nv, d), lambda bi, i: (0, 0)))
        args.append(wvt)
    if with_rope:
        in_specs += [pl.BlockSpec((tm, LANES), lambda bi, i: (i, 0))] * 2
        args += [rope[0], rope[1]]
    out_specs = [pl.BlockSpec((None, tm, n), lambda bi, i: (bi, i, 0))]
    out_shape = [jax.ShapeDtypeStruct((b, l, n), BF16)]
    if with_vt:
        out_specs.append(pl.BlockSpec((None, nv, tm), lambda bi, i: (bi, 0, i)))
        out_shape.append(jax.ShapeDtypeStruct((b, nv, l), BF16))
    res = pl.pallas_call(
        functools.partial(_proj_body, chunks=tuple(chunks), with_vt=with_vt, with_rope=with_rope,
                          q_scale=q_scale),
        grid=(b, l // tm),
        in_specs=in_specs,
        out_specs=out_specs,
        out_shape=out_shape,
        compiler_params=_cparams(("parallel", "parallel")),
        name=name,
    )(*args)
    return res if with_vt else res[0]


DIFF_HP = 4


def _diffattn_body(*refs, n_seg, lam_init):
    q_ref, lam_ref, g_ref = refs[:3]
    k_refs = [refs[3 + 2 * s] for s in range(n_seg)]
    vt_refs = [refs[4 + 2 * s] for s in range(n_seg)]
    o_ref = refs[3 + 2 * n_seg]

    tq = q_ref.shape[0]
    lane = lax.broadcasted_iota(jnp.int32, (tq, LANES), 1)
    lp = lam_ref[...]
    lam = (jnp.exp(jnp.sum(lp[0:1] * lp[1:2], axis=1, keepdims=True))
           - jnp.exp(jnp.sum(lp[2:3] * lp[3:4], axis=1, keepdims=True)) + lam_init)

    scores = []
    for h in range(DIFF_HP):
        cols = slice(LANES * h, LANES * (h + 1))
        q = q_ref[:, cols]
        zero = jnp.zeros_like(q)
        per_comp = []
        for qm in (jnp.where(lane < DIFF_DH, q, zero), jnp.where(lane >= DIFF_DH, q, zero)):
            per_comp.append([lax.dot_general(k_ref[:, cols], qm, NT_DIMS, preferred_element_type=F32)
                             for k_ref in k_refs])
        scores.append(per_comp)

    def unnormalised(s, rows):
        m = functools.reduce(jnp.maximum, [jnp.max(x, axis=0, keepdims=True) for x in s])
        tot = None
        acc = None
        for x, vt_ref in zip(s, vt_refs):
            e = jnp.exp2(x - m)
            t = jnp.sum(e, axis=0, keepdims=True)
            pv = jnp.dot(vt_ref[rows, :], e.astype(BF16), preferred_element_type=F32)
            tot = t if tot is None else tot + t
            acc = pv if acc is None else acc + pv
        return acc, tot

    for h in range(DIFF_HP):
        cols = slice(LANES * h, LANES * (h + 1))
        rows = slice(DIFF_VDIM * h, DIFF_VDIM * (h + 1))
        acc1, l1 = unnormalised(scores[h][0], rows)
        acc2, l2 = unnormalised(scores[h][1], rows)
        o_t = acc1 * (1.0 / l1) - acc2 * (lam / l2)
        o = o_t.T
        ms = jnp.mean(o * o, axis=-1, keepdims=True)
        o_ref[:, cols] = (o * lax.rsqrt(ms + RMS_EPS) * g_ref[...] * (1.0 - lam_init)).astype(BF16)


def _diff_attention(yq, segs, lam_pack, sub_g, lam_init, *, tq, q_col0, k_col0, name):
    b, lq, _ = yq.shape
    width = DIFF_HP * LANES
    qb0 = q_col0 // width
    kb0 = k_col0 // width
    in_specs = [
        pl.BlockSpec((None, tq, width), lambda bi, h, i: (bi, i, qb0 + h)),
        pl.BlockSpec((8, LANES), lambda bi, h, i: (0, 0)),
        pl.BlockSpec((1, LANES), lambda bi, h, i: (0, 0)),
    ]
    args = [yq, lam_pack, sub_g]
    for (yk, vt) in segs:
        lk = yk.shape[1]
        in_specs.append(pl.BlockSpec((None, lk, width), lambda bi, h, i: (bi, 0, kb0 + h)))
        in_specs.append(pl.BlockSpec((None, DIFF_HP * DIFF_VDIM, lk), lambda bi, h, i: (bi, h, 0)))
        args += [yk, vt]
    return pl.pallas_call(
        functools.partial(_diffattn_body, n_seg=len(segs), lam_init=lam_init),
        grid=(b, DIFF_HEADS // DIFF_HP, lq // tq),
        in_specs=in_specs,
        out_specs=pl.BlockSpec((None, tq, width), lambda bi, h, i: (bi, i, h)),
        out_shape=jax.ShapeDtypeStruct((b, lq, B_WIDTH), BF16),
        compiler_params=_cparams(("parallel", "parallel", "arbitrary")),
        name=name,
    )(*args)


HALO_ROWS = 16


def _conv_outproj_body(bg_ref, cg_ref, val_ref, cgp_ref, valp_ref, cgn_ref, valn_ref, wc_ref,
                       o_att_ref, w_ref, x_ref, gate_ref, lng_ref, lnb_ref, o_ref):
    i = pl.program_id(1)
    tm = x_ref.shape[0]
    half = o_att_ref.shape[1]
    u = cg_ref[...].astype(F32) * val_ref[...].astype(F32)
    u_before = cgp_ref[...].astype(F32) * valp_ref[...].astype(F32)
    u_after = cgn_ref[...].astype(F32) * valn_ref[...].astype(F32)
    first = jnp.where(i == 0, 0.0, u_before[HALO_ROWS - 1:HALO_ROWS, :])
    last = jnp.where(i == pl.num_programs(1) - 1, 0.0, u_after[0:1, :])
    row = lax.broadcasted_iota(jnp.int32, (tm, half), 0)
    u_prev = jnp.where(row == 0, first, pltpu.roll(u, 1, 0))
    u_next = jnp.where(row == tm - 1, last, pltpu.roll(u, tm - 1, 0))
    wc = wc_ref[...]
    conv = bg_ref[...].astype(F32) * (u_prev * wc[0:1] + u * wc[1:2] + u_next * wc[2:3])
    y = (jnp.dot(conv.astype(BF16), w_ref[0:half, :], preferred_element_type=F32)
         + jnp.dot(o_att_ref[...], w_ref[half:2 * half, :], preferred_element_type=F32))
    r = DEEPNORM_ALPHA * x_ref[...] + gate_ref[...] * y
    o_ref[...] = _layer_norm(r, lng_ref[...], lnb_ref[...])


def _conv_outproj_ln(y, w_conv, o_att, w_o, x, mod, gate_col, ln_g, ln_b, *, tm, name):
    b, l, d = x.shape
    half = w_o.shape[0] // 2
    per = tm // HALO_ROWS
    n_halo = l // HALO_ROWS
    main = lambda col: pl.BlockSpec((None, tm, half), lambda bi, i: (bi, i, col))
    before = lambda col: pl.BlockSpec((None, HALO_ROWS, half),
                                      lambda bi, i: (bi, jnp.maximum(i * per - 1, 0), col))
    after = lambda col: pl.BlockSpec((None, HALO_ROWS, half),
                                     lambda bi, i: (bi, jnp.minimum((i + 1) * per, n_halo - 1), col))
    return pl.pallas_call(
        _conv_outproj_body,
        grid=(b, l // tm),
        in_specs=[
            main(0), main(1), main(2), before(1), before(2), after(1), after(2),
            pl.BlockSpec((3, half), lambda bi, i: (0, 0)),
            pl.BlockSpec((None, tm, half), lambda bi, i: (bi, i, 0)),
            pl.BlockSpec((2 * half, d), lambda bi, i: (0, 0)),
            pl.BlockSpec((None, tm, d), lambda bi, i: (bi, i, 0)),
            pl.BlockSpec((None, 1, d), lambda bi, i: (bi, 0, gate_col)),
            pl.BlockSpec((1, d), lambda bi, i: (0, 0)),
            pl.BlockSpec((1, d), lambda bi, i: (0, 0)),
        ],
        out_specs=pl.BlockSpec((None, tm, d), lambda bi, i: (bi, i, 0)),
        out_shape=jax.ShapeDtypeStruct((b, l, d), F32),
        compiler_params=_cparams(("parallel", "parallel")),
        name=name,
    )(y, y, y, y, y, y, y, w_conv, o_att, w_o, x, mod, ln_g, ln_b)


def _outproj_rows(a_ref, b_ref, w_ref, x_ref, gate_ref, lng_ref, lnb_ref):
    half = a_ref.shape[1]
    y = (jnp.dot(a_ref[...], w_ref[0:half, :], preferred_element_type=F32)
         + jnp.dot(b_ref[...], w_ref[half:2 * half, :], preferred_element_type=F32))
    r = DEEPNORM_ALPHA * x_ref[...] + gate_ref[...] * y
    return _layer_norm(r, lng_ref[...], lnb_ref[...])


MXU_WIDTH = 256


def _ffn_body(x_ref, sh_ref, sc_ref, gate_ref, wg_ref, wu_ref, wd_ref, lng_ref, lnb_ref, o_ref, *, chunks):
    x = x_ref[...]
    h = (x * (1.0 + sc_ref[...]) + sh_ref[...]).astype(BF16)
    acc = None
    for (c0, c1) in chunks:
        g = jnp.dot(h, wg_ref[:, c0:c1], preferred_element_type=F32)
        u = jnp.dot(h, wu_ref[:, c0:c1], preferred_element_type=F32)
        a = (g * jax.nn.sigmoid(g) * u).astype(BF16)
        part = jnp.dot(a, wd_ref[c0:c1, :], preferred_element_type=F32)
        acc = part if acc is None else acc + part
    r = DEEPNORM_ALPHA * x + gate_ref[...] * acc
    o_ref[...] = _layer_norm(r, lng_ref[...], lnb_ref[...])


def _ffn_ln(x, mod, shift_col, wg, wu, wd, ln_g, ln_b, *, tm, n_chunks, name="ffn"):
    b, l, d = x.shape
    ff = wg.shape[1]
    n_mxu = ff // MXU_WIDTH
    assert ff == n_mxu * MXU_WIDTH
    bounds = [MXU_WIDTH * ((n_mxu * k + n_chunks - 1) // n_chunks) for k in range(n_chunks + 1)]
    chunks = tuple((bounds[k], bounds[k + 1]) for k in range(n_chunks))
    resident = pl.Buffered(1)
    return pl.pallas_call(
        functools.partial(_ffn_body, chunks=chunks),
        grid=(b, l // tm),
        in_specs=[
            pl.BlockSpec((None, tm, d), lambda bi, i: (bi, i, 0)),
            pl.BlockSpec((None, 1, d), lambda bi, i: (bi, 0, shift_col)),
            pl.BlockSpec((None, 1, d), lambda bi, i: (bi, 0, shift_col + 1)),
            pl.BlockSpec((None, 1, d), lambda bi, i: (bi, 0, shift_col + 2)),
            pl.BlockSpec((d, ff), lambda bi, i: (0, 0), pipeline_mode=resident),
            pl.BlockSpec((d, ff), lambda bi, i: (0, 0), pipeline_mode=resident),
            pl.BlockSpec((ff, d), lambda bi, i: (0, 0), pipeline_mode=resident),
            pl.BlockSpec((1, d), lambda bi, i: (0, 0)),
            pl.BlockSpec((1, d), lambda bi, i: (0, 0)),
        ],
        out_specs=pl.BlockSpec((None, tm, d), lambda bi, i: (bi, i, 0)),
        out_shape=jax.ShapeDtypeStruct((b, l, d), F32),
        compiler_params=_cparams(("parallel", "parallel")),
        name=name,
    )(x, mod, mod, mod, wg, wu, wd, ln_g, ln_b)


MOE_TM = 1024
ROUTE_TM = 512
ZERO_ROWS = 256
META_I1, META_I2, META_R1, META_R2, META_P1, META_P2 = range(6)


def _outproj_route_body(a_ref, b_ref, w_ref, x_ref, gate_ref, lng_ref, lnb_ref, sh_ref, sc_ref, wr_ref,
                        o_ref, meta_ref, meta_t_ref, cnt_ref, step_cnt_ref, carry_scr):
    x_new = _outproj_rows(a_ref, b_ref, w_ref, x_ref, gate_ref, lng_ref, lnb_ref)
    o_ref[...] = x_new
    _route_rows(x_new, sh_ref, sc_ref, wr_ref, meta_ref, meta_t_ref, cnt_ref, step_cnt_ref, carry_scr)


def _route_rows(x, sh_ref, sc_ref, wr_ref, meta_ref, meta_t_ref, cnt_ref, step_cnt_ref, carry_scr):
    @pl.when(pl.program_id(0) == 0)
    def _():
        carry_scr[...] = jnp.zeros_like(carry_scr)

    h = (x * (1.0 + sc_ref[...]) + sh_ref[...]).astype(BF16)
    logits = jnp.dot(h, wr_ref[...], preferred_element_type=F32)
    tm = logits.shape[0]
    lane = lax.broadcasted_iota(jnp.int32, logits.shape, 1).astype(F32)
    l1 = jnp.where(lane < N_EXPERTS, logits, -jnp.inf)
    v1 = jnp.max(l1, axis=-1, keepdims=True)
    i1 = jnp.min(jnp.where(l1 == v1, lane, float(LANES)), axis=-1, keepdims=True)
    l2 = jnp.where(lane == i1, -jnp.inf, l1)
    v2 = jnp.max(l2, axis=-1, keepdims=True)
    i2 = jnp.min(jnp.where(l2 == v2, lane, float(LANES)), axis=-1, keepdims=True)
    t = jnp.exp(v2 - v1)
    p1 = 1.0 / (1.0 + t)
    p2 = t / (1.0 + t)

    member = jnp.where(lane == i1, 1.0, jnp.where(lane == i2, 1.0, 0.0))
    rr = lax.broadcasted_iota(jnp.int32, (tm, tm), 0)
    cc = lax.broadcasted_iota(jnp.int32, (tm, tm), 1)
    earlier = jnp.where(cc < rr, 1.0, 0.0).astype(BF16)
    base = carry_scr[0:1, :]
    rank = jnp.dot(earlier, member.astype(BF16), preferred_element_type=F32) + base
    total = base + jnp.sum(member, axis=0, keepdims=True)
    carry_scr[0:1, :] = total
    r1 = jnp.sum(jnp.where(lane == i1, rank, 0.0), axis=-1, keepdims=True)
    r2 = jnp.sum(jnp.where(lane == i2, rank, 0.0), axis=-1, keepdims=True)

    meta = jnp.zeros_like(logits)
    for k, val in ((META_I1, i1), (META_I2, i2), (META_R1, r1), (META_R2, r2), (META_P1, p1), (META_P2, p2)):
        meta = jnp.where(lane == float(k), val, meta)
    meta_ref[...] = meta
    meta_t_ref[...] = meta.T[0:SUBLANES, :]
    cnt_ref[...] = jnp.broadcast_to(total, cnt_ref.shape)
    step_cnt_ref[...] = jnp.broadcast_to(total - base, step_cnt_ref.shape)


def _outproj_route(a2, b2, w_o, x2, mod, gate_col, ln_g, ln_b, shift_col, w_router_pad, *, seq_len):
    m, d = x2.shape
    half = w_o.shape[0] // 2
    tm = ROUTE_TM
    batch = lambda col: pl.BlockSpec((None, 1, d), lambda i: ((i * tm) // seq_len, 0, col))
    return pl.pallas_call(
        _outproj_route_body,
        grid=(m // tm,),
        in_specs=[
            pl.BlockSpec((tm, half), lambda i: (i, 0)),
            pl.BlockSpec((tm, half), lambda i: (i, 0)),
            pl.BlockSpec((2 * half, d), lambda i: (0, 0)),
            pl.BlockSpec((tm, d), lambda i: (i, 0)),
            batch(gate_col),
            pl.BlockSpec((1, d), lambda i: (0, 0)),
            pl.BlockSpec((1, d), lambda i: (0, 0)),
            batch(shift_col),
            batch(shift_col + 1),
            pl.BlockSpec((d, LANES), lambda i: (0, 0)),
        ],
        out_specs=[pl.BlockSpec((tm, d), lambda i: (i, 0)),
                   pl.BlockSpec((tm, LANES), lambda i: (i, 0)),
                   pl.BlockSpec((SUBLANES, tm), lambda i: (0, i)),
                   pl.BlockSpec((8, LANES), lambda i: (0, 0)),
                   pl.BlockSpec((8, LANES), lambda i: (i, 0))],
        out_shape=[jax.ShapeDtypeStruct((m, d), F32),
                   jax.ShapeDtypeStruct((m, LANES), F32), jax.ShapeDtypeStruct((SUBLANES, m), F32),
                   jax.ShapeDtypeStruct((8, LANES), F32),
                   jax.ShapeDtypeStruct((8 * (m // tm), LANES), F32)],
        scratch_shapes=[pltpu.VMEM((8, LANES), F32)],
        compiler_params=_cparams(("arbitrary",)),
        name="odd_outproj_route",
    )(a2, b2, w_o, x2, mod, ln_g, ln_b, mod, mod, w_router_pad)


def _to_tiles(tile_ref, value):
    n = value.shape[0]
    for j in range(SUBLANES):
        tile_ref[pl.ds(j, n, stride=SUBLANES), :] = value[:, LANES * j:LANES * (j + 1)]


def _from_tiles(tile_ref):
    n = tile_ref.shape[0] // SUBLANES
    return jnp.concatenate([tile_ref[pl.ds(j, n, stride=SUBLANES), :] for j in range(SUBLANES)], axis=-1)


def _tile_rows(ref, row, n=1):
    return ref.at[pl.ds(pl.multiple_of(row * SUBLANES, SUBLANES), n * SUBLANES), :]


def _row_copy(src, src_row, dst, dst_row, sem):
    return pltpu.make_async_copy(_tile_rows(src, src_row), _tile_rows(dst, dst_row), sem)


def _rows_wait(src, dst, n, sem):
    pltpu.make_async_copy(_tile_rows(src, 0, n), _tile_rows(dst, 0, n), sem).wait()


DISPATCH_CH = 32
DISPATCH_SLOTS = 2 * ROUTE_TM + N_EXPERTS * DISPATCH_CH


def _dispatch_body(pad_ref, nch_ref, loc_ref, dst_ref, lslot_ref, x_ref, sh_ref, sc_ref, xs_hbm,
                   hs_scr, z_scr, row_sem):
    step = pl.program_id(0)
    tm = x_ref.shape[0]
    h = (x_ref[...] * (1.0 + sc_ref[...]) + sh_ref[...]).astype(BF16)
    ls = lslot_ref[...]
    srow = lax.broadcasted_iota(jnp.int32, (DISPATCH_SLOTS, tm), 0)
    pick = jnp.where(srow == ls[0:1, :], 1.0, jnp.where(srow == ls[1:2, :], 1.0, 0.0)).astype(BF16)
    _to_tiles(hs_scr, jnp.dot(pick, h, preferred_element_type=F32))

    def chunk_copy(e, k):
        src = _tile_rows(hs_scr, loc_ref[step * N_EXPERTS + e] + k * DISPATCH_CH, DISPATCH_CH)
        dst = _tile_rows(xs_hbm, dst_ref[step * N_EXPERTS + e] + k * DISPATCH_CH, DISPATCH_CH)
        return pltpu.make_async_copy(src, dst, row_sem)

    for e in range(N_EXPERTS):
        def start(k, carry, e=e):
            chunk_copy(e, k).start(priority=e % 2)
            return carry

        lax.fori_loop(0, nch_ref[step * N_EXPERTS + e], start, 0)
    for e in range(N_EXPERTS):
        def wait(k, carry, e=e):
            chunk_copy(e, k).wait()
            return carry

        lax.fori_loop(0, nch_ref[step * N_EXPERTS + e], wait, 0)

    @pl.when(step == pl.num_programs(0) - 1)
    def _():
        z_scr[...] = jnp.zeros_like(z_scr)
        for e in range(N_EXPERTS):
            start = pad_ref[e]
            count = pad_ref[N_EXPERTS + e]

            def fill(k, carry, start=start):
                _row_copy(z_scr, 0, xs_hbm, start + k, row_sem).start()
                return carry

            def fill_wait(k, carry):
                _row_copy(z_scr, 0, xs_hbm, 0, row_sem).wait()
                return carry

            lax.fori_loop(0, count, fill, 0)
            lax.fori_loop(0, count, fill_wait, 0)

        zrows = z_scr.shape[0] // SUBLANES
        used_rows = pad_ref[2 * N_EXPERTS]
        n_chunks = (xs_hbm.shape[0] // SUBLANES - used_rows) // zrows

        def chunk_copy(k):
            return pltpu.make_async_copy(z_scr, _tile_rows(xs_hbm, used_rows + k * zrows, zrows), row_sem)

        def fill_chunk(k, carry):
            chunk_copy(k).start()
            return carry

        def fill_chunk_wait(k, carry):
            chunk_copy(k).wait()
            return carry

        lax.fori_loop(0, n_chunks, fill_chunk, 0)
        lax.fori_loop(0, n_chunks, fill_chunk_wait, 0)


def _dispatch(x2, mod, shift_col, lslot, n_chunk, loc0, dest0, pad_info, n_rows, *, seq_len):
    m, d = x2.shape
    tm = ROUTE_TM
    return pl.pallas_call(
        _dispatch_body,
        grid_spec=pltpu.PrefetchScalarGridSpec(
            num_scalar_prefetch=4,
            grid=(m // tm,),
            in_specs=[
                pl.BlockSpec((SUBLANES, tm), lambda i, *_: (0, i)),
                pl.BlockSpec((tm, d), lambda i, *_: (i, 0)),
                pl.BlockSpec((None, 1, d), lambda i, *_: ((i * tm) // seq_len, 0, shift_col)),
                pl.BlockSpec((None, 1, d), lambda i, *_: ((i * tm) // seq_len, 0, shift_col + 1)),
            ],
            out_specs=pl.BlockSpec(memory_space=pl.ANY),
            scratch_shapes=[
                pltpu.VMEM((DISPATCH_SLOTS * SUBLANES, LANES), F32),
                pltpu.VMEM((ZERO_ROWS * SUBLANES, LANES), F32),
                pltpu.SemaphoreType.DMA,
            ],
        ),
        out_shape=jax.ShapeDtypeStruct((n_rows * SUBLANES, LANES), F32),
        compiler_params=_cparams(("arbitrary",)),
        name="moe_dispatch",
    )(pad_info, n_chunk, loc0, dest0, lslot, x2, mod, mod)


TAIL_LEVELS = 2


def _experts_body(te_ref, tbi_ref, tbo_ref, tr_ref, xs_ref, wg_ref, wu_ref, wd_ref, ys_ref, h_scr, acc_scr, *, n_f):
    t = pl.program_id(0)
    f = pl.program_id(1)

    @pl.when(f == 0)
    def _():
        h_scr[...] = _from_tiles(xs_ref).astype(BF16)
        acc_scr[...] = jnp.zeros_like(acc_scr)

    def swiglu_rows(n):
        h = h_scr[0:n, :]
        g = jnp.dot(h, wg_ref[...].astype(BF16), preferred_element_type=F32)
        u = jnp.dot(h, wu_ref[...].astype(BF16), preferred_element_type=F32)
        a = (g * jax.nn.sigmoid(g) * u).astype(BF16)
        acc_scr[0:n, :] += jnp.dot(a, wd_ref[...].astype(BF16), preferred_element_type=F32)

    rows = tr_ref[t]
    tm = h_scr.shape[0]
    bounds = (0,) + tuple(tm >> k for k in range(TAIL_LEVELS, -1, -1))
    for lo, hi in zip(bounds[:-1], bounds[1:]):
        pl.when((rows > lo) & (rows <= hi))(functools.partial(swiglu_rows, hi))

    @pl.when(f == n_f - 1)
    def _():
        _to_tiles(ys_ref, acc_scr[...])


def _experts(xs, tile_expert, tile_in, tile_out, tile_rows, wg, wu, wd, *, tf):
    n_rows = xs.shape[0] // SUBLANES
    d = wg.shape[1]
    tm = MOE_TM
    n_tiles = tile_expert.shape[0]
    ff = wg.shape[2]
    n_f = ff // tf
    ff_blk = lambda f, tr, t: f * jnp.minimum(tr[t], 1)
    return pl.pallas_call(
        functools.partial(_experts_body, n_f=n_f),
        grid_spec=pltpu.PrefetchScalarGridSpec(
            num_scalar_prefetch=4,
            grid=(n_tiles, n_f),
            in_specs=[
                pl.BlockSpec((tm * SUBLANES, LANES), lambda t, f, te, tbi, tbo, tr: (tbi[t], 0)),
                pl.BlockSpec((None, d, tf), lambda t, f, te, tbi, tbo, tr: (te[t], 0, ff_blk(f, tr, t))),
                pl.BlockSpec((None, d, tf), lambda t, f, te, tbi, tbo, tr: (te[t], 0, ff_blk(f, tr, t))),
                pl.BlockSpec((None, tf, d), lambda t, f, te, tbi, tbo, tr: (te[t], ff_blk(f, tr, t), 0)),
            ],
            out_specs=pl.BlockSpec((tm * SUBLANES, LANES), lambda t, f, te, tbi, tbo, tr: (tbo[t], 0)),
            scratch_shapes=[pltpu.VMEM((tm, d), BF16), pltpu.VMEM((tm, d), F32)],
        ),
        out_shape=jax.ShapeDtypeStruct((n_rows * SUBLANES, LANES), F32),
        compiler_params=_cparams(("arbitrary", "arbitrary")),
        name="moe_experts",
    )(tile_expert, tile_in, tile_out, tile_rows, xs, wg, wu, wd)


def _combine_body(pos_hbm, ys_hbm, meta_ref, x_ref, gate_ref, lng_ref, lnb_ref, o_ref,
                  y1_scr, y2_scr, pos_smem, pos_sem, row_sems):
    s = pl.program_id(0)
    n_tiles = pl.num_programs(0) - 1
    tm = x_ref.shape[0]

    def gather(buf):
        pos_cp = pltpu.make_async_copy(pos_hbm.at[s], pos_smem, pos_sem)
        pos_cp.start()
        pos_cp.wait()

        def issue(r, carry):
            _row_copy(ys_hbm, pos_smem[r], y1_scr.at[buf], r, row_sems.at[buf]).start(priority=0)
            _row_copy(ys_hbm, pos_smem[tm + r], y2_scr.at[buf], r, row_sems.at[buf]).start(priority=1)
            return carry

        lax.fori_loop(0, tm, issue, 0, unroll=8)

    def finish(buf):
        _rows_wait(ys_hbm, y1_scr.at[buf], tm, row_sems.at[buf])
        _rows_wait(ys_hbm, y2_scr.at[buf], tm, row_sems.at[buf])
        meta = meta_ref[...]
        p1 = meta[:, META_P1:META_P1 + 1]
        p2 = meta[:, META_P2:META_P2 + 1]
        mix = p1 * _from_tiles(y1_scr.at[buf]) + p2 * _from_tiles(y2_scr.at[buf])
        r = DEEPNORM_ALPHA * x_ref[...] + gate_ref[...] * mix
        o_ref[...] = _layer_norm(r, lng_ref[...], lnb_ref[...])

    for parity in (0, 1):
        pl.when((s % 2 == parity) & (s < n_tiles))(functools.partial(gather, parity))
    for parity in (0, 1):
        pl.when((s % 2 == parity) & (s >= 1))(functools.partial(finish, 1 - parity))


def _combine_ln(ys, pos, meta, x2, mod, gate_col, ln_g, ln_b, *, seq_len):
    m, d = x2.shape
    tm = ROUTE_TM
    prev = lambda i: jnp.maximum(i - 1, 0)
    return pl.pallas_call(
        _combine_body,
        grid=(m // tm + 1,),
        in_specs=[
            pl.BlockSpec(memory_space=pl.ANY),
            pl.BlockSpec(memory_space=pl.ANY),
            pl.BlockSpec((tm, LANES), lambda i: (prev(i), 0)),
            pl.BlockSpec((tm, d), lambda i: (prev(i), 0)),
            pl.BlockSpec((None, 1, d), lambda i: ((prev(i) * tm) // seq_len, 0, gate_col)),
            pl.BlockSpec((1, d), lambda i: (0, 0)),
            pl.BlockSpec((1, d), lambda i: (0, 0)),
        ],
        out_specs=pl.BlockSpec((tm, d), lambda i: (prev(i), 0)),
        out_shape=jax.ShapeDtypeStruct((m, d), F32),
        scratch_shapes=[
            pltpu.VMEM((2, tm * SUBLANES, LANES), F32),
            pltpu.VMEM((2, tm * SUBLANES, LANES), F32),
            pltpu.SMEM((2 * tm,), jnp.int32),
            pltpu.SemaphoreType.DMA,
            pltpu.SemaphoreType.DMA((2,)),
        ],
        compiler_params=_cparams(("arbitrary",)),
        name="moe_combine",
    )(pos, ys, meta, x2, mod, ln_g, ln_b)


def _outproj_moe_ln(a, bsrc, w_o, x, mod, ln_g0, ln_b0, w_router, wg, wu, wd, ln_g, ln_b, *, tf):
    b, l, d = x.shape
    assert d == SUBLANES * LANES
    m = b * l
    w_r = jnp.zeros((d, LANES), BF16).at[:, :N_EXPERTS].set(w_router.astype(BF16))
    x2, meta, meta_t, cnt, step_cnt = _outproj_route(a.reshape(m, -1), bsrc.reshape(m, -1), w_o, x.reshape(m, d),
                                                     mod, 2, ln_g0, ln_b0, 3, w_r, seq_len=l)
    n_steps = m // ROUTE_TM

    counts = cnt[0, :N_EXPERTS].astype(jnp.int32)
    n_tile_e = (counts + DISPATCH_CH - 1 + MOE_TM - 1) // MOE_TM
    tile_end = jnp.cumsum(n_tile_e)
    offs = (tile_end - n_tile_e) * MOE_TM
    idx = meta_t[META_I1:META_I2 + 1].astype(jnp.int32)
    rank = meta_t[META_R1:META_R2 + 1].astype(jnp.int32)
    step_n = step_cnt.reshape(n_steps, 8, LANES)[:, 0, :N_EXPERTS].astype(jnp.int32)
    step_before = jnp.cumsum(step_n, axis=0) - step_n
    step_pad = ((step_n + DISPATCH_CH - 1) // DISPATCH_CH) * DISPATCH_CH
    step_loc0 = jnp.cumsum(step_pad, axis=1) - step_pad
    to_local = step_loc0 - step_before
    base = jnp.zeros_like(idx)
    local = jnp.zeros_like(idx)
    for e in range(N_EXPERTS):
        base = jnp.where(idx == e, offs[e], base)
        local = jnp.where(idx == e, jnp.repeat(to_local[:, e], ROUTE_TM)[None, :], local)
    lslot = jnp.concatenate([local + rank, jnp.full((SUBLANES - 2, m), -1, jnp.int32)], axis=0)
    pos = (base + rank).reshape(2, n_steps, ROUTE_TM).transpose(1, 0, 2).reshape(n_steps, 2 * ROUTE_TM)
    n_tiles = (2 * m) // MOE_TM + N_EXPERTS
    tid = jnp.arange(n_tiles, dtype=jnp.int32)
    tile_valid = (tid < tile_end[-1]).astype(jnp.int32)
    tile_expert = jnp.minimum(jnp.sum((tid[:, None] >= tile_end[None, :]).astype(jnp.int32), axis=1),
                              N_EXPERTS - 1).astype(jnp.int32)
    tile_in = jnp.where(tile_valid > 0, tid, 0).astype(jnp.int32)
    tile_out = tid
    row0 = tid * MOE_TM - jnp.take(offs, tile_expert)
    tile_rows = (jnp.clip(jnp.take(counts, tile_expert) - row0, 0, MOE_TM) * tile_valid).astype(jnp.int32)
    pad_info = jnp.concatenate([offs + counts, n_tile_e * MOE_TM - counts,
                                tile_end[-1:] * MOE_TM]).astype(jnp.int32)
    n_rows = n_tiles * MOE_TM

    xs = _dispatch(x2, mod, 3, lslot, (step_pad // DISPATCH_CH).reshape(-1), step_loc0.reshape(-1),
                   (offs[None, :] + step_before).reshape(-1), pad_info, n_rows, seq_len=l)
    ys = _experts(xs, tile_expert, tile_in, tile_out, tile_rows, wg, wu, wd, tf=tf)
    out = _combine_ln(ys, pos, meta, x2, mod, 5, ln_g, ln_b, seq_len=l)
    return out.reshape(b, l, d)


def _fourier_body(pc_ref, dl_ref, dc_ref, o_ref, t_scr, *, out_scale, row_chunk):
    l = pc_ref.shape[0]
    for g in range(C_GROUPS):
        sl = slice(C_GROUP_DIM * g, C_GROUP_DIM * (g + 1))
        xg = pc_ref[:, sl].astype(F32)
        mu = jnp.mean(xg, axis=-1, keepdims=True)
        dlt = xg - mu
        var = jnp.mean(dlt * dlt, axis=-1, keepdims=True)
        gn = (dlt * lax.rsqrt(var + LN_EPS)).astype(BF16)
        t = jnp.dot(gn, dc_ref[...], preferred_element_type=F32)
        t_scr[0:l, sl] = t[:, 0:C_GROUP_DIM].astype(BF16)
        t_scr[l:2 * l, sl] = t[:, C_GROUP_DIM:2 * C_GROUP_DIM].astype(BF16)
    for r0 in range(0, l, row_chunk):
        acc = jnp.dot(dl_ref[r0:r0 + row_chunk, :], t_scr[...], preferred_element_type=F32)
        o_ref[r0:r0 + row_chunk, :] = (acc * out_scale).astype(BF16)


def _dft_matrices(l, c):
    j = np.arange(l, dtype=np.int64)
    ang_l = (2.0 * np.pi / l) * ((j[:, None] * j[None, :]) % l)
    dl = np.concatenate([np.cos(ang_l), -np.sin(ang_l)], axis=1)
    m = np.arange(c, dtype=np.int64)
    ang_c = (2.0 * np.pi / c) * ((m[:, None] * m[None, :]) % c)
    dc = np.concatenate([np.cos(ang_c), np.sin(ang_c)], axis=1)
    return dl.astype(np.float32), dc.astype(np.float32)


def _fourier_mixer(y, name="fourier"):
    b, l, _ = y.shape
    dl_np, dc_np = _dft_matrices(l, C_GROUP_DIM)
    dl = jnp.asarray(dl_np, dtype=F32).astype(BF16)
    dc = jnp.asarray(dc_np, dtype=F32).astype(BF16)
    out_scale = 1.0 / math.sqrt(l * C_GROUP_DIM)
    return pl.pallas_call(
        functools.partial(_fourier_body, out_scale=out_scale, row_chunk=min(l, 512)),
        grid=(b,),
        in_specs=[
            pl.BlockSpec((None, l, C_WIDTH), lambda bi: (bi, 0, 0)),
            pl.BlockSpec((l, 2 * l), lambda bi: (0, 0), pipeline_mode=pl.Buffered(1)),
            pl.BlockSpec((C_GROUP_DIM, 2 * C_GROUP_DIM), lambda bi: (0, 0)),
        ],
        out_specs=pl.BlockSpec((None, l, C_WIDTH), lambda bi: (bi, 0, 0)),
        out_shape=jax.ShapeDtypeStruct((b, l, C_WIDTH), BF16),
        scratch_shapes=[pltpu.VMEM((2 * l, C_WIDTH), BF16)],
        compiler_params=_cparams(("parallel",)),
        name=name,
    )(y, dl, dc)


NA_HG = 4
NA_GW = NA_HG * NA_DH
NA_WIN = NA_KR * GRID_W


def _natten_body(q_ref, k_ref, v_ref, kc_ref, vc_ref, bias_ref, o_ref, *, rows_per_step, n_rows):
    rb = pl.program_id(2)
    hq = NA_HG * GRID_W
    rid = lax.broadcasted_iota(jnp.int32, (hq, NA_GW), 0)
    cid = lax.broadcasted_iota(jnp.int32, (hq, NA_GW), 1)
    diag = (rid // GRID_W) == (cid // NA_DH)
    cid_o = lax.broadcasted_iota(jnp.int32, (GRID_W, NA_GW), 1)
    kc = kc_ref[...]
    vc = vc_ref[...]
    staged = []
    for j in range(rows_per_step):
        r = rb * rows_per_step + j
        rs = jnp.clip(r - NA_KR // 2, 0, n_rows - NA_KR)
        start = pl.multiple_of(rs * GRID_W, GRID_W)
        tid = jnp.minimum(r, NA_KR // 2) + jnp.maximum(r - (n_rows - NA_KR // 2), 0)
        q_r = q_ref[GRID_W * j:GRID_W * (j + 1), :]
        q4 = jnp.concatenate([q_r] * NA_HG, axis=0)
        qbd = jnp.where(diag, q4, jnp.zeros_like(q4))
        kw = k_ref[pl.ds(start, NA_WIN), :]
        s_loc = lax.dot_general(qbd, kw, NT_DIMS, preferred_element_type=F32) + bias_ref[tid]
        s_ctx = lax.dot_general(qbd, kc, NT_DIMS, preferred_element_type=F32)
        staged.append((start, s_loc, s_ctx))
    for j in range(rows_per_step):
        start, s_loc, s_ctx = staged[j]
        vw = v_ref[pl.ds(start, NA_WIN), :]
        m = jnp.maximum(jnp.max(s_loc, axis=-1, keepdims=True), jnp.max(s_ctx, axis=-1, keepdims=True))
        e_loc = jnp.exp2(s_loc - m)
        e_ctx = jnp.exp2(s_ctx - m)
        tot = jnp.sum(e_loc, axis=-1, keepdims=True) + jnp.sum(e_ctx, axis=-1, keepdims=True)
        o = (jnp.dot(e_loc.astype(BF16), vw, preferred_element_type=F32)
             + jnp.dot(e_ctx.astype(BF16), vc, preferred_element_type=F32))
        o = o * (1.0 / tot)
        out = jnp.zeros((GRID_W, NA_GW), F32)
        for hh in range(NA_HG):
            out = out + jnp.where((cid_o // NA_DH) == hh, o[GRID_W * hh:GRID_W * (hh + 1), :], 0.0)
        o_ref[GRID_W * j:GRID_W * (j + 1), :] = out.astype(BF16)


def _na_bias_table(rpb, n_rows):
    h = rpb.shape[0]
    cols = jnp.arange(GRID_W)
    col_start = jnp.clip(cols - NA_KC // 2, 0, GRID_W - NA_KC)
    col_valid = (cols[None, :] >= col_start[:, None]) & (cols[None, :] < col_start[:, None] + NA_KC)
    dc_idx = jnp.clip(cols[None, :] - cols[:, None] + NA_KC - 1, 0, 2 * NA_KC - 2)
    onehot = (dc_idx[:, :, None] == jnp.arange(2 * NA_KC - 1)).astype(F32)
    rpb_c = jnp.einsum("qkc,hdc->hqdk", onehot, rpb.astype(F32), precision=lax.Precision.HIGHEST)
    full = jnp.where(col_valid[None, :, None, :], rpb_c * LOG2E, NEG_INF)
    full = full.reshape(h // NA_HG, NA_HG * GRID_W, (2 * NA_KR - 1) * GRID_W)
    half = NA_KR // 2
    rep_rows = list(range(half)) + [half] + list(range(n_rows - half + 1, n_rows))
    tabs = []
    for r in rep_rows:
        rs = min(max(r - half, 0), n_rows - NA_KR)
        d0 = rs - r + NA_KR - 1
        tabs.append(full[:, :, GRID_W * d0:GRID_W * d0 + NA_WIN])
    return jnp.stack(tabs, axis=0)


def _natten(y, q_col0, k_col0, v_col0, y_ctx, kc_col0, vc_col0, bias_tab, *, rows_per_step, name="natten"):
    b, l, _ = y.shape
    lc = y_ctx.shape[1]
    n_rows = l // GRID_W
    n_tab = bias_tab.shape[0]
    n_grp = NA_HEADS // NA_HG
    tq = rows_per_step * GRID_W
    qb, kb, vb = q_col0 // NA_GW, k_col0 // NA_GW, v_col0 // NA_GW
    kcb, vcb = kc_col0 // NA_GW, vc_col0 // NA_GW
    return pl.pallas_call(
        functools.partial(_natten_body, rows_per_step=rows_per_step, n_rows=n_rows),
        grid=(b, n_grp, n_rows // rows_per_step),
        in_specs=[
            pl.BlockSpec((None, tq, NA_GW), lambda bi, g, i: (bi, i, qb + g)),
            pl.BlockSpec((None, l, NA_GW), lambda bi, g, i: (bi, 0, kb + g)),
            pl.BlockSpec((None, l, NA_GW), lambda bi, g, i: (bi, 0, vb + g)),
            pl.BlockSpec((None, lc, NA_GW), lambda bi, g, i: (bi, 0, kcb + g)),
            pl.BlockSpec((None, lc, NA_GW), lambda bi, g, i: (bi, 0, vcb + g)),
            pl.BlockSpec((n_tab, None, NA_HG * GRID_W, NA_WIN), lambda bi, g, i: (0, g, 0, 0)),
        ],
        out_specs=pl.BlockSpec((None, tq, NA_GW), lambda bi, g, i: (bi, i, g)),
        out_shape=jax.ShapeDtypeStruct((b, l, D_WIDTH), BF16),
        compiler_params=_cparams(("parallel", "parallel", "arbitrary")),
        name=name,
    )(y, y, y, y_ctx, y_ctx, bias_tab)


def _rope_tables(l):
    t = jnp.arange(l, dtype=jnp.int32)
    row = (t // GRID_W).astype(F32)
    col = (t % GRID_W).astype(F32)
    n_freq = DIFF_DH // 4
    inv_freq = ROPE_THETA ** (-jnp.arange(n_freq, dtype=F32) / n_freq)
    ang = jnp.concatenate([row[:, None] * inv_freq, col[:, None] * inv_freq], axis=-1)
    c, s = jnp.cos(ang), jnp.sin(ang)
    cos = jnp.tile(jnp.concatenate([c, c], axis=-1), (1, 2))
    sin = jnp.tile(jnp.concatenate([-s, s], axis=-1), (1, 2))
    return cos, sin


def kernel(x, c, ctx, c_ctx, w_mod, b_mod, ln_g, ln_b, e_w_in, e_conv, e_lam_q1, e_lam_k1, e_lam_q2, e_lam_k2, e_subln_g, e_w_o, e_ffn_gate, e_ffn_up, e_ffn_down, o_w_in, o_rpb, o_w_o, o_router, o_exp_gate, o_exp_up, o_exp_down):
    b, l, d = x.shape
    lc = ctx.shape[1]
    assert d == D_MODEL and l % 512 == 0 and lc % 256 == 0 and b + 1 <= MOD_ROWS

    cond = jnp.concatenate([c, c_ctx[None, :], jnp.zeros((MOD_ROWS - b - 1, d), F32)], axis=0)
    mods = _adaln(cond, w_mod, b_mod)

    def layer_mods(i):
        lat = mods[i, :b][:, None, :]
        cx = jnp.broadcast_to(mods[i, b][None, None, :], (b, 1, 6 * d))
        return lat, cx

    q_scale_diff = DIFF_DH ** -0.5 * LOG2E
    q_scale_na = NA_DH ** -0.5 * LOG2E

    mod_lat, mod_ctx = layer_mods(0)
    lam_init = 0.8 - 0.6 * math.exp(-0.3 * 0)
    a_end = 3 * A_WIDTH
    w_in = e_w_in[0]
    w_main = w_in[:, :a_end + 2 * DIFF_QK].astype(BF16)
    w_vt = w_in[:, a_end + 2 * DIFF_QK:].T.astype(BF16)
    rope = _rope_tables(l)
    pa_chunks = [(0, 512, "plain"), (512, 512, "plain"), (1024, 512, "plain")]
    y_lat, vt_lat = _proj(x, mod_lat, 0, w_main,
                          pa_chunks + [(a_end, 512, "rope_scale"), (a_end + 512, 512, "rope")],
                          tm=512, wvt=w_vt, rope=rope, q_scale=q_scale_diff, name="even_inproj_lat")
    y_ctx, vt_ctx = _proj(ctx, mod_ctx, 0, w_main,
                          pa_chunks + [(a_end, 512, "scale"), (a_end + 512, 512, "plain")],
                          tm=lc, wvt=w_vt, q_scale=q_scale_diff, name="even_inproj_ctx")

    lam_pack = jnp.zeros((8, LANES), F32)
    lam_pack = lam_pack.at[0, :DIFF_DH].set(e_lam_q1[0]).at[1, :DIFF_DH].set(e_lam_k1[0])
    lam_pack = lam_pack.at[2, :DIFF_DH].set(e_lam_q2[0]).at[3, :DIFF_DH].set(e_lam_k2[0])
    sub_g = e_subln_g[0].reshape(1, DIFF_VDIM)
    o_lat = _diff_attention(y_lat, [(y_ctx, vt_ctx), (y_lat, vt_lat)], lam_pack, sub_g, lam_init,
                            tq=256, q_col0=a_end, k_col0=a_end + DIFF_QK, name="diffattn_lat")
    o_ctx = _diff_attention(y_ctx, [(y_ctx, vt_ctx)], lam_pack, sub_g, lam_init,
                            tq=lc, q_col0=a_end, k_col0=a_end + DIFF_QK, name="diffattn_ctx")

    w_o = e_w_o[0].astype(BF16)
    lng0, lnb0 = ln_g[0, 0][None, :], ln_b[0, 0][None, :]
    lng1, lnb1 = ln_g[0, 1][None, :], ln_b[0, 1][None, :]
    x_lat = _conv_outproj_ln(y_lat, e_conv[0], o_lat, w_o, x, mod_lat, 2, lng0, lnb0, tm=512,
                             name="even_outproj_lat")
    x_ctx = _conv_outproj_ln(y_ctx, e_conv[0], o_ctx, w_o, ctx, mod_ctx, 2, lng0, lnb0, tm=lc,
                             name="even_outproj_ctx")

    wg = e_ffn_gate[0].astype(BF16)
    wu = e_ffn_up[0].astype(BF16)
    wd = e_ffn_down[0].astype(BF16)
    x_lat = _ffn_ln(x_lat, mod_lat, 3, wg, wu, wd, lng1, lnb1, tm=512, n_chunks=2, name="ffn_lat")
    x_ctx = _ffn_ln(x_ctx, mod_ctx, 3, wg, wu, wd, lng1, lnb1, tm=lc, n_chunks=2, name="ffn_ctx")

    mod_lat, mod_ctx = layer_mods(1)
    w_in = o_w_in[0].astype(BF16)
    y_lat = _proj(x_lat, mod_lat, 0, w_in,
                  [(0, 512, "plain"), (512, 512, "scale"), (1024, 512, "plain"), (1536, 512, "plain")],
                  tm=512, q_scale=q_scale_na, name="odd_inproj_lat")
    y_ctx = _proj(x_ctx, mod_ctx, 0, w_in[:, C_WIDTH + D_WIDTH:],
                  [(0, 512, "plain"), (512, 512, "plain")], tm=lc, name="odd_inproj_ctx")
    f_lat = _fourier_mixer(y_lat)
    bias_tab = _na_bias_table(o_rpb[0], l // GRID_W)
    n_lat = _natten(y_lat, C_WIDTH, C_WIDTH + D_WIDTH, C_WIDTH + 2 * D_WIDTH, y_ctx, 0, D_WIDTH,
                    bias_tab, rows_per_step=16)

    w_o = o_w_o[0].astype(BF16)
    lng0, lnb0 = ln_g[1, 0][None, :], ln_b[1, 0][None, :]
    lng1, lnb1 = ln_g[1, 1][None, :], ln_b[1, 1][None, :]
    return _outproj_moe_ln(f_lat, n_lat, w_o, x_lat, mod_lat, lng0, lnb0, o_router[0],
                           o_exp_gate[0], o_exp_up[0], o_exp_down[0], lng1, lnb1, tf=512)
```

```python
import functools
import math

import numpy as np
import jax
import jax.numpy as jnp
from jax import lax
from jax.experimental import pallas as pl
from jax.experimental.pallas import tpu as pltpu

F32 = jnp.float32
BF16 = jnp.bfloat16

D_MODEL = 1024
GRID_W = 64
DEPTH = 2

A_WIDTH = 512
DIFF_HEADS = 4
DIFF_DH = 64
DIFF_VDIM = 128
DIFF_QK = 512
B_WIDTH = 512

C_WIDTH = 512
C_GROUPS = 4
C_GROUP_DIM = 128
NA_HEADS = 8
NA_DH = 64
D_WIDTH = 512
NA_KR = 8
NA_KC = 16

N_EXPERTS = 8

ROPE_THETA = 10000.0
LN_EPS = 1e-5
RMS_EPS = 1e-5
NEG_INF = -1e30
DEEPNORM_ALPHA = (2 * DEPTH) ** 0.25
LOG2E = 1.4426950408889634

LANES = 128
SUBLANES = 8
MOD_ROWS = 32
NT_DIMS = (((1,), (1,)), ((), ()))


def _cparams(sem, vmem_mb=48):
    return pltpu.CompilerParams(dimension_semantics=sem, vmem_limit_bytes=vmem_mb * 1024 * 1024)


def _layer_norm(r, g, b):
    mu = jnp.mean(r, axis=-1, keepdims=True)
    d = r - mu
    var = jnp.mean(d * d, axis=-1, keepdims=True)
    return d * lax.rsqrt(var + LN_EPS) * g + b


def _adaln_body(c_ref, w_ref, b_ref, o_ref):
    cnd = c_ref[...]
    s = (cnd * jax.nn.sigmoid(cnd)).astype(BF16)
    o_ref[...] = jnp.dot(s, w_ref[...].astype(BF16), preferred_element_type=F32) + b_ref[...]


def _adaln(cond, w_mod, b_mod):
    depth, d, n = w_mod.shape
    tn = 1536
    return pl.pallas_call(
        _adaln_body,
        grid=(depth, n // tn),
        in_specs=[
            pl.BlockSpec((MOD_ROWS, d), lambda l, j: (0, 0)),
            pl.BlockSpec((None, d, tn), lambda l, j: (l, 0, j)),
            pl.BlockSpec((None, 1, tn), lambda l, j: (l, 0, j)),
        ],
        out_specs=pl.BlockSpec((None, MOD_ROWS, tn), lambda l, j: (l, 0, j)),
        out_shape=jax.ShapeDtypeStruct((depth, MOD_ROWS, n), F32),
        compiler_params=_cparams(("parallel", "parallel")),
        name="adaln",
    )(cond, w_mod, b_mod.reshape(depth, 1, n))


def _proj_body(*refs, chunks, with_vt, with_rope, q_scale):
    x_ref, sh_ref, sc_ref, w_ref = refs[:4]
    i = 4
    if with_vt:
        wvt_ref = refs[i]
        i += 1
    if with_rope:
        cos_ref, sin_ref = refs[i], refs[i + 1]
        i += 2
    y_ref = refs[i]
    vt_ref = refs[i + 1] if with_vt else None

    h = (x_ref[...] * (1.0 + sc_ref[...]) + sh_ref[...]).astype(BF16)
    tm = h.shape[0]
    if with_rope:
        cos = cos_ref[...]
        sin = sin_ref[...]
        lane = lax.broadcasted_iota(jnp.int32, (tm, LANES), 1)
        low_half = (lane % 64) < 32
    for (c0, width, kind) in chunks:
        acc = jnp.dot(h, w_ref[:, c0:c0 + width], preferred_element_type=F32)
        if kind in ("rope", "rope_scale"):
            for j in range(width // LANES):
                a = acc[:, LANES * j:LANES * (j + 1)]
                rot = jnp.where(low_half, pltpu.roll(a, 96, 1), pltpu.roll(a, 32, 1))
                r = a * cos + rot * sin
                if kind == "rope_scale":
                    r = r * q_scale
                y_ref[:, c0 + LANES * j:c0 + LANES * (j + 1)] = r.astype(BF16)
        elif kind == "scale":
            y_ref[:, c0:c0 + width] = (acc * q_scale).astype(BF16)
        else:
            y_ref[:, c0:c0 + width] = acc.astype(BF16)
    if with_vt:
        vt = lax.dot_general(wvt_ref[...], h, NT_DIMS, preferred_element_type=F32)
        vt_ref[...] = vt.astype(BF16)


def _proj(x, mod, shift_col, w, chunks, *, tm, wvt=None, rope=None, q_scale=1.0, name="proj"):
    b, l, d = x.shape
    n = w.shape[1]
    with_vt = wvt is not None
    with_rope = rope is not None
    in_specs = [
        pl.BlockSpec((None, tm, d), lambda bi, i: (bi, i, 0)),
        pl.BlockSpec((None, 1, d), lambda bi, i: (bi, 0, shift_col)),
        pl.BlockSpec((None, 1, d), lambda bi, i: (bi, 0, shift_col + 1)),
        pl.BlockSpec((d, n), lambda bi, i: (0, 0)),
    ]
    args = [x, mod, mod, w]
    if with_vt:
        nv = wvt.shape[0]
        in_specs.append(pl.BlockSpec((nv, d), lambda bi, i: (0, 0)))
        args.append(wvt)
    if with_rope:
        in_specs += [pl.BlockSpec((tm, LANES), lambda bi, i: (i, 0))] * 2
        args += [rope[0], rope[1]]
    out_specs = [pl.BlockSpec((None, tm, n), lambda bi, i: (bi, i, 0))]
    out_shape = [jax.ShapeDtypeStruct((b, l, n), BF16)]
    if with_vt:
        out_specs.append(pl.BlockSpec((None, nv, tm), lambda bi, i: (bi, 0, i)))
        out_shape.append(jax.ShapeDtypeStruct((b, nv, l), BF16))
    res = pl.pallas_call(
        functools.partial(_proj_body, chunks=tuple(chunks), with_vt=with_vt, with_rope=with_rope,
                          q_scale=q_scale),
        grid=(b, l // tm),
        in_specs=in_specs,
        out_specs=out_specs,
        out_shape=out_shape,
        compiler_params=_cparams(("parallel", "parallel")),
        name=name,
    )(*args)
    return res if with_vt else res[0]


DIFF_HP = 4


def _diffattn_body(*refs, n_seg, lam_init):
    q_ref, lam_ref, g_ref = refs[:3]
    k_refs = [refs[3 + 2 * s] for s in range(n_seg)]
    vt_refs = [refs[4 + 2 * s] for s in range(n_seg)]
    o_ref = refs[3 + 2 * n_seg]

    tq = q_ref.shape[0]
    lane = lax.broadcasted_iota(jnp.int32, (tq, LANES), 1)
    lp = lam_ref[...]
    lam = (jnp.exp(jnp.sum(lp[0:1] * lp[1:2], axis=1, keepdims=True))
           - jnp.exp(jnp.sum(lp[2:3] * lp[3:4], axis=1, keepdims=True)) + lam_init)

    scores = []
    for h in range(DIFF_HP):
        cols = slice(LANES * h, LANES * (h + 1))
        q = q_ref[:, cols]
        zero = jnp.zeros_like(q)
        per_comp = []
        for qm in (jnp.where(lane < DIFF_DH, q, zero), jnp.where(lane >= DIFF_DH, q, zero)):
            per_comp.append([lax.dot_general(k_ref[:, cols], qm, NT_DIMS, preferred_element_type=F32)
                             for k_ref in k_refs])
        scores.append(per_comp)

    def unnormalised(s, rows):
        m = functools.reduce(jnp.maximum, [jnp.max(x, axis=0, keepdims=True) for x in s])
        tot = None
        acc = None
        for x, vt_ref in zip(s, vt_refs):
            e = jnp.exp2(x - m)
            t = jnp.sum(e, axis=0, keepdims=True)
            pv = jnp.dot(vt_ref[rows, :], e.astype(BF16), preferred_element_type=F32)
            tot = t if tot is None else tot + t
            acc = pv if acc is None else acc + pv
        return acc, tot

    for h in range(DIFF_HP):
        cols = slice(LANES * h, LANES * (h + 1))
        rows = slice(DIFF_VDIM * h, DIFF_VDIM * (h + 1))
        acc1, l1 = unnormalised(scores[h][0], rows)
        acc2, l2 = unnormalised(scores[h][1], rows)
        o_t = acc1 * (1.0 / l1) - acc2 * (lam / l2)
        o = o_t.T
        ms = jnp.mean(o * o, axis=-1, keepdims=True)
        o_ref[:, cols] = (o * lax.rsqrt(ms + RMS_EPS) * g_ref[...] * (1.0 - lam_init)).astype(BF16)


def _diff_attention(yq, segs, lam_pack, sub_g, lam_init, *, tq, q_col0, k_col0, name):
    b, lq, _ = yq.shape
    width = DIFF_HP * LANES
    qb0 = q_col0 // width
    kb0 = k_col0 // width
    in_specs = [
        pl.BlockSpec((None, tq, width), lambda bi, h, i: (bi, i, qb0 + h)),
        pl.BlockSpec((8, LANES), lambda bi, h, i: (0, 0)),
        pl.BlockSpec((1, LANES), lambda bi, h, i: (0, 0)),
    ]
    args = [yq, lam_pack, sub_g]
    for (yk, vt) in segs:
        lk = yk.shape[1]
        in_specs.append(pl.BlockSpec((None, lk, width), lambda bi, h, i: (bi, 0, kb0 + h)))
        in_specs.append(pl.BlockSpec((None, DIFF_HP * DIFF_VDIM, lk), lambda bi, h, i: (bi, h, 0)))
        args += [yk, vt]
    return pl.pallas_call(
        functools.partial(_diffattn_body, n_seg=len(segs), lam_init=lam_init),
        grid=(b, DIFF_HEADS // DIFF_HP, lq // tq),
        in_specs=in_specs,
        out_specs=pl.BlockSpec((None, tq, width), lambda bi, h, i: (bi, i, h)),
        out_shape=jax.ShapeDtypeStruct((b, lq, B_WIDTH), BF16),
        compiler_params=_cparams(("parallel", "parallel", "arbitrary")),
        name=name,
    )(*args)


HALO_ROWS = 16


def _conv_outproj_body(bg_ref, cg_ref, val_ref, cgp_ref, valp_ref, cgn_ref, valn_ref, wc_ref,
                       o_att_ref, w_ref, x_ref, gate_ref, lng_ref, lnb_ref, o_ref):
    i = pl.program_id(1)
    tm = x_ref.shape[0]
    half = o_att_ref.shape[1]
    u = cg_ref[...].astype(F32) * val_ref[...].astype(F32)
    u_before = cgp_ref[...].astype(F32) * valp_ref[...].astype(F32)
    u_after = cgn_ref[...].astype(F32) * valn_ref[...].astype(F32)
    first = jnp.where(i == 0, 0.0, u_before[HALO_ROWS - 1:HALO_ROWS, :])
    last = jnp.where(i == pl.num_programs(1) - 1, 0.0, u_after[0:1, :])
    row = lax.broadcasted_iota(jnp.int32, (tm, half), 0)
    u_prev = jnp.where(row == 0, first, pltpu.roll(u, 1, 0))
    u_next = jnp.where(row == tm - 1, last, pltpu.roll(u, tm - 1, 0))
    wc = wc_ref[...]
    conv = bg_ref[...].astype(F32) * (u_prev * wc[0:1] + u * wc[1:2] + u_next * wc[2:3])
    y = (jnp.dot(conv.astype(BF16), w_ref[0:half, :], preferred_element_type=F32)
         + jnp.dot(o_att_ref[...], w_ref[half:2 * half, :], preferred_element_type=F32))
    r = DEEPNORM_ALPHA * x_ref[...] + gate_ref[...] * y
    o_ref[...] = _layer_norm(r, lng_ref[...], lnb_ref[...])


def _conv_outproj_ln(y, w_conv, o_att, w_o, x, mod, gate_col, ln_g, ln_b, *, tm, name):
    b, l, d = x.shape
    half = w_o.shape[0] // 2
    per = tm // HALO_ROWS
    n_halo = l // HALO_ROWS
    main = lambda col: pl.BlockSpec((None, tm, half), lambda bi, i: (bi, i, col))
    before = lambda col: pl.BlockSpec((None, HALO_ROWS, half),
                                      lambda bi, i: (bi, jnp.maximum(i * per - 1, 0), col))
    after = lambda col: pl.BlockSpec((None, HALO_ROWS, half),
                                     lambda bi, i: (bi, jnp.minimum((i + 1) * per, n_halo - 1), col))
    return pl.pallas_call(
        _conv_outproj_body,
        grid=(b, l // tm),
        in_specs=[
            main(0), main(1), main(2), before(1), before(2), after(1), after(2),
            pl.BlockSpec((3, half), lambda bi, i: (0, 0)),
            pl.BlockSpec((None, tm, half), lambda bi, i: (bi, i, 0)),
            pl.BlockSpec((2 * half, d), lambda bi, i: (0, 0)),
            pl.BlockSpec((None, tm, d), lambda bi, i: (bi, i, 0)),
            pl.BlockSpec((None, 1, d), lambda bi, i: (bi, 0, gate_col)),
            pl.BlockSpec((1, d), lambda bi, i: (0, 0)),
            pl.BlockSpec((1, d), lambda bi, i: (0, 0)),
        ],
        out_specs=pl.BlockSpec((None, tm, d), lambda bi, i: (bi, i, 0)),
        out_shape=jax.ShapeDtypeStruct((b, l, d), F32),
        compiler_params=_cparams(("parallel", "parallel")),
        name=name,
    )(y, y, y, y, y, y, y, w_conv, o_att, w_o, x, mod, ln_g, ln_b)


def _outproj_rows(a_ref, b_ref, w_ref, x_ref, gate_ref, lng_ref, lnb_ref):
    half = a_ref.shape[1]
    y = (jnp.dot(a_ref[...], w_ref[0:half, :], preferred_element_type=F32)
         + jnp.dot(b_ref[...], w_ref[half:2 * half, :], preferred_element_type=F32))
    r = DEEPNORM_ALPHA * x_ref[...] + gate_ref[...] * y
    return _layer_norm(r, lng_ref[...], lnb_ref[...])


MXU_WIDTH = 256


def _ffn_body(x_ref, sh_ref, sc_ref, gate_ref, wg_ref, wu_ref, wd_ref, lng_ref, lnb_ref, o_ref, *, chunks):
    x = x_ref[...]
    h = (x * (1.0 + sc_ref[...]) + sh_ref[...]).astype(BF16)
    acc = None
    for (c0, c1) in chunks:
        g = jnp.dot(h, wg_ref[:, c0:c1], preferred_element_type=F32)
        u = jnp.dot(h, wu_ref[:, c0:c1], preferred_element_type=F32)
        a = (g * jax.nn.sigmoid(g) * u).astype(BF16)
        part = jnp.dot(a, wd_ref[c0:c1, :], preferred_element_type=F32)
        acc = part if acc is None else acc + part
    r = DEEPNORM_ALPHA * x + gate_ref[...] * acc
    o_ref[...] = _layer_norm(r, lng_ref[...], lnb_ref[...])


def _ffn_ln(x, mod, shift_col, wg, wu, wd, ln_g, ln_b, *, tm, n_chunks, name="ffn"):
    b, l, d = x.shape
    ff = wg.shape[1]
    n_mxu = ff // MXU_WIDTH
    assert ff == n_mxu * MXU_WIDTH
    bounds = [MXU_WIDTH * ((n_mxu * k + n_chunks - 1) // n_chunks) for k in range(n_chunks + 1)]
    chunks = tuple((bounds[k], bounds[k + 1]) for k in range(n_chunks))
    resident = pl.Buffered(1)
    return pl.pallas_call(
        functools.partial(_ffn_body, chunks=chunks),
        grid=(b, l // tm),
        in_specs=[
            pl.BlockSpec((None, tm, d), lambda bi, i: (bi, i, 0)),
            pl.BlockSpec((None, 1, d), lambda bi, i: (bi, 0, shift_col)),
            pl.BlockSpec((None, 1, d), lambda bi, i: (bi, 0, shift_col + 1)),
            pl.BlockSpec((None, 1, d), lambda bi, i: (bi, 0, shift_col + 2)),
            pl.BlockSpec((d, ff), lambda bi, i: (0, 0), pipeline_mode=resident),
            pl.BlockSpec((d, ff), lambda bi, i: (0, 0), pipeline_mode=resident),
            pl.BlockSpec((ff, d), lambda bi, i: (0, 0), pipeline_mode=resident),
            pl.BlockSpec((1, d), lambda bi, i: (0, 0)),
            pl.BlockSpec((1, d), lambda bi, i: (0, 0)),
        ],
        out_specs=pl.BlockSpec((None, tm, d), lambda bi, i: (bi, i, 0)),
        out_shape=jax.ShapeDtypeStruct((b, l, d), F32),
        compiler_params=_cparams(("parallel", "parallel")),
        name=name,
    )(x, mod, mod, mod, wg, wu, wd, ln_g, ln_b)


MOE_TM = 1024
ROUTE_TM = 512
ZERO_ROWS = 256
META_I1, META_I2, META_R1, META_R2, META_P1, META_P2 = range(6)


def _outproj_route_body(a_ref, b_ref, w_ref, x_ref, gate_ref, lng_ref, lnb_ref, sh_ref, sc_ref, wr_ref,
                        o_ref, meta_ref, meta_t_ref, cnt_ref, step_cnt_ref, carry_scr):
    x_new = _outproj_rows(a_ref, b_ref, w_ref, x_ref, gate_ref, lng_ref, lnb_ref)
    o_ref[...] = x_new
    _route_rows(x_new, sh_ref, sc_ref, wr_ref, meta_ref, meta_t_ref, cnt_ref, step_cnt_ref, carry_scr)


def _route_rows(x, sh_ref, sc_ref, wr_ref, meta_ref, meta_t_ref, cnt_ref, step_cnt_ref, carry_scr):
    @pl.when(pl.program_id(0) == 0)
    def _():
        carry_scr[...] = jnp.zeros_like(carry_scr)

    h = (x * (1.0 + sc_ref[...]) + sh_ref[...]).astype(BF16)
    logits = jnp.dot(h, wr_ref[...], preferred_element_type=F32)
    tm = logits.shape[0]
    lane = lax.broadcasted_iota(jnp.int32, logits.shape, 1).astype(F32)
    l1 = jnp.where(lane < N_EXPERTS, logits, -jnp.inf)
    v1 = jnp.max(l1, axis=-1, keepdims=True)
    i1 = jnp.min(jnp.where(l1 == v1, lane, float(LANES)), axis=-1, keepdims=True)
    l2 = jnp.where(lane == i1, -jnp.inf, l1)
    v2 = jnp.max(l2, axis=-1, keepdims=True)
    i2 = jnp.min(jnp.where(l2 == v2, lane, float(LANES)), axis=-1, keepdims=True)
    t = jnp.exp(v2 - v1)
    p1 = 1.0 / (1.0 + t)
    p2 = t / (1.0 + t)

    member = jnp.where(lane == i1, 1.0, jnp.where(lane == i2, 1.0, 0.0))
    rr = lax.broadcasted_iota(jnp.int32, (tm, tm), 0)
    cc = lax.broadcasted_iota(jnp.int32, (tm, tm), 1)
    earlier = jnp.where(cc < rr, 1.0, 0.0).astype(BF16)
    base = carry_scr[0:1, :]
    rank = jnp.dot(earlier, member.astype(BF16), preferred_element_type=F32) + base
    total = base + jnp.sum(member, axis=0, keepdims=True)
    carry_scr[0:1, :] = total
    r1 = jnp.sum(jnp.where(lane == i1, rank, 0.0), axis=-1, keepdims=True)
    r2 = jnp.sum(jnp.where(lane == i2, rank, 0.0), axis=-1, keepdims=True)

    meta = jnp.zeros_like(logits)
    for k, val in ((META_I1, i1), (META_I2, i2), (META_R1, r1), (META_R2, r2), (META_P1, p1), (META_P2, p2)):
        meta = jnp.where(lane == float(k), val, meta)
    meta_ref[...] = meta
    meta_t_ref[...] = meta.T[0:SUBLANES, :]
    cnt_ref[...] = jnp.broadcast_to(total, cnt_ref.shape)
    step_cnt_ref[...] = jnp.broadcast_to(total - base, step_cnt_ref.shape)


def _outproj_route(a2, b2, w_o, x2, mod, gate_col, ln_g, ln_b, shift_col, w_router_pad, *, seq_len):
    m, d = x2.shape
    half = w_o.shape[0] // 2
    tm = ROUTE_TM
    batch = lambda col: pl.BlockSpec((None, 1, d), lambda i: ((i * tm) // seq_len, 0, col))
    return pl.pallas_call(
        _outproj_route_body,
        grid=(m // tm,),
        in_specs=[
            pl.BlockSpec((tm, half), lambda i: (i, 0)),
            pl.BlockSpec((tm, half), lambda i: (i, 0)),
            pl.BlockSpec((2 * half, d), lambda i: (0, 0)),
            pl.BlockSpec((tm, d), lambda i: (i, 0)),
            batch(gate_col),
            pl.BlockSpec((1, d), lambda i: (0, 0)),
            pl.BlockSpec((1, d), lambda i: (0, 0)),
            batch(shift_col),
            batch(shift_col + 1),
            pl.BlockSpec((d, LANES), lambda i: (0, 0)),
        ],
        out_specs=[pl.BlockSpec((tm, d), lambda i: (i, 0)),
                   pl.BlockSpec((tm, LANES), lambda i: (i, 0)),
                   pl.BlockSpec((SUBLANES, tm), lambda i: (0, i)),
                   pl.BlockSpec((8, LANES), lambda i: (0, 0)),
                   pl.BlockSpec((8, LANES), lambda i: (i, 0))],
        out_shape=[jax.ShapeDtypeStruct((m, d), F32),
                   jax.ShapeDtypeStruct((m, LANES), F32), jax.ShapeDtypeStruct((SUBLANES, m), F32),
                   jax.ShapeDtypeStruct((8, LANES), F32),
                   jax.ShapeDtypeStruct((8 * (m // tm), LANES), F32)],
        scratch_shapes=[pltpu.VMEM((8, LANES), F32)],
        compiler_params=_cparams(("arbitrary",)),
        name="odd_outproj_route",
    )(a2, b2, w_o, x2, mod, ln_g, ln_b, mod, mod, w_router_pad)


def _to_tiles(tile_ref, value):
    n = value.shape[0]
    for j in range(SUBLANES):
        tile_ref[pl.ds(j, n, stride=SUBLANES), :] = value[:, LANES * j:LANES * (j + 1)]


def _from_tiles(tile_ref):
    n = tile_ref.shape[0] // SUBLANES
    return jnp.concatenate([tile_ref[pl.ds(j, n, stride=SUBLANES), :] for j in range(SUBLANES)], axis=-1)


def _tile_rows(ref, row, n=1):
    return ref.at[pl.ds(pl.multiple_of(row * SUBLANES, SUBLANES), n * SUBLANES), :]


def _row_copy(src, src_row, dst, dst_row, sem):
    return pltpu.make_async_copy(_tile_rows(src, src_row), _tile_rows(dst, dst_row), sem)


DISPATCH_CH = 32
DISPATCH_SLOTS = 2 * ROUTE_TM + N_EXPERTS * DISPATCH_CH


def _chunk_loops(nch_ref, step, make_copy, action):
    for e in range(N_EXPERTS):
        def body(k, carry, e=e):
            cp = make_copy(step, e, k)
            if action == "start":
                cp.start(priority=e % 2)
            else:
                cp.wait()
            return carry

        lax.fori_loop(0, nch_ref[step * N_EXPERTS + e], body, 0)


def _dispatch_body(pad_ref, nch_ref, loc_ref, dst_ref, lslot_ref, x_ref, sh_ref, sc_ref, xs_hbm,
                   hs_scr, z_scr, row_sems):
    step = pl.program_id(0)
    last = pl.num_programs(0) - 1
    tm = x_ref.shape[0]

    def chunk_copy(buf, st, e, k):
        src = _tile_rows(hs_scr.at[buf], loc_ref[st * N_EXPERTS + e] + k * DISPATCH_CH, DISPATCH_CH)
        dst = _tile_rows(xs_hbm, dst_ref[st * N_EXPERTS + e] + k * DISPATCH_CH, DISPATCH_CH)
        return pltpu.make_async_copy(src, dst, row_sems.at[buf])

    def produce(buf):
        h = (x_ref[...] * (1.0 + sc_ref[...]) + sh_ref[...]).astype(BF16)
        ls = lslot_ref[...]
        srow = lax.broadcasted_iota(jnp.int32, (DISPATCH_SLOTS, tm), 0)
        pick = jnp.where(srow == ls[0:1, :], 1.0, jnp.where(srow == ls[1:2, :], 1.0, 0.0)).astype(BF16)
        _to_tiles(hs_scr.at[buf], jnp.dot(pick, h, preferred_element_type=F32))

    for parity in (0, 1):
        pl.when(step % 2 == parity)(functools.partial(produce, parity))
    for parity in (0, 1):
        pl.when((step % 2 == parity) & (step >= 1))(functools.partial(
            _chunk_loops, nch_ref, step - 1, functools.partial(chunk_copy, 1 - parity), "wait"))
    for parity in (0, 1):
        pl.when(step % 2 == parity)(functools.partial(
            _chunk_loops, nch_ref, step, functools.partial(chunk_copy, parity), "start"))
    for parity in (0, 1):
        pl.when((step % 2 == parity) & (step == last))(functools.partial(
            _chunk_loops, nch_ref, step, functools.partial(chunk_copy, parity), "wait"))
    row_sem = row_sems.at[0]

    @pl.when(step == pl.num_programs(0) - 1)
    def _():
        z_scr[...] = jnp.zeros_like(z_scr)
        for e in range(N_EXPERTS):
            start = pad_ref[e]
            count = pad_ref[N_EXPERTS + e]

            def fill(k, carry, start=start):
                _row_copy(z_scr, 0, xs_hbm, start + k, row_sem).start()
                return carry

            def fill_wait(k, carry):
                _row_copy(z_scr, 0, xs_hbm, 0, row_sem).wait()
                return carry

            lax.fori_loop(0, count, fill, 0)
            lax.fori_loop(0, count, fill_wait, 0)

        zrows = z_scr.shape[0] // SUBLANES
        used_rows = pad_ref[2 * N_EXPERTS]
        n_chunks = (xs_hbm.shape[0] // SUBLANES - used_rows) // zrows

        def chunk_copy(k):
            return pltpu.make_async_copy(z_scr, _tile_rows(xs_hbm, used_rows + k * zrows, zrows), row_sem)

        def fill_chunk(k, carry):
            chunk_copy(k).start()
            return carry

        def fill_chunk_wait(k, carry):
            chunk_copy(k).wait()
            return carry

        lax.fori_loop(0, n_chunks, fill_chunk, 0)
        lax.fori_loop(0, n_chunks, fill_chunk_wait, 0)


def _dispatch(x2, mod, shift_col, lslot, n_chunk, loc0, dest0, pad_info, n_rows, *, seq_len):
    m, d = x2.shape
    tm = ROUTE_TM
    return pl.pallas_call(
        _dispatch_body,
        grid_spec=pltpu.PrefetchScalarGridSpec(
            num_scalar_prefetch=4,
            grid=(m // tm,),
            in_specs=[
                pl.BlockSpec((SUBLANES, tm), lambda i, *_: (0, i)),
                pl.BlockSpec((tm, d), lambda i, *_: (i, 0)),
                pl.BlockSpec((None, 1, d), lambda i, *_: ((i * tm) // seq_len, 0, shift_col)),
                pl.BlockSpec((None, 1, d), lambda i, *_: ((i * tm) // seq_len, 0, shift_col + 1)),
            ],
            out_specs=pl.BlockSpec(memory_space=pl.ANY),
            scratch_shapes=[
                pltpu.VMEM((2, DISPATCH_SLOTS * SUBLANES, LANES), F32),
                pltpu.VMEM((ZERO_ROWS * SUBLANES, LANES), F32),
                pltpu.SemaphoreType.DMA((2,)),
            ],
        ),
        out_shape=jax.ShapeDtypeStruct((n_rows * SUBLANES, LANES), F32),
        compiler_params=_cparams(("arbitrary",)),
        name="moe_dispatch",
    )(pad_info, n_chunk, loc0, dest0, lslot, x2, mod, mod)


TAIL_LEVELS = 2


def _experts_body(te_ref, tbi_ref, tbo_ref, tr_ref, xs_ref, wg_ref, wu_ref, wd_ref, ys_ref, h_scr, acc_scr, *, n_f):
    t = pl.program_id(0)
    f = pl.program_id(1)

    @pl.when(f == 0)
    def _():
        h_scr[...] = _from_tiles(xs_ref).astype(BF16)
        acc_scr[...] = jnp.zeros_like(acc_scr)

    def swiglu_rows(n):
        h = h_scr[0:n, :]
        g = jnp.dot(h, wg_ref[...].astype(BF16), preferred_element_type=F32)
        u = jnp.dot(h, wu_ref[...].astype(BF16), preferred_element_type=F32)
        a = (g * jax.nn.sigmoid(g) * u).astype(BF16)
        acc_scr[0:n, :] += jnp.dot(a, wd_ref[...].astype(BF16), preferred_element_type=F32)

    rows = tr_ref[t]
    tm = h_scr.shape[0]
    bounds = (0,) + tuple(tm >> k for k in range(TAIL_LEVELS, -1, -1))
    for lo, hi in zip(bounds[:-1], bounds[1:]):
        pl.when((rows > lo) & (rows <= hi))(functools.partial(swiglu_rows, hi))

    @pl.when(f == n_f - 1)
    def _():
        _to_tiles(ys_ref, acc_scr[...])


def _experts(xs, tile_expert, tile_in, tile_out, tile_rows, wg, wu, wd, *, tf):
    n_rows = xs.shape[0] // SUBLANES
    d = wg.shape[1]
    tm = MOE_TM
    n_tiles = tile_expert.shape[0]
    ff = wg.shape[2]
    n_f = ff // tf
    ff_blk = lambda f, tr, t: f * jnp.minimum(tr[t], 1)
    return pl.pallas_call(
        functools.partial(_experts_body, n_f=n_f),
        grid_spec=pltpu.PrefetchScalarGridSpec(
            num_scalar_prefetch=4,
            grid=(n_tiles, n_f),
            in_specs=[
                pl.BlockSpec((tm * SUBLANES, LANES), lambda t, f, te, tbi, tbo, tr: (tbi[t], 0)),
                pl.BlockSpec((None, d, tf), lambda t, f, te, tbi, tbo, tr: (te[t], 0, ff_blk(f, tr, t))),
                pl.BlockSpec((None, d, tf), lambda t, f, te, tbi, tbo, tr: (te[t], 0, ff_blk(f, tr, t))),
                pl.BlockSpec((None, tf, d), lambda t, f, te, tbi, tbo, tr: (te[t], ff_blk(f, tr, t), 0)),
            ],
            out_specs=pl.BlockSpec((tm * SUBLANES, LANES), lambda t, f, te, tbi, tbo, tr: (tbo[t], 0)),
            scratch_shapes=[pltpu.VMEM((tm, d), BF16), pltpu.VMEM((tm, d), F32)],
        ),
        out_shape=jax.ShapeDtypeStruct((n_rows * SUBLANES, LANES), F32),
        compiler_params=_cparams(("arbitrary", "arbitrary")),
        name="moe_experts",
    )(tile_expert, tile_in, tile_out, tile_rows, xs, wg, wu, wd)


def _combine_body(nch_ref, loc_ref, dst_ref, ys_hbm, lcol_ref, meta_ref, x_ref, gate_ref, lng_ref, lnb_ref,
                  o_ref, y_scr, row_sems):
    s = pl.program_id(0)
    n_tiles = pl.num_programs(0) - 1
    tm = x_ref.shape[0]

    @pl.when(s == 0)
    def _():
        y_scr[...] = jnp.zeros_like(y_scr)

    def chunk_copy(buf, st, e, k):
        src = _tile_rows(ys_hbm, dst_ref[st * N_EXPERTS + e] + k * DISPATCH_CH, DISPATCH_CH)
        dst = _tile_rows(y_scr.at[buf], loc_ref[st * N_EXPERTS + e] + k * DISPATCH_CH, DISPATCH_CH)
        return pltpu.make_async_copy(src, dst, row_sems.at[buf])

    def finish(buf):
        _chunk_loops(nch_ref, s - 1, functools.partial(chunk_copy, buf), "wait")
        yb = _from_tiles(y_scr.at[buf]).astype(BF16)
        lc = lcol_ref[...]
        meta = meta_ref[...]
        slot = lax.broadcasted_iota(jnp.int32, (tm, DISPATCH_SLOTS), 1)
        pick = jnp.where(slot == lc[:, 0:1], meta[:, META_P1:META_P1 + 1],
                         jnp.where(slot == lc[:, 1:2], meta[:, META_P2:META_P2 + 1], 0.0)).astype(BF16)
        mix = jnp.dot(pick, yb, preferred_element_type=F32)
        r = DEEPNORM_ALPHA * x_ref[...] + gate_ref[...] * mix
        o_ref[...] = _layer_norm(r, lng_ref[...], lnb_ref[...])

    for parity in (0, 1):
        pl.when((s % 2 == parity) & (s < n_tiles))(functools.partial(
            _chunk_loops, nch_ref, s, functools.partial(chunk_copy, parity), "start"))
    for parity in (0, 1):
        pl.when((s % 2 == parity) & (s >= 1))(functools.partial(finish, 1 - parity))


def _combine_ln(ys, n_chunk, loc0, dest0, lcol, meta, x2, mod, gate_col, ln_g, ln_b, *, seq_len):
    m, d = x2.shape
    tm = ROUTE_TM
    prev = lambda i: jnp.maximum(i - 1, 0)
    return pl.pallas_call(
        _combine_body,
        grid_spec=pltpu.PrefetchScalarGridSpec(
            num_scalar_prefetch=3,
            grid=(m // tm + 1,),
            in_specs=[
                pl.BlockSpec(memory_space=pl.ANY),
                pl.BlockSpec((tm, LANES), lambda i, *_: (prev(i), 0)),
                pl.BlockSpec((tm, LANES), lambda i, *_: (prev(i), 0)),
                pl.BlockSpec((tm, d), lambda i, *_: (prev(i), 0)),
                pl.BlockSpec((None, 1, d), lambda i, *_: ((prev(i) * tm) // seq_len, 0, gate_col)),
                pl.BlockSpec((1, d), lambda i, *_: (0, 0)),
                pl.BlockSpec((1, d), lambda i, *_: (0, 0)),
            ],
            out_specs=pl.BlockSpec((tm, d), lambda i, *_: (prev(i), 0)),
            scratch_shapes=[
                pltpu.VMEM((2, DISPATCH_SLOTS * SUBLANES, LANES), F32),
                pltpu.SemaphoreType.DMA((2,)),
            ],
        ),
        out_shape=jax.ShapeDtypeStruct((m, d), F32),
        compiler_params=_cparams(("arbitrary",)),
        name="moe_combine",
    )(n_chunk, loc0, dest0, ys, lcol, meta, x2, mod, ln_g, ln_b)


def _outproj_moe_ln(a, bsrc, w_o, x, mod, ln_g0, ln_b0, w_router, wg, wu, wd, ln_g, ln_b, *, tf):
    b, l, d = x.shape
    assert d == SUBLANES * LANES
    m = b * l
    w_r = jnp.zeros((d, LANES), BF16).at[:, :N_EXPERTS].set(w_router.astype(BF16))
    x2, meta, meta_t, cnt, step_cnt = _outproj_route(a.reshape(m, -1), bsrc.reshape(m, -1), w_o, x.reshape(m, d),
                                                     mod, 2, ln_g0, ln_b0, 3, w_r, seq_len=l)
    n_steps = m // ROUTE_TM

    counts = cnt[0, :N_EXPERTS].astype(jnp.int32)
    n_tile_e = (counts + DISPATCH_CH - 1 + MOE_TM - 1) // MOE_TM
    tile_end = jnp.cumsum(n_tile_e)
    offs = (tile_end - n_tile_e) * MOE_TM
    idx = meta_t[META_I1:META_I2 + 1].astype(jnp.int32)
    rank = meta_t[META_R1:META_R2 + 1].astype(jnp.int32)
    step_n = step_cnt.reshape(n_steps, 8, LANES)[:, 0, :N_EXPERTS].astype(jnp.int32)
    step_before = jnp.cumsum(step_n, axis=0) - step_n
    step_pad = ((step_n + DISPATCH_CH - 1) // DISPATCH_CH) * DISPATCH_CH
    step_loc0 = jnp.cumsum(step_pad, axis=1) - step_pad
    to_local = step_loc0 - step_before
    local = jnp.zeros_like(idx)
    for e in range(N_EXPERTS):
        local = jnp.where(idx == e, jnp.repeat(to_local[:, e], ROUTE_TM)[None, :], local)
    local = local + rank
    lslot = jnp.concatenate([local, jnp.full((SUBLANES - 2, m), -1, jnp.int32)], axis=0)
    lcol = jnp.pad(local.T, ((0, 0), (0, LANES - 2)), constant_values=-1)
    n_chunk = (step_pad // DISPATCH_CH).reshape(-1)
    loc0 = step_loc0.reshape(-1)
    dest0 = (offs[None, :] + step_before).reshape(-1)
    n_tiles = (2 * m) // MOE_TM + N_EXPERTS
    tid = jnp.arange(n_tiles, dtype=jnp.int32)
    tile_valid = (tid < tile_end[-1]).astype(jnp.int32)
    tile_expert = jnp.minimum(jnp.sum((tid[:, None] >= tile_end[None, :]).astype(jnp.int32), axis=1),
                              N_EXPERTS - 1).astype(jnp.int32)
    tile_in = jnp.where(tile_valid > 0, tid, 0).astype(jnp.int32)
    tile_out = tid
    row0 = tid * MOE_TM - jnp.take(offs, tile_expert)
    tile_rows = (jnp.clip(jnp.take(counts, tile_expert) - row0, 0, MOE_TM) * tile_valid).astype(jnp.int32)
    pad_info = jnp.concatenate([offs + counts, n_tile_e * MOE_TM - counts,
                                tile_end[-1:] * MOE_TM]).astype(jnp.int32)
    n_rows = n_tiles * MOE_TM

    xs = _dispatch(x2, mod, 3, lslot, n_chunk, loc0, dest0, pad_info, n_rows, seq_len=l)
    ys = _experts(xs, tile_expert, tile_in, tile_out, tile_rows, wg, wu, wd, tf=tf)
    out = _combine_ln(ys, n_chunk, loc0, dest0, lcol, meta, x2, mod, 5, ln_g, ln_b, seq_len=l)
    return out.reshape(b, l, d)


def _fourier_body(pc_ref, dl_ref, dc_ref, o_ref, t_scr, *, out_scale, row_chunk):
    l = pc_ref.shape[0]
    for g in range(C_GROUPS):
        sl = slice(C_GROUP_DIM * g, C_GROUP_DIM * (g + 1))
        xg = pc_ref[:, sl].astype(F32)
        mu = jnp.mean(xg, axis=-1, keepdims=True)
        dlt = xg - mu
        var = jnp.mean(dlt * dlt, axis=-1, keepdims=True)
        gn = (dlt * lax.rsqrt(var + LN_EPS)).astype(BF16)
        t = jnp.dot(gn, dc_ref[...], preferred_element_type=F32)
        t_scr[0:l, sl] = t[:, 0:C_GROUP_DIM].astype(BF16)
        t_scr[l:2 * l, sl] = t[:, C_GROUP_DIM:2 * C_GROUP_DIM].astype(BF16)
    for r0 in range(0, l, row_chunk):
        acc = jnp.dot(dl_ref[r0:r0 + row_chunk, :], t_scr[...], preferred_element_type=F32)
        o_ref[r0:r0 + row_chunk, :] = (acc * out_scale).astype(BF16)


def _dft_matrices(l, c):
    j = np.arange(l, dtype=np.int64)
    ang_l = (2.0 * np.pi / l) * ((j[:, None] * j[None, :]) % l)
    dl = np.concatenate([np.cos(ang_l), -np.sin(ang_l)], axis=1)
    m = np.arange(c, dtype=np.int64)
    ang_c = (2.0 * np.pi / c) * ((m[:, None] * m[None, :]) % c)
    dc = np.concatenate([np.cos(ang_c), np.sin(ang_c)], axis=1)
    return dl.astype(np.float32), dc.astype(np.float32)


def _fourier_mixer(y, name="fourier"):
    b, l, _ = y.shape
    dl_np, dc_np = _dft_matrices(l, C_GROUP_DIM)
    dl = jnp.asarray(dl_np, dtype=F32).astype(BF16)
    dc = jnp.asarray(dc_np, dtype=F32).astype(BF16)
    out_scale = 1.0 / math.sqrt(l * C_GROUP_DIM)
    return pl.pallas_call(
        functools.partial(_fourier_body, out_scale=out_scale, row_chunk=min(l, 512)),
        grid=(b,),
        in_specs=[
            pl.BlockSpec((None, l, C_WIDTH), lambda bi: (bi, 0, 0)),
            pl.BlockSpec((l, 2 * l), lambda bi: (0, 0), pipeline_mode=pl.Buffered(1)),
            pl.BlockSpec((C_GROUP_DIM, 2 * C_GROUP_DIM), lambda bi: (0, 0)),
        ],
        out_specs=pl.BlockSpec((None, l, C_WIDTH), lambda bi: (bi, 0, 0)),
        out_shape=jax.ShapeDtypeStruct((b, l, C_WIDTH), BF16),
        scratch_shapes=[pltpu.VMEM((2 * l, C_WIDTH), BF16)],
        compiler_params=_cparams(("parallel",)),
        name=name,
    )(y, dl, dc)


NA_HG = 4
NA_GW = NA_HG * NA_DH
NA_WIN = NA_KR * GRID_W


def _natten_body(q_ref, k_ref, v_ref, kc_ref, vc_ref, bias_ref, o_ref, *, rows_per_step, n_rows):
    rb = pl.program_id(2)
    hq = NA_HG * GRID_W
    rid = lax.broadcasted_iota(jnp.int32, (hq, NA_GW), 0)
    cid = lax.broadcasted_iota(jnp.int32, (hq, NA_GW), 1)
    diag = (rid // GRID_W) == (cid // NA_DH)
    cid_o = lax.broadcasted_iota(jnp.int32, (GRID_W, NA_GW), 1)
    kc = kc_ref[...]
    vc = vc_ref[...]
    staged = []
    for j in range(rows_per_step):
        r = rb * rows_per_step + j
        rs = jnp.clip(r - NA_KR // 2, 0, n_rows - NA_KR)
        start = pl.multiple_of(rs * GRID_W, GRID_W)
        tid = jnp.minimum(r, NA_KR // 2) + jnp.maximum(r - (n_rows - NA_KR // 2), 0)
        q_r = q_ref[GRID_W * j:GRID_W * (j + 1), :]
        q4 = jnp.concatenate([q_r] * NA_HG, axis=0)
        qbd = jnp.where(diag, q4, jnp.zeros_like(q4))
        kw = k_ref[pl.ds(start, NA_WIN), :]
        s_loc = lax.dot_general(qbd, kw, NT_DIMS, preferred_element_type=F32) + bias_ref[tid]
        s_ctx = lax.dot_general(qbd, kc, NT_DIMS, preferred_element_type=F32)
        staged.append((start, s_loc, s_ctx))
    for j in range(rows_per_step):
        start, s_loc, s_ctx = staged[j]
        vw = v_ref[pl.ds(start, NA_WIN), :]
        m = jnp.maximum(jnp.max(s_loc, axis=-1, keepdims=True), jnp.max(s_ctx, axis=-1, keepdims=True))
        e_loc = jnp.exp2(s_loc - m)
        e_ctx = jnp.exp2(s_ctx - m)
        tot = jnp.sum(e_loc, axis=-1, keepdims=True) + jnp.sum(e_ctx, axis=-1, keepdims=True)
        o = (jnp.dot(e_loc.astype(BF16), vw, preferred_element_type=F32)
             + jnp.dot(e_ctx.astype(BF16), vc, preferred_element_type=F32))
        o = o * (1.0 / tot)
        out = jnp.zeros((GRID_W, NA_GW), F32)
        for hh in range(NA_HG):
            out = out + jnp.where((cid_o // NA_DH) == hh, o[GRID_W * hh:GRID_W * (hh + 1), :], 0.0)
        o_ref[GRID_W * j:GRID_W * (j + 1), :] = out.astype(BF16)


def _na_bias_table(rpb, n_rows):
    h = rpb.shape[0]
    cols = jnp.arange(GRID_W)
    col_start = jnp.clip(cols - NA_KC // 2, 0, GRID_W - NA_KC)
    col_valid = (cols[None, :] >= col_start[:, None]) & (cols[None, :] < col_start[:, None] + NA_KC)
    dc_idx = jnp.clip(cols[None, :] - cols[:, None] + NA_KC - 1, 0, 2 * NA_KC - 2)
    onehot = (dc_idx[:, :, None] == jnp.arange(2 * NA_KC - 1)).astype(F32)
    rpb_c = jnp.einsum("qkc,hdc->hqdk", onehot, rpb.astype(F32), precision=lax.Precision.HIGHEST)
    full = jnp.where(col_valid[None, :, None, :], rpb_c * LOG2E, NEG_INF)
    full = full.reshape(h // NA_HG, NA_HG * GRID_W, (2 * NA_KR - 1) * GRID_W)
    half = NA_KR // 2
    rep_rows = list(range(half)) + [half] + list(range(n_rows - half + 1, n_rows))
    tabs = []
    for r in rep_rows:
        rs = min(max(r - half, 0), n_rows - NA_KR)
        d0 = rs - r + NA_KR - 1
        tabs.append(full[:, :, GRID_W * d0:GRID_W * d0 + NA_WIN])
    return jnp.stack(tabs, axis=0)


def _natten(y, q_col0, k_col0, v_col0, y_ctx, kc_col0, vc_col0, bias_tab, *, rows_per_step, name="natten"):
    b, l, _ = y.shape
    lc = y_ctx.shape[1]
    n_rows = l // GRID_W
    n_tab = bias_tab.shape[0]
    n_grp = NA_HEADS // NA_HG
    tq = rows_per_step * GRID_W
    qb, kb, vb = q_col0 // NA_GW, k_col0 // NA_GW, v_col0 // NA_GW
    kcb, vcb = kc_col0 // NA_GW, vc_col0 // NA_GW
    return pl.pallas_call(
        functools.partial(_natten_body, rows_per_step=rows_per_step, n_rows=n_rows),
        grid=(b, n_grp, n_rows // rows_per_step),
        in_specs=[
            pl.BlockSpec((None, tq, NA_GW), lambda bi, g, i: (bi, i, qb + g)),
            pl.BlockSpec((None, l, NA_GW), lambda bi, g, i: (bi, 0, kb + g)),
            pl.BlockSpec((None, l, NA_GW), lambda bi, g, i: (bi, 0, vb + g)),
            pl.BlockSpec((None, lc, NA_GW), lambda bi, g, i: (bi, 0, kcb + g)),
            pl.BlockSpec((None, lc, NA_GW), lambda bi, g, i: (bi, 0, vcb + g)),
            pl.BlockSpec((n_tab, None, NA_HG * GRID_W, NA_WIN), lambda bi, g, i: (0, g, 0, 0)),
        ],
        out_specs=pl.BlockSpec((None, tq, NA_GW), lambda bi, g, i: (bi, i, g)),
        out_shape=jax.ShapeDtypeStruct((b, l, D_WIDTH), BF16),
        compiler_params=_cparams(("parallel", "parallel", "arbitrary")),
        name=name,
    )(y, y, y, y_ctx, y_ctx, bias_tab)


def _rope_tables(l):
    t = jnp.arange(l, dtype=jnp.int32)
    row = (t // GRID_W).astype(F32)
    col = (t % GRID_W).astype(F32)
    n_freq = DIFF_DH // 4
    inv_freq = ROPE_THETA ** (-jnp.arange(n_freq, dtype=F32) / n_freq)
    ang = jnp.concatenate([row[:, None] * inv_freq, col[:, None] * inv_freq], axis=-1)
    c, s = jnp.cos(ang), jnp.sin(ang)
    cos = jnp.tile(jnp.concatenate([c, c], axis=-1), (1, 2))
    sin = jnp.tile(jnp.concatenate([-s, s], axis=-1), (1, 2))
    return cos, sin


def kernel(x, c, ctx, c_ctx, w_mod, b_mod, ln_g, ln_b, e_w_in, e_conv, e_lam_q1, e_lam_k1, e_lam_q2, e_lam_k2, e_subln_g, e_w_o, e_ffn_gate, e_ffn_up, e_ffn_down, o_w_in, o_rpb, o_w_o, o_router, o_exp_gate, o_exp_up, o_exp_down):
    b, l, d = x.shape
    lc = ctx.shape[1]
    assert d == D_MODEL and l % 512 == 0 and lc % 256 == 0 and b + 1 <= MOD_ROWS

    cond = jnp.concatenate([c, c_ctx[None, :], jnp.zeros((MOD_ROWS - b - 1, d), F32)], axis=0)
    mods = _adaln(cond, w_mod, b_mod)

    def layer_mods(i):
        lat = mods[i, :b][:, None, :]
        cx = jnp.broadcast_to(mods[i, b][None, None, :], (b, 1, 6 * d))
        return lat, cx

    q_scale_diff = DIFF_DH ** -0.5 * LOG2E
    q_scale_na = NA_DH ** -0.5 * LOG2E

    mod_lat, mod_ctx = layer_mods(0)
    lam_init = 0.8 - 0.6 * math.exp(-0.3 * 0)
    a_end = 3 * A_WIDTH
    w_in = e_w_in[0]
    w_main = w_in[:, :a_end + 2 * DIFF_QK].astype(BF16)
    w_vt = w_in[:, a_end + 2 * DIFF_QK:].T.astype(BF16)
    rope = _rope_tables(l)
    pa_chunks = [(0, 512, "plain"), (512, 512, "plain"), (1024, 512, "plain")]
    y_lat, vt_lat = _proj(x, mod_lat, 0, w_main,
                          pa_chunks + [(a_end, 512, "rope_scale"), (a_end + 512, 512, "rope")],
                          tm=512, wvt=w_vt, rope=rope, q_scale=q_scale_diff, name="even_inproj_lat")
    y_ctx, vt_ctx = _proj(ctx, mod_ctx, 0, w_main,
                          pa_chunks + [(a_end, 512, "scale"), (a_end + 512, 512, "plain")],
                          tm=lc, wvt=w_vt, q_scale=q_scale_diff, name="even_inproj_ctx")

    lam_pack = jnp.zeros((8, LANES), F32)
    lam_pack = lam_pack.at[0, :DIFF_DH].set(e_lam_q1[0]).at[1, :DIFF_DH].set(e_lam_k1[0])
    lam_pack = lam_pack.at[2, :DIFF_DH].set(e_lam_q2[0]).at[3, :DIFF_DH].set(e_lam_k2[0])
    sub_g = e_subln_g[0].reshape(1, DIFF_VDIM)
    o_lat = _diff_attention(y_lat, [(y_ctx, vt_ctx), (y_lat, vt_lat)], lam_pack, sub_g, lam_init,
                            tq=256, q_col0=a_end, k_col0=a_end + DIFF_QK, name="diffattn_lat")
    o_ctx = _diff_attention(y_ctx, [(y_ctx, vt_ctx)], lam_pack, sub_g, lam_init,
                            tq=lc, q_col0=a_end, k_col0=a_end + DIFF_QK, name="diffattn_ctx")

    w_o = e_w_o[0].astype(BF16)
    lng0, lnb0 = ln_g[0, 0][None, :], ln_b[0, 0][None, :]
    lng1, lnb1 = ln_g[0, 1][None, :], ln_b[0, 1][None, :]
    x_lat = _conv_outproj_ln(y_lat, e_conv[0], o_lat, w_o, x, mod_lat, 2, lng0, lnb0, tm=512,
                             name="even_outproj_lat")
    x_ctx = _conv_outproj_ln(y_ctx, e_conv[0], o_ctx, w_o, ctx, mod_ctx, 2, lng0, lnb0, tm=lc,
                             name="even_outproj_ctx")

    wg = e_ffn_gate[0].astype(BF16)
    wu = e_ffn_up[0].astype(BF16)
    wd = e_ffn_down[0].astype(BF16)
    x_lat = _ffn_ln(x_lat, mod_lat, 3, wg, wu, wd, lng1, lnb1, tm=512, n_chunks=2, name="ffn_lat")
    x_ctx = _ffn_ln(x_ctx, mod_ctx, 3, wg, wu, wd, lng1, lnb1, tm=lc, n_chunks=2, name="ffn_ctx")

    mod_lat, mod_ctx = layer_mods(1)
    w_in = o_w_in[0].astype(BF16)
    y_lat = _proj(x_lat, mod_lat, 0, w_in,
                  [(0, 512, "plain"), (512, 512, "scale"), (1024, 512, "plain"), (1536, 512, "plain")],
                  tm=512, q_scale=q_scale_na, name="odd_inproj_lat")
    y_ctx = _proj(x_ctx, mod_ctx, 0, w_in[:, C_WIDTH + D_WIDTH:],
                  [(0, 512, "plain"), (512, 512, "plain")], tm=lc, name="odd_inproj_ctx")
    f_lat = _fourier_mixer(y_lat)
    bias_tab = _na_bias_table(o_rpb[0], l // GRID_W)
    n_lat = _natten(y_lat, C_WIDTH, C_WIDTH + D_WIDTH, C_WIDTH + 2 * D_WIDTH, y_ctx, 0, D_WIDTH,
                    bias_tab, rows_per_step=16)

    w_o = o_w_o[0].astype(BF16)
    lng0, lnb0 = ln_g[1, 0][None, :], ln_b[1, 0][None, :]
    lng1, lnb1 = ln_g[1, 1][None, :], ln_b[1, 1][None, :]
    return _outproj_moe_ln(f_lat, n_lat, w_o, x_lat, mod_lat, lng0, lnb0, o_router[0],
                           o_exp_gate[0], o_exp_up[0], o_exp_down[0], lng1, lnb1, tf=512)
```

```python
import functools
import math

import numpy as np
import jax
import jax.numpy as jnp
from jax import lax
from jax.experimental import pallas as pl
from jax.experimental.pallas import tpu as pltpu

F32 = jnp.float32
BF16 = jnp.bfloat16

D_MODEL = 1024
GRID_W = 64
DEPTH = 2

A_WIDTH = 512
DIFF_HEADS = 4
DIFF_DH = 64
DIFF_VDIM = 128
DIFF_QK = 512
B_WIDTH = 512

C_WIDTH = 512
C_GROUPS = 4
C_GROUP_DIM = 128
NA_HEADS = 8
NA_DH = 64
D_WIDTH = 512
NA_KR = 8
NA_KC = 16

N_EXPERTS = 8

ROPE_THETA = 10000.0
LN_EPS = 1e-5
RMS_EPS = 1e-5
NEG_INF = -1e30
DEEPNORM_ALPHA = (2 * DEPTH) ** 0.25
LOG2E = 1.4426950408889634

LANES = 128
SUBLANES = 8
MOD_ROWS = 32
NT_DIMS = (((1,), (1,)), ((), ()))


def _cparams(sem, vmem_mb=48):
    return pltpu.CompilerParams(dimension_semantics=sem, vmem_limit_bytes=vmem_mb * 1024 * 1024)


def _layer_norm(r, g, b):
    mu = jnp.mean(r, axis=-1, keepdims=True)
    d = r - mu
    var = jnp.mean(d * d, axis=-1, keepdims=True)
    return d * lax.rsqrt(var + LN_EPS) * g + b


def _adaln_body(c_ref, w_ref, b_ref, o_ref):
    cnd = c_ref[...]
    s = (cnd * jax.nn.sigmoid(cnd)).astype(BF16)
    o_ref[...] = jnp.dot(s, w_ref[...].astype(BF16), preferred_element_type=F32) + b_ref[...]


def _adaln(cond, w_mod, b_mod):
    depth, d, n = w_mod.shape
    tn = 1536
    return pl.pallas_call(
        _adaln_body,
        grid=(depth, n // tn),
        in_specs=[
            pl.BlockSpec((MOD_ROWS, d), lambda l, j: (0, 0)),
            pl.BlockSpec((None, d, tn), lambda l, j: (l, 0, j)),
            pl.BlockSpec((None, 1, tn), lambda l, j: (l, 0, j)),
        ],
        out_specs=pl.BlockSpec((None, MOD_ROWS, tn), lambda l, j: (l, 0, j)),
        out_shape=jax.ShapeDtypeStruct((depth, MOD_ROWS, n), F32),
        compiler_params=_cparams(("parallel", "parallel")),
        name="adaln",
    )(cond, w_mod, b_mod.reshape(depth, 1, n))


def _proj_body(*refs, chunks, with_vt, with_rope, q_scale):
    x_ref, sh_ref, sc_ref, w_ref = refs[:4]
    i = 4
    if with_vt:
        wvt_ref = refs[i]
        i += 1
    if with_rope:
        cos_ref, sin_ref = refs[i], refs[i + 1]
        i += 2
    y_ref = refs[i]
    vt_ref = refs[i + 1] if with_vt else None

    h = (x_ref[...] * (1.0 + sc_ref[...]) + sh_ref[...]).astype(BF16)
    tm = h.shape[0]
    if with_rope:
        cos = cos_ref[...]
        sin = sin_ref[...]
        lane = lax.broadcasted_iota(jnp.int32, (tm, LANES), 1)
        low_half = (lane % 64) < 32
    for (c0, width, kind) in chunks:
        acc = jnp.dot(h, w_ref[:, c0:c0 + width], preferred_element_type=F32)
        if kind in ("rope", "rope_scale"):
            for j in range(width // LANES):
                a = acc[:, LANES * j:LANES * (j + 1)]
                rot = jnp.where(low_half, pltpu.roll(a, 96, 1), pltpu.roll(a, 32, 1))
                r = a * cos + rot * sin
                if kind == "rope_scale":
                    r = r * q_scale
                y_ref[:, c0 + LANES * j:c0 + LANES * (j + 1)] = r.astype(BF16)
        elif kind == "scale":
            y_ref[:, c0:c0 + width] = (acc * q_scale).astype(BF16)
        else:
            y_ref[:, c0:c0 + width] = acc.astype(BF16)
    if with_vt:
        vt = lax.dot_general(wvt_ref[...], h, NT_DIMS, preferred_element_type=F32)
        vt_ref[...] = vt.astype(BF16)


def _proj(x, mod, shift_col, w, chunks, *, tm, wvt=None, rope=None, q_scale=1.0, name="proj"):
    b, l, d = x.shape
    n = w.shape[1]
    with_vt = wvt is not None
    with_rope = rope is not None
    in_specs = [
        pl.BlockSpec((None, tm, d), lambda bi, i: (bi, i, 0)),
        pl.BlockSpec((None, 1, d), lambda bi, i: (bi, 0, shift_col)),
        pl.BlockSpec((None, 1, d), lambda bi, i: (bi, 0, shift_col + 1)),
        pl.BlockSpec((d, n), lambda bi, i: (0, 0)),
    ]
    args = [x, mod, mod, w]
    if with_vt:
        nv = wvt.shape[0]
        in_specs.append(pl.BlockSpec((nv, d), lambda bi, i: (0, 0)))
        args.append(wvt)
    if with_rope:
        in_specs += [pl.BlockSpec((tm, LANES), lambda bi, i: (i, 0))] * 2
        args += [rope[0], rope[1]]
    out_specs = [pl.BlockSpec((None, tm, n), lambda bi, i: (bi, i, 0))]
    out_shape = [jax.ShapeDtypeStruct((b, l, n), BF16)]
    if with_vt:
        out_specs.append(pl.BlockSpec((None, nv, tm), lambda bi, i: (bi, 0, i)))
        out_shape.append(jax.ShapeDtypeStruct((b, nv, l), BF16))
    res = pl.pallas_call(
        functools.partial(_proj_body, chunks=tuple(chunks), with_vt=with_vt, with_rope=with_rope,
                          q_scale=q_scale),
        grid=(b, l // tm),
        in_specs=in_specs,
        out_specs=out_specs,
        out_shape=out_shape,
        compiler_params=_cparams(("parallel", "parallel")),
        name=name,
    )(*args)
    return res if with_vt else res[0]


DIFF_HP = 4


def _diffattn_body(*refs, n_seg, lam_init):
    q_ref, lam_ref, g_ref = refs[:3]
    k_refs = [refs[3 + 2 * s] for s in range(n_seg)]
    vt_refs = [refs[4 + 2 * s] for s in range(n_seg)]
    o_ref = refs[3 + 2 * n_seg]

    tq = q_ref.shape[0]
    lane = lax.broadcasted_iota(jnp.int32, (tq, LANES), 1)
    lp = lam_ref[...]
    lam = (jnp.exp(jnp.sum(lp[0:1] * lp[1:2], axis=1, keepdims=True))
           - jnp.exp(jnp.sum(lp[2:3] * lp[3:4], axis=1, keepdims=True)) + lam_init)

    scores = []
    for h in range(DIFF_HP):
        cols = slice(LANES * h, LANES * (h + 1))
        q = q_ref[:, cols]
        zero = jnp.zeros_like(q)
        per_comp = []
        for qm in (jnp.where(lane < DIFF_DH, q, zero), jnp.where(lane >= DIFF_DH, q, zero)):
            per_comp.append([lax.dot_general(k_ref[:, cols], qm, NT_DIMS, preferred_element_type=F32)
                             for k_ref in k_refs])
        scores.append(per_comp)

    def unnormalised(s, rows):
        m = functools.reduce(jnp.maximum, [jnp.max(x, axis=0, keepdims=True) for x in s])
        tot = None
        acc = None
        for x, vt_ref in zip(s, vt_refs):
            e = jnp.exp2(x - m)
            t = jnp.sum(e, axis=0, keepdims=True)
            pv = jnp.dot(vt_ref[rows, :], e.astype(BF16), preferred_element_type=F32)
            tot = t if tot is None else tot + t
            acc = pv if acc is None else acc + pv
        return acc, tot

    for h in range(DIFF_HP):
        cols = slice(LANES * h, LANES * (h + 1))
        rows = slice(DIFF_VDIM * h, DIFF_VDIM * (h + 1))
        acc1, l1 = unnormalised(scores[h][0], rows)
        acc2, l2 = unnormalised(scores[h][1], rows)
        o_t = acc1 * (1.0 / l1) - acc2 * (lam / l2)
        o = o_t.T
        ms = jnp.mean(o * o, axis=-1, keepdims=True)
        o_ref[:, cols] = (o * lax.rsqrt(ms + RMS_EPS) * g_ref[...] * (1.0 - lam_init)).astype(BF16)


def _diff_attention(yq, segs, lam_pack, sub_g, lam_init, *, tq, q_col0, k_col0, name):
    b, lq, _ = yq.shape
    width = DIFF_HP * LANES
    qb0 = q_col0 // width
    kb0 = k_col0 // width
    in_specs = [
        pl.BlockSpec((None, tq, width), lambda bi, h, i: (bi, i, qb0 + h)),
        pl.BlockSpec((8, LANES), lambda bi, h, i: (0, 0)),
        pl.BlockSpec((1, LANES), lambda bi, h, i: (0, 0)),
    ]
    args = [yq, lam_pack, sub_g]
    for (yk, vt) in segs:
        lk = yk.shape[1]
        in_specs.append(pl.BlockSpec((None, lk, width), lambda bi, h, i: (bi, 0, kb0 + h)))
        in_specs.append(pl.BlockSpec((None, DIFF_HP * DIFF_VDIM, lk), lambda bi, h, i: (bi, h, 0)))
        args += [yk, vt]
    return pl.pallas_call(
        functools.partial(_diffattn_body, n_seg=len(segs), lam_init=lam_init),
        grid=(b, DIFF_HEADS // DIFF_HP, lq // tq),
        in_specs=in_specs,
        out_specs=pl.BlockSpec((None, tq, width), lambda bi, h, i: (bi, i, h)),
        out_shape=jax.ShapeDtypeStruct((b, lq, B_WIDTH), BF16),
        compiler_params=_cparams(("parallel", "parallel", "arbitrary")),
        name=name,
    )(*args)


HALO_ROWS = 16


def _conv_outproj_body(bg_ref, cg_ref, val_ref, cgp_ref, valp_ref, cgn_ref, valn_ref, wc_ref,
                       o_att_ref, w_ref, x_ref, gate_ref, lng_ref, lnb_ref, o_ref):
    i = pl.program_id(1)
    tm = x_ref.shape[0]
    half = o_att_ref.shape[1]
    u = cg_ref[...].astype(F32) * val_ref[...].astype(F32)
    u_before = cgp_ref[...].astype(F32) * valp_ref[...].astype(F32)
    u_after = cgn_ref[...].astype(F32) * valn_ref[...].astype(F32)
    first = jnp.where(i == 0, 0.0, u_before[HALO_ROWS - 1:HALO_ROWS, :])
    last = jnp.where(i == pl.num_programs(1) - 1, 0.0, u_after[0:1, :])
    row = lax.broadcasted_iota(jnp.int32, (tm, half), 0)
    u_prev = jnp.where(row == 0, first, pltpu.roll(u, 1, 0))
    u_next = jnp.where(row == tm - 1, last, pltpu.roll(u, tm - 1, 0))
    wc = wc_ref[...]
    conv = bg_ref[...].astype(F32) * (u_prev * wc[0:1] + u * wc[1:2] + u_next * wc[2:3])
    y = (jnp.dot(conv.astype(BF16), w_ref[0:half, :], preferred_element_type=F32)
         + jnp.dot(o_att_ref[...], w_ref[half:2 * half, :], preferred_element_type=F32))
    r = DEEPNORM_ALPHA * x_ref[...] + gate_ref[...] * y
    o_ref[...] = _layer_norm(r, lng_ref[...], lnb_ref[...])


def _conv_outproj_ln(y, w_conv, o_att, w_o, x, mod, gate_col, ln_g, ln_b, *, tm, name):
    b, l, d = x.shape
    half = w_o.shape[0] // 2
    per = tm // HALO_ROWS
    n_halo = l // HALO_ROWS
    main = lambda col: pl.BlockSpec((None, tm, half), lambda bi, i: (bi, i, col))
    before = lambda col: pl.BlockSpec((None, HALO_ROWS, half),
                                      lambda bi, i: (bi, jnp.maximum(i * per - 1, 0), col))
    after = lambda col: pl.BlockSpec((None, HALO_ROWS, half),
                                     lambda bi, i: (bi, jnp.minimum((i + 1) * per, n_halo - 1), col))
    return pl.pallas_call(
        _conv_outproj_body,
        grid=(b, l // tm),
        in_specs=[
            main(0), main(1), main(2), before(1), before(2), after(1), after(2),
            pl.BlockSpec((3, half), lambda bi, i: (0, 0)),
            pl.BlockSpec((None, tm, half), lambda bi, i: (bi, i, 0)),
            pl.BlockSpec((2 * half, d), lambda bi, i: (0, 0)),
            pl.BlockSpec((None, tm, d), lambda bi, i: (bi, i, 0)),
            pl.BlockSpec((None, 1, d), lambda bi, i: (bi, 0, gate_col)),
            pl.BlockSpec((1, d), lambda bi, i: (0, 0)),
            pl.BlockSpec((1, d), lambda bi, i: (0, 0)),
        ],
        out_specs=pl.BlockSpec((None, tm, d), lambda bi, i: (bi, i, 0)),
        out_shape=jax.ShapeDtypeStruct((b, l, d), F32),
        compiler_params=_cparams(("parallel", "parallel")),
        name=name,
    )(y, y, y, y, y, y, y, w_conv, o_att, w_o, x, mod, ln_g, ln_b)


def _outproj_rows(a_ref, b_ref, w_ref, x_ref, gate_ref, lng_ref, lnb_ref):
    half = a_ref.shape[1]
    y = (jnp.dot(a_ref[...], w_ref[0:half, :], preferred_element_type=F32)
         + jnp.dot(b_ref[...], w_ref[half:2 * half, :], preferred_element_type=F32))
    r = DEEPNORM_ALPHA * x_ref[...] + gate_ref[...] * y
    return _layer_norm(r, lng_ref[...], lnb_ref[...])


MXU_WIDTH = 256


def _ffn_body(x_ref, sh_ref, sc_ref, gate_ref, wg_ref, wu_ref, wd_ref, lng_ref, lnb_ref, o_ref, *, chunks):
    x = x_ref[...]
    h = (x * (1.0 + sc_ref[...]) + sh_ref[...]).astype(BF16)
    acc = None
    for (c0, c1) in chunks:
        g = jnp.dot(h, wg_ref[:, c0:c1], preferred_element_type=F32)
        u = jnp.dot(h, wu_ref[:, c0:c1], preferred_element_type=F32)
        a = (g * jax.nn.sigmoid(g) * u).astype(BF16)
        part = jnp.dot(a, wd_ref[c0:c1, :], preferred_element_type=F32)
        acc = part if acc is None else acc + part
    r = DEEPNORM_ALPHA * x + gate_ref[...] * acc
    o_ref[...] = _layer_norm(r, lng_ref[...], lnb_ref[...])


def _ffn_ln(x, mod, shift_col, wg, wu, wd, ln_g, ln_b, *, tm, n_chunks, name="ffn"):
    b, l, d = x.shape
    ff = wg.shape[1]
    n_mxu = ff // MXU_WIDTH
    assert ff == n_mxu * MXU_WIDTH
    bounds = [MXU_WIDTH * ((n_mxu * k + n_chunks - 1) // n_chunks) for k in range(n_chunks + 1)]
    chunks = tuple((bounds[k], bounds[k + 1]) for k in range(n_chunks))
    resident = pl.Buffered(1)
    return pl.pallas_call(
        functools.partial(_ffn_body, chunks=chunks),
        grid=(b, l // tm),
        in_specs=[
            pl.BlockSpec((None, tm, d), lambda bi, i: (bi, i, 0)),
            pl.BlockSpec((None, 1, d), lambda bi, i: (bi, 0, shift_col)),
            pl.BlockSpec((None, 1, d), lambda bi, i: (bi, 0, shift_col + 1)),
            pl.BlockSpec((None, 1, d), lambda bi, i: (bi, 0, shift_col + 2)),
            pl.BlockSpec((d, ff), lambda bi, i: (0, 0), pipeline_mode=resident),
            pl.BlockSpec((d, ff), lambda bi, i: (0, 0), pipeline_mode=resident),
            pl.BlockSpec((ff, d), lambda bi, i: (0, 0), pipeline_mode=resident),
            pl.BlockSpec((1, d), lambda bi, i: (0, 0)),
            pl.BlockSpec((1, d), lambda bi, i: (0, 0)),
        ],
        out_specs=pl.BlockSpec((None, tm, d), lambda bi, i: (bi, i, 0)),
        out_shape=jax.ShapeDtypeStruct((b, l, d), F32),
        compiler_params=_cparams(("parallel", "parallel")),
        name=name,
    )(x, mod, mod, mod, wg, wu, wd, ln_g, ln_b)


MOE_TM = 1024
ROUTE_TM = 512
ZERO_ROWS = 256
META_I1, META_I2, META_R1, META_R2, META_P1, META_P2 = range(6)


def _outproj_route_body(a_ref, b_ref, w_ref, x_ref, gate_ref, lng_ref, lnb_ref, sh_ref, sc_ref, wr_ref,
                        o_ref, meta_ref, meta_t_ref, cnt_ref, step_cnt_ref, carry_scr):
    x_new = _outproj_rows(a_ref, b_ref, w_ref, x_ref, gate_ref, lng_ref, lnb_ref)
    o_ref[...] = x_new
    _route_rows(x_new, sh_ref, sc_ref, wr_ref, meta_ref, meta_t_ref, cnt_ref, step_cnt_ref, carry_scr)


def _route_rows(x, sh_ref, sc_ref, wr_ref, meta_ref, meta_t_ref, cnt_ref, step_cnt_ref, carry_scr):
    @pl.when(pl.program_id(0) == 0)
    def _():
        carry_scr[...] = jnp.zeros_like(carry_scr)

    h = (x * (1.0 + sc_ref[...]) + sh_ref[...]).astype(BF16)
    logits = jnp.dot(h, wr_ref[...], preferred_element_type=F32)
    tm = logits.shape[0]
    lane = lax.broadcasted_iota(jnp.int32, logits.shape, 1).astype(F32)
    l1 = jnp.where(lane < N_EXPERTS, logits, -jnp.inf)
    v1 = jnp.max(l1, axis=-1, keepdims=True)
    i1 = jnp.min(jnp.where(l1 == v1, lane, float(LANES)), axis=-1, keepdims=True)
    l2 = jnp.where(lane == i1, -jnp.inf, l1)
    v2 = jnp.max(l2, axis=-1, keepdims=True)
    i2 = jnp.min(jnp.where(l2 == v2, lane, float(LANES)), axis=-1, keepdims=True)
    t = jnp.exp(v2 - v1)
    p1 = 1.0 / (1.0 + t)
    p2 = t / (1.0 + t)

    member = jnp.where(lane == i1, 1.0, jnp.where(lane == i2, 1.0, 0.0))
    rr = lax.broadcasted_iota(jnp.int32, (tm, tm), 0)
    cc = lax.broadcasted_iota(jnp.int32, (tm, tm), 1)
    earlier = jnp.where(cc < rr, 1.0, 0.0).astype(BF16)
    base = carry_scr[0:1, :]
    rank = jnp.dot(earlier, member.astype(BF16), preferred_element_type=F32) + base
    total = base + jnp.sum(member, axis=0, keepdims=True)
    carry_scr[0:1, :] = total
    r1 = jnp.sum(jnp.where(lane == i1, rank, 0.0), axis=-1, keepdims=True)
    r2 = jnp.sum(jnp.where(lane == i2, rank, 0.0), axis=-1, keepdims=True)

    meta = jnp.zeros_like(logits)
    for k, val in ((META_I1, i1), (META_I2, i2), (META_R1, r1), (META_R2, r2), (META_P1, p1), (META_P2, p2)):
        meta = jnp.where(lane == float(k), val, meta)
    meta_ref[...] = meta
    meta_t_ref[...] = meta.T[0:SUBLANES, :]
    cnt_ref[...] = jnp.broadcast_to(total, cnt_ref.shape)
    step_cnt_ref[...] = jnp.broadcast_to(total - base, step_cnt_ref.shape)


def _outproj_route(a2, b2, w_o, x2, mod, gate_col, ln_g, ln_b, shift_col, w_router_pad, *, seq_len):
    m, d = x2.shape
    half = w_o.shape[0] // 2
    tm = ROUTE_TM
    batch = lambda col: pl.BlockSpec((None, 1, d), lambda i: ((i * tm) // seq_len, 0, col))
    return pl.pallas_call(
        _outproj_route_body,
        grid=(m // tm,),
        in_specs=[
            pl.BlockSpec((tm, half), lambda i: (i, 0)),
            pl.BlockSpec((tm, half), lambda i: (i, 0)),
            pl.BlockSpec((2 * half, d), lambda i: (0, 0)),
            pl.BlockSpec((tm, d), lambda i: (i, 0)),
            batch(gate_col),
            pl.BlockSpec((1, d), lambda i: (0, 0)),
            pl.BlockSpec((1, d), lambda i: (0, 0)),
            batch(shift_col),
            batch(shift_col + 1),
            pl.BlockSpec((d, LANES), lambda i: (0, 0)),
        ],
        out_specs=[pl.BlockSpec((tm, d), lambda i: (i, 0)),
                   pl.BlockSpec((tm, LANES), lambda i: (i, 0)),
                   pl.BlockSpec((SUBLANES, tm), lambda i: (0, i)),
                   pl.BlockSpec((8, LANES), lambda i: (0, 0)),
                   pl.BlockSpec((8, LANES), lambda i: (i, 0))],
        out_shape=[jax.ShapeDtypeStruct((m, d), F32),
                   jax.ShapeDtypeStruct((m, LANES), F32), jax.ShapeDtypeStruct((SUBLANES, m), F32),
                   jax.ShapeDtypeStruct((8, LANES), F32),
                   jax.ShapeDtypeStruct((8 * (m // tm), LANES), F32)],
        scratch_shapes=[pltpu.VMEM((8, LANES), F32)],
        compiler_params=_cparams(("arbitrary",)),
        name="odd_outproj_route",
    )(a2, b2, w_o, x2, mod, ln_g, ln_b, mod, mod, w_router_pad)


def _to_tiles(tile_ref, value):
    n = value.shape[0]
    for j in range(SUBLANES):
        tile_ref[pl.ds(j, n, stride=SUBLANES), :] = value[:, LANES * j:LANES * (j + 1)]


def _from_tiles(tile_ref):
    n = tile_ref.shape[0] // SUBLANES
    return jnp.concatenate([tile_ref[pl.ds(j, n, stride=SUBLANES), :] for j in range(SUBLANES)], axis=-1)


def _tile_rows(ref, row, n=1):
    return ref.at[pl.ds(pl.multiple_of(row * SUBLANES, SUBLANES), n * SUBLANES), :]


def _row_copy(src, src_row, dst, dst_row, sem):
    return pltpu.make_async_copy(_tile_rows(src, src_row), _tile_rows(dst, dst_row), sem)


DISPATCH_CH = 32
DISPATCH_SLOTS = 2 * ROUTE_TM + N_EXPERTS * DISPATCH_CH


def _chunk_loops(nch_ref, step, make_copy, action):
    for e in range(N_EXPERTS):
        def body(k, carry, e=e):
            cp = make_copy(step, e, k)
            if action == "start":
                cp.start(priority=e % 2)
            else:
                cp.wait()
            return carry

        lax.fori_loop(0, nch_ref[step * N_EXPERTS + e], body, 0)


def _dispatch_body(pad_ref, nch_ref, loc_ref, dst_ref, lslot_ref, x_ref, sh_ref, sc_ref, xs_hbm,
                   hs_scr, z_scr, row_sems):
    step = pl.program_id(0)
    last = pl.num_programs(0) - 1
    tm = x_ref.shape[0]

    def chunk_copy(buf, st, e, k):
        src = _tile_rows(hs_scr.at[buf], loc_ref[st * N_EXPERTS + e] + k * DISPATCH_CH, DISPATCH_CH)
        dst = _tile_rows(xs_hbm, dst_ref[st * N_EXPERTS + e] + k * DISPATCH_CH, DISPATCH_CH)
        return pltpu.make_async_copy(src, dst, row_sems.at[buf])

    def produce(buf):
        h = (x_ref[...] * (1.0 + sc_ref[...]) + sh_ref[...]).astype(BF16)
        ls = lslot_ref[...]
        srow = lax.broadcasted_iota(jnp.int32, (DISPATCH_SLOTS, tm), 0)
        pick = jnp.where(srow == ls[0:1, :], 1.0, jnp.where(srow == ls[1:2, :], 1.0, 0.0)).astype(BF16)
        _to_tiles(hs_scr.at[buf], jnp.dot(pick, h, preferred_element_type=F32))

    for parity in (0, 1):
        pl.when(step % 2 == parity)(functools.partial(produce, parity))
    for parity in (0, 1):
        pl.when((step % 2 == parity) & (step >= 1))(functools.partial(
            _chunk_loops, nch_ref, step - 1, functools.partial(chunk_copy, 1 - parity), "wait"))
    for parity in (0, 1):
        pl.when(step % 2 == parity)(functools.partial(
            _chunk_loops, nch_ref, step, functools.partial(chunk_copy, parity), "start"))
    for parity in (0, 1):
        pl.when((step % 2 == parity) & (step == last))(functools.partial(
            _chunk_loops, nch_ref, step, functools.partial(chunk_copy, parity), "wait"))
    row_sem = row_sems.at[0]

    @pl.when(step == pl.num_programs(0) - 1)
    def _():
        z_scr[...] = jnp.zeros_like(z_scr)
        for e in range(N_EXPERTS):
            start = pad_ref[e]
            count = pad_ref[N_EXPERTS + e]

            def fill(k, carry, start=start):
                _row_copy(z_scr, 0, xs_hbm, start + k, row_sem).start()
                return carry

            def fill_wait(k, carry):
                _row_copy(z_scr, 0, xs_hbm, 0, row_sem).wait()
                return carry

            lax.fori_loop(0, count, fill, 0)
            lax.fori_loop(0, count, fill_wait, 0)

        zrows = z_scr.shape[0] // SUBLANES
        used_rows = pad_ref[2 * N_EXPERTS]
        n_chunks = (xs_hbm.shape[0] // SUBLANES - used_rows) // zrows

        def chunk_copy(k):
            return pltpu.make_async_copy(z_scr, _tile_rows(xs_hbm, used_rows + k * zrows, zrows), row_sem)

        def fill_chunk(k, carry):
            chunk_copy(k).start()
            return carry

        def fill_chunk_wait(k, carry):
            chunk_copy(k).wait()
            return carry

        lax.fori_loop(0, n_chunks, fill_chunk, 0)
        lax.fori_loop(0, n_chunks, fill_chunk_wait, 0)


def _dispatch(x2, mod, shift_col, lslot, n_chunk, loc0, dest0, pad_info, n_rows, *, seq_len):
    m, d = x2.shape
    tm = ROUTE_TM
    return pl.pallas_call(
        _dispatch_body,
        grid_spec=pltpu.PrefetchScalarGridSpec(
            num_scalar_prefetch=4,
            grid=(m // tm,),
            in_specs=[
                pl.BlockSpec((SUBLANES, tm), lambda i, *_: (0, i)),
                pl.BlockSpec((tm, d), lambda i, *_: (i, 0)),
                pl.BlockSpec((None, 1, d), lambda i, *_: ((i * tm) // seq_len, 0, shift_col)),
                pl.BlockSpec((None, 1, d), lambda i, *_: ((i * tm) // seq_len, 0, shift_col + 1)),
            ],
            out_specs=pl.BlockSpec(memory_space=pl.ANY),
            scratch_shapes=[
                pltpu.VMEM((2, DISPATCH_SLOTS * SUBLANES, LANES), F32),
                pltpu.VMEM((ZERO_ROWS * SUBLANES, LANES), F32),
                pltpu.SemaphoreType.DMA((2,)),
            ],
        ),
        out_shape=jax.ShapeDtypeStruct((n_rows * SUBLANES, LANES), F32),
        compiler_params=_cparams(("arbitrary",)),
        name="moe_dispatch",
    )(pad_info, n_chunk, loc0, dest0, lslot, x2, mod, mod)


TAIL_LEVELS = 2


def _experts_body(te_ref, tbi_ref, tbo_ref, tr_ref, xs_ref, wg_ref, wu_ref, wd_ref, ys_ref, h_scr, acc_scr, *, n_f):
    t = pl.program_id(0)
    f = pl.program_id(1)

    @pl.when(f == 0)
    def _():
        h_scr[...] = _from_tiles(xs_ref).astype(BF16)
        acc_scr[...] = jnp.zeros_like(acc_scr)

    def swiglu_rows(n):
        h = h_scr[0:n, :]
        g = jnp.dot(h, wg_ref[...].astype(BF16), preferred_element_type=F32)
        u = jnp.dot(h, wu_ref[...].astype(BF16), preferred_element_type=F32)
        a = (g * jax.nn.sigmoid(g) * u).astype(BF16)
        acc_scr[0:n, :] += jnp.dot(a, wd_ref[...].astype(BF16), preferred_element_type=F32)

    rows = tr_ref[t]
    tm = h_scr.shape[0]
    bounds = (0,) + tuple(tm >> k for k in range(TAIL_LEVELS, -1, -1))
    for lo, hi in zip(bounds[:-1], bounds[1:]):
        pl.when((rows > lo) & (rows <= hi))(functools.partial(swiglu_rows, hi))

    @pl.when(f == n_f - 1)
    def _():
        _to_tiles(ys_ref, acc_scr[...])


def _experts(xs, tile_expert, tile_in, tile_out, tile_rows, wg, wu, wd, *, tf):
    n_rows = xs.shape[0] // SUBLANES
    d = wg.shape[1]
    tm = MOE_TM
    n_tiles = tile_expert.shape[0]
    ff = wg.shape[2]
    n_f = ff // tf
    ff_blk = lambda f, tr, t: f * jnp.minimum(tr[t], 1)
    return pl.pallas_call(
        functools.partial(_experts_body, n_f=n_f),
        grid_spec=pltpu.PrefetchScalarGridSpec(
            num_scalar_prefetch=4,
            grid=(n_tiles, n_f),
            in_specs=[
                pl.BlockSpec((tm * SUBLANES, LANES), lambda t, f, te, tbi, tbo, tr: (tbi[t], 0)),
                pl.BlockSpec((None, d, tf), lambda t, f, te, tbi, tbo, tr: (te[t], 0, ff_blk(f, tr, t))),
                pl.BlockSpec((None, d, tf), lambda t, f, te, tbi, tbo, tr: (te[t], 0, ff_blk(f, tr, t))),
                pl.BlockSpec((None, tf, d), lambda t, f, te, tbi, tbo, tr: (te[t], ff_blk(f, tr, t), 0)),
            ],
            out_specs=pl.BlockSpec((tm * SUBLANES, LANES), lambda t, f, te, tbi, tbo, tr: (tbo[t], 0)),
            scratch_shapes=[pltpu.VMEM((tm, d), BF16), pltpu.VMEM((tm, d), F32)],
        ),
        out_shape=jax.ShapeDtypeStruct((n_rows * SUBLANES, LANES), F32),
        compiler_params=_cparams(("arbitrary", "arbitrary")),
        name="moe_experts",
    )(tile_expert, tile_in, tile_out, tile_rows, xs, wg, wu, wd)


def _combine_body(nch_ref, loc_ref, dst_ref, ys_hbm, lslot_ref, meta_ref, x_ref, gate_ref, lng_ref, lnb_ref,
                  o_ref, y_scr, row_sems):
    s = pl.program_id(0)
    n_tiles = pl.num_programs(0) - 1
    tm = x_ref.shape[0]

    @pl.when(s == 0)
    def _():
        y_scr[...] = jnp.zeros_like(y_scr)

    def chunk_copy(buf, st, e, k):
        src = _tile_rows(ys_hbm, dst_ref[st * N_EXPERTS + e] + k * DISPATCH_CH, DISPATCH_CH)
        dst = _tile_rows(y_scr.at[buf], loc_ref[st * N_EXPERTS + e] + k * DISPATCH_CH, DISPATCH_CH)
        return pltpu.make_async_copy(src, dst, row_sems.at[buf])

    def finish(buf):
        _chunk_loops(nch_ref, s - 1, functools.partial(chunk_copy, buf), "wait")
        yb = _from_tiles(y_scr.at[buf]).astype(BF16)
        ls = lslot_ref[...]
        lc = jnp.concatenate([ls, jnp.zeros((LANES - SUBLANES, tm), ls.dtype)], axis=0).T
        meta = meta_ref[...]
        slot = lax.broadcasted_iota(jnp.int32, (tm, DISPATCH_SLOTS), 1)
        pick = jnp.where(slot == lc[:, 0:1], meta[:, META_P1:META_P1 + 1],
                         jnp.where(slot == lc[:, 1:2], meta[:, META_P2:META_P2 + 1], 0.0)).astype(BF16)
        mix = jnp.dot(pick, yb, preferred_element_type=F32)
        r = DEEPNORM_ALPHA * x_ref[...] + gate_ref[...] * mix
        o_ref[...] = _layer_norm(r, lng_ref[...], lnb_ref[...])

    for parity in (0, 1):
        pl.when((s % 2 == parity) & (s < n_tiles))(functools.partial(
            _chunk_loops, nch_ref, s, functools.partial(chunk_copy, parity), "start"))
    for parity in (0, 1):
        pl.when((s % 2 == parity) & (s >= 1))(functools.partial(finish, 1 - parity))


def _combine_ln(ys, n_chunk, loc0, dest0, lslot, meta, x2, mod, gate_col, ln_g, ln_b, *, seq_len):
    m, d = x2.shape
    tm = ROUTE_TM
    prev = lambda i: jnp.maximum(i - 1, 0)
    return pl.pallas_call(
        _combine_body,
        grid_spec=pltpu.PrefetchScalarGridSpec(
            num_scalar_prefetch=3,
            grid=(m // tm + 1,),
            in_specs=[
                pl.BlockSpec(memory_space=pl.ANY),
                pl.BlockSpec((SUBLANES, tm), lambda i, *_: (0, prev(i))),
                pl.BlockSpec((tm, LANES), lambda i, *_: (prev(i), 0)),
                pl.BlockSpec((tm, d), lambda i, *_: (prev(i), 0)),
                pl.BlockSpec((None, 1, d), lambda i, *_: ((prev(i) * tm) // seq_len, 0, gate_col)),
                pl.BlockSpec((1, d), lambda i, *_: (0, 0)),
                pl.BlockSpec((1, d), lambda i, *_: (0, 0)),
            ],
            out_specs=pl.BlockSpec((tm, d), lambda i, *_: (prev(i), 0)),
            scratch_shapes=[
                pltpu.VMEM((2, DISPATCH_SLOTS * SUBLANES, LANES), F32),
                pltpu.SemaphoreType.DMA((2,)),
            ],
        ),
        out_shape=jax.ShapeDtypeStruct((m, d), F32),
        compiler_params=_cparams(("arbitrary",)),
        name="moe_combine",
    )(n_chunk, loc0, dest0, ys, lslot, meta, x2, mod, ln_g, ln_b)


def _outproj_moe_ln(a, bsrc, w_o, x, mod, ln_g0, ln_b0, w_router, wg, wu, wd, ln_g, ln_b, *, tf):
    b, l, d = x.shape
    assert d == SUBLANES * LANES
    m = b * l
    w_r = jnp.zeros((d, LANES), BF16).at[:, :N_EXPERTS].set(w_router.astype(BF16))
    x2, meta, meta_t, cnt, step_cnt = _outproj_route(a.reshape(m, -1), bsrc.reshape(m, -1), w_o, x.reshape(m, d),
                                                     mod, 2, ln_g0, ln_b0, 3, w_r, seq_len=l)
    n_steps = m // ROUTE_TM

    counts = cnt[0, :N_EXPERTS].astype(jnp.int32)
    n_tile_e = (counts + DISPATCH_CH - 1 + MOE_TM - 1) // MOE_TM
    tile_end = jnp.cumsum(n_tile_e)
    offs = (tile_end - n_tile_e) * MOE_TM
    idx = meta_t[META_I1:META_I2 + 1].astype(jnp.int32)
    rank = meta_t[META_R1:META_R2 + 1].astype(jnp.int32)
    step_n = step_cnt.reshape(n_steps, 8, LANES)[:, 0, :N_EXPERTS].astype(jnp.int32)
    step_before = jnp.cumsum(step_n, axis=0) - step_n
    step_pad = ((step_n + DISPATCH_CH - 1) // DISPATCH_CH) * DISPATCH_CH
    step_loc0 = jnp.cumsum(step_pad, axis=1) - step_pad
    to_local = step_loc0 - step_before
    idx3 = idx.reshape(2, n_steps, ROUTE_TM)
    local = jnp.zeros_like(idx3)
    for e in range(N_EXPERTS):
        local = jnp.where(idx3 == e, to_local[None, :, e, None], local)
    local = local.reshape(2, m) + rank
    lslot = jnp.concatenate([local, jnp.full((SUBLANES - 2, m), -1, jnp.int32)], axis=0)
    n_chunk = (step_pad // DISPATCH_CH).reshape(-1)
    loc0 = step_loc0.reshape(-1)
    dest0 = (offs[None, :] + step_before).reshape(-1)
    n_tiles = (2 * m) // MOE_TM + N_EXPERTS
    tid = jnp.arange(n_tiles, dtype=jnp.int32)
    tile_valid = (tid < tile_end[-1]).astype(jnp.int32)
    tile_expert = jnp.minimum(jnp.sum((tid[:, None] >= tile_end[None, :]).astype(jnp.int32), axis=1),
                              N_EXPERTS - 1).astype(jnp.int32)
    tile_in = jnp.where(tile_valid > 0, tid, 0).astype(jnp.int32)
    tile_out = tid
    row0 = tid * MOE_TM - jnp.take(offs, tile_expert)
    tile_rows = (jnp.clip(jnp.take(counts, tile_expert) - row0, 0, MOE_TM) * tile_valid).astype(jnp.int32)
    pad_info = jnp.concatenate([offs + counts, n_tile_e * MOE_TM - counts,
                                tile_end[-1:] * MOE_TM]).astype(jnp.int32)
    n_rows = n_tiles * MOE_TM

    xs = _dispatch(x2, mod, 3, lslot, n_chunk, loc0, dest0, pad_info, n_rows, seq_len=l)
    ys = _experts(xs, tile_expert, tile_in, tile_out, tile_rows, wg, wu, wd, tf=tf)
    out = _combine_ln(ys, n_chunk, loc0, dest0, lslot, meta, x2, mod, 5, ln_g, ln_b, seq_len=l)
    return out.reshape(b, l, d)


def _fourier_body(pc_ref, dl_ref, dc_ref, o_ref, t_scr, *, out_scale, row_chunk):
    l = pc_ref.shape[0]
    for g in range(C_GROUPS):
        sl = slice(C_GROUP_DIM * g, C_GROUP_DIM * (g + 1))
        xg = pc_ref[:, sl].astype(F32)
        mu = jnp.mean(xg, axis=-1, keepdims=True)
        dlt = xg - mu
        var = jnp.mean(dlt * dlt, axis=-1, keepdims=True)
        gn = (dlt * lax.rsqrt(var + LN_EPS)).astype(BF16)
        t = jnp.dot(gn, dc_ref[...], preferred_element_type=F32)
        t_scr[0:l, sl] = t[:, 0:C_GROUP_DIM].astype(BF16)
        t_scr[l:2 * l, sl] = t[:, C_GROUP_DIM:2 * C_GROUP_DIM].astype(BF16)
    for r0 in range(0, l, row_chunk):
        acc = jnp.dot(dl_ref[r0:r0 + row_chunk, :], t_scr[...], preferred_element_type=F32)
        o_ref[r0:r0 + row_chunk, :] = (acc * out_scale).astype(BF16)


def _dft_matrices(l, c):
    j = np.arange(l, dtype=np.int64)
    ang_l = (2.0 * np.pi / l) * ((j[:, None] * j[None, :]) % l)
    dl = np.concatenate([np.cos(ang_l), -np.sin(ang_l)], axis=1)
    m = np.arange(c, dtype=np.int64)
    ang_c = (2.0 * np.pi / c) * ((m[:, None] * m[None, :]) % c)
    dc = np.concatenate([np.cos(ang_c), np.sin(ang_c)], axis=1)
    return dl.astype(np.float32), dc.astype(np.float32)


def _fourier_mixer(y, name="fourier"):
    b, l, _ = y.shape
    dl_np, dc_np = _dft_matrices(l, C_GROUP_DIM)
    dl = jnp.asarray(dl_np, dtype=F32).astype(BF16)
    dc = jnp.asarray(dc_np, dtype=F32).astype(BF16)
    out_scale = 1.0 / math.sqrt(l * C_GROUP_DIM)
    return pl.pallas_call(
        functools.partial(_fourier_body, out_scale=out_scale, row_chunk=min(l, 512)),
        grid=(b,),
        in_specs=[
            pl.BlockSpec((None, l, C_WIDTH), lambda bi: (bi, 0, 0)),
            pl.BlockSpec((l, 2 * l), lambda bi: (0, 0), pipeline_mode=pl.Buffered(1)),
            pl.BlockSpec((C_GROUP_DIM, 2 * C_GROUP_DIM), lambda bi: (0, 0)),
        ],
        out_specs=pl.BlockSpec((None, l, C_WIDTH), lambda bi: (bi, 0, 0)),
        out_shape=jax.ShapeDtypeStruct((b, l, C_WIDTH), BF16),
        scratch_shapes=[pltpu.VMEM((2 * l, C_WIDTH), BF16)],
        compiler_params=_cparams(("parallel",)),
        name=name,
    )(y, dl, dc)


NA_HG = 4
NA_GW = NA_HG * NA_DH
NA_WIN = NA_KR * GRID_W


def _natten_body(q_ref, k_ref, v_ref, kc_ref, vc_ref, bias_ref, o_ref, *, rows_per_step, n_rows):
    rb = pl.program_id(2)
    hq = NA_HG * GRID_W
    rid = lax.broadcasted_iota(jnp.int32, (hq, NA_GW), 0)
    cid = lax.broadcasted_iota(jnp.int32, (hq, NA_GW), 1)
    diag = (rid // GRID_W) == (cid // NA_DH)
    cid_o = lax.broadcasted_iota(jnp.int32, (GRID_W, NA_GW), 1)
    kc = kc_ref[...]
    vc = vc_ref[...]
    staged = []
    for j in range(rows_per_step):
        r = rb * rows_per_step + j
        rs = jnp.clip(r - NA_KR // 2, 0, n_rows - NA_KR)
        start = pl.multiple_of(rs * GRID_W, GRID_W)
        tid = jnp.minimum(r, NA_KR // 2) + jnp.maximum(r - (n_rows - NA_KR // 2), 0)
        q_r = q_ref[GRID_W * j:GRID_W * (j + 1), :]
        q4 = jnp.concatenate([q_r] * NA_HG, axis=0)
        qbd = jnp.where(diag, q4, jnp.zeros_like(q4))
        kw = k_ref[pl.ds(start, NA_WIN), :]
        s_loc = lax.dot_general(qbd, kw, NT_DIMS, preferred_element_type=F32) + bias_ref[tid]
        s_ctx = lax.dot_general(qbd, kc, NT_DIMS, preferred_element_type=F32)
        staged.append((start, s_loc, s_ctx))
    for j in range(rows_per_step):
        start, s_loc, s_ctx = staged[j]
        vw = v_ref[pl.ds(start, NA_WIN), :]
        m = jnp.maximum(jnp.max(s_loc, axis=-1, keepdims=True), jnp.max(s_ctx, axis=-1, keepdims=True))
        e_loc = jnp.exp2(s_loc - m)
        e_ctx = jnp.exp2(s_ctx - m)
        tot = jnp.sum(e_loc, axis=-1, keepdims=True) + jnp.sum(e_ctx, axis=-1, keepdims=True)
        o = (jnp.dot(e_loc.astype(BF16), vw, preferred_element_type=F32)
             + jnp.dot(e_ctx.astype(BF16), vc, preferred_element_type=F32))
        o = o * (1.0 / tot)
        out = jnp.zeros((GRID_W, NA_GW), F32)
        for hh in range(NA_HG):
            out = out + jnp.where((cid_o // NA_DH) == hh, o[GRID_W * hh:GRID_W * (hh + 1), :], 0.0)
        o_ref[GRID_W * j:GRID_W * (j + 1), :] = out.astype(BF16)


def _na_bias_table(rpb, n_rows):
    h = rpb.shape[0]
    cols = jnp.arange(GRID_W)
    col_start = jnp.clip(cols - NA_KC // 2, 0, GRID_W - NA_KC)
    col_valid = (cols[None, :] >= col_start[:, None]) & (cols[None, :] < col_start[:, None] + NA_KC)
    dc_idx = jnp.clip(cols[None, :] - cols[:, None] + NA_KC - 1, 0, 2 * NA_KC - 2)
    onehot = (dc_idx[:, :, None] == jnp.arange(2 * NA_KC - 1)).astype(F32)
    rpb_c = jnp.einsum("qkc,hdc->hqdk", onehot, rpb.astype(F32), precision=lax.Precision.HIGHEST)
    full = jnp.where(col_valid[None, :, None, :], rpb_c * LOG2E, NEG_INF)
    full = full.reshape(h // NA_HG, NA_HG * GRID_W, (2 * NA_KR - 1) * GRID_W)
    half = NA_KR // 2
    rep_rows = list(range(half)) + [half] + list(range(n_rows - half + 1, n_rows))
    tabs = []
    for r in rep_rows:
        rs = min(max(r - half, 0), n_rows - NA_KR)
        d0 = rs - r + NA_KR - 1
        tabs.append(full[:, :, GRID_W * d0:GRID_W * d0 + NA_WIN])
    return jnp.stack(tabs, axis=0)


def _natten(y, q_col0, k_col0, v_col0, y_ctx, kc_col0, vc_col0, bias_tab, *, rows_per_step, name="natten"):
    b, l, _ = y.shape
    lc = y_ctx.shape[1]
    n_rows = l // GRID_W
    n_tab = bias_tab.shape[0]
    n_grp = NA_HEADS // NA_HG
    tq = rows_per_step * GRID_W
    qb, kb, vb = q_col0 // NA_GW, k_col0 // NA_GW, v_col0 // NA_GW
    kcb, vcb = kc_col0 // NA_GW, vc_col0 // NA_GW
    return pl.pallas_call(
        functools.partial(_natten_body, rows_per_step=rows_per_step, n_rows=n_rows),
        grid=(b, n_grp, n_rows // rows_per_step),
        in_specs=[
            pl.BlockSpec((None, tq, NA_GW), lambda bi, g, i: (bi, i, qb + g)),
            pl.BlockSpec((None, l, NA_GW), lambda bi, g, i: (bi, 0, kb + g)),
            pl.BlockSpec((None, l, NA_GW), lambda bi, g, i: (bi, 0, vb + g)),
            pl.BlockSpec((None, lc, NA_GW), lambda bi, g, i: (bi, 0, kcb + g)),
            pl.BlockSpec((None, lc, NA_GW), lambda bi, g, i: (bi, 0, vcb + g)),
            pl.BlockSpec((n_tab, None, NA_HG * GRID_W, NA_WIN), lambda bi, g, i: (0, g, 0, 0)),
        ],
        out_specs=pl.BlockSpec((None, tq, NA_GW), lambda bi, g, i: (bi, i, g)),
        out_shape=jax.ShapeDtypeStruct((b, l, D_WIDTH), BF16),
        compiler_params=_cparams(("parallel", "parallel", "arbitrary")),
        name=name,
    )(y, y, y, y_ctx, y_ctx, bias_tab)


def _rope_tables(l):
    t = jnp.arange(l, dtype=jnp.int32)
    row = (t // GRID_W).astype(F32)
    col = (t % GRID_W).astype(F32)
    n_freq = DIFF_DH // 4
    inv_freq = ROPE_THETA ** (-jnp.arange(n_freq, dtype=F32) / n_freq)
    ang = jnp.concatenate([row[:, None] * inv_freq, col[:, None] * inv_freq], axis=-1)
    c, s = jnp.cos(ang), jnp.sin(ang)
    cos = jnp.tile(jnp.concatenate([c, c], axis=-1), (1, 2))
    sin = jnp.tile(jnp.concatenate([-s, s], axis=-1), (1, 2))
    return cos, sin


def kernel(x, c, ctx, c_ctx, w_mod, b_mod, ln_g, ln_b, e_w_in, e_conv, e_lam_q1, e_lam_k1, e_lam_q2, e_lam_k2, e_subln_g, e_w_o, e_ffn_gate, e_ffn_up, e_ffn_down, o_w_in, o_rpb, o_w_o, o_router, o_exp_gate, o_exp_up, o_exp_down):
    b, l, d = x.shape
    lc = ctx.shape[1]
    assert d == D_MODEL and l % 512 == 0 and lc % 256 == 0 and b + 1 <= MOD_ROWS

    cond = jnp.concatenate([c, c_ctx[None, :], jnp.zeros((MOD_ROWS - b - 1, d), F32)], axis=0)
    mods = _adaln(cond, w_mod, b_mod)

    def layer_mods(i):
        lat = mods[i, :b][:, None, :]
        cx = jnp.broadcast_to(mods[i, b][None, None, :], (b, 1, 6 * d))
        return lat, cx

    q_scale_diff = DIFF_DH ** -0.5 * LOG2E
    q_scale_na = NA_DH ** -0.5 * LOG2E

    mod_lat, mod_ctx = layer_mods(0)
    lam_init = 0.8 - 0.6 * math.exp(-0.3 * 0)
    a_end = 3 * A_WIDTH
    w_in = e_w_in[0]
    w_main = w_in[:, :a_end + 2 * DIFF_QK].astype(BF16)
    w_vt = w_in[:, a_end + 2 * DIFF_QK:].T.astype(BF16)
    rope = _rope_tables(l)
    pa_chunks = [(0, 512, "plain"), (512, 512, "plain"), (1024, 512, "plain")]
    y_lat, vt_lat = _proj(x, mod_lat, 0, w_main,
                          pa_chunks + [(a_end, 512, "rope_scale"), (a_end + 512, 512, "rope")],
                          tm=512, wvt=w_vt, rope=rope, q_scale=q_scale_diff, name="even_inproj_lat")
    y_ctx, vt_ctx = _proj(ctx, mod_ctx, 0, w_main,
                          pa_chunks + [(a_end, 512, "scale"), (a_end + 512, 512, "plain")],
                          tm=lc, wvt=w_vt, q_scale=q_scale_diff, name="even_inproj_ctx")

    lam_pack = jnp.zeros((8, LANES), F32)
    lam_pack = lam_pack.at[0, :DIFF_DH].set(e_lam_q1[0]).at[1, :DIFF_DH].set(e_lam_k1[0])
    lam_pack = lam_pack.at[2, :DIFF_DH].set(e_lam_q2[0]).at[3, :DIFF_DH].set(e_lam_k2[0])
    sub_g = e_subln_g[0].reshape(1, DIFF_VDIM)
    o_lat = _diff_attention(y_lat, [(y_ctx, vt_ctx), (y_lat, vt_lat)], lam_pack, sub_g, lam_init,
                            tq=256, q_col0=a_end, k_col0=a_end + DIFF_QK, name="diffattn_lat")
    o_ctx = _diff_attention(y_ctx, [(y_ctx, vt_ctx)], lam_pack, sub_g, lam_init,
                            tq=lc, q_col0=a_end, k_col0=a_end + DIFF_QK, name="diffattn_ctx")

    w_o = e_w_o[0].astype(BF16)
    lng0, lnb0 = ln_g[0, 0][None, :], ln_b[0, 0][None, :]
    lng1, lnb1 = ln_g[0, 1][None, :], ln_b[0, 1][None, :]
    x_lat = _conv_outproj_ln(y_lat, e_conv[0], o_lat, w_o, x, mod_lat, 2, lng0, lnb0, tm=512,
                             name="even_outproj_lat")
    x_ctx = _conv_outproj_ln(y_ctx, e_conv[0], o_ctx, w_o, ctx, mod_ctx, 2, lng0, lnb0, tm=lc,
                             name="even_outproj_ctx")

    wg = e_ffn_gate[0].astype(BF16)
    wu = e_ffn_up[0].astype(BF16)
    wd = e_ffn_down[0].astype(BF16)
    x_lat = _ffn_ln(x_lat, mod_lat, 3, wg, wu, wd, lng1, lnb1, tm=512, n_chunks=2, name="ffn_lat")
    x_ctx = _ffn_ln(x_ctx, mod_ctx, 3, wg, wu, wd, lng1, lnb1, tm=lc, n_chunks=2, name="ffn_ctx")

    mod_lat, mod_ctx = layer_mods(1)
    w_in = o_w_in[0].astype(BF16)
    y_lat = _proj(x_lat, mod_lat, 0, w_in,
                  [(0, 512, "plain"), (512, 512, "scale"), (1024, 512, "plain"), (1536, 512, "plain")],
                  tm=512, q_scale=q_scale_na, name="odd_inproj_lat")
    y_ctx = _proj(x_ctx, mod_ctx, 0, w_in[:, C_WIDTH + D_WIDTH:],
                  [(0, 512, "plain"), (512, 512, "plain")], tm=lc, name="odd_inproj_ctx")
    f_lat = _fourier_mixer(y_lat)
    bias_tab = _na_bias_table(o_rpb[0], l // GRID_W)
    n_lat = _natten(y_lat, C_WIDTH, C_WIDTH + D_WIDTH, C_WIDTH + 2 * D_WIDTH, y_ctx, 0, D_WIDTH,
                    bias_tab, rows_per_step=16)

    w_o = o_w_o[0].astype(BF16)
    lng0, lnb0 = ln_g[1, 0][None, :], ln_b[1, 0][None, :]
    lng1, lnb1 = ln_g[1, 1][None, :], ln_b[1, 1][None, :]
    return _outproj_moe_ln(f_lat, n_lat, w_o, x_lat, mod_lat, lng0, lnb0, o_router[0],
                           o_exp_gate[0], o_exp_up[0], o_exp_down[0], lng1, lnb1, tf=512)
```

```python
import functools
import math

import numpy as np
import jax
import jax.numpy as jnp
from jax import lax
from jax.experimental import pallas as pl
from jax.experimental.pallas import tpu as pltpu

F32 = jnp.float32
BF16 = jnp.bfloat16

D_MODEL = 1024
GRID_W = 64
DEPTH = 2

A_WIDTH = 512
DIFF_HEADS = 4
DIFF_DH = 64
DIFF_VDIM = 128
DIFF_QK = 512
B_WIDTH = 512

C_WIDTH = 512
C_GROUPS = 4
C_GROUP_DIM = 128
NA_HEADS = 8
NA_DH = 64
D_WIDTH = 512
NA_KR = 8
NA_KC = 16

N_EXPERTS = 8

ROPE_THETA = 10000.0
LN_EPS = 1e-5
RMS_EPS = 1e-5
NEG_INF = -1e30
DEEPNORM_ALPHA = (2 * DEPTH) ** 0.25
LOG2E = 1.4426950408889634

LANES = 128
SUBLANES = 8
MOD_ROWS = 32
NT_DIMS = (((1,), (1,)), ((), ()))


def _cparams(sem, vmem_mb=48):
    return pltpu.CompilerParams(dimension_semantics=sem, vmem_limit_bytes=vmem_mb * 1024 * 1024)


def _layer_norm(r, g, b):
    mu = jnp.mean(r, axis=-1, keepdims=True)
    d = r - mu
    var = jnp.mean(d * d, axis=-1, keepdims=True)
    return d * lax.rsqrt(var + LN_EPS) * g + b


def _adaln_body(c_ref, w_ref, b_ref, o_ref):
    cnd = c_ref[...]
    s = (cnd * jax.nn.sigmoid(cnd)).astype(BF16)
    o_ref[...] = jnp.dot(s, w_ref[...].astype(BF16), preferred_element_type=F32) + b_ref[...]


def _adaln(cond, w_mod, b_mod):
    depth, d, n = w_mod.shape
    tn = 1536
    return pl.pallas_call(
        _adaln_body,
        grid=(depth, n // tn),
        in_specs=[
            pl.BlockSpec((MOD_ROWS, d), lambda l, j: (0, 0)),
            pl.BlockSpec((None, d, tn), lambda l, j: (l, 0, j)),
            pl.BlockSpec((None, 1, tn), lambda l, j: (l, 0, j)),
        ],
        out_specs=pl.BlockSpec((None, MOD_ROWS, tn), lambda l, j: (l, 0, j)),
        out_shape=jax.ShapeDtypeStruct((depth, MOD_ROWS, n), F32),
        compiler_params=_cparams(("parallel", "parallel")),
        name="adaln",
    )(cond, w_mod, b_mod.reshape(depth, 1, n))


def _proj_body(*refs, chunks, with_vt, with_rope, q_scale):
    x_ref, sh_ref, sc_ref, w_ref = refs[:4]
    i = 4
    if with_vt:
        wvt_ref = refs[i]
        i += 1
    if with_rope:
        cos_ref, sin_ref = refs[i], refs[i + 1]
        i += 2
    y_ref = refs[i]
    vt_ref = refs[i + 1] if with_vt else None

    h = (x_ref[...] * (1.0 + sc_ref[...]) + sh_ref[...]).astype(BF16)
    tm = h.shape[0]
    if with_rope:
        cos = cos_ref[...]
        sin = sin_ref[...]
        lane = lax.broadcasted_iota(jnp.int32, (tm, LANES), 1)
        low_half = (lane % 64) < 32
    for (c0, width, kind) in chunks:
        acc = jnp.dot(h, w_ref[:, c0:c0 + width], preferred_element_type=F32)
        if kind in ("rope", "rope_scale"):
            for j in range(width // LANES):
                a = acc[:, LANES * j:LANES * (j + 1)]
                rot = jnp.where(low_half, pltpu.roll(a, 96, 1), pltpu.roll(a, 32, 1))
                r = a * cos + rot * sin
                if kind == "rope_scale":
                    r = r * q_scale
                y_ref[:, c0 + LANES * j:c0 + LANES * (j + 1)] = r.astype(BF16)
        elif kind == "scale":
            y_ref[:, c0:c0 + width] = (acc * q_scale).astype(BF16)
        else:
            y_ref[:, c0:c0 + width] = acc.astype(BF16)
    if with_vt:
        vt = lax.dot_general(wvt_ref[...], h, NT_DIMS, preferred_element_type=F32)
        vt_ref[...] = vt.astype(BF16)


def _proj(x, mod, shift_col, w, chunks, *, tm, wvt=None, rope=None, q_scale=1.0, name="proj"):
    b, l, d = x.shape
    n = w.shape[1]
    with_vt = wvt is not None
    with_rope = rope is not None
    in_specs = [
        pl.BlockSpec((None, tm, d), lambda bi, i: (bi, i, 0)),
        pl.BlockSpec((None, 1, d), lambda bi, i: (bi, 0, shift_col)),
        pl.BlockSpec((None, 1, d), lambda bi, i: (bi, 0, shift_col + 1)),
        pl.BlockSpec((d, n), lambda bi, i: (0, 0)),
    ]
    args = [x, mod, mod, w]
    if with_vt:
        nv = wvt.shape[0]
        in_specs.append(pl.BlockSpec((nv, d), lambda bi, i: (0, 0)))
        args.append(wvt)
    if with_rope:
        in_specs += [pl.BlockSpec((tm, LANES), lambda bi, i: (i, 0))] * 2
        args += [rope[0], rope[1]]
    out_specs = [pl.BlockSpec((None, tm, n), lambda bi, i: (bi, i, 0))]
    out_shape = [jax.ShapeDtypeStruct((b, l, n), BF16)]
    if with_vt:
        out_specs.append(pl.BlockSpec((None, nv, tm), lambda bi, i: (bi, 0, i)))
        out_shape.append(jax.ShapeDtypeStruct((b, nv, l), BF16))
    res = pl.pallas_call(
        functools.partial(_proj_body, chunks=tuple(chunks), with_vt=with_vt, with_rope=with_rope,
                          q_scale=q_scale),
        grid=(b, l // tm),
        in_specs=in_specs,
        out_specs=out_specs,
        out_shape=out_shape,
        compiler_params=_cparams(("parallel", "parallel")),
        name=name,
    )(*args)
    return res if with_vt else res[0]


DIFF_HP = 4


def _diffattn_body(*refs, n_seg, lam_init):
    q_ref, lam_ref, g_ref = refs[:3]
    k_refs = [refs[3 + 2 * s] for s in range(n_seg)]
    vt_refs = [refs[4 + 2 * s] for s in range(n_seg)]
    o_ref = refs[3 + 2 * n_seg]

    tq = q_ref.shape[0]
    lane = lax.broadcasted_iota(jnp.int32, (tq, LANES), 1)
    lp = lam_ref[...]
    lam = (jnp.exp(jnp.sum(lp[0:1] * lp[1:2], axis=1, keepdims=True))
           - jnp.exp(jnp.sum(lp[2:3] * lp[3:4], axis=1, keepdims=True)) + lam_init)

    scores = []
    for h in range(DIFF_HP):
        cols = slice(LANES * h, LANES * (h + 1))
        q = q_ref[:, cols]
        zero = jnp.zeros_like(q)
        per_comp = []
        for qm in (jnp.where(lane < DIFF_DH, q, zero), jnp.where(lane >= DIFF_DH, q, zero)):
            per_comp.append([lax.dot_general(k_ref[:, cols], qm, NT_DIMS, preferred_element_type=F32)
                             for k_ref in k_refs])
        scores.append(per_comp)

    def unnormalised(s, rows):
        m = functools.reduce(jnp.maximum, [jnp.max(x, axis=0, keepdims=True) for x in s])
        tot = None
        acc = None
        for x, vt_ref in zip(s, vt_refs):
            e = jnp.exp2(x - m)
            t = jnp.sum(e, axis=0, keepdims=True)
            pv = jnp.dot(vt_ref[rows, :], e.astype(BF16), preferred_element_type=F32)
            tot = t if tot is None else tot + t
            acc = pv if acc is None else acc + pv
        return acc, tot

    for h in range(DIFF_HP):
        cols = slice(LANES * h, LANES * (h + 1))
        rows = slice(DIFF_VDIM * h, DIFF_VDIM * (h + 1))
        acc1, l1 = unnormalised(scores[h][0], rows)
        acc2, l2 = unnormalised(scores[h][1], rows)
        o_t = acc1 * (1.0 / l1) - acc2 * (lam / l2)
        o = o_t.T
        ms = jnp.mean(o * o, axis=-1, keepdims=True)
        o_ref[:, cols] = (o * lax.rsqrt(ms + RMS_EPS) * g_ref[...] * (1.0 - lam_init)).astype(BF16)


def _diff_attention(yq, segs, lam_pack, sub_g, lam_init, *, tq, q_col0, k_col0, name):
    b, lq, _ = yq.shape
    width = DIFF_HP * LANES
    qb0 = q_col0 // width
    kb0 = k_col0 // width
    in_specs = [
        pl.BlockSpec((None, tq, width), lambda bi, h, i: (bi, i, qb0 + h)),
        pl.BlockSpec((8, LANES), lambda bi, h, i: (0, 0)),
        pl.BlockSpec((1, LANES), lambda bi, h, i: (0, 0)),
    ]
    args = [yq, lam_pack, sub_g]
    for (yk, vt) in segs:
        lk = yk.shape[1]
        in_specs.append(pl.BlockSpec((None, lk, width), lambda bi, h, i: (bi, 0, kb0 + h)))
        in_specs.append(pl.BlockSpec((None, DIFF_HP * DIFF_VDIM, lk), lambda bi, h, i: (bi, h, 0)))
        args += [yk, vt]
    return pl.pallas_call(
        functools.partial(_diffattn_body, n_seg=len(segs), lam_init=lam_init),
        grid=(b, DIFF_HEADS // DIFF_HP, lq // tq),
        in_specs=in_specs,
        out_specs=pl.BlockSpec((None, tq, width), lambda bi, h, i: (bi, i, h)),
        out_shape=jax.ShapeDtypeStruct((b, lq, B_WIDTH), BF16),
        compiler_params=_cparams(("parallel", "parallel", "arbitrary")),
        name=name,
    )(*args)


HALO_ROWS = 16


def _conv_outproj_body(bg_ref, cg_ref, val_ref, cgp_ref, valp_ref, cgn_ref, valn_ref, wc_ref,
                       o_att_ref, w_ref, x_ref, gate_ref, lng_ref, lnb_ref, o_ref):
    i = pl.program_id(1)
    tm = x_ref.shape[0]
    half = o_att_ref.shape[1]
    u = cg_ref[...].astype(F32) * val_ref[...].astype(F32)
    u_before = cgp_ref[...].astype(F32) * valp_ref[...].astype(F32)
    u_after = cgn_ref[...].astype(F32) * valn_ref[...].astype(F32)
    first = jnp.where(i == 0, 0.0, u_before[HALO_ROWS - 1:HALO_ROWS, :])
    last = jnp.where(i == pl.num_programs(1) - 1, 0.0, u_after[0:1, :])
    row = lax.broadcasted_iota(jnp.int32, (tm, half), 0)
    u_prev = jnp.where(row == 0, first, pltpu.roll(u, 1, 0))
    u_next = jnp.where(row == tm - 1, last, pltpu.roll(u, tm - 1, 0))
    wc = wc_ref[...]
    conv = bg_ref[...].astype(F32) * (u_prev * wc[0:1] + u * wc[1:2] + u_next * wc[2:3])
    y = (jnp.dot(conv.astype(BF16), w_ref[0:half, :], preferred_element_type=F32)
         + jnp.dot(o_att_ref[...], w_ref[half:2 * half, :], preferred_element_type=F32))
    r = DEEPNORM_ALPHA * x_ref[...] + gate_ref[...] * y
    o_ref[...] = _layer_norm(r, lng_ref[...], lnb_ref[...])


def _conv_outproj_ln(y, w_conv, o_att, w_o, x, mod, gate_col, ln_g, ln_b, *, tm, name):
    b, l, d = x.shape
    half = w_o.shape[0] // 2
    per = tm // HALO_ROWS
    n_halo = l // HALO_ROWS
    main = lambda col: pl.BlockSpec((None, tm, half), lambda bi, i: (bi, i, col))
    before = lambda col: pl.BlockSpec((None, HALO_ROWS, half),
                                      lambda bi, i: (bi, jnp.maximum(i * per - 1, 0), col))
    after = lambda col: pl.BlockSpec((None, HALO_ROWS, half),
                                     lambda bi, i: (bi, jnp.minimum((i + 1) * per, n_halo - 1), col))
    return pl.pallas_call(
        _conv_outproj_body,
        grid=(b, l // tm),
        in_specs=[
            main(0), main(1), main(2), before(1), before(2), after(1), after(2),
            pl.BlockSpec((3, half), lambda bi, i: (0, 0)),
            pl.BlockSpec((None, tm, half), lambda bi, i: (bi, i, 0)),
            pl.BlockSpec((2 * half, d), lambda bi, i: (0, 0)),
            pl.BlockSpec((None, tm, d), lambda bi, i: (bi, i, 0)),
            pl.BlockSpec((None, 1, d), lambda bi, i: (bi, 0, gate_col)),
            pl.BlockSpec((1, d), lambda bi, i: (0, 0)),
            pl.BlockSpec((1, d), lambda bi, i: (0, 0)),
        ],
        out_specs=pl.BlockSpec((None, tm, d), lambda bi, i: (bi, i, 0)),
        out_shape=jax.ShapeDtypeStruct((b, l, d), F32),
        compiler_params=_cparams(("parallel", "parallel")),
        name=name,
    )(y, y, y, y, y, y, y, w_conv, o_att, w_o, x, mod, ln_g, ln_b)


def _outproj_rows(a_ref, b_ref, w_ref, x_ref, gate_ref, lng_ref, lnb_ref):
    half = a_ref.shape[1]
    y = (jnp.dot(a_ref[...], w_ref[0:half, :], preferred_element_type=F32)
         + jnp.dot(b_ref[...], w_ref[half:2 * half, :], preferred_element_type=F32))
    r = DEEPNORM_ALPHA * x_ref[...] + gate_ref[...] * y
    return _layer_norm(r, lng_ref[...], lnb_ref[...])


MXU_WIDTH = 256


def _ffn_body(x_ref, sh_ref, sc_ref, gate_ref, wg_ref, wu_ref, wd_ref, lng_ref, lnb_ref, o_ref, *, chunks):
    x = x_ref[...]
    h = (x * (1.0 + sc_ref[...]) + sh_ref[...]).astype(BF16)
    acc = None
    for (c0, c1) in chunks:
        g = jnp.dot(h, wg_ref[:, c0:c1], preferred_element_type=F32)
        u = jnp.dot(h, wu_ref[:, c0:c1], preferred_element_type=F32)
        a = (g * jax.nn.sigmoid(g) * u).astype(BF16)
        part = jnp.dot(a, wd_ref[c0:c1, :], preferred_element_type=F32)
        acc = part if acc is None else acc + part
    r = DEEPNORM_ALPHA * x + gate_ref[...] * acc
    o_ref[...] = _layer_norm(r, lng_ref[...], lnb_ref[...])


def _ffn_ln(x, mod, shift_col, wg, wu, wd, ln_g, ln_b, *, tm, n_chunks, name="ffn"):
    b, l, d = x.shape
    ff = wg.shape[1]
    n_mxu = ff // MXU_WIDTH
    assert ff == n_mxu * MXU_WIDTH
    bounds = [MXU_WIDTH * ((n_mxu * k + n_chunks - 1) // n_chunks) for k in range(n_chunks + 1)]
    chunks = tuple((bounds[k], bounds[k + 1]) for k in range(n_chunks))
    resident = pl.Buffered(1)
    return pl.pallas_call(
        functools.partial(_ffn_body, chunks=chunks),
        grid=(b, l // tm),
        in_specs=[
            pl.BlockSpec((None, tm, d), lambda bi, i: (bi, i, 0)),
            pl.BlockSpec((None, 1, d), lambda bi, i: (bi, 0, shift_col)),
            pl.BlockSpec((None, 1, d), lambda bi, i: (bi, 0, shift_col + 1)),
            pl.BlockSpec((None, 1, d), lambda bi, i: (bi, 0, shift_col + 2)),
            pl.BlockSpec((d, ff), lambda bi, i: (0, 0), pipeline_mode=resident),
            pl.BlockSpec((d, ff), lambda bi, i: (0, 0), pipeline_mode=resident),
            pl.BlockSpec((ff, d), lambda bi, i: (0, 0), pipeline_mode=resident),
            pl.BlockSpec((1, d), lambda bi, i: (0, 0)),
            pl.BlockSpec((1, d), lambda bi, i: (0, 0)),
        ],
        out_specs=pl.BlockSpec((None, tm, d), lambda bi, i: (bi, i, 0)),
        out_shape=jax.ShapeDtypeStruct((b, l, d), F32),
        compiler_params=_cparams(("parallel", "parallel")),
        name=name,
    )(x, mod, mod, mod, wg, wu, wd, ln_g, ln_b)


MOE_TM = 1024
ROUTE_TM = 512
ZERO_ROWS = 256
META_I1, META_I2, META_R1, META_R2, META_P1, META_P2 = range(6)


def _outproj_route_body(a_ref, b_ref, w_ref, x_ref, gate_ref, lng_ref, lnb_ref, sh_ref, sc_ref, wr_ref,
                        o_ref, meta_ref, meta_t_ref, cnt_ref, step_cnt_ref, carry_scr):
    x_new = _outproj_rows(a_ref, b_ref, w_ref, x_ref, gate_ref, lng_ref, lnb_ref)
    o_ref[...] = x_new
    _route_rows(x_new, sh_ref, sc_ref, wr_ref, meta_ref, meta_t_ref, cnt_ref, step_cnt_ref, carry_scr)


def _route_rows(x, sh_ref, sc_ref, wr_ref, meta_ref, meta_t_ref, cnt_ref, step_cnt_ref, carry_scr):
    @pl.when(pl.program_id(0) == 0)
    def _():
        carry_scr[...] = jnp.zeros_like(carry_scr)

    h = (x * (1.0 + sc_ref[...]) + sh_ref[...]).astype(BF16)
    logits = jnp.dot(h, wr_ref[...], preferred_element_type=F32)
    tm = logits.shape[0]
    lane = lax.broadcasted_iota(jnp.int32, logits.shape, 1).astype(F32)
    l1 = jnp.where(lane < N_EXPERTS, logits, -jnp.inf)
    v1 = jnp.max(l1, axis=-1, keepdims=True)
    i1 = jnp.min(jnp.where(l1 == v1, lane, float(LANES)), axis=-1, keepdims=True)
    l2 = jnp.where(lane == i1, -jnp.inf, l1)
    v2 = jnp.max(l2, axis=-1, keepdims=True)
    i2 = jnp.min(jnp.where(l2 == v2, lane, float(LANES)), axis=-1, keepdims=True)
    t = jnp.exp(v2 - v1)
    p1 = 1.0 / (1.0 + t)
    p2 = t / (1.0 + t)

    member = jnp.where(lane == i1, 1.0, jnp.where(lane == i2, 1.0, 0.0))
    rr = lax.broadcasted_iota(jnp.int32, (tm, tm), 0)
    cc = lax.broadcasted_iota(jnp.int32, (tm, tm), 1)
    earlier = jnp.where(cc < rr, 1.0, 0.0).astype(BF16)
    base = carry_scr[0:1, :]
    rank = jnp.dot(earlier, member.astype(BF16), preferred_element_type=F32) + base
    total = base + jnp.sum(member, axis=0, keepdims=True)
    carry_scr[0:1, :] = total
    r1 = jnp.sum(jnp.where(lane == i1, rank, 0.0), axis=-1, keepdims=True)
    r2 = jnp.sum(jnp.where(lane == i2, rank, 0.0), axis=-1, keepdims=True)

    meta = jnp.zeros_like(logits)
    for k, val in ((META_I1, i1), (META_I2, i2), (META_R1, r1), (META_R2, r2), (META_P1, p1), (META_P2, p2)):
        meta = jnp.where(lane == float(k), val, meta)
    meta_ref[...] = meta
    meta_t_ref[...] = meta.T[0:SUBLANES, :]
    cnt_ref[...] = jnp.broadcast_to(total, cnt_ref.shape)
    step_cnt_ref[...] = jnp.broadcast_to(total - base, step_cnt_ref.shape)


def _outproj_route(a2, b2, w_o, x2, mod, gate_col, ln_g, ln_b, shift_col, w_router_pad, *, seq_len):
    m, d = x2.shape
    half = w_o.shape[0] // 2
    tm = ROUTE_TM
    batch = lambda col: pl.BlockSpec((None, 1, d), lambda i: ((i * tm) // seq_len, 0, col))
    return pl.pallas_call(
        _outproj_route_body,
        grid=(m // tm,),
        in_specs=[
            pl.BlockSpec((tm, half), lambda i: (i, 0)),
            pl.BlockSpec((tm, half), lambda i: (i, 0)),
            pl.BlockSpec((2 * half, d), lambda i: (0, 0)),
            pl.BlockSpec((tm, d), lambda i: (i, 0)),
            batch(gate_col),
            pl.BlockSpec((1, d), lambda i: (0, 0)),
            pl.BlockSpec((1, d), lambda i: (0, 0)),
            batch(shift_col),
            batch(shift_col + 1),
            pl.BlockSpec((d, LANES), lambda i: (0, 0)),
        ],
        out_specs=[pl.BlockSpec((tm, d), lambda i: (i, 0)),
                   pl.BlockSpec((tm, LANES), lambda i: (i, 0)),
                   pl.BlockSpec((SUBLANES, tm), lambda i: (0, i)),
                   pl.BlockSpec((8, LANES), lambda i: (0, 0)),
                   pl.BlockSpec((8, LANES), lambda i: (i, 0))],
        out_shape=[jax.ShapeDtypeStruct((m, d), F32),
                   jax.ShapeDtypeStruct((m, LANES), F32), jax.ShapeDtypeStruct((SUBLANES, m), F32),
                   jax.ShapeDtypeStruct((8, LANES), F32),
                   jax.ShapeDtypeStruct((8 * (m // tm), LANES), F32)],
        scratch_shapes=[pltpu.VMEM((8, LANES), F32)],
        compiler_params=_cparams(("arbitrary",)),
        name="odd_outproj_route",
    )(a2, b2, w_o, x2, mod, ln_g, ln_b, mod, mod, w_router_pad)


def _to_tiles(tile_ref, value):
    n = value.shape[0]
    for j in range(SUBLANES):
        tile_ref[pl.ds(j, n, stride=SUBLANES), :] = value[:, LANES * j:LANES * (j + 1)]


def _from_tiles(tile_ref):
    n = tile_ref.shape[0] // SUBLANES
    return jnp.concatenate([tile_ref[pl.ds(j, n, stride=SUBLANES), :] for j in range(SUBLANES)], axis=-1)


def _tile_rows(ref, row, n=1):
    return ref.at[pl.ds(pl.multiple_of(row * SUBLANES, SUBLANES), n * SUBLANES), :]


def _row_copy(src, src_row, dst, dst_row, sem):
    return pltpu.make_async_copy(_tile_rows(src, src_row), _tile_rows(dst, dst_row), sem)


DISPATCH_CH = 32
DISPATCH_SLOTS = 2 * ROUTE_TM + N_EXPERTS * DISPATCH_CH


def _chunk_loops(nch_ref, step, make_copy, action):
    for e in range(N_EXPERTS):
        def body(k, carry, e=e):
            cp = make_copy(step, e, k)
            if action == "start":
                cp.start(priority=e % 2)
            else:
                cp.wait()
            return carry

        lax.fori_loop(0, nch_ref[step * N_EXPERTS + e], body, 0)


def _dispatch_body(pad_ref, nch_ref, loc_ref, dst_ref, lslot_ref, x_ref, sh_ref, sc_ref, xs_hbm,
                   hs_scr, z_scr, row_sems):
    step = pl.program_id(0)
    last = pl.num_programs(0) - 1
    tm = x_ref.shape[0]

    def chunk_copy(buf, st, e, k):
        src = _tile_rows(hs_scr.at[buf], loc_ref[st * N_EXPERTS + e] + k * DISPATCH_CH, DISPATCH_CH)
        dst = _tile_rows(xs_hbm, dst_ref[st * N_EXPERTS + e] + k * DISPATCH_CH, DISPATCH_CH)
        return pltpu.make_async_copy(src, dst, row_sems.at[buf])

    def produce(buf):
        h = (x_ref[...] * (1.0 + sc_ref[...]) + sh_ref[...]).astype(BF16)
        ls = lslot_ref[...]
        srow = lax.broadcasted_iota(jnp.int32, (DISPATCH_SLOTS, tm), 0)
        pick = jnp.where(srow == ls[0:1, :], 1.0, jnp.where(srow == ls[1:2, :], 1.0, 0.0)).astype(BF16)
        _to_tiles(hs_scr.at[buf], jnp.dot(pick, h, preferred_element_type=F32))

    for parity in (0, 1):
        pl.when(step % 2 == parity)(functools.partial(produce, parity))
    for parity in (0, 1):
        pl.when((step % 2 == parity) & (step >= 1))(functools.partial(
            _chunk_loops, nch_ref, step - 1, functools.partial(chunk_copy, 1 - parity), "wait"))
    for parity in (0, 1):
        pl.when(step % 2 == parity)(functools.partial(
            _chunk_loops, nch_ref, step, functools.partial(chunk_copy, parity), "start"))
    for parity in (0, 1):
        pl.when((step % 2 == parity) & (step == last))(functools.partial(
            _chunk_loops, nch_ref, step, functools.partial(chunk_copy, parity), "wait"))
    row_sem = row_sems.at[0]

    @pl.when(step == pl.num_programs(0) - 1)
    def _():
        z_scr[...] = jnp.zeros_like(z_scr)
        for e in range(N_EXPERTS):
            start = pad_ref[e]
            count = pad_ref[N_EXPERTS + e]

            def fill(k, carry, start=start):
                _row_copy(z_scr, 0, xs_hbm, start + k, row_sem).start()
                return carry

            def fill_wait(k, carry):
                _row_copy(z_scr, 0, xs_hbm, 0, row_sem).wait()
                return carry

            lax.fori_loop(0, count, fill, 0)
            lax.fori_loop(0, count, fill_wait, 0)

        zrows = z_scr.shape[0] // SUBLANES
        used_rows = pad_ref[2 * N_EXPERTS]
        n_chunks = (xs_hbm.shape[0] // SUBLANES - used_rows) // zrows

        def chunk_copy(k):
            return pltpu.make_async_copy(z_scr, _tile_rows(xs_hbm, used_rows + k * zrows, zrows), row_sem)

        def fill_chunk(k, carry):
            chunk_copy(k).start()
            return carry

        def fill_chunk_wait(k, carry):
            chunk_copy(k).wait()
            return carry

        lax.fori_loop(0, n_chunks, fill_chunk, 0)
        lax.fori_loop(0, n_chunks, fill_chunk_wait, 0)


def _dispatch(x2, mod, shift_col, lslot, n_chunk, loc0, dest0, pad_info, n_rows, *, seq_len):
    m, d = x2.shape
    tm = ROUTE_TM
    return pl.pallas_call(
        _dispatch_body,
        grid_spec=pltpu.PrefetchScalarGridSpec(
            num_scalar_prefetch=4,
            grid=(m // tm,),
            in_specs=[
                pl.BlockSpec((SUBLANES, tm), lambda i, *_: (0, i)),
                pl.BlockSpec((tm, d), lambda i, *_: (i, 0)),
                pl.BlockSpec((None, 1, d), lambda i, *_: ((i * tm) // seq_len, 0, shift_col)),
                pl.BlockSpec((None, 1, d), lambda i, *_: ((i * tm) // seq_len, 0, shift_col + 1)),
            ],
            out_specs=pl.BlockSpec(memory_space=pl.ANY),
            scratch_shapes=[
                pltpu.VMEM((2, DISPATCH_SLOTS * SUBLANES, LANES), F32),
                pltpu.VMEM((ZERO_ROWS * SUBLANES, LANES), F32),
                pltpu.SemaphoreType.DMA((2,)),
            ],
        ),
        out_shape=jax.ShapeDtypeStruct((n_rows * SUBLANES, LANES), F32),
        compiler_params=_cparams(("arbitrary",)),
        name="moe_dispatch",
    )(pad_info, n_chunk, loc0, dest0, lslot, x2, mod, mod)


TAIL_LEVELS = 2


def _experts_body(te_ref, tbi_ref, tbo_ref, tr_ref, xs_ref, wg_ref, wu_ref, wd_ref, ys_ref, h_scr, acc_scr, *, n_f):
    t = pl.program_id(0)
    f = pl.program_id(1)

    @pl.when(f == 0)
    def _():
        h_scr[...] = _from_tiles(xs_ref).astype(BF16)
        acc_scr[...] = jnp.zeros_like(acc_scr)

    def swiglu_rows(n):
        h = h_scr[0:n, :]
        g = jnp.dot(h, wg_ref[...].astype(BF16), preferred_element_type=F32)
        u = jnp.dot(h, wu_ref[...].astype(BF16), preferred_element_type=F32)
        a = (g * jax.nn.sigmoid(g) * u).astype(BF16)
        acc_scr[0:n, :] += jnp.dot(a, wd_ref[...].astype(BF16), preferred_element_type=F32)

    rows = tr_ref[t]
    tm = h_scr.shape[0]
    bounds = (0,) + tuple(tm >> k for k in range(TAIL_LEVELS, -1, -1))
    for lo, hi in zip(bounds[:-1], bounds[1:]):
        pl.when((rows > lo) & (rows <= hi))(functools.partial(swiglu_rows, hi))

    @pl.when(f == n_f - 1)
    def _():
        _to_tiles(ys_ref, acc_scr[...])


def _experts(xs, tile_expert, tile_in, tile_out, tile_rows, wg, wu, wd, *, tf):
    n_rows = xs.shape[0] // SUBLANES
    d = wg.shape[1]
    tm = MOE_TM
    n_tiles = tile_expert.shape[0]
    ff = wg.shape[2]
    n_f = ff // tf
    ff_blk = lambda f, tr, t: f * jnp.minimum(tr[t], 1)
    return pl.pallas_call(
        functools.partial(_experts_body, n_f=n_f),
        grid_spec=pltpu.PrefetchScalarGridSpec(
            num_scalar_prefetch=4,
            grid=(n_tiles, n_f),
            in_specs=[
                pl.BlockSpec((tm * SUBLANES, LANES), lambda t, f, te, tbi, tbo, tr: (tbi[t], 0)),
                pl.BlockSpec((None, d, tf), lambda t, f, te, tbi, tbo, tr: (te[t], 0, ff_blk(f, tr, t))),
                pl.BlockSpec((None, d, tf), lambda t, f, te, tbi, tbo, tr: (te[t], 0, ff_blk(f, tr, t))),
                pl.BlockSpec((None, tf, d), lambda t, f, te, tbi, tbo, tr: (te[t], ff_blk(f, tr, t), 0)),
            ],
            out_specs=pl.BlockSpec((tm * SUBLANES, LANES), lambda t, f, te, tbi, tbo, tr: (tbo[t], 0)),
            scratch_shapes=[pltpu.VMEM((tm, d), BF16), pltpu.VMEM((tm, d), F32)],
        ),
        out_shape=jax.ShapeDtypeStruct((n_rows * SUBLANES, LANES), F32),
        compiler_params=_cparams(("arbitrary", "arbitrary")),
        name="moe_experts",
    )(tile_expert, tile_in, tile_out, tile_rows, xs, wg, wu, wd)


def _combine_body(nch_ref, loc_ref, dst_ref, ys_hbm, lslot_ref, meta_ref, x_ref, gate_ref, lng_ref, lnb_ref,
                  o_ref, y_scr, row_sems):
    s = pl.program_id(0)
    n_tiles = pl.num_programs(0) - 1
    tm = x_ref.shape[0]

    @pl.when(s == 0)
    def _():
        y_scr[...] = jnp.zeros_like(y_scr)

    def chunk_copy(buf, st, e, k):
        src = _tile_rows(ys_hbm, dst_ref[st * N_EXPERTS + e] + k * DISPATCH_CH, DISPATCH_CH)
        dst = _tile_rows(y_scr.at[buf], loc_ref[st * N_EXPERTS + e] + k * DISPATCH_CH, DISPATCH_CH)
        return pltpu.make_async_copy(src, dst, row_sems.at[buf])

    def finish(buf):
        _chunk_loops(nch_ref, s - 1, functools.partial(chunk_copy, buf), "wait")
        yb = _from_tiles(y_scr.at[buf]).astype(BF16)
        ls = lslot_ref[...]
        lc = jnp.concatenate([ls, jnp.zeros((LANES - SUBLANES, tm), ls.dtype)], axis=0).T
        meta = meta_ref[...]
        slot = lax.broadcasted_iota(jnp.int32, (tm, DISPATCH_SLOTS), 1)
        pick = jnp.where(slot == lc[:, 0:1], meta[:, META_P1:META_P1 + 1],
                         jnp.where(slot == lc[:, 1:2], meta[:, META_P2:META_P2 + 1], 0.0)).astype(BF16)
        mix = jnp.dot(pick, yb, preferred_element_type=F32)
        r = DEEPNORM_ALPHA * x_ref[...] + gate_ref[...] * mix
        o_ref[...] = _layer_norm(r, lng_ref[...], lnb_ref[...])

    for parity in (0, 1):
        pl.when((s % 2 == parity) & (s < n_tiles))(functools.partial(
            _chunk_loops, nch_ref, s, functools.partial(chunk_copy, parity), "start"))
    for parity in (0, 1):
        pl.when((s % 2 == parity) & (s >= 1))(functools.partial(finish, 1 - parity))


def _combine_ln(ys, n_chunk, loc0, dest0, lslot, meta, x2, mod, gate_col, ln_g, ln_b, *, seq_len):
    m, d = x2.shape
    tm = ROUTE_TM
    prev = lambda i: jnp.maximum(i - 1, 0)
    return pl.pallas_call(
        _combine_body,
        grid_spec=pltpu.PrefetchScalarGridSpec(
            num_scalar_prefetch=3,
            grid=(m // tm + 1,),
            in_specs=[
                pl.BlockSpec(memory_space=pl.ANY),
                pl.BlockSpec((SUBLANES, tm), lambda i, *_: (0, prev(i))),
                pl.BlockSpec((tm, LANES), lambda i, *_: (prev(i), 0)),
                pl.BlockSpec((tm, d), lambda i, *_: (prev(i), 0)),
                pl.BlockSpec((None, 1, d), lambda i, *_: ((prev(i) * tm) // seq_len, 0, gate_col)),
                pl.BlockSpec((1, d), lambda i, *_: (0, 0)),
                pl.BlockSpec((1, d), lambda i, *_: (0, 0)),
            ],
            out_specs=pl.BlockSpec((tm, d), lambda i, *_: (prev(i), 0)),
            scratch_shapes=[
                pltpu.VMEM((2, DISPATCH_SLOTS * SUBLANES, LANES), F32),
                pltpu.SemaphoreType.DMA((2,)),
            ],
        ),
        out_shape=jax.ShapeDtypeStruct((m, d), F32),
        compiler_params=_cparams(("arbitrary",)),
        name="moe_combine",
    )(n_chunk, loc0, dest0, ys, lslot, meta, x2, mod, ln_g, ln_b)


def _outproj_moe_ln(a, bsrc, w_o, x, mod, ln_g0, ln_b0, w_router, wg, wu, wd, ln_g, ln_b, *, tf):
    b, l, d = x.shape
    assert d == SUBLANES * LANES
    m = b * l
    w_r = jnp.zeros((d, LANES), BF16).at[:, :N_EXPERTS].set(w_router.astype(BF16))
    x2, meta, meta_t, cnt, step_cnt = _outproj_route(a.reshape(m, -1), bsrc.reshape(m, -1), w_o, x.reshape(m, d),
                                                     mod, 2, ln_g0, ln_b0, 3, w_r, seq_len=l)
    n_steps = m // ROUTE_TM

    counts = cnt[0, :N_EXPERTS].astype(jnp.int32)
    n_tile_e = (counts + DISPATCH_CH - 1 + MOE_TM - 1) // MOE_TM
    tile_end = jnp.cumsum(n_tile_e)
    offs = (tile_end - n_tile_e) * MOE_TM
    idx = meta_t[META_I1:META_I2 + 1].astype(jnp.int32)
    rank = meta_t[META_R1:META_R2 + 1].astype(jnp.int32)
    step_n = step_cnt.reshape(n_steps, 8, LANES)[:, 0, :N_EXPERTS].astype(jnp.int32)
    step_before = jnp.cumsum(step_n, axis=0) - step_n
    step_pad = ((step_n + DISPATCH_CH - 1) // DISPATCH_CH) * DISPATCH_CH
    step_loc0 = jnp.cumsum(step_pad, axis=1) - step_pad
    to_local = step_loc0 - step_before
    idx3 = idx.reshape(2, n_steps, ROUTE_TM)
    local = jnp.zeros_like(idx3)
    for e in range(N_EXPERTS):
        local = jnp.where(idx3 == e, to_local[None, :, e, None], local)
    local = local.reshape(2, m) + rank
    lslot = jnp.concatenate([local, jnp.full((SUBLANES - 2, m), -1, jnp.int32)], axis=0)
    n_chunk = (step_pad // DISPATCH_CH).reshape(-1)
    loc0 = step_loc0.reshape(-1)
    dest0 = (offs[None, :] + step_before).reshape(-1)
    n_tiles = (2 * m) // MOE_TM + N_EXPERTS
    tid = jnp.arange(n_tiles, dtype=jnp.int32)
    tile_valid = (tid < tile_end[-1]).astype(jnp.int32)
    tile_expert = jnp.minimum(jnp.sum((tid[:, None] >= tile_end[None, :]).astype(jnp.int32), axis=1),
                              N_EXPERTS - 1).astype(jnp.int32)
    tile_in = jnp.where(tile_valid > 0, tid, 0).astype(jnp.int32)
    tile_out = tid
    row0 = tid * MOE_TM - jnp.take(offs, tile_expert)
    tile_rows = (jnp.clip(jnp.take(counts, tile_expert) - row0, 0, MOE_TM) * tile_valid).astype(jnp.int32)
    pad_info = jnp.concatenate([offs + counts, n_tile_e * MOE_TM - counts,
                                tile_end[-1:] * MOE_TM]).astype(jnp.int32)
    n_rows = n_tiles * MOE_TM

    xs = _dispatch(x2, mod, 3, lslot, n_chunk, loc0, dest0, pad_info, n_rows, seq_len=l)
    ys = _experts(xs, tile_expert, tile_in, tile_out, tile_rows, wg, wu, wd, tf=tf)
    out = _combine_ln(ys, n_chunk, loc0, dest0, lslot, meta, x2, mod, 5, ln_g, ln_b, seq_len=l)
    return out.reshape(b, l, d)


def _fourier_body(pc_ref, dl_ref, dc_ref, o_ref, t_scr, *, out_scale, row_chunk):
    l = pc_ref.shape[0]
    for g in range(C_GROUPS):
        sl = slice(C_GROUP_DIM * g, C_GROUP_DIM * (g + 1))
        xg = pc_ref[:, sl].astype(F32)
        mu = jnp.mean(xg, axis=-1, keepdims=True)
        dlt = xg - mu
        var = jnp.mean(dlt * dlt, axis=-1, keepdims=True)
        gn = (dlt * lax.rsqrt(var + LN_EPS)).astype(BF16)
        t = jnp.dot(gn, dc_ref[...], preferred_element_type=F32)
        t_scr[0:l, sl] = t[:, 0:C_GROUP_DIM].astype(BF16)
        t_scr[l:2 * l, sl] = t[:, C_GROUP_DIM:2 * C_GROUP_DIM].astype(BF16)
    for r0 in range(0, l, row_chunk):
        acc = jnp.dot(dl_ref[r0:r0 + row_chunk, :], t_scr[...], preferred_element_type=F32)
        o_ref[r0:r0 + row_chunk, :] = (acc * out_scale).astype(BF16)


def _dft_matrices(l, c):
    j = np.arange(l, dtype=np.int64)
    ang_l = (2.0 * np.pi / l) * ((j[:, None] * j[None, :]) % l)
    dl = np.concatenate([np.cos(ang_l), -np.sin(ang_l)], axis=1)
    m = np.arange(c, dtype=np.int64)
    ang_c = (2.0 * np.pi / c) * ((m[:, None] * m[None, :]) % c)
    dc = np.concatenate([np.cos(ang_c), np.sin(ang_c)], axis=1)
    return dl.astype(np.float32), dc.astype(np.float32)


def _fourier_mixer(y, name="fourier"):
    b, l, _ = y.shape
    dl_np, dc_np = _dft_matrices(l, C_GROUP_DIM)
    dl = jnp.asarray(dl_np, dtype=F32).astype(BF16)
    dc = jnp.asarray(dc_np, dtype=F32).astype(BF16)
    out_scale = 1.0 / math.sqrt(l * C_GROUP_DIM)
    return pl.pallas_call(
        functools.partial(_fourier_body, out_scale=out_scale, row_chunk=min(l, 512)),
        grid=(b,),
        in_specs=[
            pl.BlockSpec((None, l, C_WIDTH), lambda bi: (bi, 0, 0)),
            pl.BlockSpec((l, 2 * l), lambda bi: (0, 0), pipeline_mode=pl.Buffered(1)),
            pl.BlockSpec((C_GROUP_DIM, 2 * C_GROUP_DIM), lambda bi: (0, 0)),
        ],
        out_specs=pl.BlockSpec((None, l, C_WIDTH), lambda bi: (bi, 0, 0)),
        out_shape=jax.ShapeDtypeStruct((b, l, C_WIDTH), BF16),
        scratch_shapes=[pltpu.VMEM((2 * l, C_WIDTH), BF16)],
        compiler_params=_cparams(("parallel",)),
        name=name,
    )(y, dl, dc)


NA_HG = 4
NA_GW = NA_HG * NA_DH
NA_WIN = NA_KR * GRID_W


def _natten_body(q_ref, k_ref, v_ref, kc_ref, vc_ref, bias_ref, o_ref, *, rows_per_step, n_rows):
    rb = pl.program_id(2)
    hq = NA_HG * GRID_W
    rid = lax.broadcasted_iota(jnp.int32, (hq, NA_GW), 0)
    cid = lax.broadcasted_iota(jnp.int32, (hq, NA_GW), 1)
    diag = (rid // GRID_W) == (cid // NA_DH)
    cid_o = lax.broadcasted_iota(jnp.int32, (GRID_W, NA_GW), 1)
    kc = kc_ref[...]
    vc = vc_ref[...]
    staged = []
    for j in range(rows_per_step):
        r = rb * rows_per_step + j
        rs = jnp.clip(r - NA_KR // 2, 0, n_rows - NA_KR)
        start = pl.multiple_of(rs * GRID_W, GRID_W)
        tid = jnp.minimum(r, NA_KR // 2) + jnp.maximum(r - (n_rows - NA_KR // 2), 0)
        q_r = q_ref[GRID_W * j:GRID_W * (j + 1), :]
        q4 = jnp.concatenate([q_r] * NA_HG, axis=0)
        qbd = jnp.where(diag, q4, jnp.zeros_like(q4))
        kw = k_ref[pl.ds(start, NA_WIN), :]
        s_loc = lax.dot_general(qbd, kw, NT_DIMS, preferred_element_type=F32) + bias_ref[tid]
        s_ctx = lax.dot_general(qbd, kc, NT_DIMS, preferred_element_type=F32)
        staged.append((start, s_loc, s_ctx))
    for j in range(rows_per_step):
        start, s_loc, s_ctx = staged[j]
        vw = v_ref[pl.ds(start, NA_WIN), :]
        m = jnp.maximum(jnp.max(s_loc, axis=-1, keepdims=True), jnp.max(s_ctx, axis=-1, keepdims=True))
        e_loc = jnp.exp2(s_loc - m)
        e_ctx = jnp.exp2(s_ctx - m)
        tot = jnp.sum(e_loc, axis=-1, keepdims=True) + jnp.sum(e_ctx, axis=-1, keepdims=True)
        o = (jnp.dot(e_loc.astype(BF16), vw, preferred_element_type=F32)
             + jnp.dot(e_ctx.astype(BF16), vc, preferred_element_type=F32))
        o = o * (1.0 / tot)
        out = jnp.zeros((GRID_W, NA_GW), F32)
        for hh in range(NA_HG):
            out = out + jnp.where((cid_o // NA_DH) == hh, o[GRID_W * hh:GRID_W * (hh + 1), :], 0.0)
        o_ref[GRID_W * j:GRID_W * (j + 1), :] = out.astype(BF16)


def _na_bias_table(rpb, n_rows):
    h = rpb.shape[0]
    cols = jnp.arange(GRID_W)
    col_start = jnp.clip(cols - NA_KC // 2, 0, GRID_W - NA_KC)
    col_valid = (cols[None, :] >= col_start[:, None]) & (cols[None, :] < col_start[:, None] + NA_KC)
    dc_idx = jnp.clip(cols[None, :] - cols[:, None] + NA_KC - 1, 0, 2 * NA_KC - 2)
    onehot = (dc_idx[:, :, None] == jnp.arange(2 * NA_KC - 1)).astype(F32)
    rpb_c = jnp.einsum("qkc,hdc->hqdk", onehot, rpb.astype(F32), precision=lax.Precision.HIGHEST)
    full = jnp.where(col_valid[None, :, None, :], rpb_c * LOG2E, NEG_INF)
    full = full.reshape(h // NA_HG, NA_HG * GRID_W, (2 * NA_KR - 1) * GRID_W)
    half = NA_KR // 2
    rep_rows = list(range(half)) + [half] + list(range(n_rows - half + 1, n_rows))
    tabs = []
    for r in rep_rows:
        rs = min(max(r - half, 0), n_rows - NA_KR)
        d0 = rs - r + NA_KR - 1
        tabs.append(full[:, :, GRID_W * d0:GRID_W * d0 + NA_WIN])
    return jnp.stack(tabs, axis=0)


def _natten(y, q_col0, k_col0, v_col0, y_ctx, kc_col0, vc_col0, bias_tab, *, rows_per_step, name="natten"):
    b, l, _ = y.shape
    lc = y_ctx.shape[1]
    n_rows = l // GRID_W
    n_tab = bias_tab.shape[0]
    n_grp = NA_HEADS // NA_HG
    tq = rows_per_step * GRID_W
    qb, kb, vb = q_col0 // NA_GW, k_col0 // NA_GW, v_col0 // NA_GW
    kcb, vcb = kc_col0 // NA_GW, vc_col0 // NA_GW
    return pl.pallas_call(
        functools.partial(_natten_body, rows_per_step=rows_per_step, n_rows=n_rows),
        grid=(b, n_grp, n_rows // rows_per_step),
        in_specs=[
            pl.BlockSpec((None, tq, NA_GW), lambda bi, g, i: (bi, i, qb + g)),
            pl.BlockSpec((None, l, NA_GW), lambda bi, g, i: (bi, 0, kb + g)),
            pl.BlockSpec((None, l, NA_GW), lambda bi, g, i: (bi, 0, vb + g)),
            pl.BlockSpec((None, lc, NA_GW), lambda bi, g, i: (bi, 0, kcb + g)),
            pl.BlockSpec((None, lc, NA_GW), lambda bi, g, i: (bi, 0, vcb + g)),
            pl.BlockSpec((n_tab, None, NA_HG * GRID_W, NA_WIN), lambda bi, g, i: (0, g, 0, 0)),
        ],
        out_specs=pl.BlockSpec((None, tq, NA_GW), lambda bi, g, i: (bi, i, g)),
        out_shape=jax.ShapeDtypeStruct((b, l, D_WIDTH), BF16),
        compiler_params=_cparams(("parallel", "parallel", "arbitrary")),
        name=name,
    )(y, y, y, y_ctx, y_ctx, bias_tab)


def _rope_tables(l):
    t = jnp.arange(l, dtype=jnp.int32)
    row = (t // GRID_W).astype(F32)
    col = (t % GRID_W).astype(F32)
    n_freq = DIFF_DH // 4
    inv_freq = ROPE_THETA ** (-jnp.arange(n_freq, dtype=F32) / n_freq)
    ang = jnp.concatenate([row[:, None] * inv_freq, col[:, None] * inv_freq], axis=-1)
    c, s = jnp.cos(ang), jnp.sin(ang)
    cos = jnp.tile(jnp.concatenate([c, c], axis=-1), (1, 2))
    sin = jnp.tile(jnp.concatenate([-s, s], axis=-1), (1, 2))
    return cos, sin


def kernel(x, c, ctx, c_ctx, w_mod, b_mod, ln_g, ln_b, e_w_in, e_conv, e_lam_q1, e_lam_k1, e_lam_q2, e_lam_k2, e_subln_g, e_w_o, e_ffn_gate, e_ffn_up, e_ffn_down, o_w_in, o_rpb, o_w_o, o_router, o_exp_gate, o_exp_up, o_exp_down):
    b, l, d = x.shape
    lc = ctx.shape[1]
    assert d == D_MODEL and l % 512 == 0 and lc % 256 == 0 and b + 1 <= MOD_ROWS

    cond = jnp.concatenate([c, c_ctx[None, :], jnp.zeros((MOD_ROWS - b - 1, d), F32)], axis=0)
    mods = _adaln(cond, w_mod, b_mod)

    def layer_mods(i):
        lat = mods[i, :b][:, None, :]
        cx = jnp.broadcast_to(mods[i, b][None, None, :], (b, 1, 6 * d))
        return lat, cx

    q_scale_diff = DIFF_DH ** -0.5 * LOG2E
    q_scale_na = NA_DH ** -0.5 * LOG2E

    mod_lat, mod_ctx = layer_mods(0)
    lam_init = 0.8 - 0.6 * math.exp(-0.3 * 0)
    a_end = 3 * A_WIDTH
    w_in = e_w_in[0]
    w_main = w_in[:, :a_end + 2 * DIFF_QK].astype(BF16)
    w_vt = w_in[:, a_end + 2 * DIFF_QK:].T.astype(BF16)
    rope = _rope_tables(l)
    pa_chunks = [(0, 512, "plain"), (512, 512, "plain"), (1024, 512, "plain")]
    y_lat, vt_lat = _proj(x, mod_lat, 0, w_main,
                          pa_chunks + [(a_end, 512, "rope_scale"), (a_end + 512, 512, "rope")],
                          tm=1024, wvt=w_vt, rope=rope, q_scale=q_scale_diff, name="even_inproj_lat")
    y_ctx, vt_ctx = _proj(ctx, mod_ctx, 0, w_main,
                          pa_chunks + [(a_end, 512, "scale"), (a_end + 512, 512, "plain")],
                          tm=lc, wvt=w_vt, q_scale=q_scale_diff, name="even_inproj_ctx")

    lam_pack = jnp.zeros((8, LANES), F32)
    lam_pack = lam_pack.at[0, :DIFF_DH].set(e_lam_q1[0]).at[1, :DIFF_DH].set(e_lam_k1[0])
    lam_pack = lam_pack.at[2, :DIFF_DH].set(e_lam_q2[0]).at[3, :DIFF_DH].set(e_lam_k2[0])
    sub_g = e_subln_g[0].reshape(1, DIFF_VDIM)
    o_lat = _diff_attention(y_lat, [(y_ctx, vt_ctx), (y_lat, vt_lat)], lam_pack, sub_g, lam_init,
                            tq=256, q_col0=a_end, k_col0=a_end + DIFF_QK, name="diffattn_lat")
    o_ctx = _diff_attention(y_ctx, [(y_ctx, vt_ctx)], lam_pack, sub_g, lam_init,
                            tq=lc, q_col0=a_end, k_col0=a_end + DIFF_QK, name="diffattn_ctx")

    w_o = e_w_o[0].astype(BF16)
    lng0, lnb0 = ln_g[0, 0][None, :], ln_b[0, 0][None, :]
    lng1, lnb1 = ln_g[0, 1][None, :], ln_b[0, 1][None, :]
    x_lat = _conv_outproj_ln(y_lat, e_conv[0], o_lat, w_o, x, mod_lat, 2, lng0, lnb0, tm=512,
                             name="even_outproj_lat")
    x_ctx = _conv_outproj_ln(y_ctx, e_conv[0], o_ctx, w_o, ctx, mod_ctx, 2, lng0, lnb0, tm=lc,
                             name="even_outproj_ctx")

    wg = e_ffn_gate[0].astype(BF16)
    wu = e_ffn_up[0].astype(BF16)
    wd = e_ffn_down[0].astype(BF16)
    x_lat = _ffn_ln(x_lat, mod_lat, 3, wg, wu, wd, lng1, lnb1, tm=512, n_chunks=2, name="ffn_lat")
    x_ctx = _ffn_ln(x_ctx, mod_ctx, 3, wg, wu, wd, lng1, lnb1, tm=lc, n_chunks=2, name="ffn_ctx")

    mod_lat, mod_ctx = layer_mods(1)
    w_in = o_w_in[0].astype(BF16)
    y_lat = _proj(x_lat, mod_lat, 0, w_in,
                  [(0, 512, "plain"), (512, 512, "scale"), (1024, 512, "plain"), (1536, 512, "plain")],
                  tm=1024, q_scale=q_scale_na, name="odd_inproj_lat")
    y_ctx = _proj(x_ctx, mod_ctx, 0, w_in[:, C_WIDTH + D_WIDTH:],
                  [(0, 512, "plain"), (512, 512, "plain")], tm=lc, name="odd_inproj_ctx")
    f_lat = _fourier_mixer(y_lat)
    bias_tab = _na_bias_table(o_rpb[0], l // GRID_W)
    n_lat = _natten(y_lat, C_WIDTH, C_WIDTH + D_WIDTH, C_WIDTH + 2 * D_WIDTH, y_ctx, 0, D_WIDTH,
                    bias_tab, rows_per_step=16)

    w_o = o_w_o[0].astype(BF16)
    lng0, lnb0 = ln_g[1, 0][None, :], ln_b[1, 0][None, :]
    lng1, lnb1 = ln_g[1, 1][None, :], ln_b[1, 1][None, :]
    return _outproj_moe_ln(f_lat, n_lat, w_o, x_lat, mod_lat, lng0, lnb0, o_router[0],
                           o_exp_gate[0], o_exp_up[0], o_exp_down[0], lng1, lnb1, tf=512)
```

```python
import functools
import math

import numpy as np
import jax
import jax.numpy as jnp
from jax import lax
from jax.experimental import pallas as pl
from jax.experimental.pallas import tpu as pltpu

F32 = jnp.float32
BF16 = jnp.bfloat16

D_MODEL = 1024
GRID_W = 64
DEPTH = 2

A_WIDTH = 512
DIFF_HEADS = 4
DIFF_DH = 64
DIFF_VDIM = 128
DIFF_QK = 512
B_WIDTH = 512

C_WIDTH = 512
C_GROUPS = 4
C_GROUP_DIM = 128
NA_HEADS = 8
NA_DH = 64
D_WIDTH = 512
NA_KR = 8
NA_KC = 16

N_EXPERTS = 8

ROPE_THETA = 10000.0
LN_EPS = 1e-5
RMS_EPS = 1e-5
NEG_INF = -1e30
DEEPNORM_ALPHA = (2 * DEPTH) ** 0.25
LOG2E = 1.4426950408889634

LANES = 128
SUBLANES = 8
MOD_ROWS = 32
NT_DIMS = (((1,), (1,)), ((), ()))


def _cparams(sem, vmem_mb=48):
    return pltpu.CompilerParams(dimension_semantics=sem, vmem_limit_bytes=vmem_mb * 1024 * 1024)


def _layer_norm(r, g, b):
    mu = jnp.mean(r, axis=-1, keepdims=True)
    d = r - mu
    var = jnp.mean(d * d, axis=-1, keepdims=True)
    return d * lax.rsqrt(var + LN_EPS) * g + b


def _adaln_body(c_ref, w_ref, b_ref, o_ref):
    cnd = c_ref[...]
    s = (cnd * jax.nn.sigmoid(cnd)).astype(BF16)
    o_ref[...] = jnp.dot(s, w_ref[...].astype(BF16), preferred_element_type=F32) + b_ref[...]


def _adaln(cond, w_mod, b_mod):
    depth, d, n = w_mod.shape
    tn = 1536
    return pl.pallas_call(
        _adaln_body,
        grid=(depth, n // tn),
        in_specs=[
            pl.BlockSpec((MOD_ROWS, d), lambda l, j: (0, 0)),
            pl.BlockSpec((None, d, tn), lambda l, j: (l, 0, j)),
            pl.BlockSpec((None, 1, tn), lambda l, j: (l, 0, j)),
        ],
        out_specs=pl.BlockSpec((None, MOD_ROWS, tn), lambda l, j: (l, 0, j)),
        out_shape=jax.ShapeDtypeStruct((depth, MOD_ROWS, n), F32),
        compiler_params=_cparams(("parallel", "parallel")),
        name="adaln",
    )(cond, w_mod, b_mod.reshape(depth, 1, n))


def _proj_body(*refs, chunks, with_vt, with_rope, q_scale):
    x_ref, sh_ref, sc_ref, w_ref = refs[:4]
    i = 4
    if with_vt:
        wvt_ref = refs[i]
        i += 1
    if with_rope:
        cos_ref, sin_ref = refs[i], refs[i + 1]
        i += 2
    y_ref = refs[i]
    vt_ref = refs[i + 1] if with_vt else None

    h = (x_ref[...] * (1.0 + sc_ref[...]) + sh_ref[...]).astype(BF16)
    tm = h.shape[0]
    if with_rope:
        cos = cos_ref[...]
        sin = sin_ref[...]
        lane = lax.broadcasted_iota(jnp.int32, (tm, LANES), 1)
        low_half = (lane % 64) < 32
    for (c0, width, kind) in chunks:
        acc = jnp.dot(h, w_ref[:, c0:c0 + width], preferred_element_type=F32)
        if kind in ("rope", "rope_scale"):
            for j in range(width // LANES):
                a = acc[:, LANES * j:LANES * (j + 1)]
                rot = jnp.where(low_half, pltpu.roll(a, 96, 1), pltpu.roll(a, 32, 1))
                r = a * cos + rot * sin
                if kind == "rope_scale":
                    r = r * q_scale
                y_ref[:, c0 + LANES * j:c0 + LANES * (j + 1)] = r.astype(BF16)
        elif kind == "scale":
            y_ref[:, c0:c0 + width] = (acc * q_scale).astype(BF16)
        else:
            y_ref[:, c0:c0 + width] = acc.astype(BF16)
    if with_vt:
        vt = lax.dot_general(wvt_ref[...], h, NT_DIMS, preferred_element_type=F32)
        vt_ref[...] = vt.astype(BF16)


def _proj(x, mod, shift_col, w, chunks, *, tm, wvt=None, rope=None, q_scale=1.0, name="proj"):
    b, l, d = x.shape
    n = w.shape[1]
    with_vt = wvt is not None
    with_rope = rope is not None
    in_specs = [
        pl.BlockSpec((None, tm, d), lambda bi, i: (bi, i, 0)),
        pl.BlockSpec((None, 1, d), lambda bi, i: (bi, 0, shift_col)),
        pl.BlockSpec((None, 1, d), lambda bi, i: (bi, 0, shift_col + 1)),
        pl.BlockSpec((d, n), lambda bi, i: (0, 0)),
    ]
    args = [x, mod, mod, w]
    if with_vt:
        nv = wvt.shape[0]
        in_specs.append(pl.BlockSpec((nv, d), lambda bi, i: (0, 0)))
        args.append(wvt)
    if with_rope:
        in_specs += [pl.BlockSpec((tm, LANES), lambda bi, i: (i, 0))] * 2
        args += [rope[0], rope[1]]
    out_specs = [pl.BlockSpec((None, tm, n), lambda bi, i: (bi, i, 0))]
    out_shape = [jax.ShapeDtypeStruct((b, l, n), BF16)]
    if with_vt:
        out_specs.append(pl.BlockSpec((None, nv, tm), lambda bi, i: (bi, 0, i)))
        out_shape.append(jax.ShapeDtypeStruct((b, nv, l), BF16))
    res = pl.pallas_call(
        functools.partial(_proj_body, chunks=tuple(chunks), with_vt=with_vt, with_rope=with_rope,
                          q_scale=q_scale),
        grid=(b, l // tm),
        in_specs=in_specs,
        out_specs=out_specs,
        out_shape=out_shape,
        compiler_params=_cparams(("parallel", "parallel")),
        name=name,
    )(*args)
    return res if with_vt else res[0]


DIFF_HP = 4


def _diffattn_body(*refs, n_seg, lam_init):
    q_ref, lam_ref, g_ref = refs[:3]
    k_refs = [refs[3 + 2 * s] for s in range(n_seg)]
    vt_refs = [refs[4 + 2 * s] for s in range(n_seg)]
    o_ref = refs[3 + 2 * n_seg]

    tq = q_ref.shape[0]
    lane = lax.broadcasted_iota(jnp.int32, (tq, LANES), 1)
    lp = lam_ref[...]
    lam = (jnp.exp(jnp.sum(lp[0:1] * lp[1:2], axis=1, keepdims=True))
           - jnp.exp(jnp.sum(lp[2:3] * lp[3:4], axis=1, keepdims=True)) + lam_init)

    scores = []
    for h in range(DIFF_HP):
        cols = slice(LANES * h, LANES * (h + 1))
        q = q_ref[:, cols]
        zero = jnp.zeros_like(q)
        per_comp = []
        for qm in (jnp.where(lane < DIFF_DH, q, zero), jnp.where(lane >= DIFF_DH, q, zero)):
            per_comp.append([lax.dot_general(k_ref[:, cols], qm, NT_DIMS, preferred_element_type=F32)
                             for k_ref in k_refs])
        scores.append(per_comp)

    def unnormalised(s, rows):
        m = functools.reduce(jnp.maximum, [jnp.max(x, axis=0, keepdims=True) for x in s])
        tot = None
        acc = None
        for x, vt_ref in zip(s, vt_refs):
            e = jnp.exp2(x - m)
            t = jnp.sum(e, axis=0, keepdims=True)
            pv = jnp.dot(vt_ref[rows, :], e.astype(BF16), preferred_element_type=F32)
            tot = t if tot is None else tot + t
            acc = pv if acc is None else acc + pv
        return acc, tot

    for h in range(DIFF_HP):
        cols = slice(LANES * h, LANES * (h + 1))
        rows = slice(DIFF_VDIM * h, DIFF_VDIM * (h + 1))
        acc1, l1 = unnormalised(scores[h][0], rows)
        acc2, l2 = unnormalised(scores[h][1], rows)
        o_t = acc1 * (1.0 / l1) - acc2 * (lam / l2)
        o = o_t.T
        ms = jnp.mean(o * o, axis=-1, keepdims=True)
        o_ref[:, cols] = (o * lax.rsqrt(ms + RMS_EPS) * g_ref[...] * (1.0 - lam_init)).astype(BF16)


def _diff_attention(yq, segs, lam_pack, sub_g, lam_init, *, tq, q_col0, k_col0, name):
    b, lq, _ = yq.shape
    width = DIFF_HP * LANES
    qb0 = q_col0 // width
    kb0 = k_col0 // width
    in_specs = [
        pl.BlockSpec((None, tq, width), lambda bi, h, i: (bi, i, qb0 + h)),
        pl.BlockSpec((8, LANES), lambda bi, h, i: (0, 0)),
        pl.BlockSpec((1, LANES), lambda bi, h, i: (0, 0)),
    ]
    args = [yq, lam_pack, sub_g]
    for (yk, vt) in segs:
        lk = yk.shape[1]
        in_specs.append(pl.BlockSpec((None, lk, width), lambda bi, h, i: (bi, 0, kb0 + h)))
        in_specs.append(pl.BlockSpec((None, DIFF_HP * DIFF_VDIM, lk), lambda bi, h, i: (bi, h, 0)))
        args += [yk, vt]
    return pl.pallas_call(
        functools.partial(_diffattn_body, n_seg=len(segs), lam_init=lam_init),
        grid=(b, DIFF_HEADS // DIFF_HP, lq // tq),
        in_specs=in_specs,
        out_specs=pl.BlockSpec((None, tq, width), lambda bi, h, i: (bi, i, h)),
        out_shape=jax.ShapeDtypeStruct((b, lq, B_WIDTH), BF16),
        compiler_params=_cparams(("parallel", "parallel", "arbitrary")),
        name=name,
    )(*args)


HALO_ROWS = 16


def _conv_outproj_body(bg_ref, cg_ref, val_ref, cgp_ref, valp_ref, cgn_ref, valn_ref, wc_ref,
                       o_att_ref, w_ref, x_ref, gate_ref, lng_ref, lnb_ref, o_ref):
    i = pl.program_id(1)
    tm = x_ref.shape[0]
    half = o_att_ref.shape[1]
    u = cg_ref[...].astype(F32) * val_ref[...].astype(F32)
    u_before = cgp_ref[...].astype(F32) * valp_ref[...].astype(F32)
    u_after = cgn_ref[...].astype(F32) * valn_ref[...].astype(F32)
    first = jnp.where(i == 0, 0.0, u_before[HALO_ROWS - 1:HALO_ROWS, :])
    last = jnp.where(i == pl.num_programs(1) - 1, 0.0, u_after[0:1, :])
    row = lax.broadcasted_iota(jnp.int32, (tm, half), 0)
    u_prev = jnp.where(row == 0, first, pltpu.roll(u, 1, 0))
    u_next = jnp.where(row == tm - 1, last, pltpu.roll(u, tm - 1, 0))
    wc = wc_ref[...]
    conv = bg_ref[...].astype(F32) * (u_prev * wc[0:1] + u * wc[1:2] + u_next * wc[2:3])
    y = (jnp.dot(conv.astype(BF16), w_ref[0:half, :], preferred_element_type=F32)
         + jnp.dot(o_att_ref[...], w_ref[half:2 * half, :], preferred_element_type=F32))
    r = DEEPNORM_ALPHA * x_ref[...] + gate_ref[...] * y
    o_ref[...] = _layer_norm(r, lng_ref[...], lnb_ref[...])


def _conv_outproj_ln(y, w_conv, o_att, w_o, x, mod, gate_col, ln_g, ln_b, *, tm, name):
    b, l, d = x.shape
    half = w_o.shape[0] // 2
    per = tm // HALO_ROWS
    n_halo = l // HALO_ROWS
    main = lambda col: pl.BlockSpec((None, tm, half), lambda bi, i: (bi, i, col))
    before = lambda col: pl.BlockSpec((None, HALO_ROWS, half),
                                      lambda bi, i: (bi, jnp.maximum(i * per - 1, 0), col))
    after = lambda col: pl.BlockSpec((None, HALO_ROWS, half),
                                     lambda bi, i: (bi, jnp.minimum((i + 1) * per, n_halo - 1), col))
    return pl.pallas_call(
        _conv_outproj_body,
        grid=(b, l // tm),
        in_specs=[
            main(0), main(1), main(2), before(1), before(2), after(1), after(2),
            pl.BlockSpec((3, half), lambda bi, i: (0, 0)),
            pl.BlockSpec((None, tm, half), lambda bi, i: (bi, i, 0)),
            pl.BlockSpec((2 * half, d), lambda bi, i: (0, 0)),
            pl.BlockSpec((None, tm, d), lambda bi, i: (bi, i, 0)),
            pl.BlockSpec((None, 1, d), lambda bi, i: (bi, 0, gate_col)),
            pl.BlockSpec((1, d), lambda bi, i: (0, 0)),
            pl.BlockSpec((1, d), lambda bi, i: (0, 0)),
        ],
        out_specs=pl.BlockSpec((None, tm, d), lambda bi, i: (bi, i, 0)),
        out_shape=jax.ShapeDtypeStruct((b, l, d), F32),
        compiler_params=_cparams(("parallel", "parallel")),
        name=name,
    )(y, y, y, y, y, y, y, w_conv, o_att, w_o, x, mod, ln_g, ln_b)


def _outproj_rows(a_ref, b_ref, w_ref, x_ref, gate_ref, lng_ref, lnb_ref):
    half = a_ref.shape[1]
    y = (jnp.dot(a_ref[...], w_ref[0:half, :], preferred_element_type=F32)
         + jnp.dot(b_ref[...], w_ref[half:2 * half, :], preferred_element_type=F32))
    r = DEEPNORM_ALPHA * x_ref[...] + gate_ref[...] * y
    return _layer_norm(r, lng_ref[...], lnb_ref[...])


MXU_WIDTH = 256


def _ffn_body(x_ref, sh_ref, sc_ref, gate_ref, wg_ref, wu_ref, wd_ref, lng_ref, lnb_ref, o_ref, *, chunks):
    x = x_ref[...]
    h = (x * (1.0 + sc_ref[...]) + sh_ref[...]).astype(BF16)
    acc = None
    for (c0, c1) in chunks:
        g = jnp.dot(h, wg_ref[:, c0:c1], preferred_element_type=F32)
        u = jnp.dot(h, wu_ref[:, c0:c1], preferred_element_type=F32)
        a = (g * jax.nn.sigmoid(g) * u).astype(BF16)
        part = jnp.dot(a, wd_ref[c0:c1, :], preferred_element_type=F32)
        acc = part if acc is None else acc + part
    r = DEEPNORM_ALPHA * x + gate_ref[...] * acc
    o_ref[...] = _layer_norm(r, lng_ref[...], lnb_ref[...])


def _ffn_ln(x, mod, shift_col, wg, wu, wd, ln_g, ln_b, *, tm, n_chunks, name="ffn"):
    b, l, d = x.shape
    ff = wg.shape[1]
    n_mxu = ff // MXU_WIDTH
    assert ff == n_mxu * MXU_WIDTH
    bounds = [MXU_WIDTH * ((n_mxu * k + n_chunks - 1) // n_chunks) for k in range(n_chunks + 1)]
    chunks = tuple((bounds[k], bounds[k + 1]) for k in range(n_chunks))
    resident = pl.Buffered(1)
    return pl.pallas_call(
        functools.partial(_ffn_body, chunks=chunks),
        grid=(b, l // tm),
        in_specs=[
            pl.BlockSpec((None, tm, d), lambda bi, i: (bi, i, 0)),
            pl.BlockSpec((None, 1, d), lambda bi, i: (bi, 0, shift_col)),
            pl.BlockSpec((None, 1, d), lambda bi, i: (bi, 0, shift_col + 1)),
            pl.BlockSpec((None, 1, d), lambda bi, i: (bi, 0, shift_col + 2)),
            pl.BlockSpec((d, ff), lambda bi, i: (0, 0), pipeline_mode=resident),
            pl.BlockSpec((d, ff), lambda bi, i: (0, 0), pipeline_mode=resident),
            pl.BlockSpec((ff, d), lambda bi, i: (0, 0), pipeline_mode=resident),
            pl.BlockSpec((1, d), lambda bi, i: (0, 0)),
            pl.BlockSpec((1, d), lambda bi, i: (0, 0)),
        ],
        out_specs=pl.BlockSpec((None, tm, d), lambda bi, i: (bi, i, 0)),
        out_shape=jax.ShapeDtypeStruct((b, l, d), F32),
        compiler_params=_cparams(("parallel", "parallel")),
        name=name,
    )(x, mod, mod, mod, wg, wu, wd, ln_g, ln_b)


MOE_TM = 1024
ROUTE_TM = 512
ZERO_ROWS = 256
META_I1, META_I2, META_R1, META_R2, META_P1, META_P2 = range(6)


def _outproj_route_body(a_ref, b_ref, w_ref, x_ref, gate_ref, lng_ref, lnb_ref, sh_ref, sc_ref, wr_ref,
                        o_ref, meta_ref, meta_t_ref, cnt_ref, step_cnt_ref, carry_scr):
    x_new = _outproj_rows(a_ref, b_ref, w_ref, x_ref, gate_ref, lng_ref, lnb_ref)
    o_ref[...] = x_new
    _route_rows(x_new, sh_ref, sc_ref, wr_ref, meta_ref, meta_t_ref, cnt_ref, step_cnt_ref, carry_scr)


def _route_rows(x, sh_ref, sc_ref, wr_ref, meta_ref, meta_t_ref, cnt_ref, step_cnt_ref, carry_scr):
    @pl.when(pl.program_id(0) == 0)
    def _():
        carry_scr[...] = jnp.zeros_like(carry_scr)

    h = (x * (1.0 + sc_ref[...]) + sh_ref[...]).astype(BF16)
    logits = jnp.dot(h, wr_ref[...], preferred_element_type=F32)
    tm = logits.shape[0]
    lane = lax.broadcasted_iota(jnp.int32, logits.shape, 1).astype(F32)
    l1 = jnp.where(lane < N_EXPERTS, logits, -jnp.inf)
    v1 = jnp.max(l1, axis=-1, keepdims=True)
    i1 = jnp.min(jnp.where(l1 == v1, lane, float(LANES)), axis=-1, keepdims=True)
    l2 = jnp.where(lane == i1, -jnp.inf, l1)
    v2 = jnp.max(l2, axis=-1, keepdims=True)
    i2 = jnp.min(jnp.where(l2 == v2, lane, float(LANES)), axis=-1, keepdims=True)
    t = jnp.exp(v2 - v1)
    p1 = 1.0 / (1.0 + t)
    p2 = t / (1.0 + t)

    member = jnp.where(lane == i1, 1.0, jnp.where(lane == i2, 1.0, 0.0))
    rr = lax.broadcasted_iota(jnp.int32, (tm, tm), 0)
    cc = lax.broadcasted_iota(jnp.int32, (tm, tm), 1)
    earlier = jnp.where(cc < rr, 1.0, 0.0).astype(BF16)
    base = carry_scr[0:1, :]
    rank = jnp.dot(earlier, member.astype(BF16), preferred_element_type=F32) + base
    total = base + jnp.sum(member, axis=0, keepdims=True)
    carry_scr[0:1, :] = total
    r1 = jnp.sum(jnp.where(lane == i1, rank, 0.0), axis=-1, keepdims=True)
    r2 = jnp.sum(jnp.where(lane == i2, rank, 0.0), axis=-1, keepdims=True)

    meta = jnp.zeros_like(logits)
    for k, val in ((META_I1, i1), (META_I2, i2), (META_R1, r1), (META_R2, r2), (META_P1, p1), (META_P2, p2)):
        meta = jnp.where(lane == float(k), val, meta)
    meta_ref[...] = meta
    meta_t_ref[...] = meta.T[0:SUBLANES, :]
    cnt_ref[...] = jnp.broadcast_to(total, cnt_ref.shape)
    step_cnt_ref[...] = jnp.broadcast_to(total - base, step_cnt_ref.shape)


def _outproj_route(a2, b2, w_o, x2, mod, gate_col, ln_g, ln_b, shift_col, w_router_pad, *, seq_len):
    m, d = x2.shape
    half = w_o.shape[0] // 2
    tm = ROUTE_TM
    batch = lambda col: pl.BlockSpec((None, 1, d), lambda i: ((i * tm) // seq_len, 0, col))
    return pl.pallas_call(
        _outproj_route_body,
        grid=(m // tm,),
        in_specs=[
            pl.BlockSpec((tm, half), lambda i: (i, 0)),
            pl.BlockSpec((tm, half), lambda i: (i, 0)),
            pl.BlockSpec((2 * half, d), lambda i: (0, 0)),
            pl.BlockSpec((tm, d), lambda i: (i, 0)),
            batch(gate_col),
            pl.BlockSpec((1, d), lambda i: (0, 0)),
            pl.BlockSpec((1, d), lambda i: (0, 0)),
            batch(shift_col),
            batch(shift_col + 1),
            pl.BlockSpec((d, LANES), lambda i: (0, 0)),
        ],
        out_specs=[pl.BlockSpec((tm, d), lambda i: (i, 0)),
                   pl.BlockSpec((tm, LANES), lambda i: (i, 0)),
                   pl.BlockSpec((SUBLANES, tm), lambda i: (0, i)),
                   pl.BlockSpec((8, LANES), lambda i: (0, 0)),
                   pl.BlockSpec((8, LANES), lambda i: (i, 0))],
        out_shape=[jax.ShapeDtypeStruct((m, d), F32),
                   jax.ShapeDtypeStruct((m, LANES), F32), jax.ShapeDtypeStruct((SUBLANES, m), F32),
                   jax.ShapeDtypeStruct((8, LANES), F32),
                   jax.ShapeDtypeStruct((8 * (m // tm), LANES), F32)],
        scratch_shapes=[pltpu.VMEM((8, LANES), F32)],
        compiler_params=_cparams(("arbitrary",)),
        name="odd_outproj_route",
    )(a2, b2, w_o, x2, mod, ln_g, ln_b, mod, mod, w_router_pad)


def _to_tiles(tile_ref, value):
    n = value.shape[0]
    for j in range(SUBLANES):
        tile_ref[pl.ds(j, n, stride=SUBLANES), :] = value[:, LANES * j:LANES * (j + 1)]


def _from_tiles(tile_ref):
    n = tile_ref.shape[0] // SUBLANES
    return jnp.concatenate([tile_ref[pl.ds(j, n, stride=SUBLANES), :] for j in range(SUBLANES)], axis=-1)


def _tile_rows(ref, row, n=1):
    return ref.at[pl.ds(pl.multiple_of(row * SUBLANES, SUBLANES), n * SUBLANES), :]


def _row_copy(src, src_row, dst, dst_row, sem):
    return pltpu.make_async_copy(_tile_rows(src, src_row), _tile_rows(dst, dst_row), sem)


DISPATCH_CH = 16
DISPATCH_SLOTS = 2 * ROUTE_TM + N_EXPERTS * DISPATCH_CH


def _chunk_loops(nch_ref, step, make_copy, action):
    for e in range(N_EXPERTS):
        def body(k, carry, e=e):
            cp = make_copy(step, e, k)
            if action == "start":
                cp.start(priority=e % 2)
            else:
                cp.wait()
            return carry

        lax.fori_loop(0, nch_ref[step * N_EXPERTS + e], body, 0)


def _dispatch_body(pad_ref, nch_ref, loc_ref, dst_ref, lslot_ref, x_ref, sh_ref, sc_ref, xs_hbm,
                   hs_scr, z_scr, row_sems):
    step = pl.program_id(0)
    last = pl.num_programs(0) - 1
    tm = x_ref.shape[0]

    def chunk_copy(buf, st, e, k):
        src = _tile_rows(hs_scr.at[buf], loc_ref[st * N_EXPERTS + e] + k * DISPATCH_CH, DISPATCH_CH)
        dst = _tile_rows(xs_hbm, dst_ref[st * N_EXPERTS + e] + k * DISPATCH_CH, DISPATCH_CH)
        return pltpu.make_async_copy(src, dst, row_sems.at[buf])

    def produce(buf):
        h = (x_ref[...] * (1.0 + sc_ref[...]) + sh_ref[...]).astype(BF16)
        ls = lslot_ref[...]
        srow = lax.broadcasted_iota(jnp.int32, (DISPATCH_SLOTS, tm), 0)
        pick = jnp.where(srow == ls[0:1, :], 1.0, jnp.where(srow == ls[1:2, :], 1.0, 0.0)).astype(BF16)
        _to_tiles(hs_scr.at[buf], jnp.dot(pick, h, preferred_element_type=F32))

    for parity in (0, 1):
        pl.when(step % 2 == parity)(functools.partial(produce, parity))
    for parity in (0, 1):
        pl.when((step % 2 == parity) & (step >= 1))(functools.partial(
            _chunk_loops, nch_ref, step - 1, functools.partial(chunk_copy, 1 - parity), "wait"))
    for parity in (0, 1):
        pl.when(step % 2 == parity)(functools.partial(
            _chunk_loops, nch_ref, step, functools.partial(chunk_copy, parity), "start"))
    for parity in (0, 1):
        pl.when((step % 2 == parity) & (step == last))(functools.partial(
            _chunk_loops, nch_ref, step, functools.partial(chunk_copy, parity), "wait"))
    row_sem = row_sems.at[0]

    @pl.when(step == pl.num_programs(0) - 1)
    def _():
        z_scr[...] = jnp.zeros_like(z_scr)
        for e in range(N_EXPERTS):
            start = pad_ref[e]
            count = pad_ref[N_EXPERTS + e]

            def fill(k, carry, start=start):
                _row_copy(z_scr, 0, xs_hbm, start + k, row_sem).start()
                return carry

            def fill_wait(k, carry):
                _row_copy(z_scr, 0, xs_hbm, 0, row_sem).wait()
                return carry

            lax.fori_loop(0, count, fill, 0)
            lax.fori_loop(0, count, fill_wait, 0)

        zrows = z_scr.shape[0] // SUBLANES
        used_rows = pad_ref[2 * N_EXPERTS]
        n_chunks = (xs_hbm.shape[0] // SUBLANES - used_rows) // zrows

        def chunk_copy(k):
            return pltpu.make_async_copy(z_scr, _tile_rows(xs_hbm, used_rows + k * zrows, zrows), row_sem)

        def fill_chunk(k, carry):
            chunk_copy(k).start()
            return carry

        def fill_chunk_wait(k, carry):
            chunk_copy(k).wait()
            return carry

        lax.fori_loop(0, n_chunks, fill_chunk, 0)
        lax.fori_loop(0, n_chunks, fill_chunk_wait, 0)


def _dispatch(x2, mod, shift_col, lslot, n_chunk, loc0, dest0, pad_info, n_rows, *, seq_len):
    m, d = x2.shape
    tm = ROUTE_TM
    return pl.pallas_call(
        _dispatch_body,
        grid_spec=pltpu.PrefetchScalarGridSpec(
            num_scalar_prefetch=4,
            grid=(m // tm,),
            in_specs=[
                pl.BlockSpec((SUBLANES, tm), lambda i, *_: (0, i)),
                pl.BlockSpec((tm, d), lambda i, *_: (i, 0)),
                pl.BlockSpec((None, 1, d), lambda i, *_: ((i * tm) // seq_len, 0, shift_col)),
                pl.BlockSpec((None, 1, d), lambda i, *_: ((i * tm) // seq_len, 0, shift_col + 1)),
            ],
            out_specs=pl.BlockSpec(memory_space=pl.ANY),
            scratch_shapes=[
                pltpu.VMEM((2, DISPATCH_SLOTS * SUBLANES, LANES), F32),
                pltpu.VMEM((ZERO_ROWS * SUBLANES, LANES), F32),
                pltpu.SemaphoreType.DMA((2,)),
            ],
        ),
        out_shape=jax.ShapeDtypeStruct((n_rows * SUBLANES, LANES), F32),
        compiler_params=_cparams(("arbitrary",)),
        name="moe_dispatch",
    )(pad_info, n_chunk, loc0, dest0, lslot, x2, mod, mod)


TAIL_LEVELS = 2


def _experts_body(te_ref, tbi_ref, tbo_ref, tr_ref, xs_ref, wg_ref, wu_ref, wd_ref, ys_ref, h_scr, acc_scr, *, n_f):
    t = pl.program_id(0)
    f = pl.program_id(1)

    @pl.when(f == 0)
    def _():
        h_scr[...] = _from_tiles(xs_ref).astype(BF16)
        acc_scr[...] = jnp.zeros_like(acc_scr)

    def swiglu_rows(n):
        h = h_scr[0:n, :]
        g = jnp.dot(h, wg_ref[...].astype(BF16), preferred_element_type=F32)
        u = jnp.dot(h, wu_ref[...].astype(BF16), preferred_element_type=F32)
        a = (g * jax.nn.sigmoid(g) * u).astype(BF16)
        acc_scr[0:n, :] += jnp.dot(a, wd_ref[...].astype(BF16), preferred_element_type=F32)

    rows = tr_ref[t]
    tm = h_scr.shape[0]
    bounds = (0,) + tuple(tm >> k for k in range(TAIL_LEVELS, -1, -1))
    for lo, hi in zip(bounds[:-1], bounds[1:]):
        pl.when((rows > lo) & (rows <= hi))(functools.partial(swiglu_rows, hi))

    @pl.when(f == n_f - 1)
    def _():
        _to_tiles(ys_ref, acc_scr[...])


def _experts(xs, tile_expert, tile_in, tile_out, tile_rows, wg, wu, wd, *, tf):
    n_rows = xs.shape[0] // SUBLANES
    d = wg.shape[1]
    tm = MOE_TM
    n_tiles = tile_expert.shape[0]
    ff = wg.shape[2]
    n_f = ff // tf
    ff_blk = lambda f, tr, t: f * jnp.minimum(tr[t], 1)
    return pl.pallas_call(
        functools.partial(_experts_body, n_f=n_f),
        grid_spec=pltpu.PrefetchScalarGridSpec(
            num_scalar_prefetch=4,
            grid=(n_tiles, n_f),
            in_specs=[
                pl.BlockSpec((tm * SUBLANES, LANES), lambda t, f, te, tbi, tbo, tr: (tbi[t], 0)),
                pl.BlockSpec((None, d, tf), lambda t, f, te, tbi, tbo, tr: (te[t], 0, ff_blk(f, tr, t))),
                pl.BlockSpec((None, d, tf), lambda t, f, te, tbi, tbo, tr: (te[t], 0, ff_blk(f, tr, t))),
                pl.BlockSpec((None, tf, d), lambda t, f, te, tbi, tbo, tr: (te[t], ff_blk(f, tr, t), 0)),
            ],
            out_specs=pl.BlockSpec((tm * SUBLANES, LANES), lambda t, f, te, tbi, tbo, tr: (tbo[t], 0)),
            scratch_shapes=[pltpu.VMEM((tm, d), BF16), pltpu.VMEM((tm, d), F32)],
        ),
        out_shape=jax.ShapeDtypeStruct((n_rows * SUBLANES, LANES), F32),
        compiler_params=_cparams(("arbitrary", "arbitrary")),
        name="moe_experts",
    )(tile_expert, tile_in, tile_out, tile_rows, xs, wg, wu, wd)


def _combine_body(nch_ref, loc_ref, dst_ref, ys_hbm, lslot_ref, meta_ref, x_ref, gate_ref, lng_ref, lnb_ref,
                  o_ref, y_scr, row_sems):
    s = pl.program_id(0)
    n_tiles = pl.num_programs(0) - 1
    tm = x_ref.shape[0]

    @pl.when(s == 0)
    def _():
        y_scr[...] = jnp.zeros_like(y_scr)

    def chunk_copy(buf, st, e, k):
        src = _tile_rows(ys_hbm, dst_ref[st * N_EXPERTS + e] + k * DISPATCH_CH, DISPATCH_CH)
        dst = _tile_rows(y_scr.at[buf], loc_ref[st * N_EXPERTS + e] + k * DISPATCH_CH, DISPATCH_CH)
        return pltpu.make_async_copy(src, dst, row_sems.at[buf])

    def finish(buf):
        _chunk_loops(nch_ref, s - 1, functools.partial(chunk_copy, buf), "wait")
        yb = _from_tiles(y_scr.at[buf]).astype(BF16)
        ls = lslot_ref[...]
        lc = jnp.concatenate([ls, jnp.zeros((LANES - SUBLANES, tm), ls.dtype)], axis=0).T
        meta = meta_ref[...]
        slot = lax.broadcasted_iota(jnp.int32, (tm, DISPATCH_SLOTS), 1)
        pick = jnp.where(slot == lc[:, 0:1], meta[:, META_P1:META_P1 + 1],
                         jnp.where(slot == lc[:, 1:2], meta[:, META_P2:META_P2 + 1], 0.0)).astype(BF16)
        mix = jnp.dot(pick, yb, preferred_element_type=F32)
        r = DEEPNORM_ALPHA * x_ref[...] + gate_ref[...] * mix
        o_ref[...] = _layer_norm(r, lng_ref[...], lnb_ref[...])

    for parity in (0, 1):
        pl.when((s % 2 == parity) & (s < n_tiles))(functools.partial(
            _chunk_loops, nch_ref, s, functools.partial(chunk_copy, parity), "start"))
    for parity in (0, 1):
        pl.when((s % 2 == parity) & (s >= 1))(functools.partial(finish, 1 - parity))


def _combine_ln(ys, n_chunk, loc0, dest0, lslot, meta, x2, mod, gate_col, ln_g, ln_b, *, seq_len):
    m, d = x2.shape
    tm = ROUTE_TM
    prev = lambda i: jnp.maximum(i - 1, 0)
    return pl.pallas_call(
        _combine_body,
        grid_spec=pltpu.PrefetchScalarGridSpec(
            num_scalar_prefetch=3,
            grid=(m // tm + 1,),
            in_specs=[
                pl.BlockSpec(memory_space=pl.ANY),
                pl.BlockSpec((SUBLANES, tm), lambda i, *_: (0, prev(i))),
                pl.BlockSpec((tm, LANES), lambda i, *_: (prev(i), 0)),
                pl.BlockSpec((tm, d), lambda i, *_: (prev(i), 0)),
                pl.BlockSpec((None, 1, d), lambda i, *_: ((prev(i) * tm) // seq_len, 0, gate_col)),
                pl.BlockSpec((1, d), lambda i, *_: (0, 0)),
                pl.BlockSpec((1, d), lambda i, *_: (0, 0)),
            ],
            out_specs=pl.BlockSpec((tm, d), lambda i, *_: (prev(i), 0)),
            scratch_shapes=[
                pltpu.VMEM((2, DISPATCH_SLOTS * SUBLANES, LANES), F32),
                pltpu.SemaphoreType.DMA((2,)),
            ],
        ),
        out_shape=jax.ShapeDtypeStruct((m, d), F32),
        compiler_params=_cparams(("arbitrary",)),
        name="moe_combine",
    )(n_chunk, loc0, dest0, ys, lslot, meta, x2, mod, ln_g, ln_b)


def _outproj_moe_ln(a, bsrc, w_o, x, mod, ln_g0, ln_b0, w_router, wg, wu, wd, ln_g, ln_b, *, tf):
    b, l, d = x.shape
    assert d == SUBLANES * LANES
    m = b * l
    w_r = jnp.zeros((d, LANES), BF16).at[:, :N_EXPERTS].set(w_router.astype(BF16))
    x2, meta, meta_t, cnt, step_cnt = _outproj_route(a.reshape(m, -1), bsrc.reshape(m, -1), w_o, x.reshape(m, d),
                                                     mod, 2, ln_g0, ln_b0, 3, w_r, seq_len=l)
    n_steps = m // ROUTE_TM

    counts = cnt[0, :N_EXPERTS].astype(jnp.int32)
    n_tile_e = (counts + DISPATCH_CH - 1 + MOE_TM - 1) // MOE_TM
    tile_end = jnp.cumsum(n_tile_e)
    offs = (tile_end - n_tile_e) * MOE_TM
    idx = meta_t[META_I1:META_I2 + 1].astype(jnp.int32)
    rank = meta_t[META_R1:META_R2 + 1].astype(jnp.int32)
    step_n = step_cnt.reshape(n_steps, 8, LANES)[:, 0, :N_EXPERTS].astype(jnp.int32)
    step_before = jnp.cumsum(step_n, axis=0) - step_n
    step_pad = ((step_n + DISPATCH_CH - 1) // DISPATCH_CH) * DISPATCH_CH
    step_loc0 = jnp.cumsum(step_pad, axis=1) - step_pad
    to_local = step_loc0 - step_before
    idx3 = idx.reshape(2, n_steps, ROUTE_TM)
    local = jnp.zeros_like(idx3)
    for e in range(N_EXPERTS):
        local = jnp.where(idx3 == e, to_local[None, :, e, None], local)
    local = local.reshape(2, m) + rank
    lslot = jnp.concatenate([local, jnp.full((SUBLANES - 2, m), -1, jnp.int32)], axis=0)
    n_chunk = (step_pad // DISPATCH_CH).reshape(-1)
    loc0 = step_loc0.reshape(-1)
    dest0 = (offs[None, :] + step_before).reshape(-1)
    n_tiles = (2 * m) // MOE_TM + N_EXPERTS
    tid = jnp.arange(n_tiles, dtype=jnp.int32)
    tile_valid = (tid < tile_end[-1]).astype(jnp.int32)
    tile_expert = jnp.minimum(jnp.sum((tid[:, None] >= tile_end[None, :]).astype(jnp.int32), axis=1),
                              N_EXPERTS - 1).astype(jnp.int32)
    tile_in = jnp.where(tile_valid > 0, tid, 0).astype(jnp.int32)
    tile_out = tid
    row0 = tid * MOE_TM - jnp.take(offs, tile_expert)
    tile_rows = (jnp.clip(jnp.take(counts, tile_expert) - row0, 0, MOE_TM) * tile_valid).astype(jnp.int32)
    pad_info = jnp.concatenate([offs + counts, n_tile_e * MOE_TM - counts,
                                tile_end[-1:] * MOE_TM]).astype(jnp.int32)
    n_rows = n_tiles * MOE_TM

    xs = _dispatch(x2, mod, 3, lslot, n_chunk, loc0, dest0, pad_info, n_rows, seq_len=l)
    ys = _experts(xs, tile_expert, tile_in, tile_out, tile_rows, wg, wu, wd, tf=tf)
    out = _combine_ln(ys, n_chunk, loc0, dest0, lslot, meta, x2, mod, 5, ln_g, ln_b, seq_len=l)
    return out.reshape(b, l, d)


def _fourier_body(pc_ref, dl_ref, dc_ref, o_ref, t_scr, *, out_scale, row_chunk):
    l = pc_ref.shape[0]
    for g in range(C_GROUPS):
        sl = slice(C_GROUP_DIM * g, C_GROUP_DIM * (g + 1))
        xg = pc_ref[:, sl].astype(F32)
        mu = jnp.mean(xg, axis=-1, keepdims=True)
        dlt = xg - mu
        var = jnp.mean(dlt * dlt, axis=-1, keepdims=True)
        gn = (dlt * lax.rsqrt(var + LN_EPS)).astype(BF16)
        t = jnp.dot(gn, dc_ref[...], preferred_element_type=F32)
        t_scr[0:l, sl] = t[:, 0:C_GROUP_DIM].astype(BF16)
        t_scr[l:2 * l, sl] = t[:, C_GROUP_DIM:2 * C_GROUP_DIM].astype(BF16)
    for r0 in range(0, l, row_chunk):
        acc = jnp.dot(dl_ref[r0:r0 + row_chunk, :], t_scr[...], preferred_element_type=F32)
        o_ref[r0:r0 + row_chunk, :] = (acc * out_scale).astype(BF16)


def _dft_matrices(l, c):
    j = np.arange(l, dtype=np.int64)
    ang_l = (2.0 * np.pi / l) * ((j[:, None] * j[None, :]) % l)
    dl = np.concatenate([np.cos(ang_l), -np.sin(ang_l)], axis=1)
    m = np.arange(c, dtype=np.int64)
    ang_c = (2.0 * np.pi / c) * ((m[:, None] * m[None, :]) % c)
    dc = np.concatenate([np.cos(ang_c), np.sin(ang_c)], axis=1)
    return dl.astype(np.float32), dc.astype(np.float32)


def _fourier_mixer(y, name="fourier"):
    b, l, _ = y.shape
    dl_np, dc_np = _dft_matrices(l, C_GROUP_DIM)
    dl = jnp.asarray(dl_np, dtype=F32).astype(BF16)
    dc = jnp.asarray(dc_np, dtype=F32).astype(BF16)
    out_scale = 1.0 / math.sqrt(l * C_GROUP_DIM)
    return pl.pallas_call(
        functools.partial(_fourier_body, out_scale=out_scale, row_chunk=min(l, 512)),
        grid=(b,),
        in_specs=[
            pl.BlockSpec((None, l, C_WIDTH), lambda bi: (bi, 0, 0)),
            pl.BlockSpec((l, 2 * l), lambda bi: (0, 0), pipeline_mode=pl.Buffered(1)),
            pl.BlockSpec((C_GROUP_DIM, 2 * C_GROUP_DIM), lambda bi: (0, 0)),
        ],
        out_specs=pl.BlockSpec((None, l, C_WIDTH), lambda bi: (bi, 0, 0)),
        out_shape=jax.ShapeDtypeStruct((b, l, C_WIDTH), BF16),
        scratch_shapes=[pltpu.VMEM((2 * l, C_WIDTH), BF16)],
        compiler_params=_cparams(("parallel",)),
        name=name,
    )(y, dl, dc)


NA_HG = 4
NA_GW = NA_HG * NA_DH
NA_WIN = NA_KR * GRID_W


def _natten_body(q_ref, k_ref, v_ref, kc_ref, vc_ref, bias_ref, o_ref, *, rows_per_step, n_rows):
    rb = pl.program_id(2)
    hq = NA_HG * GRID_W
    rid = lax.broadcasted_iota(jnp.int32, (hq, NA_GW), 0)
    cid = lax.broadcasted_iota(jnp.int32, (hq, NA_GW), 1)
    diag = (rid // GRID_W) == (cid // NA_DH)
    cid_o = lax.broadcasted_iota(jnp.int32, (GRID_W, NA_GW), 1)
    kc = kc_ref[...]
    vc = vc_ref[...]
    staged = []
    for j in range(rows_per_step):
        r = rb * rows_per_step + j
        rs = jnp.clip(r - NA_KR // 2, 0, n_rows - NA_KR)
        start = pl.multiple_of(rs * GRID_W, GRID_W)
        tid = jnp.minimum(r, NA_KR // 2) + jnp.maximum(r - (n_rows - NA_KR // 2), 0)
        q_r = q_ref[GRID_W * j:GRID_W * (j + 1), :]
        q4 = jnp.concatenate([q_r] * NA_HG, axis=0)
        qbd = jnp.where(diag, q4, jnp.zeros_like(q4))
        kw = k_ref[pl.ds(start, NA_WIN), :]
        s_loc = lax.dot_general(qbd, kw, NT_DIMS, preferred_element_type=F32) + bias_ref[tid]
        s_ctx = lax.dot_general(qbd, kc, NT_DIMS, preferred_element_type=F32)
        staged.append((start, s_loc, s_ctx))
    for j in range(rows_per_step):
        start, s_loc, s_ctx = staged[j]
        vw = v_ref[pl.ds(start, NA_WIN), :]
        m = jnp.maximum(jnp.max(s_loc, axis=-1, keepdims=True), jnp.max(s_ctx, axis=-1, keepdims=True))
        e_loc = jnp.exp2(s_loc - m)
        e_ctx = jnp.exp2(s_ctx - m)
        tot = jnp.sum(e_loc, axis=-1, keepdims=True) + jnp.sum(e_ctx, axis=-1, keepdims=True)
        o = (jnp.dot(e_loc.astype(BF16), vw, preferred_element_type=F32)
             + jnp.dot(e_ctx.astype(BF16), vc, preferred_element_type=F32))
        o = o * (1.0 / tot)
        out = jnp.zeros((GRID_W, NA_GW), F32)
        for hh in range(NA_HG):
            out = out + jnp.where((cid_o // NA_DH) == hh, o[GRID_W * hh:GRID_W * (hh + 1), :], 0.0)
        o_ref[GRID_W * j:GRID_W * (j + 1), :] = out.astype(BF16)


def _na_bias_table(rpb, n_rows):
    h = rpb.shape[0]
    cols = jnp.arange(GRID_W)
    col_start = jnp.clip(cols - NA_KC // 2, 0, GRID_W - NA_KC)
    col_valid = (cols[None, :] >= col_start[:, None]) & (cols[None, :] < col_start[:, None] + NA_KC)
    dc_idx = jnp.clip(cols[None, :] - cols[:, None] + NA_KC - 1, 0, 2 * NA_KC - 2)
    onehot = (dc_idx[:, :, None] == jnp.arange(2 * NA_KC - 1)).astype(F32)
    rpb_c = jnp.einsum("qkc,hdc->hqdk", onehot, rpb.astype(F32), precision=lax.Precision.HIGHEST)
    full = jnp.where(col_valid[None, :, None, :], rpb_c * LOG2E, NEG_INF)
    full = full.reshape(h // NA_HG, NA_HG * GRID_W, (2 * NA_KR - 1) * GRID_W)
    half = NA_KR // 2
    rep_rows = list(range(half)) + [half] + list(range(n_rows - half + 1, n_rows))
    tabs = []
    for r in rep_rows:
        rs = min(max(r - half, 0), n_rows - NA_KR)
        d0 = rs - r + NA_KR - 1
        tabs.append(full[:, :, GRID_W * d0:GRID_W * d0 + NA_WIN])
    return jnp.stack(tabs, axis=0)


def _natten(y, q_col0, k_col0, v_col0, y_ctx, kc_col0, vc_col0, bias_tab, *, rows_per_step, name="natten"):
    b, l, _ = y.shape
    lc = y_ctx.shape[1]
    n_rows = l // GRID_W
    n_tab = bias_tab.shape[0]
    n_grp = NA_HEADS // NA_HG
    tq = rows_per_step * GRID_W
    qb, kb, vb = q_col0 // NA_GW, k_col0 // NA_GW, v_col0 // NA_GW
    kcb, vcb = kc_col0 // NA_GW, vc_col0 // NA_GW
    return pl.pallas_call(
        functools.partial(_natten_body, rows_per_step=rows_per_step, n_rows=n_rows),
        grid=(b, n_grp, n_rows // rows_per_step),
        in_specs=[
            pl.BlockSpec((None, tq, NA_GW), lambda bi, g, i: (bi, i, qb + g)),
            pl.BlockSpec((None, l, NA_GW), lambda bi, g, i: (bi, 0, kb + g)),
            pl.BlockSpec((None, l, NA_GW), lambda bi, g, i: (bi, 0, vb + g)),
            pl.BlockSpec((None, lc, NA_GW), lambda bi, g, i: (bi, 0, kcb + g)),
            pl.BlockSpec((None, lc, NA_GW), lambda bi, g, i: (bi, 0, vcb + g)),
            pl.BlockSpec((n_tab, None, NA_HG * GRID_W, NA_WIN), lambda bi, g, i: (0, g, 0, 0)),
        ],
        out_specs=pl.BlockSpec((None, tq, NA_GW), lambda bi, g, i: (bi, i, g)),
        out_shape=jax.ShapeDtypeStruct((b, l, D_WIDTH), BF16),
        compiler_params=_cparams(("parallel", "parallel", "arbitrary")),
        name=name,
    )(y, y, y, y_ctx, y_ctx, bias_tab)


def _rope_tables(l):
    t = jnp.arange(l, dtype=jnp.int32)
    row = (t // GRID_W).astype(F32)
    col = (t % GRID_W).astype(F32)
    n_freq = DIFF_DH // 4
    inv_freq = ROPE_THETA ** (-jnp.arange(n_freq, dtype=F32) / n_freq)
    ang = jnp.concatenate([row[:, None] * inv_freq, col[:, None] * inv_freq], axis=-1)
    c, s = jnp.cos(ang), jnp.sin(ang)
    cos = jnp.tile(jnp.concatenate([c, c], axis=-1), (1, 2))
    sin = jnp.tile(jnp.concatenate([-s, s], axis=-1), (1, 2))
    return cos, sin


def kernel(x, c, ctx, c_ctx, w_mod, b_mod, ln_g, ln_b, e_w_in, e_conv, e_lam_q1, e_lam_k1, e_lam_q2, e_lam_k2, e_subln_g, e_w_o, e_ffn_gate, e_ffn_up, e_ffn_down, o_w_in, o_rpb, o_w_o, o_router, o_exp_gate, o_exp_up, o_exp_down):
    b, l, d = x.shape
    lc = ctx.shape[1]
    assert d == D_MODEL and l % 512 == 0 and lc % 256 == 0 and b + 1 <= MOD_ROWS

    cond = jnp.concatenate([c, c_ctx[None, :], jnp.zeros((MOD_ROWS - b - 1, d), F32)], axis=0)
    mods = _adaln(cond, w_mod, b_mod)

    def layer_mods(i):
        lat = mods[i, :b][:, None, :]
        cx = jnp.broadcast_to(mods[i, b][None, None, :], (b, 1, 6 * d))
        return lat, cx

    q_scale_diff = DIFF_DH ** -0.5 * LOG2E
    q_scale_na = NA_DH ** -0.5 * LOG2E

    mod_lat, mod_ctx = layer_mods(0)
    lam_init = 0.8 - 0.6 * math.exp(-0.3 * 0)
    a_end = 3 * A_WIDTH
    w_in = e_w_in[0]
    w_main = w_in[:, :a_end + 2 * DIFF_QK].astype(BF16)
    w_vt = w_in[:, a_end + 2 * DIFF_QK:].T.astype(BF16)
    rope = _rope_tables(l)
    pa_chunks = [(0, 512, "plain"), (512, 512, "plain"), (1024, 512, "plain")]
    y_lat, vt_lat = _proj(x, mod_lat, 0, w_main,
                          pa_chunks + [(a_end, 512, "rope_scale"), (a_end + 512, 512, "rope")],
                          tm=1024, wvt=w_vt, rope=rope, q_scale=q_scale_diff, name="even_inproj_lat")
    y_ctx, vt_ctx = _proj(ctx, mod_ctx, 0, w_main,
                          pa_chunks + [(a_end, 512, "scale"), (a_end + 512, 512, "plain")],
                          tm=lc, wvt=w_vt, q_scale=q_scale_diff, name="even_inproj_ctx")

    lam_pack = jnp.zeros((8, LANES), F32)
    lam_pack = lam_pack.at[0, :DIFF_DH].set(e_lam_q1[0]).at[1, :DIFF_DH].set(e_lam_k1[0])
    lam_pack = lam_pack.at[2, :DIFF_DH].set(e_lam_q2[0]).at[3, :DIFF_DH].set(e_lam_k2[0])
    sub_g = e_subln_g[0].reshape(1, DIFF_VDIM)
    o_lat = _diff_attention(y_lat, [(y_ctx, vt_ctx), (y_lat, vt_lat)], lam_pack, sub_g, lam_init,
                            tq=256, q_col0=a_end, k_col0=a_end + DIFF_QK, name="diffattn_lat")
    o_ctx = _diff_attention(y_ctx, [(y_ctx, vt_ctx)], lam_pack, sub_g, lam_init,
                            tq=lc, q_col0=a_end, k_col0=a_end + DIFF_QK, name="diffattn_ctx")

    w_o = e_w_o[0].astype(BF16)
    lng0, lnb0 = ln_g[0, 0][None, :], ln_b[0, 0][None, :]
    lng1, lnb1 = ln_g[0, 1][None, :], ln_b[0, 1][None, :]
    x_lat = _conv_outproj_ln(y_lat, e_conv[0], o_lat, w_o, x, mod_lat, 2, lng0, lnb0, tm=512,
                             name="even_outproj_lat")
    x_ctx = _conv_outproj_ln(y_ctx, e_conv[0], o_ctx, w_o, ctx, mod_ctx, 2, lng0, lnb0, tm=lc,
                             name="even_outproj_ctx")

    wg = e_ffn_gate[0].astype(BF16)
    wu = e_ffn_up[0].astype(BF16)
    wd = e_ffn_down[0].astype(BF16)
    x_lat = _ffn_ln(x_lat, mod_lat, 3, wg, wu, wd, lng1, lnb1, tm=512, n_chunks=2, name="ffn_lat")
    x_ctx = _ffn_ln(x_ctx, mod_ctx, 3, wg, wu, wd, lng1, lnb1, tm=lc, n_chunks=2, name="ffn_ctx")

    mod_lat, mod_ctx = layer_mods(1)
    w_in = o_w_in[0].astype(BF16)
    y_lat = _proj(x_lat, mod_lat, 0, w_in,
                  [(0, 512, "plain"), (512, 512, "scale"), (1024, 512, "plain"), (1536, 512, "plain")],
                  tm=1024, q_scale=q_scale_na, name="odd_inproj_lat")
    y_ctx = _proj(x_ctx, mod_ctx, 0, w_in[:, C_WIDTH + D_WIDTH:],
                  [(0, 512, "plain"), (512, 512, "plain")], tm=lc, name="odd_inproj_ctx")
    f_lat = _fourier_mixer(y_lat)
    bias_tab = _na_bias_table(o_rpb[0], l // GRID_W)
    n_lat = _natten(y_lat, C_WIDTH, C_WIDTH + D_WIDTH, C_WIDTH + 2 * D_WIDTH, y_ctx, 0, D_WIDTH,
                    bias_tab, rows_per_step=16)

    w_o = o_w_o[0].astype(BF16)
    lng0, lnb0 = ln_g[1, 0][None, :], ln_b[1, 0][None, :]
    lng1, lnb1 = ln_g[1, 1][None, :], ln_b[1, 1][None, :]
    return _outproj_moe_ln(f_lat, n_lat, w_o, x_lat, mod_lat, lng0, lnb0, o_router[0],
                           o_exp_gate[0], o_exp_up[0], o_exp_down[0], lng1, lnb1, tf=512)
```
